```python
import math
import jax, jax.numpy as jnp
from jax import lax
import numpy as np

D_MODEL = 2048
BATCH = 2
SEQ = 4096
DEPTH = 1

ATTN_HEADS = 8
ATTN_QK_DIM = 64
ATTN_V_DIM = 2 * ATTN_QK_DIM
ATTN_QK_WIDTH = ATTN_HEADS * 2 * ATTN_QK_DIM
ATTN_WIDTH = ATTN_HEADS * ATTN_V_DIM
Q_BLOCK = 128
MLSTM_HEADS = 4
MLSTM_QK_DIM = 128
MLSTM_V_DIM = 256
MLSTM_QK_WIDTH = MLSTM_HEADS * MLSTM_QK_DIM
MLSTM_WIDTH = MLSTM_HEADS * MLSTM_V_DIM
CONV_WIDTH = 4
CHUNK = 64
N_MEM = 256
CROSS_HEADS = 4
CROSS_HEAD_DIM = 128
CROSS_WIDTH = CROSS_HEADS * CROSS_HEAD_DIM
D_FF = 4 * D_MODEL
EPS = 1e-6
IN_WIDTHS = (ATTN_QK_WIDTH, ATTN_QK_WIDTH, ATTN_WIDTH, 2 * MLSTM_QK_WIDTH, MLSTM_WIDTH, MLSTM_WIDTH, MLSTM_HEADS, MLSTM_HEADS, D_MODEL, D_MODEL)
IN_WIDTH = sum(IN_WIDTHS)

kernel_name = "hybrid_diffattn_mlstm_gated_block"


def rms_norm(x, g):
    xf = x.astype(jnp.float32)
    y = xf * lax.rsqrt(jnp.mean(xf * xf, axis=-1, keepdims=True) + EPS)
    return (y * g.astype(jnp.float32)).astype(x.dtype)


def split_cols(z):
    idx, acc = [], 0
    for w in IN_WIDTHS[:-1]:
        acc += w
        idx.append(acc)
    return jnp.split(z, idx, axis=-1)


def causal_conv(u, w, b):
    c = u.shape[-1]
    out = lax.conv_general_dilated(u, w[:, None, :].astype(u.dtype), window_strides=(1,), padding=[(CONV_WIDTH - 1, 0)], dimension_numbers=('NWC', 'WIO', 'NWC'), feature_group_count=c)
    return out + b


def diff_attention(q, k, v, lam, slopes):
    b_, s_ = q.shape[:2]
    nb = s_ // Q_BLOCK
    scale = ATTN_QK_DIM ** -0.5
    q_blocks = q.reshape(b_, nb, Q_BLOCK, ATTN_HEADS, 2, ATTN_QK_DIM).swapaxes(0, 1)
    starts = jnp.arange(nb, dtype=jnp.int32) * Q_BLOCK
    k_pos = jnp.arange(s_, dtype=jnp.int32)
    v32 = v.astype(jnp.float32)

    def block(args):
        q_blk, start = args
        q_pos = start + jnp.arange(Q_BLOCK, dtype=jnp.int32)
        dist = q_pos[:, None] - k_pos[None, :]
        bias = -slopes[:, None, None] * dist.astype(jnp.float32)
        s = jnp.einsum('bqhmd,bkhmd->bhmqk', q_blk, k).astype(jnp.float32) * scale + bias[None, :, None]
        s = jnp.where(dist >= 0, s, -jnp.inf)
        p = jax.nn.softmax(s, axis=-1)
        w = p[:, :, 0] - lam * p[:, :, 1]
        return jnp.einsum('bhqk,bkhd->bqhd', w, v32)

    out = lax.map(block, (q_blocks, starts))
    return out.swapaxes(0, 1).reshape(b_, s_, ATTN_HEADS, ATTN_V_DIM)


def mlstm(q, k, v, i_pre, f_pre):
    b_, s_, h_, dk = q.shape
    dv = v.shape[-1]
    nc = s_ // CHUNK

    def chunks(t):
        return t.astype(jnp.float32).swapaxes(1, 2).reshape(b_, h_, nc, CHUNK, *t.shape[3:])

    q = chunks(q)
    k = chunks(k) * (dk ** -0.5)
    v = chunks(v)
    log_i = chunks(i_pre)
    log_f = jax.nn.log_sigmoid(chunks(f_pre))
    b = jnp.cumsum(log_f, axis=-1)
    g = b[..., -1]
    causal = jnp.tril(jnp.ones((CHUNK, CHUNK), dtype=bool))
    d = jnp.where(causal, b[..., :, None] - b[..., None, :] + log_i[..., None, :], -jnp.inf)
    a = g[..., None] - b + log_i
    m_loc = jnp.max(a, axis=-1)
    wa = jnp.exp(a - m_loc[..., None])
    c_loc = jnp.einsum('bhcsk,bhcsv->bhckv', wa[..., None] * k, v)
    n_loc = jnp.einsum('bhcs,bhcsk->bhck', wa, k)

    def step(carry, inp):
        c, n, m = carry
        c_l, n_l, m_l, g_c = inp
        m_new = jnp.maximum(g_c + m, m_l)
        a_old = jnp.exp(g_c + m - m_new)
        a_new = jnp.exp(m_l - m_new)
        c_new = a_old[..., None, None] * c + a_new[..., None, None] * c_l
        n_new = a_old[..., None] * n + a_new[..., None] * n_l
        return (c_new, n_new, m_new), (c, n, m)

    init = (jnp.zeros((b_, h_, dk, dv), jnp.float32), jnp.zeros((b_, h_, dk), jnp.float32), jnp.zeros((b_, h_), jnp.float32))
    xs = (jnp.moveaxis(c_loc, 2, 0), jnp.moveaxis(n_loc, 2, 0), jnp.moveaxis(m_loc, 2, 0), jnp.moveaxis(g, 2, 0))
    _, (c_prev, n_prev, m_prev) = lax.scan(step, init, xs)
    c_prev = jnp.moveaxis(c_prev, 0, 2)
    n_prev = jnp.moveaxis(n_prev, 0, 2)
    m_prev = jnp.moveaxis(m_prev, 0, 2)

    m_inter = b + m_prev[..., None]
    m_j = jnp.maximum(m_inter, jnp.max(d, axis=-1))
    w_inter = jnp.exp(m_inter - m_j)
    qk = jnp.einsum('bhcjd,bhcsd->bhcjs', q, k) * jnp.exp(d - m_j[..., None])
    num = w_inter[..., None] * jnp.einsum('bhcjk,bhckv->bhcjv', q, c_prev) + jnp.einsum('bhcjs,bhcsv->bhcjv', qk, v)
    den = w_inter * jnp.einsum('bhcjk,bhck->bhcj', q, n_prev) + jnp.sum(qk, axis=-1)
    h = num / jnp.maximum(jnp.abs(den), jnp.exp(-m_j))[..., None]
    return h.reshape(b_, h_, s_, dv).swapaxes(1, 2)


def setup_inputs(seed: int = 0) -> dict:
    key = jax.random.key(seed)
    ks = iter(jax.random.split(key, 40))
    f32 = jnp.float32

    def normal(shape, scale):
        return jax.random.normal(next(ks), shape, f32) * scale

    def gain(shape):
        return 1.0 + normal(shape, 0.02)

    L = DEPTH
    return {
        "x": normal((BATCH, SEQ, D_MODEL), 1.0),
        "mem": normal((BATCH, N_MEM, D_MODEL), 1.0),
        "norm_mix": gain((L, D_MODEL)),
        "w_in": normal((L, D_MODEL, IN_WIDTH), D_MODEL ** -0.5),
        "b_igate": normal((L, MLSTM_HEADS), 0.1),
        "b_fgate": jnp.broadcast_to(jnp.linspace(3.0, 6.0, MLSTM_HEADS, dtype=f32), (L, MLSTM_HEADS)) + normal((L, MLSTM_HEADS), 0.1),
        "conv_w": normal((L, CONV_WIDTH, 2 * MLSTM_QK_WIDTH), CONV_WIDTH ** -0.5),
        "conv_b": normal((L, 2 * MLSTM_QK_WIDTH), 0.02),
        "lam_q1": normal((L, ATTN_QK_DIM), 0.1),
        "lam_k1": normal((L, ATTN_QK_DIM), 0.1),
        "lam_q2": normal((L, ATTN_QK_DIM), 0.1),
        "lam_k2": normal((L, ATTN_QK_DIM), 0.1),
        "attn_norm": gain((L, ATTN_V_DIM)),
        "mlstm_norm": gain((L, MLSTM_WIDTH)),
        "w_attn_br": normal((L, ATTN_WIDTH, D_MODEL), ATTN_WIDTH ** -0.5),
        "w_mlstm_br": normal((L, MLSTM_WIDTH, D_MODEL), MLSTM_WIDTH ** -0.5),
        "w_out": normal((L, D_MODEL, D_MODEL), D_MODEL ** -0.5),
        "norm_cross": gain((L, D_MODEL)),
        "norm_mem": gain((L, D_MODEL)),
        "w_cq": normal((L, D_MODEL, CROSS_WIDTH), D_MODEL ** -0.5),
        "w_ckv": normal((L, D_MODEL, 2 * CROSS_WIDTH), D_MODEL ** -0.5),
        "w_co": normal((L, CROSS_WIDTH, D_MODEL), CROSS_WIDTH ** -0.5),
        "norm_mlp": gain((L, D_MODEL)),
        "w_up": normal((L, D_MODEL, D_FF), D_MODEL ** -0.5),
        "w_down": normal((L, D_FF, D_MODEL), D_FF ** -0.5),
        "norm_final": gain((D_MODEL,)),
    }


def reference(x, mem, norm_mix, w_in, b_igate, b_fgate, conv_w, conv_b, lam_q1, lam_k1, lam_q2, lam_k2, attn_norm, mlstm_norm, w_attn_br, w_mlstm_br, w_out, norm_cross, norm_mem, w_cq, w_ckv, w_co, norm_mlp, w_up, w_down, norm_final):
    b_, s_, _ = x.shape
    n_mem = mem.shape[1]
    slopes = jnp.exp2(-8.0 * jnp.arange(1, ATTN_HEADS + 1, dtype=jnp.float32) / ATTN_HEADS)
    for l in range(DEPTH):
        h = rms_norm(x, norm_mix[l])
        z = h @ w_in[l]
        aq, ak, av, mqk, mv, mo, mi, mf, ga, gm = split_cols(z)

        lam_init = 0.8 - 0.6 * math.exp(-0.3 * l)
        lam = (jnp.exp(jnp.dot(lam_q1[l], lam_k1[l]).astype(jnp.float32))
               - jnp.exp(jnp.dot(lam_q2[l], lam_k2[l]).astype(jnp.float32)) + lam_init)
        att = diff_attention(aq.reshape(b_, s_, ATTN_HEADS, 2, ATTN_QK_DIM),
                             ak.reshape(b_, s_, ATTN_HEADS, 2, ATTN_QK_DIM),
                             av.reshape(b_, s_, ATTN_HEADS, ATTN_V_DIM), lam, slopes)
        att = rms_norm(att, attn_norm[l]) * (1.0 - lam_init)
        att = att.reshape(b_, s_, ATTN_WIDTH).astype(x.dtype)

        mqk = jax.nn.silu(causal_conv(mqk, conv_w[l], conv_b[l]))
        mq, mk = jnp.split(mqk, 2, axis=-1)
        hm = mlstm(mq.reshape(b_, s_, MLSTM_HEADS, MLSTM_QK_DIM),
                   mk.reshape(b_, s_, MLSTM_HEADS, MLSTM_QK_DIM),
                   mv.reshape(b_, s_, MLSTM_HEADS, MLSTM_V_DIM),
                   mi + b_igate[l], mf + b_fgate[l])
        hm = rms_norm(hm, mlstm_norm[l].reshape(MLSTM_HEADS, MLSTM_V_DIM))
        hm = (hm.reshape(b_, s_, MLSTM_WIDTH) * jax.nn.sigmoid(mo.astype(jnp.float32))).astype(x.dtype)

        y = jax.nn.sigmoid(ga) * (att @ w_attn_br[l]) + jax.nn.sigmoid(gm) * (hm @ w_mlstm_br[l])
        x = x + y @ w_out[l]

        hc = rms_norm(x, norm_cross[l])
        mn = rms_norm(mem, norm_mem[l])
        cq = (hc @ w_cq[l]).reshape(b_, s_, CROSS_HEADS, CROSS_HEAD_DIM)
        ck, cv = jnp.split(mn @ w_ckv[l], 2, axis=-1)
        ck = ck.reshape(b_, n_mem, CROSS_HEADS, CROSS_HEAD_DIM)
        cv = cv.reshape(b_, n_mem, CROSS_HEADS, CROSS_HEAD_DIM)
        cs = jnp.einsum('bqhd,bkhd->bhqk', cq, ck).astype(jnp.float32) * (CROSS_HEAD_DIM ** -0.5)
        cp = jax.nn.softmax(cs, axis=-1)
        co = jnp.einsum('bhqk,bkhd->bqhd', cp, cv.astype(jnp.float32)).reshape(b_, s_, CROSS_WIDTH).astype(x.dtype)
        x = x + co @ w_co[l]

        hf = rms_norm(x, norm_mlp[l])
        x = x + jnp.square(jax.nn.relu(hf @ w_up[l])) @ w_down[l]
    return rms_norm(x, norm_final)
```

```python
import functools
import math

import jax
import jax.numpy as jnp
from jax import lax
from jax.experimental import pallas as pl
from jax.experimental.pallas import tpu as pltpu

F32 = jnp.float32
BF16 = jnp.bfloat16

D_MODEL = 2048
DEPTH = 1
ATTN_HEADS = 8
ATTN_QK_DIM = 64
ATTN_V_DIM = 128
ATTN_WIDTH = ATTN_HEADS * ATTN_V_DIM
MLSTM_HEADS = 4
MLSTM_QK_DIM = 128
MLSTM_V_DIM = 256
MLSTM_QK_WIDTH = MLSTM_HEADS * MLSTM_QK_DIM
MLSTM_WIDTH = MLSTM_HEADS * MLSTM_V_DIM
CONV_WIDTH = 4
CROSS_HEADS = 4
CROSS_HEAD_DIM = 128
CROSS_WIDTH = CROSS_HEADS * CROSS_HEAD_DIM
D_FF = 4 * D_MODEL
EPS = 1e-6
LANES = 128
SUBLANES = 8
NEG_BIG = -1e30

COL_AQ = 0
COL_AK = 1024
COL_AV = 2048
COL_MQ = 3072
COL_MK = 3584
COL_MV = 4096
COL_MO = 5120
COL_GA = 6144
COL_GM = 8192
IN_MAIN = 10240
GATE_LO = 6144
GATE_HI = 6152

VMEM_LIMIT = 56 * 1024 * 1024


def _rms(x, g):
    ms = jnp.mean(x * x, axis=-1, keepdims=True)
    return x * lax.rsqrt(ms + EPS) * g


def _sigmoid(x):
    return 1.0 / (1.0 + jnp.exp(-x))


def _log_sigmoid(x):
    return jnp.minimum(x, 0.0) - jnp.log(1.0 + jnp.exp(-jnp.abs(x)))


def _dot(a, b):
    return jnp.dot(a, b, preferred_element_type=F32)


def _dot_nt(a, b):
    return lax.dot_general(a, b, (((1,), (1,)), ((), ())), preferred_element_type=F32)


def _memkv_kernel(mem_ref, g_ref, w_ref, o_ref):
    mn = _rms(mem_ref[0], g_ref[...]).astype(BF16)
    o_ref[0] = _dot(mn, w_ref[...]).astype(BF16)


def _memkv(mem, g, w_ckv):
    b, n_mem, _ = mem.shape
    return pl.pallas_call(
        _memkv_kernel,
        grid=(b,),
        in_specs=[
            pl.BlockSpec((1, n_mem, D_MODEL), lambda i: (i, 0, 0)),
            pl.BlockSpec((1, D_MODEL), lambda i: (0, 0)),
            pl.BlockSpec((D_MODEL, 2 * CROSS_WIDTH), lambda i: (0, 0)),
        ],
        out_specs=pl.BlockSpec((1, n_mem, 2 * CROSS_WIDTH), lambda i: (i, 0, 0)),
        out_shape=jax.ShapeDtypeStruct((b, n_mem, 2 * CROSS_WIDTH), BF16),
        compiler_params=pltpu.CompilerParams(
            dimension_semantics=("arbitrary",), vmem_limit_bytes=VMEM_LIMIT),
        name="memkv",
    )(mem, g, w_ckv)


def _inproj_kernel(x_ref, g_ref, w_ref, wg_ref, z_ref, gate_ref, h_scr):
    @pl.when(pl.program_id(1) == 0)
    def _():
        h = _rms(x_ref[...], g_ref[...]).astype(BF16)
        h_scr[...] = h
        gate_ref[...] = _dot(h, wg_ref[...])

    z_ref[...] = _dot(h_scr[...], w_ref[...]).astype(BF16)


def _inproj(x2d, g, w_main, w_gate, tm=512, tn=1024):
    t = x2d.shape[0]
    n = w_main.shape[1]
    return pl.pallas_call(
        _inproj_kernel,
        grid=(t // tm, n // tn),
        in_specs=[
            pl.BlockSpec((tm, D_MODEL), lambda i, j: (i, 0)),
            pl.BlockSpec((1, D_MODEL), lambda i, j: (0, 0)),
            pl.BlockSpec((D_MODEL, tn), lambda i, j: (0, j)),
            pl.BlockSpec((D_MODEL, LANES), lambda i, j: (0, 0)),
        ],
        out_specs=[
            pl.BlockSpec((tm, tn), lambda i, j: (i, j)),
            pl.BlockSpec((tm, LANES), lambda i, j: (i, 0)),
        ],
        out_shape=[
            jax.ShapeDtypeStruct((t, n), BF16),
            jax.ShapeDtypeStruct((t, LANES), F32),
        ],
        scratch_shapes=[pltpu.VMEM((tm, D_MODEL), BF16)],
        compiler_params=pltpu.CompilerParams(
            dimension_semantics=("parallel", "arbitrary"), vmem_limit_bytes=VMEM_LIMIT),
        name="inproj",
    )(x2d, g, w_main, w_gate)


def _attn_kernel(slope_ref, lam_ref, q_ref, k_ref, vt_ref, gain_ref, o_ref, *, tq, lam_init):
    tk = tq
    h = pl.program_id(1)
    i = pl.program_id(2)
    slope = slope_ref[h]

    lv = lam_ref[...]
    d1 = jnp.sum(lv[0:1] * lv[1:2], axis=-1, keepdims=True)
    d2 = jnp.sum(lv[2:3] * lv[3:4], axis=-1, keepdims=True)
    lam = jnp.exp(d1) - jnp.exp(d2) + lam_init

    q = q_ref[0] * jnp.asarray(ATTN_QK_DIM ** -0.5, BF16)
    lane = lax.broadcasted_iota(jnp.int32, (tq, 2 * ATTN_QK_DIM), 1)
    zero = jnp.zeros_like(q)
    qs = jnp.concatenate(
        [jnp.where(lane < ATTN_QK_DIM, q, zero), jnp.where(lane >= ATTN_QK_DIM, q, zero)],
        axis=0)

    krow = lax.broadcasted_iota(jnp.int32, (tk, 2 * tq), 0)
    qcol = lax.broadcasted_iota(jnp.int32, (tk, 2 * tq), 1)
    qcol = jnp.where(qcol >= tq, qcol - tq, qcol)
    sbias = slope * (krow - qcol).astype(F32)

    def step(j, carry, diagonal):
        m, l, acc = carry
        k = k_ref[0, pl.ds(pl.multiple_of(j * tk, tk), tk), :]
        t = _dot_nt(k, qs) + sbias
        if diagonal:
            t = jnp.where(krow <= qcol, t, NEG_BIG)
        soff = slope * ((j - i) * tk).astype(F32)
        m_new = jnp.maximum(m, jnp.max(t, axis=0, keepdims=True) + soff)
        alpha = jnp.exp(m - m_new)
        p = jnp.exp(t - (m_new - soff))
        l = alpha * l + jnp.sum(p, axis=0, keepdims=True)
        acc = alpha * acc + _dot(vt_ref[0, 0, j], p.astype(BF16))
        return m_new, l, acc

    init = (jnp.full((1, 2 * tq), NEG_BIG, F32),
            jnp.zeros((1, 2 * tq), F32),
            jnp.zeros((ATTN_V_DIM, 2 * tq), F32))
    carry = lax.fori_loop(0, i, lambda j, c: step(j, c, False), init)
    _, l, acc = step(i, carry, True)

    out = acc[:, :tq] / l[:, :tq] - lam * (acc[:, tq:] / l[:, tq:])
    ms = jnp.mean(out * out, axis=0, keepdims=True)
    on = out * lax.rsqrt(ms + EPS)
    o_ref[0] = (on.T * gain_ref[...] * (1.0 - lam_init)).astype(BF16)


def _attn(z3, vt5, slopes, lamv, gain, lam_init, tq=256):
    b, s, _ = z3.shape
    nq = s // tq
    kern = functools.partial(_attn_kernel, tq=tq, lam_init=lam_init)
    return pl.pallas_call(
        kern,
        grid=(b, ATTN_HEADS, nq),
        in_specs=[
            pl.BlockSpec(memory_space=pltpu.SMEM),
            pl.BlockSpec((SUBLANES, LANES), lambda bi, h, i: (0, 0)),
            pl.BlockSpec((1, tq, LANES), lambda bi, h, i: (bi, i, COL_AQ // LANES + h)),
            pl.BlockSpec((1, s, LANES), lambda bi, h, i: (bi, 0, COL_AK // LANES + h)),
            pl.BlockSpec((1, 1, nq, ATTN_V_DIM, tq), lambda bi, h, i: (bi, h, 0, 0, 0)),
            pl.BlockSpec((1, ATTN_V_DIM), lambda bi, h, i: (0, 0)),
        ],
        out_specs=pl.BlockSpec((1, tq, ATTN_V_DIM), lambda bi, h, i: (bi, i, h)),
        out_shape=jax.ShapeDtypeStruct((b, s, ATTN_WIDTH), BF16),
        compiler_params=pltpu.CompilerParams(
            dimension_semantics=("parallel", "parallel", "arbitrary"),
            vmem_limit_bytes=VMEM_LIMIT),
        name="attn",
    )(slopes, lamv, z3, z3, vt5, gain)


def _split3(x):
    hi = x.astype(BF16)
    r = x - hi.astype(F32)
    mid = r.astype(BF16)
    lo = (r - mid.astype(F32)).astype(BF16)
    return hi, mid, lo


def _mlstm_kernel(bias_ref, uq_ref, uk_ref, v_ref, mo_ref, gcol_ref, grow_ref,
                  cwq_ref, cwk_ref, cbq_ref, cbk_ref, gn_ref, o_ref,
                  extq, extk, c_scr, n_scr, m_scr, *, chunk):
    L = chunk
    h = pl.program_id(1)

    @pl.when(pl.program_id(2) == 0)
    def _():
        extq[pl.ds(0, SUBLANES), :] = jnp.zeros((SUBLANES, MLSTM_QK_DIM), F32)
        extk[pl.ds(0, SUBLANES), :] = jnp.zeros((SUBLANES, MLSTM_QK_DIM), F32)
        c_scr[...] = jnp.zeros_like(c_scr)
        n_scr[...] = jnp.zeros_like(n_scr)
        m_scr[...] = jnp.zeros_like(m_scr)

    def conv_silu(u_ref, ext, cw_ref, cb_ref):
        ext[pl.ds(SUBLANES, L), :] = u_ref[0].astype(F32)
        w = cw_ref[...]
        y = cb_ref[...]
        for tap in range(CONV_WIDTH):
            off = SUBLANES - (CONV_WIDTH - 1) + tap
            y = y + ext[pl.ds(off, L), :] * w[tap:tap + 1]
        ext[pl.ds(0, SUBLANES), :] = ext[pl.ds(L, SUBLANES), :]
        return y * _sigmoid(y)

    q = conv_silu(uq_ref, extq, cwq_ref, cbq_ref)
    k = conv_silu(uk_ref, extk, cwk_ref, cbk_ref) * (MLSTM_QK_DIM ** -0.5)
    qb = q.astype(BF16)
    kb = k.astype(BF16)
    vb = v_ref[0]

    b_i = bias_ref[h]
    b_f = bias_ref[MLSTM_HEADS + h]

    g_all = gcol_ref[...]
    lane = lax.broadcasted_iota(jnp.int32, (L, LANES), 1)
    i_col = jnp.sum(jnp.where(lane == h, g_all, 0.0), axis=1, keepdims=True) + b_i
    f_col = jnp.sum(jnp.where(lane == MLSTM_HEADS + h, g_all, 0.0), axis=1, keepdims=True) + b_f
    lf_col = _log_sigmoid(f_col)
    i_row = grow_ref[0, pl.ds(h, 1), :] + b_i
    f_row = grow_ref[0, pl.ds(MLSTM_HEADS + h, 1), :] + b_f
    lf_row = _log_sigmoid(f_row)

    r_i = lax.broadcasted_iota(jnp.int32, (L, L), 0)
    c_i = lax.broadcasted_iota(jnp.int32, (L, L), 1)
    causal = c_i <= r_i
    tri = jnp.where(causal, 1.0, 0.0).astype(BF16)
    tri_t = jnp.where(r_i <= c_i, 1.0, 0.0).astype(BF16)
    hi, mid, lo = _split3(jnp.broadcast_to(lf_col, (L, LANES)))
    b_col = (_dot(tri, hi) + _dot(tri, mid) + _dot(tri, lo))[:, 0:1]
    hi, mid, lo = _split3(jnp.broadcast_to(lf_row, (SUBLANES, L)))
    b_row = (_dot(hi, tri_t) + _dot(mid, tri_t) + _dot(lo, tri_t))[0:1, :]
    g_tot = jnp.sum(lf_row, axis=1, keepdims=True)

    c_prev = c_scr[...]
    n_prev = n_scr[...]
    m_prev = m_scr[:, 0:1]

    d = jnp.where(causal, b_col - b_row + i_row, NEG_BIG)
    m_inter = b_col + m_prev
    m_j = jnp.maximum(m_inter, jnp.max(d, axis=1, keepdims=True))
    w_inter = jnp.exp(m_inter - m_j)
    p = _dot_nt(qb, kb) * jnp.exp(d - m_j)
    num = w_inter * _dot(qb, c_prev.astype(BF16)) + _dot(p.astype(BF16), vb)
    den = (w_inter * jnp.sum(q * n_prev, axis=1, keepdims=True)
           + jnp.sum(p, axis=1, keepdims=True))
    hval = num / jnp.maximum(jnp.abs(den), jnp.exp(-m_j))
    hn = _rms(hval, gn_ref[...])
    o_ref[0] = (hn * _sigmoid(mo_ref[0].astype(F32))).astype(BF16)

    a_col = g_tot - b_col + i_col
    m_loc = jnp.max(g_tot - b_row + i_row, axis=1, keepdims=True)
    kw = k * jnp.exp(a_col - m_loc)
    n_loc = jnp.sum(kw, axis=0, keepdims=True)
    c_loc = _dot(kw.T.astype(BF16), vb)
    m_new = jnp.maximum(g_tot + m_prev, m_loc)
    a_old = jnp.exp(g_tot + m_prev - m_new)
    a_new = jnp.exp(m_loc - m_new)
    c_scr[...] = a_old * c_prev + a_new * c_loc
    n_scr[...] = a_old * n_prev + a_new * n_loc
    m_scr[...] = jnp.broadcast_to(m_new, m_scr.shape)


def _mlstm(z3, gates, gates_t, gate_bias, conv_w8, conv_b, gnorm, chunk=256):
    b, s, _ = z3.shape
    nc = s // chunk
    kern = functools.partial(_mlstm_kernel, chunk=chunk)
    qk = MLSTM_QK_DIM
    dv = MLSTM_V_DIM
    return pl.pallas_call(
        kern,
        grid=(b, MLSTM_HEADS, nc),
        in_specs=[
            pl.BlockSpec(memory_space=pltpu.SMEM),
            pl.BlockSpec((1, chunk, qk), lambda bi, h, c: (bi, c, COL_MQ // qk + h)),
            pl.BlockSpec((1, chunk, qk), lambda bi, h, c: (bi, c, COL_MK // qk + h)),
            pl.BlockSpec((1, chunk, dv), lambda bi, h, c: (bi, c, COL_MV // dv + h)),
            pl.BlockSpec((1, chunk, dv), lambda bi, h, c: (bi, c, COL_MO // dv + h)),
            pl.BlockSpec((chunk, LANES), lambda bi, h, c: (bi * nc + c, 0)),
            pl.BlockSpec((1, SUBLANES, chunk), lambda bi, h, c: (bi, 0, c)),
            pl.BlockSpec((SUBLANES, qk), lambda bi, h, c: (0, h)),
            pl.BlockSpec((SUBLANES, qk), lambda bi, h, c: (0, MLSTM_HEADS + h)),
            pl.BlockSpec((1, qk), lambda bi, h, c: (0, h)),
            pl.BlockSpec((1, qk), lambda bi, h, c: (0, MLSTM_HEADS + h)),
            pl.BlockSpec((1, dv), lambda bi, h, c: (0, h)),
        ],
        out_specs=pl.BlockSpec((1, chunk, dv), lambda bi, h, c: (bi, c, h)),
        out_shape=jax.ShapeDtypeStruct((b, s, MLSTM_WIDTH), BF16),
        scratch_shapes=[
            pltpu.VMEM((chunk + 2 * SUBLANES, qk), F32),
            pltpu.VMEM((chunk + 2 * SUBLANES, qk), F32),
            pltpu.VMEM((qk, dv), F32),
            pltpu.VMEM((1, qk), F32),
            pltpu.VMEM((1, LANES), F32),
        ],
        compiler_params=pltpu.CompilerParams(
            dimension_semantics=("parallel", "parallel", "arbitrary"),
            vmem_limit_bytes=VMEM_LIMIT),
        name="mlstm",
    )(gate_bias, z3, z3, z3, z3, gates, gates_t, conv_w8, conv_w8, conv_b, conv_b, gnorm)


def _merge_kernel(att_ref, hm_ref, ga_ref, gm_ref, x_ref, wa_ref, wm_ref, wo_ref, o_ref):
    @pl.when(pl.program_id(1) == 0)
    def _():
        o_ref[...] = x_ref[...]

    a = _dot(att_ref[...], wa_ref[...])
    bm = _dot(hm_ref[...], wm_ref[...])
    y = _sigmoid(ga_ref[...].astype(F32)) * a + _sigmoid(gm_ref[...].astype(F32)) * bm
    o_ref[...] += _dot(y.astype(BF16), wo_ref[...])


def _merge(att2d, hm2d, z2d, x2d, wa, wm, wo, tm=512, tn=512):
    t = x2d.shape[0]
    nj = D_MODEL // tn
    return pl.pallas_call(
        _merge_kernel,
        grid=(t // tm, nj),
        in_specs=[
            pl.BlockSpec((tm, ATTN_WIDTH), lambda i, j: (i, 0)),
            pl.BlockSpec((tm, MLSTM_WIDTH), lambda i, j: (i, 0)),
            pl.BlockSpec((tm, tn), lambda i, j: (i, COL_GA // tn + j)),
            pl.BlockSpec((tm, tn), lambda i, j: (i, COL_GM // tn + j)),
            pl.BlockSpec((tm, D_MODEL), lambda i, j: (i, 0)),
            pl.BlockSpec((ATTN_WIDTH, tn), lambda i, j: (0, j)),
            pl.BlockSpec((MLSTM_WIDTH, tn), lambda i, j: (0, j)),
            pl.BlockSpec((tn, D_MODEL), lambda i, j: (j, 0)),
        ],
        out_specs=pl.BlockSpec((tm, D_MODEL), lambda i, j: (i, 0)),
        out_shape=jax.ShapeDtypeStruct((t, D_MODEL), F32),
        compiler_params=pltpu.CompilerParams(
            dimension_semantics=("parallel", "arbitrary"), vmem_limit_bytes=VMEM_LIMIT),
        name="merge",
    )(att2d, hm2d, z2d, z2d, x2d, wa, wm, wo)


def _cross_kernel(x_ref, g_ref, wq_ref, kv_ref, wo_ref, o_ref):
    x = x_ref[...]
    hc = _rms(x, g_ref[...]).astype(BF16)
    cq = (_dot(hc, wq_ref[...]) * (CROSS_HEAD_DIM ** -0.5)).astype(BF16)
    outs = []
    for hh in range(CROSS_HEADS):
        lo = hh * CROSS_HEAD_DIM
        qh = cq[:, lo:lo + CROSS_HEAD_DIM]
        kh = kv_ref[0, :, lo:lo + CROSS_HEAD_DIM]
        vh = kv_ref[0, :, CROSS_WIDTH + lo:CROSS_WIDTH + lo + CROSS_HEAD_DIM]
        s = _dot_nt(qh, kh)
        p = jnp.exp(s - jnp.max(s, axis=-1, keepdims=True))
        l = jnp.sum(p, axis=-1, keepdims=True)
        outs.append((_dot(p.astype(BF16), vh) / l).astype(BF16))
    co = jnp.concatenate(outs, axis=1)
    o_ref[...] = x + _dot(co, wo_ref[...])


def _cross(x2d, g, wq, ckv, wo, seq, tm=512):
    t = x2d.shape[0]
    n_mem = ckv.shape[1]
    per_batch = seq // tm
    return pl.pallas_call(
        _cross_kernel,
        grid=(t // tm,),
        in_specs=[
            pl.BlockSpec((tm, D_MODEL), lambda i: (i, 0)),
            pl.BlockSpec((1, D_MODEL), lambda i: (0, 0)),
            pl.BlockSpec((D_MODEL, CROSS_WIDTH), lambda i: (0, 0)),
            pl.BlockSpec((1, n_mem, 2 * CROSS_WIDTH), lambda i: (i // per_batch, 0, 0)),
            pl.BlockSpec((CROSS_WIDTH, D_MODEL), lambda i: (0, 0)),
        ],
        out_specs=pl.BlockSpec((tm, D_MODEL), lambda i: (i, 0)),
        out_shape=jax.ShapeDtypeStruct((t, D_MODEL), F32),
        compiler_params=pltpu.CompilerParams(
            dimension_semantics=("parallel",), vmem_limit_bytes=VMEM_LIMIT),
        name="cross",
    )(x2d, g, wq, ckv, wo)


def _mlp_kernel(x_ref, g_ref, wu_ref, wd_ref, gf_ref, o_ref, h_scr, acc, *, final_norm):
    j = pl.program_id(1)

    @pl.when(j == 0)
    def _():
        x = x_ref[...]
        h_scr[...] = _rms(x, g_ref[...]).astype(BF16)
        acc[...] = x

    u = jnp.square(jnp.maximum(_dot(h_scr[...], wu_ref[...]), 0.0)).astype(BF16)
    acc[...] += _dot(u, wd_ref[...])

    @pl.when(j == pl.num_programs(1) - 1)
    def _():
        if final_norm:
            o_ref[...] = _rms(acc[...], gf_ref[...])
        else:
            o_ref[...] = acc[...]


def _mlp(x2d, g, wu, wd, gf, final_norm, tm=512, tf=512):
    t = x2d.shape[0]
    kern = functools.partial(_mlp_kernel, final_norm=final_norm)
    return pl.pallas_call(
        kern,
        grid=(t // tm, D_FF // tf),
        in_specs=[
            pl.BlockSpec((tm, D_MODEL), lambda i, j: (i, 0)),
            pl.BlockSpec((1, D_MODEL), lambda i, j: (0, 0)),
            pl.BlockSpec((D_MODEL, tf), lambda i, j: (0, j)),
            pl.BlockSpec((tf, D_MODEL), lambda i, j: (j, 0)),
            pl.BlockSpec((1, D_MODEL), lambda i, j: (0, 0)),
        ],
        out_specs=pl.BlockSpec((tm, D_MODEL), lambda i, j: (i, 0)),
        out_shape=jax.ShapeDtypeStruct((t, D_MODEL), F32),
        scratch_shapes=[pltpu.VMEM((tm, D_MODEL), BF16), pltpu.VMEM((tm, D_MODEL), F32)],
        compiler_params=pltpu.CompilerParams(
            dimension_semantics=("parallel", "arbitrary"), vmem_limit_bytes=VMEM_LIMIT),
        name="mlp",
    )(x2d, g, wu, wd, gf)


def _pad_rows(a, rows):
    return jnp.pad(a, ((0, rows - a.shape[0]), (0, 0)))


def kernel(x, mem, norm_mix, w_in, b_igate, b_fgate, conv_w, conv_b, lam_q1, lam_k1, lam_q2, lam_k2, attn_norm, mlstm_norm, w_attn_br, w_mlstm_br, w_out, norm_cross, norm_mem, w_cq, w_ckv, w_co, norm_mlp, w_up, w_down, norm_final):
    b, s, _ = x.shape
    t = b * s
    tq = 256
    slopes = jnp.exp2(-8.0 * jnp.arange(1, ATTN_HEADS + 1, dtype=F32) / ATTN_HEADS)
    x2d = x.reshape(t, D_MODEL)
    for l in range(DEPTH):
        lam_init = 0.8 - 0.6 * math.exp(-0.3 * l)
        w = w_in[l]
        w_main = jnp.concatenate([w[:, :GATE_LO], w[:, GATE_HI:]], axis=1).astype(BF16)
        w_gate = jnp.pad(w[:, GATE_LO:GATE_HI], ((0, 0), (0, LANES - (GATE_HI - GATE_LO)))).astype(BF16)

        z2d, gates = _inproj(x2d, norm_mix[l][None, :], w_main, w_gate)
        z3 = z2d.reshape(b, s, IN_MAIN)

        vt5 = (z3[:, :, COL_AV:COL_AV + ATTN_WIDTH]
               .reshape(b, s // tq, tq, ATTN_HEADS, ATTN_V_DIM).transpose(0, 3, 1, 4, 2))
        lamv = _pad_rows(jnp.pad(jnp.stack([lam_q1[l], lam_k1[l], lam_q2[l], lam_k2[l]]),
                                 ((0, 0), (0, LANES - ATTN_QK_DIM))), SUBLANES)
        att = _attn(z3, vt5, slopes, lamv, attn_norm[l][None, :], lam_init, tq=tq)

        gates_t = gates[:, :SUBLANES].reshape(b, s, SUBLANES).transpose(0, 2, 1)
        gate_bias = jnp.concatenate([b_igate[l], b_fgate[l]])
        hm = _mlstm(z3, gates, gates_t, gate_bias, _pad_rows(conv_w[l], SUBLANES),
                    conv_b[l][None, :], mlstm_norm[l][None, :])

        x2d = _merge(att.reshape(t, ATTN_WIDTH), hm.reshape(t, MLSTM_WIDTH), z2d, x2d,
                     w_attn_br[l].astype(BF16), w_mlstm_br[l].astype(BF16), w_out[l].astype(BF16))

        ckv = _memkv(mem, norm_mem[l][None, :], w_ckv[l].astype(BF16))
        x2d = _cross(x2d, norm_cross[l][None, :], w_cq[l].astype(BF16), ckv,
                     w_co[l].astype(BF16), s)

        x2d = _mlp(x2d, norm_mlp[l][None, :], w_up[l].astype(BF16), w_down[l].astype(BF16),
                   norm_final[None, :], final_norm=(l == DEPTH - 1))
    return x2d.reshape(b, s, D_MODEL)
```

```python
import functools
import math

import ml_dtypes
import numpy as np
import jax
import jax.numpy as jnp
from jax import lax
from jax.experimental import pallas as pl
from jax.experimental.pallas import tpu as pltpu

F32 = jnp.float32
BF16 = jnp.bfloat16

D_MODEL = 2048
DEPTH = 1
ATTN_HEADS = 8
ATTN_QK_DIM = 64
ATTN_V_DIM = 128
ATTN_WIDTH = ATTN_HEADS * ATTN_V_DIM
MLSTM_HEADS = 4
MLSTM_QK_DIM = 128
MLSTM_V_DIM = 256
MLSTM_QK_WIDTH = MLSTM_HEADS * MLSTM_QK_DIM
MLSTM_WIDTH = MLSTM_HEADS * MLSTM_V_DIM
CONV_WIDTH = 4
CROSS_HEADS = 4
CROSS_HEAD_DIM = 128
CROSS_WIDTH = CROSS_HEADS * CROSS_HEAD_DIM
D_FF = 4 * D_MODEL
EPS = 1e-6
LANES = 128
SUBLANES = 8
NEG_BIG = -1e30
LOG2E = 1.4426950408889634
ALIBI_TERMS = 3

COL_AQ = 0
COL_AK = 1024
COL_AV = 2048
COL_MQ = 3072
COL_MK = 3584
COL_MV = 4096
COL_MO = 5120
COL_GA = 6144
COL_GM = 8192
IN_MAIN = 10240
GATE_LO = 6144
GATE_HI = 6152

VMEM_LIMIT = 56 * 1024 * 1024


def _rms(x, g):
    ms = jnp.mean(x * x, axis=-1, keepdims=True)
    return x * lax.rsqrt(ms + EPS) * g


def _sigmoid(x):
    return 1.0 / (1.0 + jnp.exp(-x))


def _log_sigmoid(x):
    return jnp.minimum(x, 0.0) - jnp.log(1.0 + jnp.exp(-jnp.abs(x)))


def _dot(a, b):
    return jnp.dot(a, b, preferred_element_type=F32)


def _dot_nt(a, b):
    return lax.dot_general(a, b, (((1,), (1,)), ((), ())), preferred_element_type=F32)


def _memkv_kernel(mem_ref, g_ref, w_ref, o_ref):
    mn = _rms(mem_ref[0], g_ref[...]).astype(BF16)
    o_ref[0] = _dot(mn, w_ref[...]).astype(BF16)


def _memkv(mem, g, w_ckv):
    b, n_mem, _ = mem.shape
    return pl.pallas_call(
        _memkv_kernel,
        grid=(b,),
        in_specs=[
            pl.BlockSpec((1, n_mem, D_MODEL), lambda i: (i, 0, 0)),
            pl.BlockSpec((1, D_MODEL), lambda i: (0, 0)),
            pl.BlockSpec((D_MODEL, 2 * CROSS_WIDTH), lambda i: (0, 0)),
        ],
        out_specs=pl.BlockSpec((1, n_mem, 2 * CROSS_WIDTH), lambda i: (i, 0, 0)),
        out_shape=jax.ShapeDtypeStruct((b, n_mem, 2 * CROSS_WIDTH), BF16),
        compiler_params=pltpu.CompilerParams(
            dimension_semantics=("arbitrary",), vmem_limit_bytes=VMEM_LIMIT),
        name="memkv",
    )(mem, g, w_ckv)


def _inproj_kernel(x_ref, g_ref, w_ref, wg_ref, z_ref, gate_ref, h_scr):
    @pl.when(pl.program_id(1) == 0)
    def _():
        h = _rms(x_ref[...], g_ref[...]).astype(BF16)
        h_scr[...] = h
        gate_ref[...] = _dot(h, wg_ref[...])

    z_ref[...] = _dot(h_scr[...], w_ref[...]).astype(BF16)


def _inproj(x2d, g, w_main, w_gate, tm=512, tn=1024):
    t = x2d.shape[0]
    n = w_main.shape[1]
    return pl.pallas_call(
        _inproj_kernel,
        grid=(t // tm, n // tn),
        in_specs=[
            pl.BlockSpec((tm, D_MODEL), lambda i, j: (i, 0)),
            pl.BlockSpec((1, D_MODEL), lambda i, j: (0, 0)),
            pl.BlockSpec((D_MODEL, tn), lambda i, j: (0, j)),
            pl.BlockSpec((D_MODEL, LANES), lambda i, j: (0, 0)),
        ],
        out_specs=[
            pl.BlockSpec((tm, tn), lambda i, j: (i, j)),
            pl.BlockSpec((tm, LANES), lambda i, j: (i, 0)),
        ],
        out_shape=[
            jax.ShapeDtypeStruct((t, n), BF16),
            jax.ShapeDtypeStruct((t, LANES), F32),
        ],
        scratch_shapes=[pltpu.VMEM((tm, D_MODEL), BF16)],
        compiler_params=pltpu.CompilerParams(
            dimension_semantics=("parallel", "arbitrary"), vmem_limit_bytes=VMEM_LIMIT),
        name="inproj",
    )(x2d, g, w_main, w_gate)


def _attn_kernel(cs_ref, lam_ref, q_ref, k_ref, vt_ref, gain_ref, o_ref,
                 qs_ref, kf_ref, acc_ref, m_ref, *, tq, lam_init):
    tk = tq
    nfeat = ALIBI_TERMS
    dq = 2 * ATTN_QK_DIM
    dv = ATTN_V_DIM
    i = pl.program_id(1)
    cs = [[cs_ref[nfeat * h + t] for t in range(nfeat)] for h in range(ATTN_HEADS)]
    cs_tot = [sum(c[1:], c[0]) for c in cs]

    @pl.when(i == 0)
    def _():
        klane = lax.broadcasted_iota(jnp.int32, (tk, LANES), 1)
        krow = lax.broadcasted_iota(jnp.int32, (tk, LANES), 0).astype(F32)
        qlane = lax.broadcasted_iota(jnp.int32, (2 * tq, LANES), 1)
        qrow = lax.broadcasted_iota(jnp.int32, (2 * tq, LANES), 0)
        qrow = jnp.where(qrow >= tq, qrow - tq, qrow).astype(F32)
        for h in range(ATTN_HEADS):
            kf = jnp.where(klane < nfeat, krow, 0.0)
            qf = jnp.where((qlane >= nfeat) & (qlane < 2 * nfeat), -qrow, 0.0)
            for t in range(nfeat):
                kf = jnp.where(klane == nfeat + t, cs[h][t], kf)
                qf = jnp.where(qlane == t, cs[h][t], qf)
            kf_ref[h] = kf.astype(BF16)
            qs_ref[h, :, pl.ds(dq, LANES)] = qf.astype(BF16)

    lane = lax.broadcasted_iota(jnp.int32, (tq, dq), 1)
    for h in range(ATTN_HEADS):
        q = q_ref[0, :, pl.ds(h * dq, dq)]
        zero = jnp.zeros_like(q)
        qs_ref[h, pl.ds(0, tq), pl.ds(0, dq)] = jnp.where(lane < ATTN_QK_DIM, q, zero)
        qs_ref[h, pl.ds(tq, tq), pl.ds(0, dq)] = jnp.where(lane >= ATTN_QK_DIM, q, zero)
    acc_ref[...] = jnp.zeros_like(acc_ref)
    m_ref[...] = jnp.full_like(m_ref, NEG_BIG)

    orow = lax.broadcasted_iota(jnp.int32, (2 * SUBLANES, tk), 0)
    ones_blk = jnp.where(orow == 0, 1.0, 0.0).astype(BF16)

    def scores(h, j):
        k = k_ref[0, pl.ds(pl.multiple_of(j * tk, tk), tk), pl.ds(h * dq, dq)]
        kx = jnp.concatenate([k, kf_ref[h]], axis=1)
        return _dot_nt(kx, qs_ref[h])

    def softmax_pv(h, j, t, diagonal):
        if diagonal:
            krow = lax.broadcasted_iota(jnp.int32, (tk, 2 * tq), 0)
            qcol = lax.broadcasted_iota(jnp.int32, (tk, 2 * tq), 1)
            qcol = jnp.where(qcol >= tq, qcol - tq, qcol)
            t = jnp.where(krow <= qcol, t, NEG_BIG)
        soff = cs_tot[h] * ((j - i) * tk).astype(F32)
        m = m_ref[h]
        m_new = jnp.maximum(m, jnp.max(t, axis=0, keepdims=True) + soff)
        alpha = jnp.exp2(m - m_new)
        p = jnp.exp2(t - (m_new - soff)).astype(BF16)
        vx = jnp.concatenate([vt_ref[0, h, j], ones_blk], axis=0)
        acc_ref[h] = alpha * acc_ref[h] + _dot(vx, p)
        m_ref[h] = m_new

    def run(units):
        t_next = scores(*units[0][:2])
        for n, (h, j, diagonal) in enumerate(units):
            t = t_next
            if n + 1 < len(units):
                t_next = scores(*units[n + 1][:2])
            softmax_pv(h, j, t, diagonal)

    def tile_units(j, diagonal):
        return [(h, j, diagonal) for h in range(ATTN_HEADS)]

    def body(jj, carry):
        run(tile_units(2 * jj, False) + tile_units(2 * jj + 1, False))
        return carry

    lax.fori_loop(0, i // 2, body, 0)

    @pl.when(i % 2 == 0)
    def _():
        run(tile_units(i, True))

    @pl.when(i % 2 == 1)
    def _():
        run(tile_units(i - 1, False) + tile_units(i, True))

    lv = lam_ref[...]
    d1 = jnp.sum(lv[0:1] * lv[1:2], axis=-1, keepdims=True)
    d2 = jnp.sum(lv[2:3] * lv[3:4], axis=-1, keepdims=True)
    lam = jnp.exp(d1) - jnp.exp(d2) + lam_init
    for h in range(ATTN_HEADS):
        l = acc_ref[h, pl.ds(dv, 1), :]
        out = (acc_ref[h, pl.ds(0, dv), pl.ds(0, tq)] / l[:, :tq]
               - lam * (acc_ref[h, pl.ds(0, dv), pl.ds(tq, tq)] / l[:, tq:]))
        ms = jnp.mean(out * out, axis=0, keepdims=True)
        on = out * lax.rsqrt(ms + EPS)
        o_ref[0, :, pl.ds(h * dv, dv)] = (on.T * gain_ref[...] * (1.0 - lam_init)).astype(BF16)


def _attn(z3, vt5, cs_terms, lamv, gain, lam_init, tq=256):
    b, s, _ = z3.shape
    nq = s // tq
    kern = functools.partial(_attn_kernel, tq=tq, lam_init=lam_init)
    width = ATTN_HEADS * 2 * ATTN_QK_DIM
    return pl.pallas_call(
        kern,
        grid=(b, nq),
        in_specs=[
            pl.BlockSpec(memory_space=pltpu.SMEM),
            pl.BlockSpec((SUBLANES, LANES), lambda bi, i: (0, 0)),
            pl.BlockSpec((1, tq, width), lambda bi, i: (bi, i, COL_AQ // width)),
            pl.BlockSpec((1, s, width), lambda bi, i: (bi, 0, COL_AK // width)),
            pl.BlockSpec((1, ATTN_HEADS, nq, ATTN_V_DIM, tq), lambda bi, i: (bi, 0, 0, 0, 0)),
            pl.BlockSpec((1, ATTN_V_DIM), lambda bi, i: (0, 0)),
        ],
        out_specs=pl.BlockSpec((1, tq, ATTN_WIDTH), lambda bi, i: (bi, i, 0)),
        out_shape=jax.ShapeDtypeStruct((b, s, ATTN_WIDTH), BF16),
        scratch_shapes=[
            pltpu.VMEM((ATTN_HEADS, 2 * tq, 2 * ATTN_QK_DIM + LANES), BF16),
            pltpu.VMEM((ATTN_HEADS, tq, LANES), BF16),
            pltpu.VMEM((ATTN_HEADS, ATTN_V_DIM + 2 * SUBLANES, 2 * tq), F32),
            pltpu.VMEM((ATTN_HEADS, 1, 2 * tq), F32),
        ],
        compiler_params=pltpu.CompilerParams(
            dimension_semantics=("parallel", "arbitrary"),
            vmem_limit_bytes=VMEM_LIMIT),
        name="attn",
    )(cs_terms, lamv, z3, z3, vt5, gain)


def _split3(x):
    hi = x.astype(BF16)
    r = x - hi.astype(F32)
    mid = r.astype(BF16)
    lo = (r - mid.astype(F32)).astype(BF16)
    return hi, mid, lo


def _mlstm_kernel(bias_ref, uq_ref, uk_ref, v_ref, mo_ref, gcol_ref, grow_ref,
                  cwq_ref, cwk_ref, cbq_ref, cbk_ref, gn_ref, o_ref,
                  extq, extk, c_scr, n_scr, m_scr, *, chunk):
    L = chunk
    h = pl.program_id(1)

    @pl.when(pl.program_id(2) == 0)
    def _():
        extq[pl.ds(0, SUBLANES), :] = jnp.zeros((SUBLANES, MLSTM_QK_DIM), F32)
        extk[pl.ds(0, SUBLANES), :] = jnp.zeros((SUBLANES, MLSTM_QK_DIM), F32)
        c_scr[...] = jnp.zeros_like(c_scr)
        n_scr[...] = jnp.zeros_like(n_scr)
        m_scr[...] = jnp.zeros_like(m_scr)

    def conv_silu(u_ref, ext, cw_ref, cb_ref):
        ext[pl.ds(SUBLANES, L), :] = u_ref[0].astype(F32)
        w = cw_ref[...]
        y = cb_ref[...]
        for tap in range(CONV_WIDTH):
            off = SUBLANES - (CONV_WIDTH - 1) + tap
            y = y + ext[pl.ds(off, L), :] * w[tap:tap + 1]
        ext[pl.ds(0, SUBLANES), :] = ext[pl.ds(L, SUBLANES), :]
        return y * _sigmoid(y)

    q = conv_silu(uq_ref, extq, cwq_ref, cbq_ref)
    k = conv_silu(uk_ref, extk, cwk_ref, cbk_ref) * (MLSTM_QK_DIM ** -0.5)
    qb = q.astype(BF16)
    kb = k.astype(BF16)
    vb = v_ref[0]

    b_i = bias_ref[h]
    b_f = bias_ref[MLSTM_HEADS + h]

    g_all = gcol_ref[...]
    lane = lax.broadcasted_iota(jnp.int32, (L, LANES), 1)
    i_col = jnp.sum(jnp.where(lane == h, g_all, 0.0), axis=1, keepdims=True) + b_i
    f_col = jnp.sum(jnp.where(lane == MLSTM_HEADS + h, g_all, 0.0), axis=1, keepdims=True) + b_f
    lf_col = _log_sigmoid(f_col)
    i_row = grow_ref[0, pl.ds(h, 1), :] + b_i
    f_row = grow_ref[0, pl.ds(MLSTM_HEADS + h, 1), :] + b_f
    lf_row = _log_sigmoid(f_row)

    r_i = lax.broadcasted_iota(jnp.int32, (L, L), 0)
    c_i = lax.broadcasted_iota(jnp.int32, (L, L), 1)
    causal = c_i <= r_i
    tri = jnp.where(causal, 1.0, 0.0).astype(BF16)
    tri_t = jnp.where(r_i <= c_i, 1.0, 0.0).astype(BF16)
    hi, mid, lo = _split3(jnp.broadcast_to(lf_col, (L, LANES)))
    b_col = (_dot(tri, hi) + _dot(tri, mid) + _dot(tri, lo))[:, 0:1]
    hi, mid, lo = _split3(jnp.broadcast_to(lf_row, (SUBLANES, L)))
    b_row = (_dot(hi, tri_t) + _dot(mid, tri_t) + _dot(lo, tri_t))[0:1, :]
    g_tot = jnp.sum(lf_row, axis=1, keepdims=True)

    c_prev = c_scr[...]
    n_prev = n_scr[...]
    m_prev = m_scr[:, 0:1]

    d = jnp.where(causal, b_col - b_row + i_row, NEG_BIG)
    m_inter = b_col + m_prev
    m_j = jnp.maximum(m_inter, jnp.max(d, axis=1, keepdims=True))
    w_inter = jnp.exp(m_inter - m_j)
    p = _dot_nt(qb, kb) * jnp.exp(d - m_j)
    num = w_inter * _dot(qb, c_prev.astype(BF16)) + _dot(p.astype(BF16), vb)
    den = (w_inter * jnp.sum(q * n_prev, axis=1, keepdims=True)
           + jnp.sum(p, axis=1, keepdims=True))
    hval = num / jnp.maximum(jnp.abs(den), jnp.exp(-m_j))
    hn = _rms(hval, gn_ref[...])
    o_ref[0] = (hn * _sigmoid(mo_ref[0].astype(F32))).astype(BF16)

    a_col = g_tot - b_col + i_col
    m_loc = jnp.max(g_tot - b_row + i_row, axis=1, keepdims=True)
    kw = k * jnp.exp(a_col - m_loc)
    n_loc = jnp.sum(kw, axis=0, keepdims=True)
    c_loc = _dot(kw.T.astype(BF16), vb)
    m_new = jnp.maximum(g_tot + m_prev, m_loc)
    a_old = jnp.exp(g_tot + m_prev - m_new)
    a_new = jnp.exp(m_loc - m_new)
    c_scr[...] = a_old * c_prev + a_new * c_loc
    n_scr[...] = a_old * n_prev + a_new * n_loc
    m_scr[...] = jnp.broadcast_to(m_new, m_scr.shape)


def _mlstm(z3, gates, gates_t, gate_bias, conv_w8, conv_b, gnorm, chunk=256):
    b, s, _ = z3.shape
    nc = s // chunk
    kern = functools.partial(_mlstm_kernel, chunk=chunk)
    qk = MLSTM_QK_DIM
    dv = MLSTM_V_DIM
    return pl.pallas_call(
        kern,
        grid=(b, MLSTM_HEADS, nc),
        in_specs=[
            pl.BlockSpec(memory_space=pltpu.SMEM),
            pl.BlockSpec((1, chunk, qk), lambda bi, h, c: (bi, c, COL_MQ // qk + h)),
            pl.BlockSpec((1, chunk, qk), lambda bi, h, c: (bi, c, COL_MK // qk + h)),
            pl.BlockSpec((1, chunk, dv), lambda bi, h, c: (bi, c, COL_MV // dv + h)),
            pl.BlockSpec((1, chunk, dv), lambda bi, h, c: (bi, c, COL_MO // dv + h)),
            pl.BlockSpec((chunk, LANES), lambda bi, h, c: (bi * nc + c, 0)),
            pl.BlockSpec((1, SUBLANES, chunk), lambda bi, h, c: (bi, 0, c)),
            pl.BlockSpec((SUBLANES, qk), lambda bi, h, c: (0, h)),
            pl.BlockSpec((SUBLANES, qk), lambda bi, h, c: (0, MLSTM_HEADS + h)),
            pl.BlockSpec((1, qk), lambda bi, h, c: (0, h)),
            pl.BlockSpec((1, qk), lambda bi, h, c: (0, MLSTM_HEADS + h)),
            pl.BlockSpec((1, dv), lambda bi, h, c: (0, h)),
        ],
        out_specs=pl.BlockSpec((1, chunk, dv), lambda bi, h, c: (bi, c, h)),
        out_shape=jax.ShapeDtypeStruct((b, s, MLSTM_WIDTH), BF16),
        scratch_shapes=[
            pltpu.VMEM((chunk + 2 * SUBLANES, qk), F32),
            pltpu.VMEM((chunk + 2 * SUBLANES, qk), F32),
            pltpu.VMEM((qk, dv), F32),
            pltpu.VMEM((1, qk), F32),
            pltpu.VMEM((1, LANES), F32),
        ],
        compiler_params=pltpu.CompilerParams(
            dimension_semantics=("parallel", "parallel", "arbitrary"),
            vmem_limit_bytes=VMEM_LIMIT),
        name="mlstm",
    )(gate_bias, z3, z3, z3, z3, gates, gates_t, conv_w8, conv_w8, conv_b, conv_b, gnorm)


def _merge_kernel(att_ref, hm_ref, ga_ref, gm_ref, x_ref, wa_ref, wm_ref, wo_ref, o_ref):
    @pl.when(pl.program_id(1) == 0)
    def _():
        o_ref[...] = x_ref[...]

    a = _dot(att_ref[...], wa_ref[...])
    bm = _dot(hm_ref[...], wm_ref[...])
    y = _sigmoid(ga_ref[...].astype(F32)) * a + _sigmoid(gm_ref[...].astype(F32)) * bm
    o_ref[...] += _dot(y.astype(BF16), wo_ref[...])


def _merge(att2d, hm2d, z2d, x2d, wa, wm, wo, tm=512, tn=512):
    t = x2d.shape[0]
    nj = D_MODEL // tn
    return pl.pallas_call(
        _merge_kernel,
        grid=(t // tm, nj),
        in_specs=[
            pl.BlockSpec((tm, ATTN_WIDTH), lambda i, j: (i, 0)),
            pl.BlockSpec((tm, MLSTM_WIDTH), lambda i, j: (i, 0)),
            pl.BlockSpec((tm, tn), lambda i, j: (i, COL_GA // tn + j)),
            pl.BlockSpec((tm, tn), lambda i, j: (i, COL_GM // tn + j)),
            pl.BlockSpec((tm, D_MODEL), lambda i, j: (i, 0)),
            pl.BlockSpec((ATTN_WIDTH, tn), lambda i, j: (0, j)),
            pl.BlockSpec((MLSTM_WIDTH, tn), lambda i, j: (0, j)),
            pl.BlockSpec((tn, D_MODEL), lambda i, j: (j, 0)),
        ],
        out_specs=pl.BlockSpec((tm, D_MODEL), lambda i, j: (i, 0)),
        out_shape=jax.ShapeDtypeStruct((t, D_MODEL), F32),
        compiler_params=pltpu.CompilerParams(
            dimension_semantics=("parallel", "arbitrary"), vmem_limit_bytes=VMEM_LIMIT),
        name="merge",
    )(att2d, hm2d, z2d, z2d, x2d, wa, wm, wo)


def _cross_kernel(x_ref, g_ref, wq_ref, kv_ref, wo_ref, o_ref):
    x = x_ref[...]
    hc = _rms(x, g_ref[...]).astype(BF16)
    cq = (_dot(hc, wq_ref[...]) * (CROSS_HEAD_DIM ** -0.5)).astype(BF16)
    outs = []
    for hh in range(CROSS_HEADS):
        lo = hh * CROSS_HEAD_DIM
        qh = cq[:, lo:lo + CROSS_HEAD_DIM]
        kh = kv_ref[0, :, lo:lo + CROSS_HEAD_DIM]
        vh = kv_ref[0, :, CROSS_WIDTH + lo:CROSS_WIDTH + lo + CROSS_HEAD_DIM]
        s = _dot_nt(qh, kh)
        p = jnp.exp(s - jnp.max(s, axis=-1, keepdims=True))
        l = jnp.sum(p, axis=-1, keepdims=True)
        outs.append((_dot(p.astype(BF16), vh) / l).astype(BF16))
    co = jnp.concatenate(outs, axis=1)
    o_ref[...] = x + _dot(co, wo_ref[...])


def _cross(x2d, g, wq, ckv, wo, seq, tm=512):
    t = x2d.shape[0]
    n_mem = ckv.shape[1]
    per_batch = seq // tm
    return pl.pallas_call(
        _cross_kernel,
        grid=(t // tm,),
        in_specs=[
            pl.BlockSpec((tm, D_MODEL), lambda i: (i, 0)),
            pl.BlockSpec((1, D_MODEL), lambda i: (0, 0)),
            pl.BlockSpec((D_MODEL, CROSS_WIDTH), lambda i: (0, 0)),
            pl.BlockSpec((1, n_mem, 2 * CROSS_WIDTH), lambda i: (i // per_batch, 0, 0)),
            pl.BlockSpec((CROSS_WIDTH, D_MODEL), lambda i: (0, 0)),
        ],
        out_specs=pl.BlockSpec((tm, D_MODEL), lambda i: (i, 0)),
        out_shape=jax.ShapeDtypeStruct((t, D_MODEL), F32),
        compiler_params=pltpu.CompilerParams(
            dimension_semantics=("parallel",), vmem_limit_bytes=VMEM_LIMIT),
        name="cross",
    )(x2d, g, wq, ckv, wo)


def _mlp_kernel(x_ref, g_ref, wu_ref, wd_ref, gf_ref, o_ref, h_scr, acc, *, final_norm):
    j = pl.program_id(1)

    @pl.when(j == 0)
    def _():
        x = x_ref[...]
        h_scr[...] = _rms(x, g_ref[...]).astype(BF16)
        acc[...] = x

    u = jnp.square(jnp.maximum(_dot(h_scr[...], wu_ref[...]), 0.0)).astype(BF16)
    acc[...] += _dot(u, wd_ref[...])

    @pl.when(j == pl.num_programs(1) - 1)
    def _():
        if final_norm:
            o_ref[...] = _rms(acc[...], gf_ref[...])
        else:
            o_ref[...] = acc[...]


def _mlp(x2d, g, wu, wd, gf, final_norm, tm=512, tf=512):
    t = x2d.shape[0]
    kern = functools.partial(_mlp_kernel, final_norm=final_norm)
    return pl.pallas_call(
        kern,
        grid=(t // tm, D_FF // tf),
        in_specs=[
            pl.BlockSpec((tm, D_MODEL), lambda i, j: (i, 0)),
            pl.BlockSpec((1, D_MODEL), lambda i, j: (0, 0)),
            pl.BlockSpec((D_MODEL, tf), lambda i, j: (0, j)),
            pl.BlockSpec((tf, D_MODEL), lambda i, j: (j, 0)),
            pl.BlockSpec((1, D_MODEL), lambda i, j: (0, 0)),
        ],
        out_specs=pl.BlockSpec((tm, D_MODEL), lambda i, j: (i, 0)),
        out_shape=jax.ShapeDtypeStruct((t, D_MODEL), F32),
        scratch_shapes=[pltpu.VMEM((tm, D_MODEL), BF16), pltpu.VMEM((tm, D_MODEL), F32)],
        compiler_params=pltpu.CompilerParams(
            dimension_semantics=("parallel", "arbitrary"), vmem_limit_bytes=VMEM_LIMIT),
        name="mlp",
    )(x2d, g, wu, wd, gf)


def _alibi_terms():
    slopes = 2.0 ** (-8.0 * np.arange(1, ATTN_HEADS + 1, dtype=np.float64) / ATTN_HEADS)
    rem = slopes * LOG2E
    terms = []
    for _ in range(ALIBI_TERMS):
        t = rem.astype(np.float32).astype(ml_dtypes.bfloat16).astype(np.float64)
        terms.append(t)
        rem = rem - t
    return np.stack(terms, axis=1).reshape(-1).astype(np.float32)


def _pad_rows(a, rows):
    return jnp.pad(a, ((0, rows - a.shape[0]), (0, 0)))


def kernel(x, mem, norm_mix, w_in, b_igate, b_fgate, conv_w, conv_b, lam_q1, lam_k1, lam_q2, lam_k2, attn_norm, mlstm_norm, w_attn_br, w_mlstm_br, w_out, norm_cross, norm_mem, w_cq, w_ckv, w_co, norm_mlp, w_up, w_down, norm_final):
    b, s, _ = x.shape
    t = b * s
    tq = 256
    cs_terms = jnp.asarray(_alibi_terms())
    x2d = x.reshape(t, D_MODEL)
    for l in range(DEPTH):
        lam_init = 0.8 - 0.6 * math.exp(-0.3 * l)
        w = w_in[l]
        w_aq = w[:, COL_AQ:COL_AK] * (ATTN_QK_DIM ** -0.5 * LOG2E)
        w_main = jnp.concatenate([w_aq, w[:, COL_AK:GATE_LO], w[:, GATE_HI:]], axis=1).astype(BF16)
        w_gate = jnp.pad(w[:, GATE_LO:GATE_HI], ((0, 0), (0, LANES - (GATE_HI - GATE_LO)))).astype(BF16)

        z2d, gates = _inproj(x2d, norm_mix[l][None, :], w_main, w_gate)
        z3 = z2d.reshape(b, s, IN_MAIN)

        vt5 = (z3[:, :, COL_AV:COL_AV + ATTN_WIDTH]
               .reshape(b, s // tq, tq, ATTN_HEADS, ATTN_V_DIM).transpose(0, 3, 1, 4, 2))
        lamv = _pad_rows(jnp.pad(jnp.stack([lam_q1[l], lam_k1[l], lam_q2[l], lam_k2[l]]),
                                 ((0, 0), (0, LANES - ATTN_QK_DIM))), SUBLANES)
        att = _attn(z3, vt5, cs_terms, lamv, attn_norm[l][None, :], lam_init, tq=tq)

        gates_t = gates[:, :SUBLANES].reshape(b, s, SUBLANES).transpose(0, 2, 1)
        gate_bias = jnp.concatenate([b_igate[l], b_fgate[l]])
        hm = _mlstm(z3, gates, gates_t, gate_bias, _pad_rows(conv_w[l], SUBLANES),
                    conv_b[l][None, :], mlstm_norm[l][None, :])

        x2d = _merge(att.reshape(t, ATTN_WIDTH), hm.reshape(t, MLSTM_WIDTH), z2d, x2d,
                     w_attn_br[l].astype(BF16), w_mlstm_br[l].astype(BF16), w_out[l].astype(BF16))

        ckv = _memkv(mem, norm_mem[l][None, :], w_ckv[l].astype(BF16))
        x2d = _cross(x2d, norm_cross[l][None, :], w_cq[l].astype(BF16), ckv,
                     w_co[l].astype(BF16), s)

        x2d = _mlp(x2d, norm_mlp[l][None, :], w_up[l].astype(BF16), w_down[l].astype(BF16),
                   norm_final[None, :], final_norm=(l == DEPTH - 1))
    return x2d.reshape(b, s, D_MODEL)
```

```python
import functools
import math

import ml_dtypes
import numpy as np
import jax
import jax.numpy as jnp
from jax import lax
from jax.experimental import pallas as pl
from jax.experimental.pallas import tpu as pltpu

F32 = jnp.float32
BF16 = jnp.bfloat16

D_MODEL = 2048
DEPTH = 1
ATTN_HEADS = 8
ATTN_QK_DIM = 64
ATTN_V_DIM = 128
ATTN_WIDTH = ATTN_HEADS * ATTN_V_DIM
MLSTM_HEADS = 4
MLSTM_QK_DIM = 128
MLSTM_V_DIM = 256
MLSTM_QK_WIDTH = MLSTM_HEADS * MLSTM_QK_DIM
MLSTM_WIDTH = MLSTM_HEADS * MLSTM_V_DIM
CONV_WIDTH = 4
CROSS_HEADS = 4
CROSS_HEAD_DIM = 128
CROSS_WIDTH = CROSS_HEADS * CROSS_HEAD_DIM
D_FF = 4 * D_MODEL
EPS = 1e-6
LANES = 128
SUBLANES = 8
NEG_BIG = -1e30
LOG2E = 1.4426950408889634
ALIBI_TERMS = 3

COL_AQ = 0
COL_AK = 1024
COL_AV = 2048
COL_MQ = 3072
COL_MK = 3584
COL_MV = 4096
COL_MO = 5120
COL_GA = 6144
COL_GM = 8192
IN_MAIN = 10240
GATE_LO = 6144
GATE_HI = 6152

VMEM_LIMIT = 56 * 1024 * 1024


def _rms(x, g):
    ms = jnp.mean(x * x, axis=-1, keepdims=True)
    return x * lax.rsqrt(ms + EPS) * g


def _sigmoid(x):
    return 1.0 / (1.0 + jnp.exp(-x))


def _log_sigmoid(x):
    return jnp.minimum(x, 0.0) - jnp.log(1.0 + jnp.exp(-jnp.abs(x)))


def _dot(a, b):
    return jnp.dot(a, b, preferred_element_type=F32)


def _dot_nt(a, b):
    return lax.dot_general(a, b, (((1,), (1,)), ((), ())), preferred_element_type=F32)


def _memkv_kernel(mem_ref, g_ref, w_ref, o_ref):
    mn = _rms(mem_ref[0], g_ref[...]).astype(BF16)
    o_ref[0] = _dot(mn, w_ref[...]).astype(BF16)


def _memkv(mem, g, w_ckv):
    b, n_mem, _ = mem.shape
    return pl.pallas_call(
        _memkv_kernel,
        grid=(b,),
        in_specs=[
            pl.BlockSpec((1, n_mem, D_MODEL), lambda i: (i, 0, 0)),
            pl.BlockSpec((1, D_MODEL), lambda i: (0, 0)),
            pl.BlockSpec((D_MODEL, 2 * CROSS_WIDTH), lambda i: (0, 0)),
        ],
        out_specs=pl.BlockSpec((1, n_mem, 2 * CROSS_WIDTH), lambda i: (i, 0, 0)),
        out_shape=jax.ShapeDtypeStruct((b, n_mem, 2 * CROSS_WIDTH), BF16),
        compiler_params=pltpu.CompilerParams(
            dimension_semantics=("arbitrary",), vmem_limit_bytes=VMEM_LIMIT),
        name="memkv",
    )(mem, g, w_ckv)


def _inproj_kernel(x_ref, g_ref, wa_ref, wb_ref, wg_ref, z_ref, gate_ref, h_scr, *, na):
    j = pl.program_id(1)

    @pl.when(j == 0)
    def _():
        h = _rms(x_ref[...], g_ref[...]).astype(BF16)
        h_scr[...] = h
        gate_ref[...] = _dot(h, wg_ref[...])

    @pl.when(j < na)
    def _():
        z_ref[...] = _dot(h_scr[...], wa_ref[...]).astype(BF16)

    @pl.when(j >= na)
    def _():
        z_ref[...] = _dot(h_scr[...], wb_ref[...]).astype(BF16)


def _inproj(x2d, g, w_a, w_b, w_gate, tm=1024, tn=1024):
    t = x2d.shape[0]
    na = w_a.shape[1] // tn
    n = w_a.shape[1] + w_b.shape[1]
    return pl.pallas_call(
        functools.partial(_inproj_kernel, na=na),
        grid=(t // tm, n // tn),
        in_specs=[
            pl.BlockSpec((tm, D_MODEL), lambda i, j: (i, 0)),
            pl.BlockSpec((1, D_MODEL), lambda i, j: (0, 0)),
            pl.BlockSpec((D_MODEL, tn), lambda i, j: (0, jnp.minimum(j, na - 1))),
            pl.BlockSpec((D_MODEL, tn), lambda i, j: (0, jnp.maximum(j - na, 0))),
            pl.BlockSpec((D_MODEL, LANES), lambda i, j: (0, 0)),
        ],
        out_specs=[
            pl.BlockSpec((tm, tn), lambda i, j: (i, j)),
            pl.BlockSpec((tm, LANES), lambda i, j: (i, 0)),
        ],
        out_shape=[
            jax.ShapeDtypeStruct((t, n), BF16),
            jax.ShapeDtypeStruct((t, LANES), F32),
        ],
        scratch_shapes=[pltpu.VMEM((tm, D_MODEL), BF16)],
        compiler_params=pltpu.CompilerParams(
            dimension_semantics=("parallel", "arbitrary"), vmem_limit_bytes=VMEM_LIMIT),
        name="inproj",
    )(x2d, g, w_a, w_b, w_gate)


def _attn_kernel(cs_ref, lam_ref, q_ref, k_ref, vt_ref, gain_ref, o_ref,
                 qs_ref, kf_ref, acc_ref, m_ref, *, tq, lam_init):
    tk = tq
    nfeat = ALIBI_TERMS
    dq = 2 * ATTN_QK_DIM
    dv = ATTN_V_DIM
    i = pl.program_id(1)
    cs = [[cs_ref[nfeat * h + t] for t in range(nfeat)] for h in range(ATTN_HEADS)]
    cs_tot = [sum(c[1:], c[0]) for c in cs]

    @pl.when(i == 0)
    def _():
        klane = lax.broadcasted_iota(jnp.int32, (tk, LANES), 1)
        krow = lax.broadcasted_iota(jnp.int32, (tk, LANES), 0).astype(F32)
        qlane = lax.broadcasted_iota(jnp.int32, (2 * tq, LANES), 1)
        qrow = lax.broadcasted_iota(jnp.int32, (2 * tq, LANES), 0)
        qrow = jnp.where(qrow >= tq, qrow - tq, qrow).astype(F32)
        for h in range(ATTN_HEADS):
            kf = jnp.where(klane < nfeat, krow, 0.0)
            qf = jnp.where((qlane >= nfeat) & (qlane < 2 * nfeat), -qrow, 0.0)
            for t in range(nfeat):
                kf = jnp.where(klane == nfeat + t, cs[h][t], kf)
                qf = jnp.where(qlane == t, cs[h][t], qf)
            kf_ref[h] = kf.astype(BF16)
            qs_ref[h, :, pl.ds(dq, LANES)] = qf.astype(BF16)

    lane = lax.broadcasted_iota(jnp.int32, (tq, dq), 1)
    for h in range(ATTN_HEADS):
        q = q_ref[0, :, pl.ds(h * dq, dq)]
        zero = jnp.zeros_like(q)
        qs_ref[h, pl.ds(0, tq), pl.ds(0, dq)] = jnp.where(lane < ATTN_QK_DIM, q, zero)
        qs_ref[h, pl.ds(tq, tq), pl.ds(0, dq)] = jnp.where(lane >= ATTN_QK_DIM, q, zero)
    acc_ref[...] = jnp.zeros_like(acc_ref)
    m_ref[...] = jnp.full_like(m_ref, NEG_BIG)

    orow = lax.broadcasted_iota(jnp.int32, (2 * SUBLANES, tk), 0)
    ones_blk = jnp.where(orow == 0, 1.0, 0.0).astype(BF16)

    def scores(h, j):
        k = k_ref[0, pl.ds(pl.multiple_of(j * tk, tk), tk), pl.ds(h * dq, dq)]
        kx = jnp.concatenate([k, kf_ref[h]], axis=1)
        return _dot_nt(kx, qs_ref[h])

    def softmax_pv(h, j, t, diagonal):
        if diagonal:
            krow = lax.broadcasted_iota(jnp.int32, (tk, 2 * tq), 0)
            qcol = lax.broadcasted_iota(jnp.int32, (tk, 2 * tq), 1)
            qcol = jnp.where(qcol >= tq, qcol - tq, qcol)
            t = jnp.where(krow <= qcol, t, NEG_BIG)
        soff = cs_tot[h] * ((j - i) * tk).astype(F32)
        m = m_ref[h]
        m_new = jnp.maximum(m, jnp.max(t, axis=0, keepdims=True) + soff)
        alpha = jnp.exp2(m - m_new)
        p = jnp.exp2(t - (m_new - soff)).astype(BF16)
        vx = jnp.concatenate([vt_ref[0, h, j], ones_blk], axis=0)
        acc_ref[h] = alpha * acc_ref[h] + _dot(vx, p)
        m_ref[h] = m_new

    def run(units):
        t_next = scores(*units[0][:2])
        for n, (h, j, diagonal) in enumerate(units):
            t = t_next
            if n + 1 < len(units):
                t_next = scores(*units[n + 1][:2])
            softmax_pv(h, j, t, diagonal)

    def tile_units(j, diagonal):
        return [(h, j, diagonal) for h in range(ATTN_HEADS)]

    def body(jj, carry):
        run(tile_units(2 * jj, False) + tile_units(2 * jj + 1, False))
        return carry

    lax.fori_loop(0, i // 2, body, 0)

    @pl.when(i % 2 == 0)
    def _():
        run(tile_units(i, True))

    @pl.when(i % 2 == 1)
    def _():
        run(tile_units(i - 1, False) + tile_units(i, True))

    lv = lam_ref[...]
    d1 = jnp.sum(lv[0:1] * lv[1:2], axis=-1, keepdims=True)
    d2 = jnp.sum(lv[2:3] * lv[3:4], axis=-1, keepdims=True)
    lam = jnp.exp(d1) - jnp.exp(d2) + lam_init
    for h in range(ATTN_HEADS):
        l = acc_ref[h, pl.ds(dv, 1), :]
        out = (acc_ref[h, pl.ds(0, dv), pl.ds(0, tq)] / l[:, :tq]
               - lam * (acc_ref[h, pl.ds(0, dv), pl.ds(tq, tq)] / l[:, tq:]))
        ms = jnp.mean(out * out, axis=0, keepdims=True)
        on = out * lax.rsqrt(ms + EPS)
        o_ref[0, :, pl.ds(h * dv, dv)] = (on.T * gain_ref[...] * (1.0 - lam_init)).astype(BF16)


def _attn(z3, vt5, cs_terms, lamv, gain, lam_init, tq=256):
    b, s, _ = z3.shape
    nq = s // tq
    kern = functools.partial(_attn_kernel, tq=tq, lam_init=lam_init)
    width = ATTN_HEADS * 2 * ATTN_QK_DIM
    return pl.pallas_call(
        kern,
        grid=(b, nq),
        in_specs=[
            pl.BlockSpec(memory_space=pltpu.SMEM),
            pl.BlockSpec((SUBLANES, LANES), lambda bi, i: (0, 0)),
            pl.BlockSpec((1, tq, width), lambda bi, i: (bi, i, COL_AQ // width)),
            pl.BlockSpec((1, s, width), lambda bi, i: (bi, 0, COL_AK // width)),
            pl.BlockSpec((1, ATTN_HEADS, nq, ATTN_V_DIM, tq), lambda bi, i: (bi, 0, 0, 0, 0)),
            pl.BlockSpec((1, ATTN_V_DIM), lambda bi, i: (0, 0)),
        ],
        out_specs=pl.BlockSpec((1, tq, ATTN_WIDTH), lambda bi, i: (bi, i, 0)),
        out_shape=jax.ShapeDtypeStruct((b, s, ATTN_WIDTH), BF16),
        scratch_shapes=[
            pltpu.VMEM((ATTN_HEADS, 2 * tq, 2 * ATTN_QK_DIM + LANES), BF16),
            pltpu.VMEM((ATTN_HEADS, tq, LANES), BF16),
            pltpu.VMEM((ATTN_HEADS, ATTN_V_DIM + 2 * SUBLANES, 2 * tq), F32),
            pltpu.VMEM((ATTN_HEADS, 1, 2 * tq), F32),
        ],
        compiler_params=pltpu.CompilerParams(
            dimension_semantics=("parallel", "arbitrary"),
            vmem_limit_bytes=VMEM_LIMIT),
        name="attn",
    )(cs_terms, lamv, z3, z3, vt5, gain)


def _split3(x):
    hi = x.astype(BF16)
    r = x - hi.astype(F32)
    mid = r.astype(BF16)
    lo = (r - mid.astype(F32)).astype(BF16)
    return hi, mid, lo


def _mlstm_kernel(bias_ref, uq_ref, uk_ref, v_ref, mo_ref, gcol_ref, grow_ref,
                  cwq_ref, cwk_ref, cbq_ref, cbk_ref, gn_ref, o_ref,
                  extq, extk, c_scr, n_scr, m_scr, *, chunk):
    L = chunk
    h = pl.program_id(1)

    @pl.when(pl.program_id(2) == 0)
    def _():
        extq[pl.ds(0, SUBLANES), :] = jnp.zeros((SUBLANES, MLSTM_QK_DIM), F32)
        extk[pl.ds(0, SUBLANES), :] = jnp.zeros((SUBLANES, MLSTM_QK_DIM), F32)
        c_scr[...] = jnp.zeros_like(c_scr)
        n_scr[...] = jnp.zeros_like(n_scr)
        m_scr[...] = jnp.zeros_like(m_scr)

    def conv_silu(u_ref, ext, cw_ref, cb_ref):
        ext[pl.ds(SUBLANES, L), :] = u_ref[0].astype(F32)
        w = cw_ref[...]
        y = cb_ref[...]
        for tap in range(CONV_WIDTH):
            off = SUBLANES - (CONV_WIDTH - 1) + tap
            y = y + ext[pl.ds(off, L), :] * w[tap:tap + 1]
        ext[pl.ds(0, SUBLANES), :] = ext[pl.ds(L, SUBLANES), :]
        return y * _sigmoid(y)

    q = conv_silu(uq_ref, extq, cwq_ref, cbq_ref)
    k = conv_silu(uk_ref, extk, cwk_ref, cbk_ref) * (MLSTM_QK_DIM ** -0.5)
    qb = q.astype(BF16)
    kb = k.astype(BF16)
    vb = v_ref[0]

    b_i = bias_ref[h]
    b_f = bias_ref[MLSTM_HEADS + h]

    g_all = gcol_ref[...]
    lane = lax.broadcasted_iota(jnp.int32, (L, LANES), 1)
    i_col = jnp.sum(jnp.where(lane == h, g_all, 0.0), axis=1, keepdims=True) + b_i
    f_col = jnp.sum(jnp.where(lane == MLSTM_HEADS + h, g_all, 0.0), axis=1, keepdims=True) + b_f
    lf_col = _log_sigmoid(f_col)
    i_row = grow_ref[0, pl.ds(h, 1), :] + b_i
    f_row = grow_ref[0, pl.ds(MLSTM_HEADS + h, 1), :] + b_f
    lf_row = _log_sigmoid(f_row)

    r_i = lax.broadcasted_iota(jnp.int32, (L, L), 0)
    c_i = lax.broadcasted_iota(jnp.int32, (L, L), 1)
    causal = c_i <= r_i
    tri = jnp.where(causal, 1.0, 0.0).astype(BF16)
    tri_t = jnp.where(r_i <= c_i, 1.0, 0.0).astype(BF16)
    hi, mid, lo = _split3(jnp.broadcast_to(lf_col, (L, LANES)))
    b_col = (_dot(tri, hi) + _dot(tri, mid) + _dot(tri, lo))[:, 0:1]
    hi, mid, lo = _split3(jnp.broadcast_to(lf_row, (SUBLANES, L)))
    b_row = (_dot(hi, tri_t) + _dot(mid, tri_t) + _dot(lo, tri_t))[0:1, :]
    g_tot = jnp.sum(lf_row, axis=1, keepdims=True)

    c_prev = c_scr[...]
    n_prev = n_scr[...]
    m_prev = m_scr[:, 0:1]

    d = jnp.where(causal, b_col - b_row + i_row, NEG_BIG)
    m_inter = b_col + m_prev
    m_j = jnp.maximum(m_inter, jnp.max(d, axis=1, keepdims=True))
    w_inter = jnp.exp(m_inter - m_j)
    p = _dot_nt(qb, kb) * jnp.exp(d - m_j)
    num = w_inter * _dot(qb, c_prev.astype(BF16)) + _dot(p.astype(BF16), vb)
    den = (w_inter * jnp.sum(q * n_prev, axis=1, keepdims=True)
           + jnp.sum(p, axis=1, keepdims=True))
    hval = num / jnp.maximum(jnp.abs(den), jnp.exp(-m_j))
    hn = _rms(hval, gn_ref[...])
    o_ref[0] = (hn * _sigmoid(mo_ref[0].astype(F32))).astype(BF16)

    a_col = g_tot - b_col + i_col
    m_loc = jnp.max(g_tot - b_row + i_row, axis=1, keepdims=True)
    kw = k * jnp.exp(a_col - m_loc)
    n_loc = jnp.sum(kw, axis=0, keepdims=True)
    c_loc = _dot(kw.T.astype(BF16), vb)
    m_new = jnp.maximum(g_tot + m_prev, m_loc)
    a_old = jnp.exp(g_tot + m_prev - m_new)
    a_new = jnp.exp(m_loc - m_new)
    c_scr[...] = a_old * c_prev + a_new * c_loc
    n_scr[...] = a_old * n_prev + a_new * n_loc
    m_scr[...] = jnp.broadcast_to(m_new, m_scr.shape)


def _mlstm(z3, gates, gates_t, gate_bias, conv_w8, conv_b, gnorm, chunk=256):
    b, s, _ = z3.shape
    nc = s // chunk
    kern = functools.partial(_mlstm_kernel, chunk=chunk)
    qk = MLSTM_QK_DIM
    dv = MLSTM_V_DIM
    return pl.pallas_call(
        kern,
        grid=(b, MLSTM_HEADS, nc),
        in_specs=[
            pl.BlockSpec(memory_space=pltpu.SMEM),
            pl.BlockSpec((1, chunk, qk), lambda bi, h, c: (bi, c, COL_MQ // qk + h)),
            pl.BlockSpec((1, chunk, qk), lambda bi, h, c: (bi, c, COL_MK // qk + h)),
            pl.BlockSpec((1, chunk, dv), lambda bi, h, c: (bi, c, COL_MV // dv + h)),
            pl.BlockSpec((1, chunk, dv), lambda bi, h, c: (bi, c, COL_MO // dv + h)),
            pl.BlockSpec((chunk, LANES), lambda bi, h, c: (bi * nc + c, 0)),
            pl.BlockSpec((1, SUBLANES, chunk), lambda bi, h, c: (bi, 0, c)),
            pl.BlockSpec((SUBLANES, qk), lambda bi, h, c: (0, h)),
            pl.BlockSpec((SUBLANES, qk), lambda bi, h, c: (0, MLSTM_HEADS + h)),
            pl.BlockSpec((1, qk), lambda bi, h, c: (0, h)),
            pl.BlockSpec((1, qk), lambda bi, h, c: (0, MLSTM_HEADS + h)),
            pl.BlockSpec((1, dv), lambda bi, h, c: (0, h)),
        ],
        out_specs=pl.BlockSpec((1, chunk, dv), lambda bi, h, c: (bi, c, h)),
        out_shape=jax.ShapeDtypeStruct((b, s, MLSTM_WIDTH), BF16),
        scratch_shapes=[
            pltpu.VMEM((chunk + 2 * SUBLANES, qk), F32),
            pltpu.VMEM((chunk + 2 * SUBLANES, qk), F32),
            pltpu.VMEM((qk, dv), F32),
            pltpu.VMEM((1, qk), F32),
            pltpu.VMEM((1, LANES), F32),
        ],
        compiler_params=pltpu.CompilerParams(
            dimension_semantics=("parallel", "parallel", "arbitrary"),
            vmem_limit_bytes=VMEM_LIMIT),
        name="mlstm",
    )(gate_bias, z3, z3, z3, z3, gates, gates_t, conv_w8, conv_w8, conv_b, conv_b, gnorm)


def _merge_kernel(att_ref, hm_ref, ga_ref, gm_ref, x_ref, wa_ref, wm_ref, wo_ref, o_ref):
    @pl.when(pl.program_id(1) == 0)
    def _():
        o_ref[...] = x_ref[...]

    a = _dot(att_ref[...], wa_ref[...])
    bm = _dot(hm_ref[...], wm_ref[...])
    y = _sigmoid(ga_ref[...].astype(F32)) * a + _sigmoid(gm_ref[...].astype(F32)) * bm
    o_ref[...] += _dot(y.astype(BF16), wo_ref[...])


def _merge(att2d, hm2d, z2d, x2d, wa, wm, wo, tm=512, tn=1024):
    t = x2d.shape[0]
    nj = D_MODEL // tn
    return pl.pallas_call(
        _merge_kernel,
        grid=(t // tm, nj),
        in_specs=[
            pl.BlockSpec((tm, ATTN_WIDTH), lambda i, j: (i, 0)),
            pl.BlockSpec((tm, MLSTM_WIDTH), lambda i, j: (i, 0)),
            pl.BlockSpec((tm, tn), lambda i, j: (i, COL_GA // tn + j)),
            pl.BlockSpec((tm, tn), lambda i, j: (i, COL_GM // tn + j)),
            pl.BlockSpec((tm, D_MODEL), lambda i, j: (i, 0)),
            pl.BlockSpec((ATTN_WIDTH, tn), lambda i, j: (0, j)),
            pl.BlockSpec((MLSTM_WIDTH, tn), lambda i, j: (0, j)),
            pl.BlockSpec((tn, D_MODEL), lambda i, j: (j, 0)),
        ],
        out_specs=pl.BlockSpec((tm, D_MODEL), lambda i, j: (i, 0)),
        out_shape=jax.ShapeDtypeStruct((t, D_MODEL), F32),
        compiler_params=pltpu.CompilerParams(
            dimension_semantics=("parallel", "arbitrary"), vmem_limit_bytes=VMEM_LIMIT),
        name="merge",
    )(att2d, hm2d, z2d, z2d, x2d, wa, wm, wo)


def _cross_kernel(x_ref, g_ref, wq_ref, kv_ref, wo_ref, o_ref):
    x = x_ref[...]
    hc = _rms(x, g_ref[...]).astype(BF16)
    cq = (_dot(hc, wq_ref[...]) * (CROSS_HEAD_DIM ** -0.5)).astype(BF16)
    outs = []
    for hh in range(CROSS_HEADS):
        lo = hh * CROSS_HEAD_DIM
        qh = cq[:, lo:lo + CROSS_HEAD_DIM]
        kh = kv_ref[0, :, lo:lo + CROSS_HEAD_DIM]
        vh = kv_ref[0, :, CROSS_WIDTH + lo:CROSS_WIDTH + lo + CROSS_HEAD_DIM]
        s = _dot_nt(qh, kh)
        p = jnp.exp(s - jnp.max(s, axis=-1, keepdims=True))
        l = jnp.sum(p, axis=-1, keepdims=True)
        outs.append((_dot(p.astype(BF16), vh) / l).astype(BF16))
    co = jnp.concatenate(outs, axis=1)
    o_ref[...] = x + _dot(co, wo_ref[...])


def _cross(x2d, g, wq, ckv, wo, seq, tm=512):
    t = x2d.shape[0]
    n_mem = ckv.shape[1]
    per_batch = seq // tm
    return pl.pallas_call(
        _cross_kernel,
        grid=(t // tm,),
        in_specs=[
            pl.BlockSpec((tm, D_MODEL), lambda i: (i, 0)),
            pl.BlockSpec((1, D_MODEL), lambda i: (0, 0)),
            pl.BlockSpec((D_MODEL, CROSS_WIDTH), lambda i: (0, 0)),
            pl.BlockSpec((1, n_mem, 2 * CROSS_WIDTH), lambda i: (i // per_batch, 0, 0)),
            pl.BlockSpec((CROSS_WIDTH, D_MODEL), lambda i: (0, 0)),
        ],
        out_specs=pl.BlockSpec((tm, D_MODEL), lambda i: (i, 0)),
        out_shape=jax.ShapeDtypeStruct((t, D_MODEL), F32),
        compiler_params=pltpu.CompilerParams(
            dimension_semantics=("parallel",), vmem_limit_bytes=VMEM_LIMIT),
        name="cross",
    )(x2d, g, wq, ckv, wo)


def _mlp_kernel(x_ref, g_ref, wu_ref, wd_ref, gf_ref, o_ref, h_scr, acc, *, final_norm):
    j = pl.program_id(1)

    @pl.when(j == 0)
    def _():
        x = x_ref[...]
        h_scr[...] = _rms(x, g_ref[...]).astype(BF16)
        acc[...] = x

    u = jnp.square(jnp.maximum(_dot(h_scr[...], wu_ref[...]), 0.0)).astype(BF16)
    acc[...] += _dot(u, wd_ref[...])

    @pl.when(j == pl.num_programs(1) - 1)
    def _():
        if final_norm:
            o_ref[...] = _rms(acc[...], gf_ref[...])
        else:
            o_ref[...] = acc[...]


def _mlp(x2d, g, wu, wd, gf, final_norm, tm=512, tf=1024):
    t = x2d.shape[0]
    kern = functools.partial(_mlp_kernel, final_norm=final_norm)
    return pl.pallas_call(
        kern,
        grid=(t // tm, D_FF // tf),
        in_specs=[
            pl.BlockSpec((tm, D_MODEL), lambda i, j: (i, 0)),
            pl.BlockSpec((1, D_MODEL), lambda i, j: (0, 0)),
            pl.BlockSpec((D_MODEL, tf), lambda i, j: (0, j)),
            pl.BlockSpec((tf, D_MODEL), lambda i, j: (j, 0)),
            pl.BlockSpec((1, D_MODEL), lambda i, j: (0, 0)),
        ],
        out_specs=pl.BlockSpec((tm, D_MODEL), lambda i, j: (i, 0)),
        out_shape=jax.ShapeDtypeStruct((t, D_MODEL), F32),
        scratch_shapes=[pltpu.VMEM((tm, D_MODEL), BF16), pltpu.VMEM((tm, D_MODEL), F32)],
        compiler_params=pltpu.CompilerParams(
            dimension_semantics=("parallel", "arbitrary"), vmem_limit_bytes=VMEM_LIMIT),
        name="mlp",
    )(x2d, g, wu, wd, gf)


def _alibi_terms():
    slopes = 2.0 ** (-8.0 * np.arange(1, ATTN_HEADS + 1, dtype=np.float64) / ATTN_HEADS)
    rem = slopes * LOG2E
    terms = []
    for _ in range(ALIBI_TERMS):
        t = rem.astype(np.float32).astype(ml_dtypes.bfloat16).astype(np.float64)
        terms.append(t)
        rem = rem - t
    return np.stack(terms, axis=1).reshape(-1).astype(np.float32)


def _in_col_scale():
    scale = np.ones((1, GATE_LO), np.float32)
    scale[:, COL_AQ:COL_AK] = ATTN_QK_DIM ** -0.5 * LOG2E
    return scale


def _pad_rows(a, rows):
    return jnp.pad(a, ((0, rows - a.shape[0]), (0, 0)))


def kernel(x, mem, norm_mix, w_in, b_igate, b_fgate, conv_w, conv_b, lam_q1, lam_k1, lam_q2, lam_k2, attn_norm, mlstm_norm, w_attn_br, w_mlstm_br, w_out, norm_cross, norm_mem, w_cq, w_ckv, w_co, norm_mlp, w_up, w_down, norm_final):
    b, s, _ = x.shape
    t = b * s
    tq = 256
    cs_terms = jnp.asarray(_alibi_terms())
    x2d = x.reshape(t, D_MODEL)
    for l in range(DEPTH):
        lam_init = 0.8 - 0.6 * math.exp(-0.3 * l)
        w = w_in[l]
        w_a = (w[:, :GATE_LO] * jnp.asarray(_in_col_scale())).astype(BF16)
        w_b = w[:, GATE_HI:].astype(BF16)
        w_gate = jnp.pad(w[:, GATE_LO:GATE_HI], ((0, 0), (0, LANES - (GATE_HI - GATE_LO)))).astype(BF16)

        z2d, gates = _inproj(x2d, norm_mix[l][None, :], w_a, w_b, w_gate)
        z3 = z2d.reshape(b, s, IN_MAIN)

        vt5 = (z3[:, :, COL_AV:COL_AV + ATTN_WIDTH]
               .reshape(b, s // tq, tq, ATTN_HEADS, ATTN_V_DIM).transpose(0, 3, 1, 4, 2))
        lamv = _pad_rows(jnp.pad(jnp.stack([lam_q1[l], lam_k1[l], lam_q2[l], lam_k2[l]]),
                                 ((0, 0), (0, LANES - ATTN_QK_DIM))), SUBLANES)
        att = _attn(z3, vt5, cs_terms, lamv, attn_norm[l][None, :], lam_init, tq=tq)

        gates_t = gates[:, :SUBLANES].reshape(b, s, SUBLANES).transpose(0, 2, 1)
        gate_bias = jnp.concatenate([b_igate[l], b_fgate[l]])
        hm = _mlstm(z3, gates, gates_t, gate_bias, _pad_rows(conv_w[l], SUBLANES),
                    conv_b[l][None, :], mlstm_norm[l][None, :])

        x2d = _merge(att.reshape(t, ATTN_WIDTH), hm.reshape(t, MLSTM_WIDTH), z2d, x2d,
                     w_attn_br[l].astype(BF16), w_mlstm_br[l].astype(BF16), w_out[l].astype(BF16))

        ckv = _memkv(mem, norm_mem[l][None, :], w_ckv[l].astype(BF16))
        x2d = _cross(x2d, norm_cross[l][None, :], w_cq[l].astype(BF16), ckv,
                     w_co[l].astype(BF16), s)

        x2d = _mlp(x2d, norm_mlp[l][None, :], w_up[l].astype(BF16), w_down[l].astype(BF16),
                   norm_final[None, :], final_norm=(l == DEPTH - 1))
    return x2d.reshape(b, s, D_MODEL)
```

```python
import functools
import math

import ml_dtypes
import numpy as np
import jax
import jax.numpy as jnp
from jax import lax
from jax.experimental import pallas as pl
from jax.experimental.pallas import tpu as pltpu

F32 = jnp.float32
BF16 = jnp.bfloat16

D_MODEL = 2048
DEPTH = 1
ATTN_HEADS = 8
ATTN_QK_DIM = 64
ATTN_V_DIM = 128
ATTN_WIDTH = ATTN_HEADS * ATTN_V_DIM
MLSTM_HEADS = 4
MLSTM_QK_DIM = 128
MLSTM_V_DIM = 256
MLSTM_QK_WIDTH = MLSTM_HEADS * MLSTM_QK_DIM
MLSTM_WIDTH = MLSTM_HEADS * MLSTM_V_DIM
CONV_WIDTH = 4
CROSS_HEADS = 4
CROSS_HEAD_DIM = 128
CROSS_WIDTH = CROSS_HEADS * CROSS_HEAD_DIM
D_FF = 4 * D_MODEL
EPS = 1e-6
LANES = 128
SUBLANES = 8
NEG_BIG = -1e30
LOG2E = 1.4426950408889634
ALIBI_TERMS = 3

COL_AQ = 0
COL_AK = 1024
COL_AV = 2048
COL_MQ = 3072
COL_MK = 3584
COL_MV = 4096
COL_MO = 5120
COL_GA = 6144
COL_GM = 8192
IN_MAIN = 10240
GATE_LO = 6144
GATE_HI = 6152

VMEM_LIMIT = 56 * 1024 * 1024


def _rms(x, g):
    ms = jnp.mean(x * x, axis=-1, keepdims=True)
    return x * lax.rsqrt(ms + EPS) * g


def _sigmoid(x):
    return 1.0 / (1.0 + jnp.exp(-x))


def _log_sigmoid(x):
    return jnp.minimum(x, 0.0) - jnp.log(1.0 + jnp.exp(-jnp.abs(x)))


def _dot(a, b):
    return jnp.dot(a, b, preferred_element_type=F32)


def _dot_nt(a, b):
    return lax.dot_general(a, b, (((1,), (1,)), ((), ())), preferred_element_type=F32)


def _memkv_kernel(mem_ref, g_ref, w_ref, o_ref):
    mn = _rms(mem_ref[0], g_ref[...]).astype(BF16)
    o_ref[0] = _dot(mn, w_ref[...]).astype(BF16)


def _memkv(mem, g, w_ckv):
    b, n_mem, _ = mem.shape
    return pl.pallas_call(
        _memkv_kernel,
        grid=(b,),
        in_specs=[
            pl.BlockSpec((1, n_mem, D_MODEL), lambda i: (i, 0, 0)),
            pl.BlockSpec((1, D_MODEL), lambda i: (0, 0)),
            pl.BlockSpec((D_MODEL, 2 * CROSS_WIDTH), lambda i: (0, 0)),
        ],
        out_specs=pl.BlockSpec((1, n_mem, 2 * CROSS_WIDTH), lambda i: (i, 0, 0)),
        out_shape=jax.ShapeDtypeStruct((b, n_mem, 2 * CROSS_WIDTH), BF16),
        compiler_params=pltpu.CompilerParams(
            dimension_semantics=("arbitrary",), vmem_limit_bytes=VMEM_LIMIT),
        name="memkv",
    )(mem, g, w_ckv)


def _inproj_kernel(x_ref, g_ref, wa_ref, wb_ref, wg_ref, z_ref, gate_ref, h_scr, *, na):
    j = pl.program_id(1)

    @pl.when(j == 0)
    def _():
        h = _rms(x_ref[...], g_ref[...]).astype(BF16)
        h_scr[...] = h
        gate_ref[...] = _dot_nt(h, wg_ref[...])

    @pl.when(j < na)
    def _():
        z_ref[...] = _dot_nt(h_scr[...], wa_ref[...]).astype(BF16)

    @pl.when(j >= na)
    def _():
        z_ref[...] = _dot_nt(h_scr[...], wb_ref[...]).astype(BF16)


def _inproj(x2d, g, w_a, w_b, w_gate, tm=1024, tn=1024):
    t = x2d.shape[0]
    na = w_a.shape[0] // tn
    n = w_a.shape[0] + w_b.shape[0]
    return pl.pallas_call(
        functools.partial(_inproj_kernel, na=na),
        grid=(t // tm, n // tn),
        in_specs=[
            pl.BlockSpec((tm, D_MODEL), lambda i, j: (i, 0)),
            pl.BlockSpec((1, D_MODEL), lambda i, j: (0, 0)),
            pl.BlockSpec((tn, D_MODEL), lambda i, j: (jnp.minimum(j, na - 1), 0)),
            pl.BlockSpec((tn, D_MODEL), lambda i, j: (jnp.maximum(j - na, 0), 0)),
            pl.BlockSpec((LANES, D_MODEL), lambda i, j: (0, 0)),
        ],
        out_specs=[
            pl.BlockSpec((tm, tn), lambda i, j: (i, j)),
            pl.BlockSpec((tm, LANES), lambda i, j: (i, 0)),
        ],
        out_shape=[
            jax.ShapeDtypeStruct((t, n), BF16),
            jax.ShapeDtypeStruct((t, LANES), F32),
        ],
        scratch_shapes=[pltpu.VMEM((tm, D_MODEL), BF16)],
        compiler_params=pltpu.CompilerParams(
            dimension_semantics=("parallel", "arbitrary"), vmem_limit_bytes=VMEM_LIMIT),
        name="inproj",
    )(x2d, g, w_a, w_b, w_gate)


def _attn_kernel(cs_ref, lam_ref, q_ref, k_ref, vt_ref, gain_ref, o_ref,
                 qs_ref, kf_ref, acc_ref, m_ref, *, tq, lam_init):
    tk = tq
    nfeat = ALIBI_TERMS
    dq = 2 * ATTN_QK_DIM
    dv = ATTN_V_DIM
    i = pl.program_id(1)
    cs = [[cs_ref[nfeat * h + t] for t in range(nfeat)] for h in range(ATTN_HEADS)]
    cs_tot = [sum(c[1:], c[0]) for c in cs]

    @pl.when(i == 0)
    def _():
        klane = lax.broadcasted_iota(jnp.int32, (tk, LANES), 1)
        krow = lax.broadcasted_iota(jnp.int32, (tk, LANES), 0).astype(F32)
        qlane = lax.broadcasted_iota(jnp.int32, (2 * tq, LANES), 1)
        qrow = lax.broadcasted_iota(jnp.int32, (2 * tq, LANES), 0)
        qrow = jnp.where(qrow >= tq, qrow - tq, qrow).astype(F32)
        for h in range(ATTN_HEADS):
            kf = jnp.where(klane < nfeat, krow, 0.0)
            qf = jnp.where((qlane >= nfeat) & (qlane < 2 * nfeat), -qrow, 0.0)
            for t in range(nfeat):
                kf = jnp.where(klane == nfeat + t, cs[h][t], kf)
                qf = jnp.where(qlane == t, cs[h][t], qf)
            kf_ref[h] = kf.astype(BF16)
            qs_ref[h, :, pl.ds(dq, LANES)] = qf.astype(BF16)

    lane = lax.broadcasted_iota(jnp.int32, (tq, dq), 1)
    for h in range(ATTN_HEADS):
        q = q_ref[0, :, pl.ds(h * dq, dq)]
        zero = jnp.zeros_like(q)
        qs_ref[h, pl.ds(0, tq), pl.ds(0, dq)] = jnp.where(lane < ATTN_QK_DIM, q, zero)
        qs_ref[h, pl.ds(tq, tq), pl.ds(0, dq)] = jnp.where(lane >= ATTN_QK_DIM, q, zero)
    acc_ref[...] = jnp.zeros_like(acc_ref)
    m_ref[...] = jnp.full_like(m_ref, NEG_BIG)

    orow = lax.broadcasted_iota(jnp.int32, (2 * SUBLANES, tk), 0)
    ones_blk = jnp.where(orow == 0, 1.0, 0.0).astype(BF16)

    def scores(h, j):
        k = k_ref[0, pl.ds(pl.multiple_of(j * tk, tk), tk), pl.ds(h * dq, dq)]
        kx = jnp.concatenate([k, kf_ref[h]], axis=1)
        return _dot_nt(kx, qs_ref[h])

    def softmax_pv(h, j, t, diagonal):
        if diagonal:
            krow = lax.broadcasted_iota(jnp.int32, (tk, 2 * tq), 0)
            qcol = lax.broadcasted_iota(jnp.int32, (tk, 2 * tq), 1)
            qcol = jnp.where(qcol >= tq, qcol - tq, qcol)
            t = jnp.where(krow <= qcol, t, NEG_BIG)
        soff = cs_tot[h] * ((j - i) * tk).astype(F32)
        m = m_ref[h]
        m_new = jnp.maximum(m, jnp.max(t, axis=0, keepdims=True) + soff)
        alpha = jnp.exp2(m - m_new)
        p = jnp.exp2(t - (m_new - soff)).astype(BF16)
        vx = jnp.concatenate([vt_ref[0, h, j], ones_blk], axis=0)
        acc_ref[h] = alpha * acc_ref[h] + _dot(vx, p)
        m_ref[h] = m_new

    def run(units):
        t_next = scores(*units[0][:2])
        for n, (h, j, diagonal) in enumerate(units):
            t = t_next
            if n + 1 < len(units):
                t_next = scores(*units[n + 1][:2])
            softmax_pv(h, j, t, diagonal)

    def tile_units(j, diagonal):
        return [(h, j, diagonal) for h in range(ATTN_HEADS)]

    def body(jj, carry):
        run(tile_units(2 * jj, False) + tile_units(2 * jj + 1, False))
        return carry

    lax.fori_loop(0, i // 2, body, 0)

    @pl.when(i % 2 == 0)
    def _():
        run(tile_units(i, True))

    @pl.when(i % 2 == 1)
    def _():
        run(tile_units(i - 1, False) + tile_units(i, True))

    lv = lam_ref[...]
    d1 = jnp.sum(lv[0:1] * lv[1:2], axis=-1, keepdims=True)
    d2 = jnp.sum(lv[2:3] * lv[3:4], axis=-1, keepdims=True)
    lam = jnp.exp(d1) - jnp.exp(d2) + lam_init
    for h in range(ATTN_HEADS):
        l = acc_ref[h, pl.ds(dv, 1), :]
        out = (acc_ref[h, pl.ds(0, dv), pl.ds(0, tq)] / l[:, :tq]
               - lam * (acc_ref[h, pl.ds(0, dv), pl.ds(tq, tq)] / l[:, tq:]))
        ms = jnp.mean(out * out, axis=0, keepdims=True)
        on = out * lax.rsqrt(ms + EPS)
        o_ref[0, :, pl.ds(h * dv, dv)] = (on.T * gain_ref[...] * (1.0 - lam_init)).astype(BF16)


def _attn(z3, vt5, cs_terms, lamv, gain, lam_init, tq=256):
    b, s, _ = z3.shape
    nq = s // tq
    kern = functools.partial(_attn_kernel, tq=tq, lam_init=lam_init)
    width = ATTN_HEADS * 2 * ATTN_QK_DIM
    return pl.pallas_call(
        kern,
        grid=(b, nq),
        in_specs=[
            pl.BlockSpec(memory_space=pltpu.SMEM),
            pl.BlockSpec((SUBLANES, LANES), lambda bi, i: (0, 0)),
            pl.BlockSpec((1, tq, width), lambda bi, i: (bi, i, COL_AQ // width)),
            pl.BlockSpec((1, s, width), lambda bi, i: (bi, 0, COL_AK // width)),
            pl.BlockSpec((1, ATTN_HEADS, nq, ATTN_V_DIM, tq), lambda bi, i: (bi, 0, 0, 0, 0)),
            pl.BlockSpec((1, ATTN_V_DIM), lambda bi, i: (0, 0)),
        ],
        out_specs=pl.BlockSpec((1, tq, ATTN_WIDTH), lambda bi, i: (bi, i, 0)),
        out_shape=jax.ShapeDtypeStruct((b, s, ATTN_WIDTH), BF16),
        scratch_shapes=[
            pltpu.VMEM((ATTN_HEADS, 2 * tq, 2 * ATTN_QK_DIM + LANES), BF16),
            pltpu.VMEM((ATTN_HEADS, tq, LANES), BF16),
            pltpu.VMEM((ATTN_HEADS, ATTN_V_DIM + 2 * SUBLANES, 2 * tq), F32),
            pltpu.VMEM((ATTN_HEADS, 1, 2 * tq), F32),
        ],
        compiler_params=pltpu.CompilerParams(
            dimension_semantics=("parallel", "arbitrary"),
            vmem_limit_bytes=VMEM_LIMIT),
        name="attn",
    )(cs_terms, lamv, z3, z3, vt5, gain)


def _split3(x):
    hi = x.astype(BF16)
    r = x - hi.astype(F32)
    mid = r.astype(BF16)
    lo = (r - mid.astype(F32)).astype(BF16)
    return hi, mid, lo


def _mlstm_kernel(bias_ref, uq_ref, uk_ref, v_ref, mo_ref, gcol_ref, grow_ref,
                  cwq_ref, cwk_ref, cbq_ref, cbk_ref, gn_ref, o_ref,
                  extq, extk, c_scr, n_scr, m_scr, *, chunk):
    L = chunk
    h = pl.program_id(1)

    @pl.when(pl.program_id(2) == 0)
    def _():
        extq[pl.ds(0, SUBLANES), :] = jnp.zeros((SUBLANES, MLSTM_QK_DIM), F32)
        extk[pl.ds(0, SUBLANES), :] = jnp.zeros((SUBLANES, MLSTM_QK_DIM), F32)
        c_scr[...] = jnp.zeros_like(c_scr)
        n_scr[...] = jnp.zeros_like(n_scr)
        m_scr[...] = jnp.zeros_like(m_scr)

    def conv_silu(u_ref, ext, cw_ref, cb_ref):
        ext[pl.ds(SUBLANES, L), :] = u_ref[0].astype(F32)
        w = cw_ref[...]
        y = cb_ref[...]
        for tap in range(CONV_WIDTH):
            off = SUBLANES - (CONV_WIDTH - 1) + tap
            y = y + ext[pl.ds(off, L), :] * w[tap:tap + 1]
        ext[pl.ds(0, SUBLANES), :] = ext[pl.ds(L, SUBLANES), :]
        return y * _sigmoid(y)

    q = conv_silu(uq_ref, extq, cwq_ref, cbq_ref)
    k = conv_silu(uk_ref, extk, cwk_ref, cbk_ref) * (MLSTM_QK_DIM ** -0.5)
    qb = q.astype(BF16)
    kb = k.astype(BF16)
    vb = v_ref[0]

    b_i = bias_ref[h]
    b_f = bias_ref[MLSTM_HEADS + h]

    g_all = gcol_ref[...]
    lane = lax.broadcasted_iota(jnp.int32, (L, LANES), 1)
    i_col = jnp.sum(jnp.where(lane == h, g_all, 0.0), axis=1, keepdims=True) + b_i
    f_col = jnp.sum(jnp.where(lane == MLSTM_HEADS + h, g_all, 0.0), axis=1, keepdims=True) + b_f
    lf_col = _log_sigmoid(f_col)
    i_row = grow_ref[0, pl.ds(h, 1), :] + b_i
    f_row = grow_ref[0, pl.ds(MLSTM_HEADS + h, 1), :] + b_f
    lf_row = _log_sigmoid(f_row)

    r_i = lax.broadcasted_iota(jnp.int32, (L, L), 0)
    c_i = lax.broadcasted_iota(jnp.int32, (L, L), 1)
    causal = c_i <= r_i
    tri = jnp.where(causal, 1.0, 0.0).astype(BF16)
    tri_t = jnp.where(r_i <= c_i, 1.0, 0.0).astype(BF16)
    hi, mid, lo = _split3(jnp.broadcast_to(lf_col, (L, LANES)))
    b_col = (_dot(tri, hi) + _dot(tri, mid) + _dot(tri, lo))[:, 0:1]
    hi, mid, lo = _split3(jnp.broadcast_to(lf_row, (SUBLANES, L)))
    b_row = (_dot(hi, tri_t) + _dot(mid, tri_t) + _dot(lo, tri_t))[0:1, :]
    g_tot = jnp.sum(lf_row, axis=1, keepdims=True)

    c_prev = c_scr[...]
    n_prev = n_scr[...]
    m_prev = m_scr[:, 0:1]

    d = jnp.where(causal, b_col - b_row + i_row, NEG_BIG)
    m_inter = b_col + m_prev
    m_j = jnp.maximum(m_inter, jnp.max(d, axis=1, keepdims=True))
    w_inter = jnp.exp(m_inter - m_j)
    p = _dot_nt(qb, kb) * jnp.exp(d - m_j)
    num = w_inter * _dot(qb, c_prev.astype(BF16)) + _dot(p.astype(BF16), vb)
    den = (w_inter * jnp.sum(q * n_prev, axis=1, keepdims=True)
           + jnp.sum(p, axis=1, keepdims=True))
    hval = num / jnp.maximum(jnp.abs(den), jnp.exp(-m_j))
    hn = _rms(hval, gn_ref[...])
    o_ref[0] = (hn * _sigmoid(mo_ref[0].astype(F32))).astype(BF16)

    a_col = g_tot - b_col + i_col
    m_loc = jnp.max(g_tot - b_row + i_row, axis=1, keepdims=True)
    kw = k * jnp.exp(a_col - m_loc)
    n_loc = jnp.sum(kw, axis=0, keepdims=True)
    c_loc = _dot(kw.T.astype(BF16), vb)
    m_new = jnp.maximum(g_tot + m_prev, m_loc)
    a_old = jnp.exp(g_tot + m_prev - m_new)
    a_new = jnp.exp(m_loc - m_new)
    c_scr[...] = a_old * c_prev + a_new * c_loc
    n_scr[...] = a_old * n_prev + a_new * n_loc
    m_scr[...] = jnp.broadcast_to(m_new, m_scr.shape)


def _mlstm(z3, gates, gates_t, gate_bias, conv_w8, conv_b, gnorm, chunk=256):
    b, s, _ = z3.shape
    nc = s // chunk
    kern = functools.partial(_mlstm_kernel, chunk=chunk)
    qk = MLSTM_QK_DIM
    dv = MLSTM_V_DIM
    return pl.pallas_call(
        kern,
        grid=(b, MLSTM_HEADS, nc),
        in_specs=[
            pl.BlockSpec(memory_space=pltpu.SMEM),
            pl.BlockSpec((1, chunk, qk), lambda bi, h, c: (bi, c, COL_MQ // qk + h)),
            pl.BlockSpec((1, chunk, qk), lambda bi, h, c: (bi, c, COL_MK // qk + h)),
            pl.BlockSpec((1, chunk, dv), lambda bi, h, c: (bi, c, COL_MV // dv + h)),
            pl.BlockSpec((1, chunk, dv), lambda bi, h, c: (bi, c, COL_MO // dv + h)),
            pl.BlockSpec((chunk, LANES), lambda bi, h, c: (bi * nc + c, 0)),
            pl.BlockSpec((1, SUBLANES, chunk), lambda bi, h, c: (bi, 0, c)),
            pl.BlockSpec((SUBLANES, qk), lambda bi, h, c: (0, h)),
            pl.BlockSpec((SUBLANES, qk), lambda bi, h, c: (0, MLSTM_HEADS + h)),
            pl.BlockSpec((1, qk), lambda bi, h, c: (0, h)),
            pl.BlockSpec((1, qk), lambda bi, h, c: (0, MLSTM_HEADS + h)),
            pl.BlockSpec((1, dv), lambda bi, h, c: (0, h)),
        ],
        out_specs=pl.BlockSpec((1, chunk, dv), lambda bi, h, c: (bi, c, h)),
        out_shape=jax.ShapeDtypeStruct((b, s, MLSTM_WIDTH), BF16),
        scratch_shapes=[
            pltpu.VMEM((chunk + 2 * SUBLANES, qk), F32),
            pltpu.VMEM((chunk + 2 * SUBLANES, qk), F32),
            pltpu.VMEM((qk, dv), F32),
            pltpu.VMEM((1, qk), F32),
            pltpu.VMEM((1, LANES), F32),
        ],
        compiler_params=pltpu.CompilerParams(
            dimension_semantics=("parallel", "parallel", "arbitrary"),
            vmem_limit_bytes=VMEM_LIMIT),
        name="mlstm",
    )(gate_bias, z3, z3, z3, z3, gates, gates_t, conv_w8, conv_w8, conv_b, conv_b, gnorm)


def _merge_kernel(att_ref, hm_ref, ga_ref, gm_ref, x_ref, wa_ref, wm_ref, wo_ref, o_ref):
    @pl.when(pl.program_id(1) == 0)
    def _():
        o_ref[...] = x_ref[...]

    a = _dot(att_ref[...], wa_ref[...])
    bm = _dot(hm_ref[...], wm_ref[...])
    y = _sigmoid(ga_ref[...].astype(F32)) * a + _sigmoid(gm_ref[...].astype(F32)) * bm
    o_ref[...] += _dot(y.astype(BF16), wo_ref[...])


def _merge(att2d, hm2d, z2d, x2d, wa, wm, wo, tm=512, tn=1024):
    t = x2d.shape[0]
    nj = D_MODEL // tn
    return pl.pallas_call(
        _merge_kernel,
        grid=(t // tm, nj),
        in_specs=[
            pl.BlockSpec((tm, ATTN_WIDTH), lambda i, j: (i, 0)),
            pl.BlockSpec((tm, MLSTM_WIDTH), lambda i, j: (i, 0)),
            pl.BlockSpec((tm, tn), lambda i, j: (i, COL_GA // tn + j)),
            pl.BlockSpec((tm, tn), lambda i, j: (i, COL_GM // tn + j)),
            pl.BlockSpec((tm, D_MODEL), lambda i, j: (i, 0)),
            pl.BlockSpec((ATTN_WIDTH, tn), lambda i, j: (0, j)),
            pl.BlockSpec((MLSTM_WIDTH, tn), lambda i, j: (0, j)),
            pl.BlockSpec((tn, D_MODEL), lambda i, j: (j, 0)),
        ],
        out_specs=pl.BlockSpec((tm, D_MODEL), lambda i, j: (i, 0)),
        out_shape=jax.ShapeDtypeStruct((t, D_MODEL), F32),
        compiler_params=pltpu.CompilerParams(
            dimension_semantics=("parallel", "arbitrary"), vmem_limit_bytes=VMEM_LIMIT),
        name="merge",
    )(att2d, hm2d, z2d, z2d, x2d, wa, wm, wo)


def _cross_kernel(x_ref, g_ref, wq_ref, kv_ref, wo_ref, o_ref):
    x = x_ref[...]
    hc = _rms(x, g_ref[...]).astype(BF16)
    cq = (_dot(hc, wq_ref[...]) * (CROSS_HEAD_DIM ** -0.5)).astype(BF16)
    outs = []
    for hh in range(CROSS_HEADS):
        lo = hh * CROSS_HEAD_DIM
        qh = cq[:, lo:lo + CROSS_HEAD_DIM]
        kh = kv_ref[0, :, lo:lo + CROSS_HEAD_DIM]
        vh = kv_ref[0, :, CROSS_WIDTH + lo:CROSS_WIDTH + lo + CROSS_HEAD_DIM]
        s = _dot_nt(qh, kh)
        p = jnp.exp(s - jnp.max(s, axis=-1, keepdims=True))
        l = jnp.sum(p, axis=-1, keepdims=True)
        outs.append((_dot(p.astype(BF16), vh) / l).astype(BF16))
    co = jnp.concatenate(outs, axis=1)
    o_ref[...] = x + _dot(co, wo_ref[...])


def _cross(x2d, g, wq, ckv, wo, seq, tm=512):
    t = x2d.shape[0]
    n_mem = ckv.shape[1]
    per_batch = seq // tm
    return pl.pallas_call(
        _cross_kernel,
        grid=(t // tm,),
        in_specs=[
            pl.BlockSpec((tm, D_MODEL), lambda i: (i, 0)),
            pl.BlockSpec((1, D_MODEL), lambda i: (0, 0)),
            pl.BlockSpec((D_MODEL, CROSS_WIDTH), lambda i: (0, 0)),
            pl.BlockSpec((1, n_mem, 2 * CROSS_WIDTH), lambda i: (i // per_batch, 0, 0)),
            pl.BlockSpec((CROSS_WIDTH, D_MODEL), lambda i: (0, 0)),
        ],
        out_specs=pl.BlockSpec((tm, D_MODEL), lambda i: (i, 0)),
        out_shape=jax.ShapeDtypeStruct((t, D_MODEL), F32),
        compiler_params=pltpu.CompilerParams(
            dimension_semantics=("parallel",), vmem_limit_bytes=VMEM_LIMIT),
        name="cross",
    )(x2d, g, wq, ckv, wo)


def _mlp_kernel(x_ref, g_ref, wu_ref, wd_ref, gf_ref, o_ref, h_scr, acc, *, final_norm):
    j = pl.program_id(1)

    @pl.when(j == 0)
    def _():
        x = x_ref[...]
        h_scr[...] = _rms(x, g_ref[...]).astype(BF16)
        acc[...] = x

    u = jnp.square(jnp.maximum(_dot(h_scr[...], wu_ref[...]), 0.0)).astype(BF16)
    acc[...] += _dot(u, wd_ref[...])

    @pl.when(j == pl.num_programs(1) - 1)
    def _():
        if final_norm:
            o_ref[...] = _rms(acc[...], gf_ref[...])
        else:
            o_ref[...] = acc[...]


def _mlp(x2d, g, wu, wd, gf, final_norm, tm=512, tf=1024):
    t = x2d.shape[0]
    kern = functools.partial(_mlp_kernel, final_norm=final_norm)
    return pl.pallas_call(
        kern,
        grid=(t // tm, D_FF // tf),
        in_specs=[
            pl.BlockSpec((tm, D_MODEL), lambda i, j: (i, 0)),
            pl.BlockSpec((1, D_MODEL), lambda i, j: (0, 0)),
            pl.BlockSpec((D_MODEL, tf), lambda i, j: (0, j)),
            pl.BlockSpec((tf, D_MODEL), lambda i, j: (j, 0)),
            pl.BlockSpec((1, D_MODEL), lambda i, j: (0, 0)),
        ],
        out_specs=pl.BlockSpec((tm, D_MODEL), lambda i, j: (i, 0)),
        out_shape=jax.ShapeDtypeStruct((t, D_MODEL), F32),
        scratch_shapes=[pltpu.VMEM((tm, D_MODEL), BF16), pltpu.VMEM((tm, D_MODEL), F32)],
        compiler_params=pltpu.CompilerParams(
            dimension_semantics=("parallel", "arbitrary"), vmem_limit_bytes=VMEM_LIMIT),
        name="mlp",
    )(x2d, g, wu, wd, gf)


def _alibi_terms():
    slopes = 2.0 ** (-8.0 * np.arange(1, ATTN_HEADS + 1, dtype=np.float64) / ATTN_HEADS)
    rem = slopes * LOG2E
    terms = []
    for _ in range(ALIBI_TERMS):
        t = rem.astype(np.float32).astype(ml_dtypes.bfloat16).astype(np.float64)
        terms.append(t)
        rem = rem - t
    return np.stack(terms, axis=1).reshape(-1).astype(np.float32)


def _in_col_scale():
    scale = np.ones((GATE_LO, 1), np.float32)
    scale[COL_AQ:COL_AK] = ATTN_QK_DIM ** -0.5 * LOG2E
    return scale


def _pad_rows(a, rows):
    return jnp.pad(a, ((0, rows - a.shape[0]), (0, 0)))


def kernel(x, mem, norm_mix, w_in, b_igate, b_fgate, conv_w, conv_b, lam_q1, lam_k1, lam_q2, lam_k2, attn_norm, mlstm_norm, w_attn_br, w_mlstm_br, w_out, norm_cross, norm_mem, w_cq, w_ckv, w_co, norm_mlp, w_up, w_down, norm_final):
    b, s, _ = x.shape
    t = b * s
    tq = 256
    cs_terms = jnp.asarray(_alibi_terms())
    x2d = x.reshape(t, D_MODEL)
    for l in range(DEPTH):
        lam_init = 0.8 - 0.6 * math.exp(-0.3 * l)
        w = w_in[l]
        wt = w.T
        w_a = (wt[:GATE_LO] * jnp.asarray(_in_col_scale())).astype(BF16)
        w_b = wt[GATE_HI:].astype(BF16)
        w_gate = _pad_rows(wt[GATE_LO:GATE_HI], LANES).astype(BF16)

        z2d, gates = _inproj(x2d, norm_mix[l][None, :], w_a, w_b, w_gate)
        z3 = z2d.reshape(b, s, IN_MAIN)

        vt5 = (z3[:, :, COL_AV:COL_AV + ATTN_WIDTH]
               .reshape(b, s // tq, tq, ATTN_HEADS, ATTN_V_DIM).transpose(0, 3, 1, 4, 2))
        lamv = _pad_rows(jnp.pad(jnp.stack([lam_q1[l], lam_k1[l], lam_q2[l], lam_k2[l]]),
                                 ((0, 0), (0, LANES - ATTN_QK_DIM))), SUBLANES)
        att = _attn(z3, vt5, cs_terms, lamv, attn_norm[l][None, :], lam_init, tq=tq)

        gates_t = gates[:, :SUBLANES].reshape(b, s, SUBLANES).transpose(0, 2, 1)
        gate_bias = jnp.concatenate([b_igate[l], b_fgate[l]])
        hm = _mlstm(z3, gates, gates_t, gate_bias, _pad_rows(conv_w[l], SUBLANES),
                    conv_b[l][None, :], mlstm_norm[l][None, :])

        x2d = _merge(att.reshape(t, ATTN_WIDTH), hm.reshape(t, MLSTM_WIDTH), z2d, x2d,
                     w_attn_br[l].astype(BF16), w_mlstm_br[l].astype(BF16), w_out[l].astype(BF16))

        ckv = _memkv(mem, norm_mem[l][None, :], w_ckv[l].astype(BF16))
        x2d = _cross(x2d, norm_cross[l][None, :], w_cq[l].astype(BF16), ckv,
                     w_co[l].astype(BF16), s)

        x2d = _mlp(x2d, norm_mlp[l][None, :], w_up[l].astype(BF16), w_down[l].astype(BF16),
                   norm_final[None, :], final_norm=(l == DEPTH - 1))
    return x2d.reshape(b, s, D_MODEL)
```

```python
import functools
import math

import ml_dtypes
import numpy as np
import jax
import jax.numpy as jnp
from jax import lax
from jax.experimental import pallas as pl
from jax.experimental.pallas import tpu as pltpu

F32 = jnp.float32
BF16 = jnp.bfloat16

D_MODEL = 2048
DEPTH = 1
ATTN_HEADS = 8
ATTN_QK_DIM = 64
ATTN_V_DIM = 128
ATTN_WIDTH = ATTN_HEADS * ATTN_V_DIM
MLSTM_HEADS = 4
MLSTM_QK_DIM = 128
MLSTM_V_DIM = 256
MLSTM_QK_WIDTH = MLSTM_HEADS * MLSTM_QK_DIM
MLSTM_WIDTH = MLSTM_HEADS * MLSTM_V_DIM
CONV_WIDTH = 4
CROSS_HEADS = 4
CROSS_HEAD_DIM = 128
CROSS_WIDTH = CROSS_HEADS * CROSS_HEAD_DIM
D_FF = 4 * D_MODEL
EPS = 1e-6
LANES = 128
SUBLANES = 8
NEG_BIG = -1e30
LOG2E = 1.4426950408889634
ALIBI_TERMS = 3

COL_AQ = 0
COL_AK = 1024
COL_AV = 2048
COL_MQ = 3072
COL_MK = 3584
COL_MV = 4096
COL_MO = 5120
COL_GA = 6144
COL_GM = 8192
IN_MAIN = 10240
GATE_LO = 6144
GATE_HI = 6152

VMEM_LIMIT = 56 * 1024 * 1024


def _rms(x, g):
    ms = jnp.mean(x * x, axis=-1, keepdims=True)
    return x * lax.rsqrt(ms + EPS) * g


def _sigmoid(x):
    return 1.0 / (1.0 + jnp.exp(-x))


def _log_sigmoid(x):
    return jnp.minimum(x, 0.0) - jnp.log(1.0 + jnp.exp(-jnp.abs(x)))


def _dot(a, b):
    return jnp.dot(a, b, preferred_element_type=F32)


def _dot_nt(a, b):
    return lax.dot_general(a, b, (((1,), (1,)), ((), ())), preferred_element_type=F32)


def _memkv_kernel(mem_ref, g_ref, w_ref, o_ref):
    mn = _rms(mem_ref[0], g_ref[...]).astype(BF16)
    o_ref[0] = _dot(mn, w_ref[...]).astype(BF16)


def _memkv(mem, g, w_ckv):
    b, n_mem, _ = mem.shape
    return pl.pallas_call(
        _memkv_kernel,
        grid=(b,),
        in_specs=[
            pl.BlockSpec((1, n_mem, D_MODEL), lambda i: (i, 0, 0)),
            pl.BlockSpec((1, D_MODEL), lambda i: (0, 0)),
            pl.BlockSpec((D_MODEL, 2 * CROSS_WIDTH), lambda i: (0, 0)),
        ],
        out_specs=pl.BlockSpec((1, n_mem, 2 * CROSS_WIDTH), lambda i: (i, 0, 0)),
        out_shape=jax.ShapeDtypeStruct((b, n_mem, 2 * CROSS_WIDTH), BF16),
        compiler_params=pltpu.CompilerParams(
            dimension_semantics=("arbitrary",), vmem_limit_bytes=VMEM_LIMIT),
        name="memkv",
    )(mem, g, w_ckv)


def _inproj_kernel(x_ref, g_ref, wa_ref, wb_ref, wg_ref, z_ref, gate_ref, h_scr, *, na):
    j = pl.program_id(1)

    @pl.when(j == 0)
    def _():
        h = _rms(x_ref[...], g_ref[...]).astype(BF16)
        h_scr[...] = h
        gate_ref[...] = _dot_nt(h, wg_ref[...])

    @pl.when(j < na)
    def _():
        z_ref[...] = _dot_nt(h_scr[...], wa_ref[...]).astype(BF16)

    @pl.when(j >= na)
    def _():
        z_ref[...] = _dot_nt(h_scr[...], wb_ref[...]).astype(BF16)


def _inproj(x2d, g, w_a, w_b, w_gate, tm=1024, tn=1024):
    t = x2d.shape[0]
    na = w_a.shape[0] // tn
    n = w_a.shape[0] + w_b.shape[0]
    return pl.pallas_call(
        functools.partial(_inproj_kernel, na=na),
        grid=(t // tm, n // tn),
        in_specs=[
            pl.BlockSpec((tm, D_MODEL), lambda i, j: (i, 0)),
            pl.BlockSpec((1, D_MODEL), lambda i, j: (0, 0)),
            pl.BlockSpec((tn, D_MODEL), lambda i, j: (jnp.minimum(j, na - 1), 0)),
            pl.BlockSpec((tn, D_MODEL), lambda i, j: (jnp.maximum(j - na, 0), 0)),
            pl.BlockSpec((LANES, D_MODEL), lambda i, j: (0, 0)),
        ],
        out_specs=[
            pl.BlockSpec((tm, tn), lambda i, j: (i, j)),
            pl.BlockSpec((tm, LANES), lambda i, j: (i, 0)),
        ],
        out_shape=[
            jax.ShapeDtypeStruct((t, n), BF16),
            jax.ShapeDtypeStruct((t, LANES), F32),
        ],
        scratch_shapes=[pltpu.VMEM((tm, D_MODEL), BF16)],
        compiler_params=pltpu.CompilerParams(
            dimension_semantics=("parallel", "arbitrary"), vmem_limit_bytes=VMEM_LIMIT),
        name="inproj",
    )(x2d, g, w_a, w_b, w_gate)


def _attn_kernel(cs_ref, lam_ref, q_ref, k_ref, vt_ref, gain_ref, o_ref,
                 qs_ref, kf_ref, acc_ref, m_ref, *, tq, lam_init):
    tk = tq
    nfeat = ALIBI_TERMS
    dq = 2 * ATTN_QK_DIM
    dv = ATTN_V_DIM
    i = pl.program_id(1)
    cs = [[cs_ref[nfeat * h + t] for t in range(nfeat)] for h in range(ATTN_HEADS)]
    cs_tot = [sum(c[1:], c[0]) for c in cs]

    @pl.when(i == 0)
    def _():
        klane = lax.broadcasted_iota(jnp.int32, (tk, LANES), 1)
        krow = lax.broadcasted_iota(jnp.int32, (tk, LANES), 0).astype(F32)
        qlane = lax.broadcasted_iota(jnp.int32, (2 * tq, LANES), 1)
        qrow = lax.broadcasted_iota(jnp.int32, (2 * tq, LANES), 0)
        qrow = jnp.where(qrow >= tq, qrow - tq, qrow).astype(F32)
        for h in range(ATTN_HEADS):
            kf = jnp.where(klane < nfeat, krow, 0.0)
            qf = jnp.where((qlane >= nfeat) & (qlane < 2 * nfeat), -qrow, 0.0)
            for t in range(nfeat):
                kf = jnp.where(klane == nfeat + t, cs[h][t], kf)
                qf = jnp.where(qlane == t, cs[h][t], qf)
            kf_ref[h] = kf.astype(BF16)
            qs_ref[h, :, pl.ds(dq, LANES)] = qf.astype(BF16)

    lane = lax.broadcasted_iota(jnp.int32, (tq, dq), 1)
    for h in range(ATTN_HEADS):
        q = q_ref[0, :, pl.ds(h * dq, dq)]
        zero = jnp.zeros_like(q)
        qs_ref[h, pl.ds(0, tq), pl.ds(0, dq)] = jnp.where(lane < ATTN_QK_DIM, q, zero)
        qs_ref[h, pl.ds(tq, tq), pl.ds(0, dq)] = jnp.where(lane >= ATTN_QK_DIM, q, zero)
    acc_ref[...] = jnp.zeros_like(acc_ref)
    m_ref[...] = jnp.full_like(m_ref, NEG_BIG)

    orow = lax.broadcasted_iota(jnp.int32, (2 * SUBLANES, tk), 0)
    ones_blk = jnp.where(orow == 0, 1.0, 0.0).astype(BF16)

    def scores(h, j):
        k = k_ref[0, pl.ds(pl.multiple_of(j * tk, tk), tk), pl.ds(h * dq, dq)]
        kx = jnp.concatenate([k, kf_ref[h]], axis=1)
        return _dot_nt(kx, qs_ref[h])

    def softmax_pv(h, j, t, diagonal):
        if diagonal:
            krow = lax.broadcasted_iota(jnp.int32, (tk, 2 * tq), 0)
            qcol = lax.broadcasted_iota(jnp.int32, (tk, 2 * tq), 1)
            qcol = jnp.where(qcol >= tq, qcol - tq, qcol)
            t = jnp.where(krow <= qcol, t, NEG_BIG)
        soff = cs_tot[h] * ((j - i) * tk).astype(F32)
        m = m_ref[h]
        m_new = jnp.maximum(m, jnp.max(t, axis=0, keepdims=True) + soff)
        alpha = jnp.exp2(m - m_new)
        p = jnp.exp2(t - (m_new - soff)).astype(BF16)
        vx = jnp.concatenate([vt_ref[0, h, j], ones_blk], axis=0)
        acc_ref[h] = alpha * acc_ref[h] + _dot(vx, p)
        m_ref[h] = m_new

    def run(units):
        t_next = scores(*units[0][:2])
        for n, (h, j, diagonal) in enumerate(units):
            t = t_next
            if n + 1 < len(units):
                t_next = scores(*units[n + 1][:2])
            softmax_pv(h, j, t, diagonal)

    def tile_units(j, diagonal):
        return [(h, j, diagonal) for h in range(ATTN_HEADS)]

    def body(jj, carry):
        run(tile_units(2 * jj, False) + tile_units(2 * jj + 1, False))
        return carry

    lax.fori_loop(0, i // 2, body, 0)

    @pl.when(i % 2 == 0)
    def _():
        run(tile_units(i, True))

    @pl.when(i % 2 == 1)
    def _():
        run(tile_units(i - 1, False) + tile_units(i, True))

    lv = lam_ref[...]
    d1 = jnp.sum(lv[0:1] * lv[1:2], axis=-1, keepdims=True)
    d2 = jnp.sum(lv[2:3] * lv[3:4], axis=-1, keepdims=True)
    lam = jnp.exp(d1) - jnp.exp(d2) + lam_init
    for h in range(ATTN_HEADS):
        l = acc_ref[h, pl.ds(dv, 1), :]
        out = (acc_ref[h, pl.ds(0, dv), pl.ds(0, tq)] / l[:, :tq]
               - lam * (acc_ref[h, pl.ds(0, dv), pl.ds(tq, tq)] / l[:, tq:]))
        ms = jnp.mean(out * out, axis=0, keepdims=True)
        on = out * lax.rsqrt(ms + EPS)
        o_ref[0, :, pl.ds(h * dv, dv)] = (on.T * gain_ref[...] * (1.0 - lam_init)).astype(BF16)


def _attn(z3, vt5, cs_terms, lamv, gain, lam_init, tq=256):
    b, s, _ = z3.shape
    nq = s // tq
    kern = functools.partial(_attn_kernel, tq=tq, lam_init=lam_init)
    width = ATTN_HEADS * 2 * ATTN_QK_DIM
    return pl.pallas_call(
        kern,
        grid=(b, nq),
        in_specs=[
            pl.BlockSpec(memory_space=pltpu.SMEM),
            pl.BlockSpec((SUBLANES, LANES), lambda bi, i: (0, 0)),
            pl.BlockSpec((1, tq, width), lambda bi, i: (bi, i, COL_AQ // width)),
            pl.BlockSpec((1, s, width), lambda bi, i: (bi, 0, COL_AK // width)),
            pl.BlockSpec((1, ATTN_HEADS, nq, ATTN_V_DIM, tq), lambda bi, i: (bi, 0, 0, 0, 0)),
            pl.BlockSpec((1, ATTN_V_DIM), lambda bi, i: (0, 0)),
        ],
        out_specs=pl.BlockSpec((1, tq, ATTN_WIDTH), lambda bi, i: (bi, i, 0)),
        out_shape=jax.ShapeDtypeStruct((b, s, ATTN_WIDTH), BF16),
        scratch_shapes=[
            pltpu.VMEM((ATTN_HEADS, 2 * tq, 2 * ATTN_QK_DIM + LANES), BF16),
            pltpu.VMEM((ATTN_HEADS, tq, LANES), BF16),
            pltpu.VMEM((ATTN_HEADS, ATTN_V_DIM + 2 * SUBLANES, 2 * tq), F32),
            pltpu.VMEM((ATTN_HEADS, 1, 2 * tq), F32),
        ],
        compiler_params=pltpu.CompilerParams(
            dimension_semantics=("parallel", "arbitrary"),
            vmem_limit_bytes=VMEM_LIMIT),
        name="attn",
    )(cs_terms, lamv, z3, z3, vt5, gain)


def _split3(x):
    hi = x.astype(BF16)
    r = x - hi.astype(F32)
    mid = r.astype(BF16)
    lo = (r - mid.astype(F32)).astype(BF16)
    return hi, mid, lo


def _mlstm_kernel(uq_ref, uk_ref, v_ref, mo_ref, gcol_ref, grow_ref, bcol_ref, brow_ref,
                  cw_ref, cb_ref, gn_ref, o_ref, extq, extk, c_scr, n_scr, m_scr, *, chunk):
    L = chunk
    nh = MLSTM_HEADS
    dk = MLSTM_QK_DIM
    dv = MLSTM_V_DIM
    heads = range(nh)

    @pl.when(pl.program_id(1) == 0)
    def _():
        extq[pl.ds(0, SUBLANES), :] = jnp.zeros((SUBLANES, MLSTM_QK_WIDTH), F32)
        extk[pl.ds(0, SUBLANES), :] = jnp.zeros((SUBLANES, MLSTM_QK_WIDTH), F32)
        c_scr[...] = jnp.zeros_like(c_scr)
        n_scr[...] = jnp.zeros_like(n_scr)
        m_scr[...] = jnp.zeros_like(m_scr)

    def conv_silu(u_ref, ext, col0):
        ext[pl.ds(SUBLANES, L), :] = u_ref[0].astype(F32)
        w = cw_ref[:, pl.ds(col0, MLSTM_QK_WIDTH)]
        y = cb_ref[:, pl.ds(col0, MLSTM_QK_WIDTH)]
        for tap in range(CONV_WIDTH):
            off = SUBLANES - (CONV_WIDTH - 1) + tap
            y = y + ext[pl.ds(off, L), :] * w[tap:tap + 1]
        ext[pl.ds(0, SUBLANES), :] = ext[pl.ds(L, SUBLANES), :]
        return y * _sigmoid(y)

    q_all = conv_silu(uq_ref, extq, 0)
    k_all = conv_silu(uk_ref, extk, MLSTM_QK_WIDTH) * (dk ** -0.5)
    q = [q_all[:, h * dk:(h + 1) * dk] for h in heads]
    k = [k_all[:, h * dk:(h + 1) * dk] for h in heads]
    qb = [x.astype(BF16) for x in q]
    kb = [x.astype(BF16) for x in k]
    vb = [v_ref[0, :, pl.ds(h * dv, dv)] for h in heads]

    g_c = gcol_ref[...] + bcol_ref[...]
    g_r = grow_ref[0] + brow_ref[:, 0:1]
    lf_c = _log_sigmoid(g_c)
    lf_r = _log_sigmoid(g_r)

    r_i = lax.broadcasted_iota(jnp.int32, (L, L), 0)
    c_i = lax.broadcasted_iota(jnp.int32, (L, L), 1)
    causal = c_i <= r_i
    tri = jnp.where(causal, 1.0, 0.0).astype(BF16)
    tri_t = jnp.where(r_i <= c_i, 1.0, 0.0).astype(BF16)
    hi, mid, lo = _split3(lf_c)
    b_c = _dot(tri, hi) + _dot(tri, mid) + _dot(tri, lo)
    hi, mid, lo = _split3(lf_r)
    b_r = _dot(hi, tri_t) + _dot(mid, tri_t) + _dot(lo, tri_t)
    g_sum = jnp.sum(lf_r, axis=1, keepdims=True)

    lane = lax.broadcasted_iota(jnp.int32, (L, LANES), 1)

    def col(x, idx):
        return jnp.sum(jnp.where(lane == idx, x, 0.0), axis=1, keepdims=True)

    i_col = [col(g_c, h) for h in heads]
    b_col = [col(b_c, nh + h) for h in heads]
    i_row = [g_r[h:h + 1, :] for h in heads]
    b_row = [b_r[nh + h:nh + h + 1, :] for h in heads]
    g_tot = [g_sum[nh + h:nh + h + 1, :] for h in heads]

    c_prev = [c_scr[h] for h in heads]
    n_prev = [n_scr[h] for h in heads]
    m_prev = [m_scr[h][:, 0:1] for h in heads]

    s_qk = [_dot_nt(qb[h], kb[h]) for h in heads]
    inter = [_dot(qb[h], c_prev[h].astype(BF16)) for h in heads]
    m_loc = [jnp.max(g_tot[h] - b_row[h] + i_row[h], axis=1, keepdims=True) for h in heads]
    kw = [k[h] * jnp.exp(g_tot[h] - b_col[h] + i_col[h] - m_loc[h]) for h in heads]
    c_loc = [_dot(kw[h].T.astype(BF16), vb[h]) for h in heads]

    for h in heads:
        d = jnp.where(causal, b_col[h] - b_row[h] + i_row[h], NEG_BIG)
        m_inter = b_col[h] + m_prev[h]
        m_j = jnp.maximum(m_inter, jnp.max(d, axis=1, keepdims=True))
        w_inter = jnp.exp(m_inter - m_j)
        p = s_qk[h] * jnp.exp(d - m_j)
        num = w_inter * inter[h] + _dot(p.astype(BF16), vb[h])
        den = (w_inter * jnp.sum(q[h] * n_prev[h], axis=1, keepdims=True)
               + jnp.sum(p, axis=1, keepdims=True))
        hval = num / jnp.maximum(jnp.abs(den), jnp.exp(-m_j))
        hn = _rms(hval, gn_ref[:, pl.ds(h * dv, dv)])
        gate = _sigmoid(mo_ref[0, :, pl.ds(h * dv, dv)].astype(F32))
        o_ref[0, :, pl.ds(h * dv, dv)] = (hn * gate).astype(BF16)

    for h in heads:
        n_loc = jnp.sum(kw[h], axis=0, keepdims=True)
        m_new = jnp.maximum(g_tot[h] + m_prev[h], m_loc[h])
        a_old = jnp.exp(g_tot[h] + m_prev[h] - m_new)
        a_new = jnp.exp(m_loc[h] - m_new)
        c_scr[h] = a_old * c_prev[h] + a_new * c_loc[h]
        n_scr[h] = a_old * n_prev[h] + a_new * n_loc
        m_scr[h] = jnp.broadcast_to(m_new, (1, LANES))


def _mlstm(z3, gates, gates_t, bias_col, bias_row, conv_w8, conv_b, gnorm, chunk=256):
    b, s, _ = z3.shape
    nc = s // chunk
    kern = functools.partial(_mlstm_kernel, chunk=chunk)
    qw = MLSTM_QK_WIDTH
    vw = MLSTM_WIDTH
    return pl.pallas_call(
        kern,
        grid=(b, nc),
        in_specs=[
            pl.BlockSpec((1, chunk, qw), lambda bi, c: (bi, c, COL_MQ // qw)),
            pl.BlockSpec((1, chunk, qw), lambda bi, c: (bi, c, COL_MK // qw)),
            pl.BlockSpec((1, chunk, vw), lambda bi, c: (bi, c, COL_MV // vw)),
            pl.BlockSpec((1, chunk, vw), lambda bi, c: (bi, c, COL_MO // vw)),
            pl.BlockSpec((chunk, LANES), lambda bi, c: (bi * nc + c, 0)),
            pl.BlockSpec((1, SUBLANES, chunk), lambda bi, c: (bi, 0, c)),
            pl.BlockSpec((1, LANES), lambda bi, c: (0, 0)),
            pl.BlockSpec((SUBLANES, LANES), lambda bi, c: (0, 0)),
            pl.BlockSpec((SUBLANES, 2 * qw), lambda bi, c: (0, 0)),
            pl.BlockSpec((1, 2 * qw), lambda bi, c: (0, 0)),
            pl.BlockSpec((1, vw), lambda bi, c: (0, 0)),
        ],
        out_specs=pl.BlockSpec((1, chunk, vw), lambda bi, c: (bi, c, 0)),
        out_shape=jax.ShapeDtypeStruct((b, s, vw), BF16),
        scratch_shapes=[
            pltpu.VMEM((chunk + 2 * SUBLANES, qw), F32),
            pltpu.VMEM((chunk + 2 * SUBLANES, qw), F32),
            pltpu.VMEM((MLSTM_HEADS, MLSTM_QK_DIM, MLSTM_V_DIM), F32),
            pltpu.VMEM((MLSTM_HEADS, 1, MLSTM_QK_DIM), F32),
            pltpu.VMEM((MLSTM_HEADS, 1, LANES), F32),
        ],
        compiler_params=pltpu.CompilerParams(
            dimension_semantics=("parallel", "arbitrary"),
            vmem_limit_bytes=VMEM_LIMIT),
        name="mlstm",
    )(z3, z3, z3, z3, gates, gates_t, bias_col, bias_row, conv_w8, conv_b, gnorm)


def _merge_kernel(att_ref, hm_ref, ga_ref, gm_ref, x_ref, wa_ref, wm_ref, wo_ref, o_ref):
    @pl.when(pl.program_id(1) == 0)
    def _():
        o_ref[...] = x_ref[...]

    a = _dot(att_ref[...], wa_ref[...])
    bm = _dot(hm_ref[...], wm_ref[...])
    y = _sigmoid(ga_ref[...].astype(F32)) * a + _sigmoid(gm_ref[...].astype(F32)) * bm
    o_ref[...] += _dot(y.astype(BF16), wo_ref[...])


def _merge(att2d, hm2d, z2d, x2d, wa, wm, wo, tm=512, tn=1024):
    t = x2d.shape[0]
    nj = D_MODEL // tn
    return pl.pallas_call(
        _merge_kernel,
        grid=(t // tm, nj),
        in_specs=[
            pl.BlockSpec((tm, ATTN_WIDTH), lambda i, j: (i, 0)),
            pl.BlockSpec((tm, MLSTM_WIDTH), lambda i, j: (i, 0)),
            pl.BlockSpec((tm, tn), lambda i, j: (i, COL_GA // tn + j)),
            pl.BlockSpec((tm, tn), lambda i, j: (i, COL_GM // tn + j)),
            pl.BlockSpec((tm, D_MODEL), lambda i, j: (i, 0)),
            pl.BlockSpec((ATTN_WIDTH, tn), lambda i, j: (0, j)),
            pl.BlockSpec((MLSTM_WIDTH, tn), lambda i, j: (0, j)),
            pl.BlockSpec((tn, D_MODEL), lambda i, j: (j, 0)),
        ],
        out_specs=pl.BlockSpec((tm, D_MODEL), lambda i, j: (i, 0)),
        out_shape=jax.ShapeDtypeStruct((t, D_MODEL), F32),
        compiler_params=pltpu.CompilerParams(
            dimension_semantics=("parallel", "arbitrary"), vmem_limit_bytes=VMEM_LIMIT),
        name="merge",
    )(att2d, hm2d, z2d, z2d, x2d, wa, wm, wo)


def _cross_kernel(x_ref, g_ref, wq_ref, kv_ref, wo_ref, o_ref):
    x = x_ref[...]
    hc = _rms(x, g_ref[...]).astype(BF16)
    cq = (_dot(hc, wq_ref[...]) * (CROSS_HEAD_DIM ** -0.5)).astype(BF16)
    outs = []
    for hh in range(CROSS_HEADS):
        lo = hh * CROSS_HEAD_DIM
        qh = cq[:, lo:lo + CROSS_HEAD_DIM]
        kh = kv_ref[0, :, lo:lo + CROSS_HEAD_DIM]
        vh = kv_ref[0, :, CROSS_WIDTH + lo:CROSS_WIDTH + lo + CROSS_HEAD_DIM]
        s = _dot_nt(qh, kh)
        p = jnp.exp(s - jnp.max(s, axis=-1, keepdims=True))
        l = jnp.sum(p, axis=-1, keepdims=True)
        outs.append((_dot(p.astype(BF16), vh) / l).astype(BF16))
    co = jnp.concatenate(outs, axis=1)
    o_ref[...] = x + _dot(co, wo_ref[...])


def _cross(x2d, g, wq, ckv, wo, seq, tm=512):
    t = x2d.shape[0]
    n_mem = ckv.shape[1]
    per_batch = seq // tm
    return pl.pallas_call(
        _cross_kernel,
        grid=(t // tm,),
        in_specs=[
            pl.BlockSpec((tm, D_MODEL), lambda i: (i, 0)),
            pl.BlockSpec((1, D_MODEL), lambda i: (0, 0)),
            pl.BlockSpec((D_MODEL, CROSS_WIDTH), lambda i: (0, 0)),
            pl.BlockSpec((1, n_mem, 2 * CROSS_WIDTH), lambda i: (i // per_batch, 0, 0)),
            pl.BlockSpec((CROSS_WIDTH, D_MODEL), lambda i: (0, 0)),
        ],
        out_specs=pl.BlockSpec((tm, D_MODEL), lambda i: (i, 0)),
        out_shape=jax.ShapeDtypeStruct((t, D_MODEL), F32),
        compiler_params=pltpu.CompilerParams(
            dimension_semantics=("parallel",), vmem_limit_bytes=VMEM_LIMIT),
        name="cross",
    )(x2d, g, wq, ckv, wo)


def _mlp_kernel(x_ref, g_ref, wu_ref, wd_ref, gf_ref, o_ref, h_scr, acc, *, final_norm):
    j = pl.program_id(1)

    @pl.when(j == 0)
    def _():
        x = x_ref[...]
        h_scr[...] = _rms(x, g_ref[...]).astype(BF16)
        acc[...] = x

    u = jnp.square(jnp.maximum(_dot(h_scr[...], wu_ref[...]), 0.0)).astype(BF16)
    acc[...] += _dot(u, wd_ref[...])

    @pl.when(j == pl.num_programs(1) - 1)
    def _():
        if final_norm:
            o_ref[...] = _rms(acc[...], gf_ref[...])
        else:
            o_ref[...] = acc[...]


def _mlp(x2d, g, wu, wd, gf, final_norm, tm=512, tf=1024):
    t = x2d.shape[0]
    kern = functools.partial(_mlp_kernel, final_norm=final_norm)
    return pl.pallas_call(
        kern,
        grid=(t // tm, D_FF // tf),
        in_specs=[
            pl.BlockSpec((tm, D_MODEL), lambda i, j: (i, 0)),
            pl.BlockSpec((1, D_MODEL), lambda i, j: (0, 0)),
            pl.BlockSpec((D_MODEL, tf), lambda i, j: (0, j)),
            pl.BlockSpec((tf, D_MODEL), lambda i, j: (j, 0)),
            pl.BlockSpec((1, D_MODEL), lambda i, j: (0, 0)),
        ],
        out_specs=pl.BlockSpec((tm, D_MODEL), lambda i, j: (i, 0)),
        out_shape=jax.ShapeDtypeStruct((t, D_MODEL), F32),
        scratch_shapes=[pltpu.VMEM((tm, D_MODEL), BF16), pltpu.VMEM((tm, D_MODEL), F32)],
        compiler_params=pltpu.CompilerParams(
            dimension_semantics=("parallel", "arbitrary"), vmem_limit_bytes=VMEM_LIMIT),
        name="mlp",
    )(x2d, g, wu, wd, gf)


def _alibi_terms():
    slopes = 2.0 ** (-8.0 * np.arange(1, ATTN_HEADS + 1, dtype=np.float64) / ATTN_HEADS)
    rem = slopes * LOG2E
    terms = []
    for _ in range(ALIBI_TERMS):
        t = rem.astype(np.float32).astype(ml_dtypes.bfloat16).astype(np.float64)
        terms.append(t)
        rem = rem - t
    return np.stack(terms, axis=1).reshape(-1).astype(np.float32)


def _in_col_scale():
    scale = np.ones((GATE_LO, 1), np.float32)
    scale[COL_AQ:COL_AK] = ATTN_QK_DIM ** -0.5 * LOG2E
    return scale


def _pad_rows(a, rows):
    return jnp.pad(a, ((0, rows - a.shape[0]), (0, 0)))


def kernel(x, mem, norm_mix, w_in, b_igate, b_fgate, conv_w, conv_b, lam_q1, lam_k1, lam_q2, lam_k2, attn_norm, mlstm_norm, w_attn_br, w_mlstm_br, w_out, norm_cross, norm_mem, w_cq, w_ckv, w_co, norm_mlp, w_up, w_down, norm_final):
    b, s, _ = x.shape
    t = b * s
    tq = 256
    cs_terms = jnp.asarray(_alibi_terms())
    x2d = x.reshape(t, D_MODEL)
    for l in range(DEPTH):
        lam_init = 0.8 - 0.6 * math.exp(-0.3 * l)
        w = w_in[l]
        wt = w.T
        w_a = (wt[:GATE_LO] * jnp.asarray(_in_col_scale())).astype(BF16)
        w_b = wt[GATE_HI:].astype(BF16)
        w_gate = _pad_rows(wt[GATE_LO:GATE_HI], LANES).astype(BF16)

        z2d, gates = _inproj(x2d, norm_mix[l][None, :], w_a, w_b, w_gate)
        z3 = z2d.reshape(b, s, IN_MAIN)

        vt5 = (z3[:, :, COL_AV:COL_AV + ATTN_WIDTH]
               .reshape(b, s // tq, tq, ATTN_HEADS, ATTN_V_DIM).transpose(0, 3, 1, 4, 2))
        lamv = _pad_rows(jnp.pad(jnp.stack([lam_q1[l], lam_k1[l], lam_q2[l], lam_k2[l]]),
                                 ((0, 0), (0, LANES - ATTN_QK_DIM))), SUBLANES)
        att = _attn(z3, vt5, cs_terms, lamv, attn_norm[l][None, :], lam_init, tq=tq)

        gates_t = gates[:, :SUBLANES].reshape(b, s, SUBLANES).transpose(0, 2, 1)
        gate_bias = jnp.concatenate([b_igate[l], b_fgate[l]])
        bias_col = jnp.pad(gate_bias, (0, LANES - gate_bias.shape[0]))[None, :]
        bias_row = jnp.broadcast_to(gate_bias[:, None], (SUBLANES, LANES))
        hm = _mlstm(z3, gates, gates_t, bias_col, bias_row, _pad_rows(conv_w[l], SUBLANES),
                    conv_b[l][None, :], mlstm_norm[l][None, :])

        x2d = _merge(att.reshape(t, ATTN_WIDTH), hm.reshape(t, MLSTM_WIDTH), z2d, x2d,
                     w_attn_br[l].astype(BF16), w_mlstm_br[l].astype(BF16), w_out[l].astype(BF16))

        ckv = _memkv(mem, norm_mem[l][None, :], w_ckv[l].astype(BF16))
        x2d = _cross(x2d, norm_cross[l][None, :], w_cq[l].astype(BF16), ckv,
                     w_co[l].astype(BF16), s)

        x2d = _mlp(x2d, norm_mlp[l][None, :], w_up[l].astype(BF16), w_down[l].astype(BF16),
                   norm_final[None, :], final_norm=(l == DEPTH - 1))
    return x2d.reshape(b, s, D_MODEL)
```

```python
import functools
import math

import ml_dtypes
import numpy as np
import jax
import jax.numpy as jnp
from jax import lax
from jax.experimental import pallas as pl
from jax.experimental.pallas import tpu as pltpu

F32 = jnp.float32
BF16 = jnp.bfloat16

D_MODEL = 2048
DEPTH = 1
ATTN_HEADS = 8
ATTN_QK_DIM = 64
ATTN_V_DIM = 128
ATTN_WIDTH = ATTN_HEADS * ATTN_V_DIM
MLSTM_HEADS = 4
MLSTM_QK_DIM = 128
MLSTM_V_DIM = 256
MLSTM_QK_WIDTH = MLSTM_HEADS * MLSTM_QK_DIM
MLSTM_WIDTH = MLSTM_HEADS * MLSTM_V_DIM
CONV_WIDTH = 4
CROSS_HEADS = 4
CROSS_HEAD_DIM = 128
CROSS_WIDTH = CROSS_HEADS * CROSS_HEAD_DIM
D_FF = 4 * D_MODEL
EPS = 1e-6
LANES = 128
SUBLANES = 8
NEG_BIG = -1e30
LOG2E = 1.4426950408889634
ALIBI_TERMS = 3

COL_AQ = 0
COL_AK = 1024
COL_AV = 2048
COL_MQ = 3072
COL_MK = 3584
COL_MV = 4096
COL_MO = 5120
COL_GA = 6144
COL_GM = 8192
IN_MAIN = 10240
GATE_LO = 6144
GATE_HI = 6152

VMEM_LIMIT = 56 * 1024 * 1024


def _rms(x, g):
    ms = jnp.mean(x * x, axis=-1, keepdims=True)
    return x * lax.rsqrt(ms + EPS) * g


def _sigmoid(x):
    return 1.0 / (1.0 + jnp.exp(-x))


def _log_sigmoid(x):
    return jnp.minimum(x, 0.0) - jnp.log(1.0 + jnp.exp(-jnp.abs(x)))


def _dot(a, b):
    return jnp.dot(a, b, preferred_element_type=F32)


def _dot_nt(a, b):
    return lax.dot_general(a, b, (((1,), (1,)), ((), ())), preferred_element_type=F32)


def _memkv_kernel(mem_ref, g_ref, w_ref, o_ref):
    mn = _rms(mem_ref[0], g_ref[...]).astype(BF16)
    o_ref[0] = _dot(mn, w_ref[...]).astype(BF16)


def _memkv(mem, g, w_ckv):
    b, n_mem, _ = mem.shape
    return pl.pallas_call(
        _memkv_kernel,
        grid=(b,),
        in_specs=[
            pl.BlockSpec((1, n_mem, D_MODEL), lambda i: (i, 0, 0)),
            pl.BlockSpec((1, D_MODEL), lambda i: (0, 0)),
            pl.BlockSpec((D_MODEL, 2 * CROSS_WIDTH), lambda i: (0, 0)),
        ],
        out_specs=pl.BlockSpec((1, n_mem, 2 * CROSS_WIDTH), lambda i: (i, 0, 0)),
        out_shape=jax.ShapeDtypeStruct((b, n_mem, 2 * CROSS_WIDTH), BF16),
        compiler_params=pltpu.CompilerParams(
            dimension_semantics=("arbitrary",), vmem_limit_bytes=VMEM_LIMIT),
        name="memkv",
    )(mem, g, w_ckv)


def _inproj_kernel(x_ref, g_ref, wa_ref, wb_ref, wg_ref, z_ref, gate_ref, h_scr, *, na):
    j = pl.program_id(1)

    @pl.when(j == 0)
    def _():
        h = _rms(x_ref[...], g_ref[...]).astype(BF16)
        h_scr[...] = h
        gate_ref[...] = _dot_nt(h, wg_ref[...])

    @pl.when(j < na)
    def _():
        z_ref[...] = _dot_nt(h_scr[...], wa_ref[...]).astype(BF16)

    @pl.when(j >= na)
    def _():
        z_ref[...] = _dot_nt(h_scr[...], wb_ref[...]).astype(BF16)


def _inproj(x2d, g, w_a, w_b, w_gate, tm=1024, tn=1024):
    t = x2d.shape[0]
    na = w_a.shape[0] // tn
    n = w_a.shape[0] + w_b.shape[0]
    return pl.pallas_call(
        functools.partial(_inproj_kernel, na=na),
        grid=(t // tm, n // tn),
        in_specs=[
            pl.BlockSpec((tm, D_MODEL), lambda i, j: (i, 0)),
            pl.BlockSpec((1, D_MODEL), lambda i, j: (0, 0)),
            pl.BlockSpec((tn, D_MODEL), lambda i, j: (jnp.minimum(j, na - 1), 0)),
            pl.BlockSpec((tn, D_MODEL), lambda i, j: (jnp.maximum(j - na, 0), 0)),
            pl.BlockSpec((LANES, D_MODEL), lambda i, j: (0, 0)),
        ],
        out_specs=[
            pl.BlockSpec((tm, tn), lambda i, j: (i, j)),
            pl.BlockSpec((tm, LANES), lambda i, j: (i, 0)),
        ],
        out_shape=[
            jax.ShapeDtypeStruct((t, n), BF16),
            jax.ShapeDtypeStruct((t, LANES), F32),
        ],
        scratch_shapes=[pltpu.VMEM((tm, D_MODEL), BF16)],
        compiler_params=pltpu.CompilerParams(
            dimension_semantics=("parallel", "arbitrary"), vmem_limit_bytes=VMEM_LIMIT),
        name="inproj",
    )(x2d, g, w_a, w_b, w_gate)


def _attn_kernel(cs_ref, lam_ref, q_ref, k_ref, vt_ref, gain_ref, o_ref,
                 qs_ref, kf_ref, acc_ref, m_ref, *, tq, lam_init):
    tk = tq
    nfeat = ALIBI_TERMS
    dq = 2 * ATTN_QK_DIM
    dv = ATTN_V_DIM
    i = pl.program_id(1)
    cs = [[cs_ref[nfeat * h + t] for t in range(nfeat)] for h in range(ATTN_HEADS)]
    cs_tot = [sum(c[1:], c[0]) for c in cs]

    @pl.when(i == 0)
    def _():
        klane = lax.broadcasted_iota(jnp.int32, (tk, LANES), 1)
        krow = lax.broadcasted_iota(jnp.int32, (tk, LANES), 0).astype(F32)
        frow = lax.broadcasted_iota(jnp.int32, (LANES, 2 * tq), 0)
        fcol = lax.broadcasted_iota(jnp.int32, (LANES, 2 * tq), 1)
        fcol = jnp.where(fcol >= tq, fcol - tq, fcol).astype(F32)
        for h in range(ATTN_HEADS):
            kf = jnp.where(klane < nfeat, krow, 0.0)
            qf = jnp.where((frow >= nfeat) & (frow < 2 * nfeat), -fcol, 0.0)
            for t in range(nfeat):
                kf = jnp.where(klane == nfeat + t, cs[h][t], kf)
                qf = jnp.where(frow == t, cs[h][t], qf)
            kf_ref[h] = kf.astype(BF16)
            qs_ref[h, pl.ds(dq, LANES), :] = qf.astype(BF16)

    row = lax.broadcasted_iota(jnp.int32, (dq, tq), 0)
    for h in range(ATTN_HEADS):
        qt = q_ref[0, :, pl.ds(h * dq, dq)].astype(F32).T
        qs_ref[h, pl.ds(0, dq), pl.ds(0, tq)] = jnp.where(row < ATTN_QK_DIM, qt, 0.0).astype(BF16)
        qs_ref[h, pl.ds(0, dq), pl.ds(tq, tq)] = jnp.where(row >= ATTN_QK_DIM, qt, 0.0).astype(BF16)
    acc_ref[...] = jnp.zeros_like(acc_ref)
    m_ref[...] = jnp.full_like(m_ref, NEG_BIG)

    orow = lax.broadcasted_iota(jnp.int32, (2 * SUBLANES, tk), 0)
    ones_blk = jnp.where(orow == 0, 1.0, 0.0).astype(BF16)

    def scores(h, j):
        k = k_ref[0, pl.ds(pl.multiple_of(j * tk, tk), tk), pl.ds(h * dq, dq)]
        kx = jnp.concatenate([k, kf_ref[h]], axis=1)
        return _dot(kx, qs_ref[h])

    def softmax_pv(h, j, t, diagonal):
        if diagonal:
            krow = lax.broadcasted_iota(jnp.int32, (tk, 2 * tq), 0)
            qcol = lax.broadcasted_iota(jnp.int32, (tk, 2 * tq), 1)
            qcol = jnp.where(qcol >= tq, qcol - tq, qcol)
            t = jnp.where(krow <= qcol, t, NEG_BIG)
        soff = cs_tot[h] * ((j - i) * tk).astype(F32)
        m = m_ref[h]
        m_new = jnp.maximum(m, jnp.max(t, axis=0, keepdims=True) + soff)
        alpha = jnp.exp2(m - m_new)
        p = jnp.exp2(t - (m_new - soff)).astype(BF16)
        vx = jnp.concatenate([vt_ref[0, h, j], ones_blk], axis=0)
        acc_ref[h] = alpha * acc_ref[h] + _dot(vx, p)
        m_ref[h] = m_new

    def run(units):
        t_next = scores(*units[0][:2])
        for n, (h, j, diagonal) in enumerate(units):
            t = t_next
            if n + 1 < len(units):
                t_next = scores(*units[n + 1][:2])
            softmax_pv(h, j, t, diagonal)

    def tile_units(j, diagonal):
        return [(h, j, diagonal) for h in range(ATTN_HEADS)]

    def body(jj, carry):
        run(tile_units(2 * jj, False) + tile_units(2 * jj + 1, False))
        return carry

    lax.fori_loop(0, i // 2, body, 0)

    @pl.when(i % 2 == 0)
    def _():
        run(tile_units(i, True))

    @pl.when(i % 2 == 1)
    def _():
        run(tile_units(i - 1, False) + tile_units(i, True))

    lv = lam_ref[...]
    d1 = jnp.sum(lv[0:1] * lv[1:2], axis=-1, keepdims=True)
    d2 = jnp.sum(lv[2:3] * lv[3:4], axis=-1, keepdims=True)
    lam = jnp.exp(d1) - jnp.exp(d2) + lam_init
    for h in range(ATTN_HEADS):
        l = acc_ref[h, pl.ds(dv, 1), :]
        out = (acc_ref[h, pl.ds(0, dv), pl.ds(0, tq)] / l[:, :tq]
               - lam * (acc_ref[h, pl.ds(0, dv), pl.ds(tq, tq)] / l[:, tq:]))
        ms = jnp.mean(out * out, axis=0, keepdims=True)
        on = out * lax.rsqrt(ms + EPS)
        o_ref[0, :, pl.ds(h * dv, dv)] = (on.T * gain_ref[...] * (1.0 - lam_init)).astype(BF16)


def _attn(z3, vt5, cs_terms, lamv, gain, lam_init, tq=256):
    b, s, _ = z3.shape
    nq = s // tq
    kern = functools.partial(_attn_kernel, tq=tq, lam_init=lam_init)
    width = ATTN_HEADS * 2 * ATTN_QK_DIM
    return pl.pallas_call(
        kern,
        grid=(b, nq),
        in_specs=[
            pl.BlockSpec(memory_space=pltpu.SMEM),
            pl.BlockSpec((SUBLANES, LANES), lambda bi, i: (0, 0)),
            pl.BlockSpec((1, tq, width), lambda bi, i: (bi, i, COL_AQ // width)),
            pl.BlockSpec((1, s, width), lambda bi, i: (bi, 0, COL_AK // width)),
            pl.BlockSpec((1, ATTN_HEADS, nq, ATTN_V_DIM, tq), lambda bi, i: (bi, 0, 0, 0, 0)),
            pl.BlockSpec((1, ATTN_V_DIM), lambda bi, i: (0, 0)),
        ],
        out_specs=pl.BlockSpec((1, tq, ATTN_WIDTH), lambda bi, i: (bi, i, 0)),
        out_shape=jax.ShapeDtypeStruct((b, s, ATTN_WIDTH), BF16),
        scratch_shapes=[
            pltpu.VMEM((ATTN_HEADS, 2 * ATTN_QK_DIM + LANES, 2 * tq), BF16),
            pltpu.VMEM((ATTN_HEADS, tq, LANES), BF16),
            pltpu.VMEM((ATTN_HEADS, ATTN_V_DIM + 2 * SUBLANES, 2 * tq), F32),
            pltpu.VMEM((ATTN_HEADS, 1, 2 * tq), F32),
        ],
        compiler_params=pltpu.CompilerParams(
            dimension_semantics=("parallel", "arbitrary"),
            vmem_limit_bytes=VMEM_LIMIT),
        name="attn",
    )(cs_terms, lamv, z3, z3, vt5, gain)


def _split3(x):
    hi = x.astype(BF16)
    r = x - hi.astype(F32)
    mid = r.astype(BF16)
    lo = (r - mid.astype(F32)).astype(BF16)
    return hi, mid, lo


def _mlstm_kernel(uq_ref, uk_ref, v_ref, mo_ref, gcol_ref, grow_ref, bcol_ref, brow_ref,
                  cw_ref, cb_ref, gn_ref, o_ref, extq, extk, c_scr, n_scr, m_scr, *, chunk):
    L = chunk
    nh = MLSTM_HEADS
    dk = MLSTM_QK_DIM
    dv = MLSTM_V_DIM
    heads = range(nh)

    @pl.when(pl.program_id(1) == 0)
    def _():
        extq[pl.ds(0, SUBLANES), :] = jnp.zeros((SUBLANES, MLSTM_QK_WIDTH), F32)
        extk[pl.ds(0, SUBLANES), :] = jnp.zeros((SUBLANES, MLSTM_QK_WIDTH), F32)
        c_scr[...] = jnp.zeros_like(c_scr)
        n_scr[...] = jnp.zeros_like(n_scr)
        m_scr[...] = jnp.zeros_like(m_scr)

    def conv_silu(u_ref, ext, col0):
        ext[pl.ds(SUBLANES, L), :] = u_ref[0].astype(F32)
        w = cw_ref[:, pl.ds(col0, MLSTM_QK_WIDTH)]
        y = cb_ref[:, pl.ds(col0, MLSTM_QK_WIDTH)]
        for tap in range(CONV_WIDTH):
            off = SUBLANES - (CONV_WIDTH - 1) + tap
            y = y + ext[pl.ds(off, L), :] * w[tap:tap + 1]
        ext[pl.ds(0, SUBLANES), :] = ext[pl.ds(L, SUBLANES), :]
        return y * _sigmoid(y)

    q_all = conv_silu(uq_ref, extq, 0)
    k_all = conv_silu(uk_ref, extk, MLSTM_QK_WIDTH) * (dk ** -0.5)
    q = [q_all[:, h * dk:(h + 1) * dk] for h in heads]
    k = [k_all[:, h * dk:(h + 1) * dk] for h in heads]
    qb = [x.astype(BF16) for x in q]
    kb = [x.astype(BF16) for x in k]
    vb = [v_ref[0, :, pl.ds(h * dv, dv)] for h in heads]

    g_c = gcol_ref[...] + bcol_ref[...]
    g_r = grow_ref[0] + brow_ref[:, 0:1]
    lf_c = _log_sigmoid(g_c)
    lf_r = _log_sigmoid(g_r)

    r_i = lax.broadcasted_iota(jnp.int32, (L, L), 0)
    c_i = lax.broadcasted_iota(jnp.int32, (L, L), 1)
    causal = c_i <= r_i
    tri = jnp.where(causal, 1.0, 0.0).astype(BF16)
    tri_t = jnp.where(r_i <= c_i, 1.0, 0.0).astype(BF16)
    hi, mid, lo = _split3(lf_c)
    b_c = _dot(tri, hi) + _dot(tri, mid) + _dot(tri, lo)
    hi, mid, lo = _split3(lf_r)
    b_r = _dot(hi, tri_t) + _dot(mid, tri_t) + _dot(lo, tri_t)
    g_sum = jnp.sum(lf_r, axis=1, keepdims=True)

    lane = lax.broadcasted_iota(jnp.int32, (L, LANES), 1)

    def col(x, idx):
        return jnp.sum(jnp.where(lane == idx, x, 0.0), axis=1, keepdims=True)

    i_col = [col(g_c, h) for h in heads]
    b_col = [col(b_c, nh + h) for h in heads]
    i_row = [g_r[h:h + 1, :] for h in heads]
    b_row = [b_r[nh + h:nh + h + 1, :] for h in heads]
    g_tot = [g_sum[nh + h:nh + h + 1, :] for h in heads]

    c_prev = [c_scr[h] for h in heads]
    n_prev = [n_scr[h] for h in heads]
    m_prev = [m_scr[h][:, 0:1] for h in heads]

    s_qk = [_dot_nt(qb[h], kb[h]) for h in heads]
    inter = [_dot(qb[h], c_prev[h].astype(BF16)) for h in heads]
    m_loc = [jnp.max(g_tot[h] - b_row[h] + i_row[h], axis=1, keepdims=True) for h in heads]
    kw = [k[h] * jnp.exp(g_tot[h] - b_col[h] + i_col[h] - m_loc[h]) for h in heads]
    c_loc = [_dot(kw[h].T.astype(BF16), vb[h]) for h in heads]

    for h in heads:
        d = jnp.where(causal, b_col[h] - b_row[h] + i_row[h], NEG_BIG)
        m_inter = b_col[h] + m_prev[h]
        m_j = jnp.maximum(m_inter, jnp.max(d, axis=1, keepdims=True))
        w_inter = jnp.exp(m_inter - m_j)
        p = s_qk[h] * jnp.exp(d - m_j)
        num = w_inter * inter[h] + _dot(p.astype(BF16), vb[h])
        den = (w_inter * jnp.sum(q[h] * n_prev[h], axis=1, keepdims=True)
               + jnp.sum(p, axis=1, keepdims=True))
        hval = num / jnp.maximum(jnp.abs(den), jnp.exp(-m_j))
        hn = _rms(hval, gn_ref[:, pl.ds(h * dv, dv)])
        gate = _sigmoid(mo_ref[0, :, pl.ds(h * dv, dv)].astype(F32))
        o_ref[0, :, pl.ds(h * dv, dv)] = (hn * gate).astype(BF16)

    for h in heads:
        n_loc = jnp.sum(kw[h], axis=0, keepdims=True)
        m_new = jnp.maximum(g_tot[h] + m_prev[h], m_loc[h])
        a_old = jnp.exp(g_tot[h] + m_prev[h] - m_new)
        a_new = jnp.exp(m_loc[h] - m_new)
        c_scr[h] = a_old * c_prev[h] + a_new * c_loc[h]
        n_scr[h] = a_old * n_prev[h] + a_new * n_loc
        m_scr[h] = jnp.broadcast_to(m_new, (1, LANES))


def _mlstm(z3, gates, gates_t, bias_col, bias_row, conv_w8, conv_b, gnorm, chunk=256):
    b, s, _ = z3.shape
    nc = s // chunk
    kern = functools.partial(_mlstm_kernel, chunk=chunk)
    qw = MLSTM_QK_WIDTH
    vw = MLSTM_WIDTH
    return pl.pallas_call(
        kern,
        grid=(b, nc),
        in_specs=[
            pl.BlockSpec((1, chunk, qw), lambda bi, c: (bi, c, COL_MQ // qw)),
            pl.BlockSpec((1, chunk, qw), lambda bi, c: (bi, c, COL_MK // qw)),
            pl.BlockSpec((1, chunk, vw), lambda bi, c: (bi, c, COL_MV // vw)),
            pl.BlockSpec((1, chunk, vw), lambda bi, c: (bi, c, COL_MO // vw)),
            pl.BlockSpec((chunk, LANES), lambda bi, c: (bi * nc + c, 0)),
            pl.BlockSpec((1, SUBLANES, chunk), lambda bi, c: (bi, 0, c)),
            pl.BlockSpec((1, LANES), lambda bi, c: (0, 0)),
            pl.BlockSpec((SUBLANES, LANES), lambda bi, c: (0, 0)),
            pl.BlockSpec((SUBLANES, 2 * qw), lambda bi, c: (0, 0)),
            pl.BlockSpec((1, 2 * qw), lambda bi, c: (0, 0)),
            pl.BlockSpec((1, vw), lambda bi, c: (0, 0)),
        ],
        out_specs=pl.BlockSpec((1, chunk, vw), lambda bi, c: (bi, c, 0)),
        out_shape=jax.ShapeDtypeStruct((b, s, vw), BF16),
        scratch_shapes=[
            pltpu.VMEM((chunk + 2 * SUBLANES, qw), F32),
            pltpu.VMEM((chunk + 2 * SUBLANES, qw), F32),
            pltpu.VMEM((MLSTM_HEADS, MLSTM_QK_DIM, MLSTM_V_DIM), F32),
            pltpu.VMEM((MLSTM_HEADS, 1, MLSTM_QK_DIM), F32),
            pltpu.VMEM((MLSTM_HEADS, 1, LANES), F32),
        ],
        compiler_params=pltpu.CompilerParams(
            dimension_semantics=("parallel", "arbitrary"),
            vmem_limit_bytes=VMEM_LIMIT),
        name="mlstm",
    )(z3, z3, z3, z3, gates, gates_t, bias_col, bias_row, conv_w8, conv_b, gnorm)


def _merge_kernel(att_ref, hm_ref, ga_ref, gm_ref, x_ref, wa_ref, wm_ref, wo_ref, o_ref):
    @pl.when(pl.program_id(1) == 0)
    def _():
        o_ref[...] = x_ref[...]

    a = _dot(att_ref[...], wa_ref[...])
    bm = _dot(hm_ref[...], wm_ref[...])
    y = _sigmoid(ga_ref[...].astype(F32)) * a + _sigmoid(gm_ref[...].astype(F32)) * bm
    o_ref[...] += _dot(y.astype(BF16), wo_ref[...])


def _merge(att2d, hm2d, z2d, x2d, wa, wm, wo, tm=512, tn=1024):
    t = x2d.shape[0]
    nj = D_MODEL // tn
    return pl.pallas_call(
        _merge_kernel,
        grid=(t // tm, nj),
        in_specs=[
            pl.BlockSpec((tm, ATTN_WIDTH), lambda i, j: (i, 0)),
            pl.BlockSpec((tm, MLSTM_WIDTH), lambda i, j: (i, 0)),
            pl.BlockSpec((tm, tn), lambda i, j: (i, COL_GA // tn + j)),
            pl.BlockSpec((tm, tn), lambda i, j: (i, COL_GM // tn + j)),
            pl.BlockSpec((tm, D_MODEL), lambda i, j: (i, 0)),
            pl.BlockSpec((ATTN_WIDTH, tn), lambda i, j: (0, j)),
            pl.BlockSpec((MLSTM_WIDTH, tn), lambda i, j: (0, j)),
            pl.BlockSpec((tn, D_MODEL), lambda i, j: (j, 0)),
        ],
        out_specs=pl.BlockSpec((tm, D_MODEL), lambda i, j: (i, 0)),
        out_shape=jax.ShapeDtypeStruct((t, D_MODEL), F32),
        compiler_params=pltpu.CompilerParams(
            dimension_semantics=("parallel", "arbitrary"), vmem_limit_bytes=VMEM_LIMIT),
        name="merge",
    )(att2d, hm2d, z2d, z2d, x2d, wa, wm, wo)


def _cross_kernel(x_ref, g_ref, wq_ref, kv_ref, wo_ref, o_ref):
    x = x_ref[...]
    hc = _rms(x, g_ref[...]).astype(BF16)
    cq = (_dot(hc, wq_ref[...]) * (CROSS_HEAD_DIM ** -0.5)).astype(BF16)
    outs = []
    for hh in range(CROSS_HEADS):
        lo = hh * CROSS_HEAD_DIM
        qh = cq[:, lo:lo + CROSS_HEAD_DIM]
        kh = kv_ref[0, :, lo:lo + CROSS_HEAD_DIM]
        vh = kv_ref[0, :, CROSS_WIDTH + lo:CROSS_WIDTH + lo + CROSS_HEAD_DIM]
        s = _dot_nt(qh, kh)
        p = jnp.exp(s - jnp.max(s, axis=-1, keepdims=True))
        l = jnp.sum(p, axis=-1, keepdims=True)
        outs.append((_dot(p.astype(BF16), vh) / l).astype(BF16))
    co = jnp.concatenate(outs, axis=1)
    o_ref[...] = x + _dot(co, wo_ref[...])


def _cross(x2d, g, wq, ckv, wo, seq, tm=512):
    t = x2d.shape[0]
    n_mem = ckv.shape[1]
    per_batch = seq // tm
    return pl.pallas_call(
        _cross_kernel,
        grid=(t // tm,),
        in_specs=[
            pl.BlockSpec((tm, D_MODEL), lambda i: (i, 0)),
            pl.BlockSpec((1, D_MODEL), lambda i: (0, 0)),
            pl.BlockSpec((D_MODEL, CROSS_WIDTH), lambda i: (0, 0)),
            pl.BlockSpec((1, n_mem, 2 * CROSS_WIDTH), lambda i: (i // per_batch, 0, 0)),
            pl.BlockSpec((CROSS_WIDTH, D_MODEL), lambda i: (0, 0)),
        ],
        out_specs=pl.BlockSpec((tm, D_MODEL), lambda i: (i, 0)),
        out_shape=jax.ShapeDtypeStruct((t, D_MODEL), F32),
        compiler_params=pltpu.CompilerParams(
            dimension_semantics=("parallel",), vmem_limit_bytes=VMEM_LIMIT),
        name="cross",
    )(x2d, g, wq, ckv, wo)


def _mlp_kernel(x_ref, g_ref, wu_ref, wd_ref, gf_ref, o_ref, h_scr, acc, *, final_norm):
    j = pl.program_id(1)

    @pl.when(j == 0)
    def _():
        x = x_ref[...]
        h_scr[...] = _rms(x, g_ref[...]).astype(BF16)
        acc[...] = x

    u = jnp.square(jnp.maximum(_dot(h_scr[...], wu_ref[...]), 0.0)).astype(BF16)
    acc[...] += _dot(u, wd_ref[...])

    @pl.when(j == pl.num_programs(1) - 1)
    def _():
        if final_norm:
            o_ref[...] = _rms(acc[...], gf_ref[...])
        else:
            o_ref[...] = acc[...]


def _mlp(x2d, g, wu, wd, gf, final_norm, tm=512, tf=1024):
    t = x2d.shape[0]
    kern = functools.partial(_mlp_kernel, final_norm=final_norm)
    return pl.pallas_call(
        kern,
        grid=(t // tm, D_FF // tf),
        in_specs=[
            pl.BlockSpec((tm, D_MODEL), lambda i, j: (i, 0)),
            pl.BlockSpec((1, D_MODEL), lambda i, j: (0, 0)),
            pl.BlockSpec((D_MODEL, tf), lambda i, j: (0, j)),
            pl.BlockSpec((tf, D_MODEL), lambda i, j: (j, 0)),
            pl.BlockSpec((1, D_MODEL), lambda i, j: (0, 0)),
        ],
        out_specs=pl.BlockSpec((tm, D_MODEL), lambda i, j: (i, 0)),
        out_shape=jax.ShapeDtypeStruct((t, D_MODEL), F32),
        scratch_shapes=[pltpu.VMEM((tm, D_MODEL), BF16), pltpu.VMEM((tm, D_MODEL), F32)],
        compiler_params=pltpu.CompilerParams(
            dimension_semantics=("parallel", "arbitrary"), vmem_limit_bytes=VMEM_LIMIT),
        name="mlp",
    )(x2d, g, wu, wd, gf)


def _alibi_terms():
    slopes = 2.0 ** (-8.0 * np.arange(1, ATTN_HEADS + 1, dtype=np.float64) / ATTN_HEADS)
    rem = slopes * LOG2E
    terms = []
    for _ in range(ALIBI_TERMS):
        t = rem.astype(np.float32).astype(ml_dtypes.bfloat16).astype(np.float64)
        terms.append(t)
        rem = rem - t
    return np.stack(terms, axis=1).reshape(-1).astype(np.float32)


def _in_col_scale():
    scale = np.ones((GATE_LO, 1), np.float32)
    scale[COL_AQ:COL_AK] = ATTN_QK_DIM ** -0.5 * LOG2E
    return scale


def _pad_rows(a, rows):
    return jnp.pad(a, ((0, rows - a.shape[0]), (0, 0)))


def kernel(x, mem, norm_mix, w_in, b_igate, b_fgate, conv_w, conv_b, lam_q1, lam_k1, lam_q2, lam_k2, attn_norm, mlstm_norm, w_attn_br, w_mlstm_br, w_out, norm_cross, norm_mem, w_cq, w_ckv, w_co, norm_mlp, w_up, w_down, norm_final):
    b, s, _ = x.shape
    t = b * s
    tq = 256
    cs_terms = jnp.asarray(_alibi_terms())
    x2d = x.reshape(t, D_MODEL)
    for l in range(DEPTH):
        lam_init = 0.8 - 0.6 * math.exp(-0.3 * l)
        w = w_in[l]
        wt = w.T
        w_a = (wt[:GATE_LO] * jnp.asarray(_in_col_scale())).astype(BF16)
        w_b = wt[GATE_HI:].astype(BF16)
        w_gate = _pad_rows(wt[GATE_LO:GATE_HI], LANES).astype(BF16)

        z2d, gates = _inproj(x2d, norm_mix[l][None, :], w_a, w_b, w_gate)
        z3 = z2d.reshape(b, s, IN_MAIN)

        vt5 = (z3[:, :, COL_AV:COL_AV + ATTN_WIDTH]
               .reshape(b, s // tq, tq, ATTN_HEADS, ATTN_V_DIM).transpose(0, 3, 1, 4, 2))
        lamv = _pad_rows(jnp.pad(jnp.stack([lam_q1[l], lam_k1[l], lam_q2[l], lam_k2[l]]),
                                 ((0, 0), (0, LANES - ATTN_QK_DIM))), SUBLANES)
        att = _attn(z3, vt5, cs_terms, lamv, attn_norm[l][None, :], lam_init, tq=tq)

        gates_t = gates[:, :SUBLANES].reshape(b, s, SUBLANES).transpose(0, 2, 1)
        gate_bias = jnp.concatenate([b_igate[l], b_fgate[l]])
        bias_col = jnp.pad(gate_bias, (0, LANES - gate_bias.shape[0]))[None, :]
        bias_row = jnp.broadcast_to(gate_bias[:, None], (SUBLANES, LANES))
        hm = _mlstm(z3, gates, gates_t, bias_col, bias_row, _pad_rows(conv_w[l], SUBLANES),
                    conv_b[l][None, :], mlstm_norm[l][None, :])

        x2d = _merge(att.reshape(t, ATTN_WIDTH), hm.reshape(t, MLSTM_WIDTH), z2d, x2d,
                     w_attn_br[l].astype(BF16), w_mlstm_br[l].astype(BF16), w_out[l].astype(BF16))

        ckv = _memkv(mem, norm_mem[l][None, :], w_ckv[l].astype(BF16))
        x2d = _cross(x2d, norm_cross[l][None, :], w_cq[l].astype(BF16), ckv,
                     w_co[l].astype(BF16), s)

        x2d = _mlp(x2d, norm_mlp[l][None, :], w_up[l].astype(BF16), w_down[l].astype(BF16),
                   norm_final[None, :], final_norm=(l == DEPTH - 1))
    return x2d.reshape(b, s, D_MODEL)
```

```python
import functools
import math

import ml_dtypes
import numpy as np
import jax
import jax.numpy as jnp
from jax import lax
from jax.experimental import pallas as pl
from jax.experimental.pallas import tpu as pltpu

F32 = jnp.float32
BF16 = jnp.bfloat16

D_MODEL = 2048
DEPTH = 1
ATTN_HEADS = 8
ATTN_QK_DIM = 64
ATTN_V_DIM = 128
ATTN_WIDTH = ATTN_HEADS * ATTN_V_DIM
MLSTM_HEADS = 4
MLSTM_QK_DIM = 128
MLSTM_V_DIM = 256
MLSTM_QK_WIDTH = MLSTM_HEADS * MLSTM_QK_DIM
MLSTM_WIDTH = MLSTM_HEADS * MLSTM_V_DIM
CONV_WIDTH = 4
CROSS_HEADS = 4
CROSS_HEAD_DIM = 128
CROSS_WIDTH = CROSS_HEADS * CROSS_HEAD_DIM
D_FF = 4 * D_MODEL
EPS = 1e-6
LANES = 128
SUBLANES = 8
NEG_BIG = -1e30
LOG2E = 1.4426950408889634
ALIBI_TERMS = 3

COL_AQ = 0
COL_AK = 1024
COL_AV = 2048
COL_MQ = 3072
COL_MK = 3584
COL_MV = 4096
COL_MO = 5120
COL_GA = 6144
COL_GM = 8192
IN_MAIN = 10240
GATE_LO = 6144
GATE_HI = 6152

VMEM_LIMIT = 56 * 1024 * 1024


def _rms(x, g):
    ms = jnp.mean(x * x, axis=-1, keepdims=True)
    return x * lax.rsqrt(ms + EPS) * g


def _sigmoid(x):
    return 1.0 / (1.0 + jnp.exp(-x))


def _log_sigmoid(x):
    return jnp.minimum(x, 0.0) - jnp.log(1.0 + jnp.exp(-jnp.abs(x)))


def _dot(a, b):
    return jnp.dot(a, b, preferred_element_type=F32)


def _dot_nt(a, b):
    return lax.dot_general(a, b, (((1,), (1,)), ((), ())), preferred_element_type=F32)


def _memkv_kernel(mem_ref, g_ref, w_ref, o_ref):
    mn = _rms(mem_ref[0], g_ref[...]).astype(BF16)
    o_ref[0] = _dot(mn, w_ref[...]).astype(BF16)


def _memkv(mem, g, w_ckv):
    b, n_mem, _ = mem.shape
    return pl.pallas_call(
        _memkv_kernel,
        grid=(b,),
        in_specs=[
            pl.BlockSpec((1, n_mem, D_MODEL), lambda i: (i, 0, 0)),
            pl.BlockSpec((1, D_MODEL), lambda i: (0, 0)),
            pl.BlockSpec((D_MODEL, 2 * CROSS_WIDTH), lambda i: (0, 0)),
        ],
        out_specs=pl.BlockSpec((1, n_mem, 2 * CROSS_WIDTH), lambda i: (i, 0, 0)),
        out_shape=jax.ShapeDtypeStruct((b, n_mem, 2 * CROSS_WIDTH), BF16),
        compiler_params=pltpu.CompilerParams(
            dimension_semantics=("arbitrary",), vmem_limit_bytes=VMEM_LIMIT),
        name="memkv",
    )(mem, g, w_ckv)


def _inproj_kernel(x_ref, g_ref, wa_ref, wb_ref, wg_ref, z_ref, gate_ref, h_scr, *, na):
    j = pl.program_id(1)

    @pl.when(j == 0)
    def _():
        h = _rms(x_ref[...], g_ref[...]).astype(BF16)
        h_scr[...] = h
        gate_ref[...] = _dot_nt(h, wg_ref[...])

    @pl.when(j < na)
    def _():
        z_ref[...] = _dot_nt(h_scr[...], wa_ref[...]).astype(BF16)

    @pl.when(j >= na)
    def _():
        z_ref[...] = _dot_nt(h_scr[...], wb_ref[...]).astype(BF16)


def _inproj(x2d, g, w_a, w_b, w_gate, tm=1024, tn=1024):
    t = x2d.shape[0]
    na = w_a.shape[0] // tn
    n = w_a.shape[0] + w_b.shape[0]
    return pl.pallas_call(
        functools.partial(_inproj_kernel, na=na),
        grid=(t // tm, n // tn),
        in_specs=[
            pl.BlockSpec((tm, D_MODEL), lambda i, j: (i, 0)),
            pl.BlockSpec((1, D_MODEL), lambda i, j: (0, 0)),
            pl.BlockSpec((tn, D_MODEL), lambda i, j: (jnp.minimum(j, na - 1), 0)),
            pl.BlockSpec((tn, D_MODEL), lambda i, j: (jnp.maximum(j - na, 0), 0)),
            pl.BlockSpec((LANES, D_MODEL), lambda i, j: (0, 0)),
        ],
        out_specs=[
            pl.BlockSpec((tm, tn), lambda i, j: (i, j)),
            pl.BlockSpec((tm, LANES), lambda i, j: (i, 0)),
        ],
        out_shape=[
            jax.ShapeDtypeStruct((t, n), BF16),
            jax.ShapeDtypeStruct((t, LANES), F32),
        ],
        scratch_shapes=[pltpu.VMEM((tm, D_MODEL), BF16)],
        compiler_params=pltpu.CompilerParams(
            dimension_semantics=("parallel", "arbitrary"), vmem_limit_bytes=VMEM_LIMIT),
        name="inproj",
    )(x2d, g, w_a, w_b, w_gate)


def _attn_kernel(cs_ref, lam_ref, q_ref, k_ref, vt_ref, gain_ref, o_ref,
                 qs_ref, kf_ref, acc_ref, m_ref, *, tq, lam_init):
    tk = tq
    nfeat = ALIBI_TERMS
    dq = 2 * ATTN_QK_DIM
    dv = ATTN_V_DIM
    i = pl.program_id(1)
    cs = [[cs_ref[nfeat * h + t] for t in range(nfeat)] for h in range(ATTN_HEADS)]
    cs_tot = [sum(c[1:], c[0]) for c in cs]

    @pl.when(i == 0)
    def _():
        klane = lax.broadcasted_iota(jnp.int32, (tk, LANES), 1)
        krow = lax.broadcasted_iota(jnp.int32, (tk, LANES), 0).astype(F32)
        frow = lax.broadcasted_iota(jnp.int32, (LANES, 2 * tq), 0)
        fcol = lax.broadcasted_iota(jnp.int32, (LANES, 2 * tq), 1)
        fcol = jnp.where(fcol >= tq, fcol - tq, fcol).astype(F32)
        for h in range(ATTN_HEADS):
            kf = jnp.where(klane < nfeat, krow, 0.0)
            qf = jnp.where((frow >= nfeat) & (frow < 2 * nfeat), -fcol, 0.0)
            for t in range(nfeat):
                kf = jnp.where(klane == nfeat + t, cs[h][t], kf)
                qf = jnp.where(frow == t, cs[h][t], qf)
            kf_ref[h] = kf.astype(BF16)
            qs_ref[h, pl.ds(dq, LANES), :] = qf.astype(BF16)

    row = lax.broadcasted_iota(jnp.int32, (dq, tq), 0)
    for h in range(ATTN_HEADS):
        qt = q_ref[0, :, pl.ds(h * dq, dq)].astype(F32).T
        qs_ref[h, pl.ds(0, dq), pl.ds(0, tq)] = jnp.where(row < ATTN_QK_DIM, qt, 0.0).astype(BF16)
        qs_ref[h, pl.ds(0, dq), pl.ds(tq, tq)] = jnp.where(row >= ATTN_QK_DIM, qt, 0.0).astype(BF16)
    acc_ref[...] = jnp.zeros_like(acc_ref)
    m_ref[...] = jnp.full_like(m_ref, NEG_BIG)

    orow = lax.broadcasted_iota(jnp.int32, (2 * SUBLANES, tk), 0)
    ones_blk = jnp.where(orow == 0, 1.0, 0.0).astype(BF16)

    def scores(h, j):
        k = k_ref[0, pl.ds(pl.multiple_of(j * tk, tk), tk), pl.ds(h * dq, dq)]
        kx = jnp.concatenate([k, kf_ref[h]], axis=1)
        return _dot(kx, qs_ref[h])

    def softmax_pv(h, j, t, diagonal):
        if diagonal:
            krow = lax.broadcasted_iota(jnp.int32, (tk, 2 * tq), 0)
            qcol = lax.broadcasted_iota(jnp.int32, (tk, 2 * tq), 1)
            qcol = jnp.where(qcol >= tq, qcol - tq, qcol)
            t = jnp.where(krow <= qcol, t, NEG_BIG)
        soff = cs_tot[h] * ((j - i) * tk).astype(F32)
        m = m_ref[h]
        m_new = jnp.maximum(m, jnp.max(t, axis=0, keepdims=True) + soff)
        alpha = jnp.exp2(m - m_new)
        p = jnp.exp2(t - (m_new - soff)).astype(BF16)
        vx = jnp.concatenate([vt_ref[0, h, j], ones_blk], axis=0)
        acc_ref[h] = alpha * acc_ref[h] + _dot(vx, p)
        m_ref[h] = m_new

    def run(units):
        ahead = 2
        pending = [scores(*u[:2]) for u in units[:ahead]]
        for n, (h, j, diagonal) in enumerate(units):
            if n + ahead < len(units):
                pending.append(scores(*units[n + ahead][:2]))
            softmax_pv(h, j, pending.pop(0), diagonal)

    def tile_units(j, diagonal):
        return [(h, j, diagonal) for h in range(ATTN_HEADS)]

    def body(jj, carry):
        run(tile_units(2 * jj, False) + tile_units(2 * jj + 1, False))
        return carry

    lax.fori_loop(0, i // 2, body, 0)

    @pl.when(i % 2 == 0)
    def _():
        run(tile_units(i, True))

    @pl.when(i % 2 == 1)
    def _():
        run(tile_units(i - 1, False) + tile_units(i, True))

    lv = lam_ref[...]
    d1 = jnp.sum(lv[0:1] * lv[1:2], axis=-1, keepdims=True)
    d2 = jnp.sum(lv[2:3] * lv[3:4], axis=-1, keepdims=True)
    lam = jnp.exp(d1) - jnp.exp(d2) + lam_init
    for h in range(ATTN_HEADS):
        l = acc_ref[h, pl.ds(dv, 1), :]
        out = (acc_ref[h, pl.ds(0, dv), pl.ds(0, tq)] / l[:, :tq]
               - lam * (acc_ref[h, pl.ds(0, dv), pl.ds(tq, tq)] / l[:, tq:]))
        ms = jnp.mean(out * out, axis=0, keepdims=True)
        on = out * lax.rsqrt(ms + EPS)
        o_ref[0, :, pl.ds(h * dv, dv)] = (on.T * gain_ref[...] * (1.0 - lam_init)).astype(BF16)


def _attn(z3, vt5, cs_terms, lamv, gain, lam_init, tq=256):
    b, s, _ = z3.shape
    nq = s // tq
    kern = functools.partial(_attn_kernel, tq=tq, lam_init=lam_init)
    width = ATTN_HEADS * 2 * ATTN_QK_DIM
    return pl.pallas_call(
        kern,
        grid=(b, nq),
        in_specs=[
            pl.BlockSpec(memory_space=pltpu.SMEM),
            pl.BlockSpec((SUBLANES, LANES), lambda bi, i: (0, 0)),
            pl.BlockSpec((1, tq, width), lambda bi, i: (bi, i, COL_AQ // width)),
            pl.BlockSpec((1, s, width), lambda bi, i: (bi, 0, COL_AK // width)),
            pl.BlockSpec((1, ATTN_HEADS, nq, ATTN_V_DIM, tq), lambda bi, i: (bi, 0, 0, 0, 0)),
            pl.BlockSpec((1, ATTN_V_DIM), lambda bi, i: (0, 0)),
        ],
        out_specs=pl.BlockSpec((1, tq, ATTN_WIDTH), lambda bi, i: (bi, i, 0)),
        out_shape=jax.ShapeDtypeStruct((b, s, ATTN_WIDTH), BF16),
        scratch_shapes=[
            pltpu.VMEM((ATTN_HEADS, 2 * ATTN_QK_DIM + LANES, 2 * tq), BF16),
            pltpu.VMEM((ATTN_HEADS, tq, LANES), BF16),
            pltpu.VMEM((ATTN_HEADS, ATTN_V_DIM + 2 * SUBLANES, 2 * tq), F32),
            pltpu.VMEM((ATTN_HEADS, 1, 2 * tq), F32),
        ],
        compiler_params=pltpu.CompilerParams(
            dimension_semantics=("parallel", "arbitrary"),
            vmem_limit_bytes=VMEM_LIMIT),
        name="attn",
    )(cs_terms, lamv, z3, z3, vt5, gain)


def _split3(x):
    hi = x.astype(BF16)
    r = x - hi.astype(F32)
    mid = r.astype(BF16)
    lo = (r - mid.astype(F32)).astype(BF16)
    return hi, mid, lo


def _mlstm_kernel(uq_ref, uk_ref, v_ref, mo_ref, gcol_ref, grow_ref, bcol_ref, brow_ref,
                  cw_ref, cb_ref, gn_ref, o_ref, extq, extk, c_scr, n_scr, m_scr, *, chunk):
    L = chunk
    nh = MLSTM_HEADS
    dk = MLSTM_QK_DIM
    dv = MLSTM_V_DIM
    heads = range(nh)

    @pl.when(pl.program_id(1) == 0)
    def _():
        extq[pl.ds(0, SUBLANES), :] = jnp.zeros((SUBLANES, MLSTM_QK_WIDTH), F32)
        extk[pl.ds(0, SUBLANES), :] = jnp.zeros((SUBLANES, MLSTM_QK_WIDTH), F32)
        c_scr[...] = jnp.zeros_like(c_scr)
        n_scr[...] = jnp.zeros_like(n_scr)
        m_scr[...] = jnp.zeros_like(m_scr)

    def conv_silu(u_ref, ext, col0):
        ext[pl.ds(SUBLANES, L), :] = u_ref[0].astype(F32)
        w = cw_ref[:, pl.ds(col0, MLSTM_QK_WIDTH)]
        y = cb_ref[:, pl.ds(col0, MLSTM_QK_WIDTH)]
        for tap in range(CONV_WIDTH):
            off = SUBLANES - (CONV_WIDTH - 1) + tap
            y = y + ext[pl.ds(off, L), :] * w[tap:tap + 1]
        ext[pl.ds(0, SUBLANES), :] = ext[pl.ds(L, SUBLANES), :]
        return y * _sigmoid(y)

    q_all = conv_silu(uq_ref, extq, 0)
    k_all = conv_silu(uk_ref, extk, MLSTM_QK_WIDTH) * (dk ** -0.5)
    q = [q_all[:, h * dk:(h + 1) * dk] for h in heads]
    k = [k_all[:, h * dk:(h + 1) * dk] for h in heads]
    qb = [x.astype(BF16) for x in q]
    kb = [x.astype(BF16) for x in k]
    vb = [v_ref[0, :, pl.ds(h * dv, dv)] for h in heads]

    g_c = gcol_ref[...] + bcol_ref[...]
    g_r = grow_ref[0] + brow_ref[:, 0:1]
    lf_c = _log_sigmoid(g_c)
    lf_r = _log_sigmoid(g_r)

    r_i = lax.broadcasted_iota(jnp.int32, (L, L), 0)
    c_i = lax.broadcasted_iota(jnp.int32, (L, L), 1)
    causal = c_i <= r_i
    tri = jnp.where(causal, 1.0, 0.0).astype(BF16)
    tri_t = jnp.where(r_i <= c_i, 1.0, 0.0).astype(BF16)
    hi, mid, lo = _split3(lf_c)
    b_c = _dot(tri, hi) + _dot(tri, mid) + _dot(tri, lo)
    hi, mid, lo = _split3(lf_r)
    b_r = _dot(hi, tri_t) + _dot(mid, tri_t) + _dot(lo, tri_t)
    g_sum = jnp.sum(lf_r, axis=1, keepdims=True)

    lane = lax.broadcasted_iota(jnp.int32, (L, LANES), 1)

    def col(x, idx):
        return jnp.sum(jnp.where(lane == idx, x, 0.0), axis=1, keepdims=True)

    i_col = [col(g_c, h) for h in heads]
    b_col = [col(b_c, nh + h) for h in heads]
    i_row = [g_r[h:h + 1, :] for h in heads]
    b_row = [b_r[nh + h:nh + h + 1, :] for h in heads]
    g_tot = [g_sum[nh + h:nh + h + 1, :] for h in heads]

    c_prev = [c_scr[h] for h in heads]
    n_prev = [n_scr[h] for h in heads]
    m_prev = [m_scr[h][:, 0:1] for h in heads]

    s_qk = [_dot_nt(qb[h], kb[h]) for h in heads]
    inter = [_dot(qb[h], c_prev[h].astype(BF16)) for h in heads]
    m_loc = [jnp.max(g_tot[h] - b_row[h] + i_row[h], axis=1, keepdims=True) for h in heads]
    kw = [k[h] * jnp.exp(g_tot[h] - b_col[h] + i_col[h] - m_loc[h]) for h in heads]
    c_loc = [_dot(kw[h].T.astype(BF16), vb[h]) for h in heads]

    for h in heads:
        d = jnp.where(causal, b_col[h] - b_row[h] + i_row[h], NEG_BIG)
        m_inter = b_col[h] + m_prev[h]
        m_j = jnp.maximum(m_inter, jnp.max(d, axis=1, keepdims=True))
        w_inter = jnp.exp(m_inter - m_j)
        p = s_qk[h] * jnp.exp(d - m_j)
        num = w_inter * inter[h] + _dot(p.astype(BF16), vb[h])
        den = (w_inter * jnp.sum(q[h] * n_prev[h], axis=1, keepdims=True)
               + jnp.sum(p, axis=1, keepdims=True))
        hval = num / jnp.maximum(jnp.abs(den), jnp.exp(-m_j))
        hn = _rms(hval, gn_ref[:, pl.ds(h * dv, dv)])
        gate = _sigmoid(mo_ref[0, :, pl.ds(h * dv, dv)].astype(F32))
        o_ref[0, :, pl.ds(h * dv, dv)] = (hn * gate).astype(BF16)

    for h in heads:
        n_loc = jnp.sum(kw[h], axis=0, keepdims=True)
        m_new = jnp.maximum(g_tot[h] + m_prev[h], m_loc[h])
        a_old = jnp.exp(g_tot[h] + m_prev[h] - m_new)
        a_new = jnp.exp(m_loc[h] - m_new)
        c_scr[h] = a_old * c_prev[h] + a_new * c_loc[h]
        n_scr[h] = a_old * n_prev[h] + a_new * n_loc
        m_scr[h] = jnp.broadcast_to(m_new, (1, LANES))


def _mlstm(z3, gates, gates_t, bias_col, bias_row, conv_w8, conv_b, gnorm, chunk=256):
    b, s, _ = z3.shape
    nc = s // chunk
    kern = functools.partial(_mlstm_kernel, chunk=chunk)
    qw = MLSTM_QK_WIDTH
    vw = MLSTM_WIDTH
    return pl.pallas_call(
        kern,
        grid=(b, nc),
        in_specs=[
            pl.BlockSpec((1, chunk, qw), lambda bi, c: (bi, c, COL_MQ // qw)),
            pl.BlockSpec((1, chunk, qw), lambda bi, c: (bi, c, COL_MK // qw)),
            pl.BlockSpec((1, chunk, vw), lambda bi, c: (bi, c, COL_MV // vw)),
            pl.BlockSpec((1, chunk, vw), lambda bi, c: (bi, c, COL_MO // vw)),
            pl.BlockSpec((chunk, LANES), lambda bi, c: (bi * nc + c, 0)),
            pl.BlockSpec((1, SUBLANES, chunk), lambda bi, c: (bi, 0, c)),
            pl.BlockSpec((1, LANES), lambda bi, c: (0, 0)),
            pl.BlockSpec((SUBLANES, LANES), lambda bi, c: (0, 0)),
            pl.BlockSpec((SUBLANES, 2 * qw), lambda bi, c: (0, 0)),
            pl.BlockSpec((1, 2 * qw), lambda bi, c: (0, 0)),
            pl.BlockSpec((1, vw), lambda bi, c: (0, 0)),
        ],
        out_specs=pl.BlockSpec((1, chunk, vw), lambda bi, c: (bi, c, 0)),
        out_shape=jax.ShapeDtypeStruct((b, s, vw), BF16),
        scratch_shapes=[
            pltpu.VMEM((chunk + 2 * SUBLANES, qw), F32),
            pltpu.VMEM((chunk + 2 * SUBLANES, qw), F32),
            pltpu.VMEM((MLSTM_HEADS, MLSTM_QK_DIM, MLSTM_V_DIM), F32),
            pltpu.VMEM((MLSTM_HEADS, 1, MLSTM_QK_DIM), F32),
            pltpu.VMEM((MLSTM_HEADS, 1, LANES), F32),
        ],
        compiler_params=pltpu.CompilerParams(
            dimension_semantics=("parallel", "arbitrary"),
            vmem_limit_bytes=VMEM_LIMIT),
        name="mlstm",
    )(z3, z3, z3, z3, gates, gates_t, bias_col, bias_row, conv_w8, conv_b, gnorm)


def _merge_kernel(att_ref, hm_ref, ga_ref, gm_ref, x_ref, wa_ref, wm_ref, wo_ref, o_ref):
    @pl.when(pl.program_id(1) == 0)
    def _():
        o_ref[...] = x_ref[...]

    a = _dot(att_ref[...], wa_ref[...])
    bm = _dot(hm_ref[...], wm_ref[...])
    y = _sigmoid(ga_ref[...].astype(F32)) * a + _sigmoid(gm_ref[...].astype(F32)) * bm
    o_ref[...] += _dot(y.astype(BF16), wo_ref[...])


def _merge(att2d, hm2d, z2d, x2d, wa, wm, wo, tm=512, tn=1024):
    t = x2d.shape[0]
    nj = D_MODEL // tn
    return pl.pallas_call(
        _merge_kernel,
        grid=(t // tm, nj),
        in_specs=[
            pl.BlockSpec((tm, ATTN_WIDTH), lambda i, j: (i, 0)),
            pl.BlockSpec((tm, MLSTM_WIDTH), lambda i, j: (i, 0)),
            pl.BlockSpec((tm, tn), lambda i, j: (i, COL_GA // tn + j)),
            pl.BlockSpec((tm, tn), lambda i, j: (i, COL_GM // tn + j)),
            pl.BlockSpec((tm, D_MODEL), lambda i, j: (i, 0)),
            pl.BlockSpec((ATTN_WIDTH, tn), lambda i, j: (0, j)),
            pl.BlockSpec((MLSTM_WIDTH, tn), lambda i, j: (0, j)),
            pl.BlockSpec((tn, D_MODEL), lambda i, j: (j, 0)),
        ],
        out_specs=pl.BlockSpec((tm, D_MODEL), lambda i, j: (i, 0)),
        out_shape=jax.ShapeDtypeStruct((t, D_MODEL), F32),
        compiler_params=pltpu.CompilerParams(
            dimension_semantics=("parallel", "arbitrary"), vmem_limit_bytes=VMEM_LIMIT),
        name="merge",
    )(att2d, hm2d, z2d, z2d, x2d, wa, wm, wo)


def _cross_kernel(x_ref, g_ref, wq_ref, kv_ref, wo_ref, o_ref):
    x = x_ref[...]
    hc = _rms(x, g_ref[...]).astype(BF16)
    cq = (_dot(hc, wq_ref[...]) * (CROSS_HEAD_DIM ** -0.5)).astype(BF16)
    outs = []
    for hh in range(CROSS_HEADS):
        lo = hh * CROSS_HEAD_DIM
        qh = cq[:, lo:lo + CROSS_HEAD_DIM]
        kh = kv_ref[0, :, lo:lo + CROSS_HEAD_DIM]
        vh = kv_ref[0, :, CROSS_WIDTH + lo:CROSS_WIDTH + lo + CROSS_HEAD_DIM]
        s = _dot_nt(qh, kh)
        p = jnp.exp(s - jnp.max(s, axis=-1, keepdims=True))
        l = jnp.sum(p, axis=-1, keepdims=True)
        outs.append((_dot(p.astype(BF16), vh) / l).astype(BF16))
    co = jnp.concatenate(outs, axis=1)
    o_ref[...] = x + _dot(co, wo_ref[...])


def _cross(x2d, g, wq, ckv, wo, seq, tm=512):
    t = x2d.shape[0]
    n_mem = ckv.shape[1]
    per_batch = seq // tm
    return pl.pallas_call(
        _cross_kernel,
        grid=(t // tm,),
        in_specs=[
            pl.BlockSpec((tm, D_MODEL), lambda i: (i, 0)),
            pl.BlockSpec((1, D_MODEL), lambda i: (0, 0)),
            pl.BlockSpec((D_MODEL, CROSS_WIDTH), lambda i: (0, 0)),
            pl.BlockSpec((1, n_mem, 2 * CROSS_WIDTH), lambda i: (i // per_batch, 0, 0)),
            pl.BlockSpec((CROSS_WIDTH, D_MODEL), lambda i: (0, 0)),
        ],
        out_specs=pl.BlockSpec((tm, D_MODEL), lambda i: (i, 0)),
        out_shape=jax.ShapeDtypeStruct((t, D_MODEL), F32),
        compiler_params=pltpu.CompilerParams(
            dimension_semantics=("parallel",), vmem_limit_bytes=VMEM_LIMIT),
        name="cross",
    )(x2d, g, wq, ckv, wo)


def _mlp_kernel(x_ref, g_ref, wu_ref, wd_ref, gf_ref, o_ref, h_scr, acc, *, final_norm):
    j = pl.program_id(1)

    @pl.when(j == 0)
    def _():
        x = x_ref[...]
        h_scr[...] = _rms(x, g_ref[...]).astype(BF16)
        acc[...] = x

    u = jnp.square(jnp.maximum(_dot(h_scr[...], wu_ref[...]), 0.0)).astype(BF16)
    acc[...] += _dot(u, wd_ref[...])

    @pl.when(j == pl.num_programs(1) - 1)
    def _():
        if final_norm:
            o_ref[...] = _rms(acc[...], gf_ref[...])
        else:
            o_ref[...] = acc[...]


def _mlp(x2d, g, wu, wd, gf, final_norm, tm=512, tf=1024):
    t = x2d.shape[0]
    kern = functools.partial(_mlp_kernel, final_norm=final_norm)
    return pl.pallas_call(
        kern,
        grid=(t // tm, D_FF // tf),
        in_specs=[
            pl.BlockSpec((tm, D_MODEL), lambda i, j: (i, 0)),
            pl.BlockSpec((1, D_MODEL), lambda i, j: (0, 0)),
            pl.BlockSpec((D_MODEL, tf), lambda i, j: (0, j)),
            pl.BlockSpec((tf, D_MODEL), lambda i, j: (j, 0)),
            pl.BlockSpec((1, D_MODEL), lambda i, j: (0, 0)),
        ],
        out_specs=pl.BlockSpec((tm, D_MODEL), lambda i, j: (i, 0)),
        out_shape=jax.ShapeDtypeStruct((t, D_MODEL), F32),
        scratch_shapes=[pltpu.VMEM((tm, D_MODEL), BF16), pltpu.VMEM((tm, D_MODEL), F32)],
        compiler_params=pltpu.CompilerParams(
            dimension_semantics=("parallel", "arbitrary"), vmem_limit_bytes=VMEM_LIMIT),
        name="mlp",
    )(x2d, g, wu, wd, gf)


def _alibi_terms():
    slopes = 2.0 ** (-8.0 * np.arange(1, ATTN_HEADS + 1, dtype=np.float64) / ATTN_HEADS)
    rem = slopes * LOG2E
    terms = []
    for _ in range(ALIBI_TERMS):
        t = rem.astype(np.float32).astype(ml_dtypes.bfloat16).astype(np.float64)
        terms.append(t)
        rem = rem - t
    return np.stack(terms, axis=1).reshape(-1).astype(np.float32)


def _in_col_scale():
    scale = np.ones((GATE_LO, 1), np.float32)
    scale[COL_AQ:COL_AK] = ATTN_QK_DIM ** -0.5 * LOG2E
    return scale


def _pad_rows(a, rows):
    return jnp.pad(a, ((0, rows - a.shape[0]), (0, 0)))


def kernel(x, mem, norm_mix, w_in, b_igate, b_fgate, conv_w, conv_b, lam_q1, lam_k1, lam_q2, lam_k2, attn_norm, mlstm_norm, w_attn_br, w_mlstm_br, w_out, norm_cross, norm_mem, w_cq, w_ckv, w_co, norm_mlp, w_up, w_down, norm_final):
    b, s, _ = x.shape
    t = b * s
    tq = 256
    cs_terms = jnp.asarray(_alibi_terms())
    x2d = x.reshape(t, D_MODEL)
    for l in range(DEPTH):
        lam_init = 0.8 - 0.6 * math.exp(-0.3 * l)
        w = w_in[l]
        wt = w.T
        w_a = (wt[:GATE_LO] * jnp.asarray(_in_col_scale())).astype(BF16)
        w_b = wt[GATE_HI:].astype(BF16)
        w_gate = _pad_rows(wt[GATE_LO:GATE_HI], LANES).astype(BF16)

        z2d, gates = _inproj(x2d, norm_mix[l][None, :], w_a, w_b, w_gate)
        z3 = z2d.reshape(b, s, IN_MAIN)

        vt5 = (z3[:, :, COL_AV:COL_AV + ATTN_WIDTH]
               .reshape(b, s // tq, tq, ATTN_HEADS, ATTN_V_DIM).transpose(0, 3, 1, 4, 2))
        lamv = _pad_rows(jnp.pad(jnp.stack([lam_q1[l], lam_k1[l], lam_q2[l], lam_k2[l]]),
                                 ((0, 0), (0, LANES - ATTN_QK_DIM))), SUBLANES)
        att = _attn(z3, vt5, cs_terms, lamv, attn_norm[l][None, :], lam_init, tq=tq)

        gates_t = gates[:, :SUBLANES].reshape(b, s, SUBLANES).transpose(0, 2, 1)
        gate_bias = jnp.concatenate([b_igate[l], b_fgate[l]])
        bias_col = jnp.pad(gate_bias, (0, LANES - gate_bias.shape[0]))[None, :]
        bias_row = jnp.broadcast_to(gate_bias[:, None], (SUBLANES, LANES))
        hm = _mlstm(z3, gates, gates_t, bias_col, bias_row, _pad_rows(conv_w[l], SUBLANES),
                    conv_b[l][None, :], mlstm_norm[l][None, :])

        x2d = _merge(att.reshape(t, ATTN_WIDTH), hm.reshape(t, MLSTM_WIDTH), z2d, x2d,
                     w_attn_br[l].astype(BF16), w_mlstm_br[l].astype(BF16), w_out[l].astype(BF16))

        ckv = _memkv(mem, norm_mem[l][None, :], w_ckv[l].astype(BF16))
        x2d = _cross(x2d, norm_cross[l][None, :], w_cq[l].astype(BF16), ckv,
                     w_co[l].astype(BF16), s)

        x2d = _mlp(x2d, norm_mlp[l][None, :], w_up[l].astype(BF16), w_down[l].astype(BF16),
                   norm_final[None, :], final_norm=(l == DEPTH - 1))
    return x2d.reshape(b, s, D_MODEL)
```

```python
import functools
import math

import ml_dtypes
import numpy as np
import jax
import jax.numpy as jnp
from jax import lax
from jax.experimental import pallas as pl
from jax.experimental.pallas import tpu as pltpu

F32 = jnp.float32
BF16 = jnp.bfloat16

D_MODEL = 2048
DEPTH = 1
ATTN_HEADS = 8
ATTN_QK_DIM = 64
ATTN_V_DIM = 128
ATTN_WIDTH = ATTN_HEADS * ATTN_V_DIM
MLSTM_HEADS = 4
MLSTM_QK_DIM = 128
MLSTM_V_DIM = 256
MLSTM_QK_WIDTH = MLSTM_HEADS * MLSTM_QK_DIM
MLSTM_WIDTH = MLSTM_HEADS * MLSTM_V_DIM
CONV_WIDTH = 4
CROSS_HEADS = 4
CROSS_HEAD_DIM = 128
CROSS_WIDTH = CROSS_HEADS * CROSS_HEAD_DIM
D_FF = 4 * D_MODEL
EPS = 1e-6
LANES = 128
SUBLANES = 8
NEG_BIG = -1e30
LOG2E = 1.4426950408889634
ALIBI_TERMS = 3

COL_AQ = 0
COL_AK = 1024
COL_AV = 2048
COL_MQ = 3072
COL_MK = 3584
COL_MV = 4096
COL_MO = 5120
COL_GA = 6144
COL_GM = 8192
IN_MAIN = 10240
GATE_LO = 6144
GATE_HI = 6152

VMEM_LIMIT = 56 * 1024 * 1024


def _rms(x, g):
    ms = jnp.mean(x * x, axis=-1, keepdims=True)
    return x * lax.rsqrt(ms + EPS) * g


def _sigmoid(x):
    return 1.0 / (1.0 + jnp.exp(-x))


def _log_sigmoid(x):
    return jnp.minimum(x, 0.0) - jnp.log(1.0 + jnp.exp(-jnp.abs(x)))


def _dot(a, b):
    return jnp.dot(a, b, preferred_element_type=F32)


def _dot_nt(a, b):
    return lax.dot_general(a, b, (((1,), (1,)), ((), ())), preferred_element_type=F32)


def _memkv_kernel(mem_ref, g_ref, w_ref, o_ref):
    mn = _rms(mem_ref[0], g_ref[...]).astype(BF16)
    o_ref[0] = _dot(mn, w_ref[...].astype(BF16)).astype(BF16)


def _memkv(mem, g, w_ckv):
    b, n_mem, _ = mem.shape
    return pl.pallas_call(
        _memkv_kernel,
        grid=(b,),
        in_specs=[
            pl.BlockSpec((1, n_mem, D_MODEL), lambda i: (i, 0, 0)),
            pl.BlockSpec((1, D_MODEL), lambda i: (0, 0)),
            pl.BlockSpec((D_MODEL, 2 * CROSS_WIDTH), lambda i: (0, 0)),
        ],
        out_specs=pl.BlockSpec((1, n_mem, 2 * CROSS_WIDTH), lambda i: (i, 0, 0)),
        out_shape=jax.ShapeDtypeStruct((b, n_mem, 2 * CROSS_WIDTH), BF16),
        compiler_params=pltpu.CompilerParams(
            dimension_semantics=("arbitrary",), vmem_limit_bytes=VMEM_LIMIT),
        name="memkv",
    )(mem, g, w_ckv)


def _inproj_kernel(x_ref, g_ref, wa_ref, wb_ref, wg_ref, wu_ref, wd_ref,
                   z_ref, gate_ref, wu_out, wd_out, h_scr, *, na, ncast):
    j = pl.program_id(1)

    @pl.when(j == 0)
    def _():
        h = _rms(x_ref[...], g_ref[...]).astype(BF16)
        h_scr[...] = h
        gate_ref[...] = _dot_nt(h, wg_ref[...])

    @pl.when(j < na)
    def _():
        z_ref[...] = _dot_nt(h_scr[...], wa_ref[...]).astype(BF16)

    @pl.when(j >= na)
    def _():
        z_ref[...] = _dot_nt(h_scr[...], wb_ref[...]).astype(BF16)

    @pl.when(j < ncast)
    def _():
        wu_out[...] = wu_ref[...].astype(BF16)
        wd_out[...] = wd_ref[...].astype(BF16)


def _inproj(x2d, g, w_a, w_b, w_gate, w_up, w_down, tm=1024, tn=1024):
    t = x2d.shape[0]
    na = w_a.shape[0] // tn
    n = w_a.shape[0] + w_b.shape[0]
    ni, nj = t // tm, n // tn
    ncast = 8
    assert ncast <= nj
    ub = (w_up.shape[0] // ni, w_up.shape[1] // ncast)
    db = (w_down.shape[0] // ni, w_down.shape[1] // ncast)
    cast_map = lambda i, j: (i, jnp.minimum(j, ncast - 1))
    return pl.pallas_call(
        functools.partial(_inproj_kernel, na=na, ncast=ncast),
        grid=(ni, nj),
        in_specs=[
            pl.BlockSpec((tm, D_MODEL), lambda i, j: (i, 0)),
            pl.BlockSpec((1, D_MODEL), lambda i, j: (0, 0)),
            pl.BlockSpec((tn, D_MODEL), lambda i, j: (jnp.minimum(j, na - 1), 0)),
            pl.BlockSpec((tn, D_MODEL), lambda i, j: (jnp.maximum(j - na, 0), 0)),
            pl.BlockSpec((LANES, D_MODEL), lambda i, j: (0, 0)),
            pl.BlockSpec(ub, cast_map),
            pl.BlockSpec(db, cast_map),
        ],
        out_specs=[
            pl.BlockSpec((tm, tn), lambda i, j: (i, j)),
            pl.BlockSpec((tm, LANES), lambda i, j: (i, 0)),
            pl.BlockSpec(ub, cast_map),
            pl.BlockSpec(db, cast_map),
        ],
        out_shape=[
            jax.ShapeDtypeStruct((t, n), BF16),
            jax.ShapeDtypeStruct((t, LANES), F32),
            jax.ShapeDtypeStruct(w_up.shape, BF16),
            jax.ShapeDtypeStruct(w_down.shape, BF16),
        ],
        scratch_shapes=[pltpu.VMEM((tm, D_MODEL), BF16)],
        compiler_params=pltpu.CompilerParams(
            dimension_semantics=("arbitrary", "arbitrary"), vmem_limit_bytes=VMEM_LIMIT),
        name="inproj",
    )(x2d, g, w_a, w_b, w_gate, w_up, w_down)


def _attn_kernel(cs_ref, lam_ref, q_ref, k_ref, vt_ref, gain_ref, o_ref,
                 qs_ref, kf_ref, acc_ref, m_ref, *, tq, lam_init):
    tk = tq
    nfeat = ALIBI_TERMS
    dq = 2 * ATTN_QK_DIM
    dv = ATTN_V_DIM
    i = pl.program_id(1)
    cs = [[cs_ref[nfeat * h + t] for t in range(nfeat)] for h in range(ATTN_HEADS)]
    cs_tot = [sum(c[1:], c[0]) for c in cs]

    @pl.when(i == 0)
    def _():
        klane = lax.broadcasted_iota(jnp.int32, (tk, LANES), 1)
        krow = lax.broadcasted_iota(jnp.int32, (tk, LANES), 0).astype(F32)
        frow = lax.broadcasted_iota(jnp.int32, (LANES, 2 * tq), 0)
        fcol = lax.broadcasted_iota(jnp.int32, (LANES, 2 * tq), 1)
        fcol = jnp.where(fcol >= tq, fcol - tq, fcol).astype(F32)
        for h in range(ATTN_HEADS):
            kf = jnp.where(klane < nfeat, krow, 0.0)
            qf = jnp.where((frow >= nfeat) & (frow < 2 * nfeat), -fcol, 0.0)
            for t in range(nfeat):
                kf = jnp.where(klane == nfeat + t, cs[h][t], kf)
                qf = jnp.where(frow == t, cs[h][t], qf)
            kf_ref[h] = kf.astype(BF16)
            qs_ref[h, pl.ds(dq, LANES), :] = qf.astype(BF16)

    row = lax.broadcasted_iota(jnp.int32, (dq, tq), 0)
    for h in range(ATTN_HEADS):
        qt = q_ref[0, :, pl.ds(h * dq, dq)].astype(F32).T
        qs_ref[h, pl.ds(0, dq), pl.ds(0, tq)] = jnp.where(row < ATTN_QK_DIM, qt, 0.0).astype(BF16)
        qs_ref[h, pl.ds(0, dq), pl.ds(tq, tq)] = jnp.where(row >= ATTN_QK_DIM, qt, 0.0).astype(BF16)
    acc_ref[...] = jnp.zeros_like(acc_ref)
    m_ref[...] = jnp.full_like(m_ref, NEG_BIG)

    orow = lax.broadcasted_iota(jnp.int32, (2 * SUBLANES, tk), 0)
    ones_blk = jnp.where(orow == 0, 1.0, 0.0).astype(BF16)

    def scores(h, j):
        k = k_ref[0, pl.ds(pl.multiple_of(j * tk, tk), tk), pl.ds(h * dq, dq)]
        kx = jnp.concatenate([k, kf_ref[h]], axis=1)
        return _dot(kx, qs_ref[h])

    def softmax_pv(h, j, t, diagonal):
        if diagonal:
            krow = lax.broadcasted_iota(jnp.int32, (tk, 2 * tq), 0)
            qcol = lax.broadcasted_iota(jnp.int32, (tk, 2 * tq), 1)
            qcol = jnp.where(qcol >= tq, qcol - tq, qcol)
            t = jnp.where(krow <= qcol, t, NEG_BIG)
        soff = cs_tot[h] * ((j - i) * tk).astype(F32)
        m = m_ref[h]
        m_new = jnp.maximum(m, jnp.max(t, axis=0, keepdims=True) + soff)
        alpha = jnp.exp2(m - m_new)
        p = jnp.exp2(t - (m_new - soff)).astype(BF16)
        vx = jnp.concatenate([vt_ref[0, h, j], ones_blk], axis=0)
        acc_ref[h] = alpha * acc_ref[h] + _dot(vx, p)
        m_ref[h] = m_new

    def run(units):
        ahead = 2
        pending = [scores(*u[:2]) for u in units[:ahead]]
        for n, (h, j, diagonal) in enumerate(units):
            if n + ahead < len(units):
                pending.append(scores(*units[n + ahead][:2]))
            softmax_pv(h, j, pending.pop(0), diagonal)

    def tile_units(j, diagonal):
        return [(h, j, diagonal) for h in range(ATTN_HEADS)]

    def body(jj, carry):
        run(tile_units(2 * jj, False) + tile_units(2 * jj + 1, False))
        return carry

    lax.fori_loop(0, i // 2, body, 0)

    @pl.when(i % 2 == 0)
    def _():
        run(tile_units(i, True))

    @pl.when(i % 2 == 1)
    def _():
        run(tile_units(i - 1, False) + tile_units(i, True))

    lv = lam_ref[...]
    d1 = jnp.sum(lv[0:1] * lv[1:2], axis=-1, keepdims=True)
    d2 = jnp.sum(lv[2:3] * lv[3:4], axis=-1, keepdims=True)
    lam = jnp.exp(d1) - jnp.exp(d2) + lam_init
    for h in range(ATTN_HEADS):
        l = acc_ref[h, pl.ds(dv, 1), :]
        out = (acc_ref[h, pl.ds(0, dv), pl.ds(0, tq)] / l[:, :tq]
               - lam * (acc_ref[h, pl.ds(0, dv), pl.ds(tq, tq)] / l[:, tq:]))
        ms = jnp.mean(out * out, axis=0, keepdims=True)
        on = out * lax.rsqrt(ms + EPS)
        o_ref[0, :, pl.ds(h * dv, dv)] = (on.T * gain_ref[...] * (1.0 - lam_init)).astype(BF16)


def _attn(z3, vt5, cs_terms, lamv, gain, lam_init, tq=256):
    b, s, _ = z3.shape
    nq = s // tq
    kern = functools.partial(_attn_kernel, tq=tq, lam_init=lam_init)
    width = ATTN_HEADS * 2 * ATTN_QK_DIM
    return pl.pallas_call(
        kern,
        grid=(b, nq),
        in_specs=[
            pl.BlockSpec(memory_space=pltpu.SMEM),
            pl.BlockSpec((SUBLANES, LANES), lambda bi, i: (0, 0)),
            pl.BlockSpec((1, tq, width), lambda bi, i: (bi, i, COL_AQ // width)),
            pl.BlockSpec((1, s, width), lambda bi, i: (bi, 0, COL_AK // width)),
            pl.BlockSpec((1, ATTN_HEADS, nq, ATTN_V_DIM, tq), lambda bi, i: (bi, 0, 0, 0, 0)),
            pl.BlockSpec((1, ATTN_V_DIM), lambda bi, i: (0, 0)),
        ],
        out_specs=pl.BlockSpec((1, tq, ATTN_WIDTH), lambda bi, i: (bi, i, 0)),
        out_shape=jax.ShapeDtypeStruct((b, s, ATTN_WIDTH), BF16),
        scratch_shapes=[
            pltpu.VMEM((ATTN_HEADS, 2 * ATTN_QK_DIM + LANES, 2 * tq), BF16),
            pltpu.VMEM((ATTN_HEADS, tq, LANES), BF16),
            pltpu.VMEM((ATTN_HEADS, ATTN_V_DIM + 2 * SUBLANES, 2 * tq), F32),
            pltpu.VMEM((ATTN_HEADS, 1, 2 * tq), F32),
        ],
        compiler_params=pltpu.CompilerParams(
            dimension_semantics=("parallel", "arbitrary"),
            vmem_limit_bytes=VMEM_LIMIT),
        name="attn",
    )(cs_terms, lamv, z3, z3, vt5, gain)


def _split3(x):
    hi = x.astype(BF16)
    r = x - hi.astype(F32)
    mid = r.astype(BF16)
    lo = (r - mid.astype(F32)).astype(BF16)
    return hi, mid, lo


def _mlstm_kernel(uq_ref, uk_ref, v_ref, mo_ref, gcol_ref, grow_ref, bcol_ref, brow_ref,
                  cw_ref, cb_ref, gn_ref, o_ref, extq, extk, c_scr, n_scr, m_scr, *, chunk):
    L = chunk
    nh = MLSTM_HEADS
    dk = MLSTM_QK_DIM
    dv = MLSTM_V_DIM
    heads = range(nh)

    @pl.when(pl.program_id(1) == 0)
    def _():
        extq[pl.ds(0, SUBLANES), :] = jnp.zeros((SUBLANES, MLSTM_QK_WIDTH), F32)
        extk[pl.ds(0, SUBLANES), :] = jnp.zeros((SUBLANES, MLSTM_QK_WIDTH), F32)
        c_scr[...] = jnp.zeros_like(c_scr)
        n_scr[...] = jnp.zeros_like(n_scr)
        m_scr[...] = jnp.zeros_like(m_scr)

    def conv_silu(u_ref, ext, col0):
        ext[pl.ds(SUBLANES, L), :] = u_ref[0].astype(F32)
        w = cw_ref[:, pl.ds(col0, MLSTM_QK_WIDTH)]
        y = cb_ref[:, pl.ds(col0, MLSTM_QK_WIDTH)]
        for tap in range(CONV_WIDTH):
            off = SUBLANES - (CONV_WIDTH - 1) + tap
            y = y + ext[pl.ds(off, L), :] * w[tap:tap + 1]
        ext[pl.ds(0, SUBLANES), :] = ext[pl.ds(L, SUBLANES), :]
        return y * _sigmoid(y)

    q_all = conv_silu(uq_ref, extq, 0)
    k_all = conv_silu(uk_ref, extk, MLSTM_QK_WIDTH) * (dk ** -0.5)
    q = [q_all[:, h * dk:(h + 1) * dk] for h in heads]
    k = [k_all[:, h * dk:(h + 1) * dk] for h in heads]
    qb = [x.astype(BF16) for x in q]
    kb = [x.astype(BF16) for x in k]
    vb = [v_ref[0, :, pl.ds(h * dv, dv)] for h in heads]

    g_c = gcol_ref[...] + bcol_ref[...]
    g_r = grow_ref[0] + brow_ref[:, 0:1]
    lf_c = _log_sigmoid(g_c)
    lf_r = _log_sigmoid(g_r)

    r_i = lax.broadcasted_iota(jnp.int32, (L, L), 0)
    c_i = lax.broadcasted_iota(jnp.int32, (L, L), 1)
    causal = c_i <= r_i
    tri = jnp.where(causal, 1.0, 0.0).astype(BF16)
    tri_t = jnp.where(r_i <= c_i, 1.0, 0.0).astype(BF16)
    hi, mid, lo = _split3(lf_c)
    b_c = _dot(tri, hi) + _dot(tri, mid) + _dot(tri, lo)
    hi, mid, lo = _split3(lf_r)
    b_r = _dot(hi, tri_t) + _dot(mid, tri_t) + _dot(lo, tri_t)
    g_sum = jnp.sum(lf_r, axis=1, keepdims=True)

    lane = lax.broadcasted_iota(jnp.int32, (L, LANES), 1)

    def col(x, idx):
        return jnp.sum(jnp.where(lane == idx, x, 0.0), axis=1, keepdims=True)

    i_col = [col(g_c, h) for h in heads]
    b_col = [col(b_c, nh + h) for h in heads]
    i_row = [g_r[h:h + 1, :] for h in heads]
    b_row = [b_r[nh + h:nh + h + 1, :] for h in heads]
    g_tot = [g_sum[nh + h:nh + h + 1, :] for h in heads]

    c_prev = [c_scr[h] for h in heads]
    n_prev = [n_scr[h] for h in heads]
    m_prev = [m_scr[h][:, 0:1] for h in heads]

    s_qk = [_dot_nt(qb[h], kb[h]) for h in heads]
    inter = [_dot(qb[h], c_prev[h].astype(BF16)) for h in heads]
    m_loc = [jnp.max(g_tot[h] - b_row[h] + i_row[h], axis=1, keepdims=True) for h in heads]
    kw = [k[h] * jnp.exp(g_tot[h] - b_col[h] + i_col[h] - m_loc[h]) for h in heads]
    c_loc = [_dot(kw[h].T.astype(BF16), vb[h]) for h in heads]

    for h in heads:
        d = jnp.where(causal, b_col[h] - b_row[h] + i_row[h], NEG_BIG)
        m_inter = b_col[h] + m_prev[h]
        m_j = jnp.maximum(m_inter, jnp.max(d, axis=1, keepdims=True))
        w_inter = jnp.exp(m_inter - m_j)
        p = s_qk[h] * jnp.exp(d - m_j)
        num = w_inter * inter[h] + _dot(p.astype(BF16), vb[h])
        den = (w_inter * jnp.sum(q[h] * n_prev[h], axis=1, keepdims=True)
               + jnp.sum(p, axis=1, keepdims=True))
        hval = num / jnp.maximum(jnp.abs(den), jnp.exp(-m_j))
        hn = _rms(hval, gn_ref[:, pl.ds(h * dv, dv)])
        gate = _sigmoid(mo_ref[0, :, pl.ds(h * dv, dv)].astype(F32))
        o_ref[0, :, pl.ds(h * dv, dv)] = (hn * gate).astype(BF16)

    for h in heads:
        n_loc = jnp.sum(kw[h], axis=0, keepdims=True)
        m_new = jnp.maximum(g_tot[h] + m_prev[h], m_loc[h])
        a_old = jnp.exp(g_tot[h] + m_prev[h] - m_new)
        a_new = jnp.exp(m_loc[h] - m_new)
        c_scr[h] = a_old * c_prev[h] + a_new * c_loc[h]
        n_scr[h] = a_old * n_prev[h] + a_new * n_loc
        m_scr[h] = jnp.broadcast_to(m_new, (1, LANES))


def _mlstm(z3, gates, gates_t, bias_col, bias_row, conv_w8, conv_b, gnorm, chunk=256):
    b, s, _ = z3.shape
    nc = s // chunk
    kern = functools.partial(_mlstm_kernel, chunk=chunk)
    qw = MLSTM_QK_WIDTH
    vw = MLSTM_WIDTH
    return pl.pallas_call(
        kern,
        grid=(b, nc),
        in_specs=[
            pl.BlockSpec((1, chunk, qw), lambda bi, c: (bi, c, COL_MQ // qw)),
            pl.BlockSpec((1, chunk, qw), lambda bi, c: (bi, c, COL_MK // qw)),
            pl.BlockSpec((1, chunk, vw), lambda bi, c: (bi, c, COL_MV // vw)),
            pl.BlockSpec((1, chunk, vw), lambda bi, c: (bi, c, COL_MO // vw)),
            pl.BlockSpec((chunk, LANES), lambda bi, c: (bi * nc + c, 0)),
            pl.BlockSpec((1, SUBLANES, chunk), lambda bi, c: (bi, 0, c)),
            pl.BlockSpec((1, LANES), lambda bi, c: (0, 0)),
            pl.BlockSpec((SUBLANES, LANES), lambda bi, c: (0, 0)),
            pl.BlockSpec((SUBLANES, 2 * qw), lambda bi, c: (0, 0)),
            pl.BlockSpec((1, 2 * qw), lambda bi, c: (0, 0)),
            pl.BlockSpec((1, vw), lambda bi, c: (0, 0)),
        ],
        out_specs=pl.BlockSpec((1, chunk, vw), lambda bi, c: (bi, c, 0)),
        out_shape=jax.ShapeDtypeStruct((b, s, vw), BF16),
        scratch_shapes=[
            pltpu.VMEM((chunk + 2 * SUBLANES, qw), F32),
            pltpu.VMEM((chunk + 2 * SUBLANES, qw), F32),
            pltpu.VMEM((MLSTM_HEADS, MLSTM_QK_DIM, MLSTM_V_DIM), F32),
            pltpu.VMEM((MLSTM_HEADS, 1, MLSTM_QK_DIM), F32),
            pltpu.VMEM((MLSTM_HEADS, 1, LANES), F32),
        ],
        compiler_params=pltpu.CompilerParams(
            dimension_semantics=("parallel", "arbitrary"),
            vmem_limit_bytes=VMEM_LIMIT),
        name="mlstm",
    )(z3, z3, z3, z3, gates, gates_t, bias_col, bias_row, conv_w8, conv_b, gnorm)


def _merge_kernel(att_ref, hm_ref, ga_ref, gm_ref, x_ref, wa_ref, wm_ref, wo_ref, o_ref):
    @pl.when(pl.program_id(1) == 0)
    def _():
        o_ref[...] = x_ref[...]

    a = _dot(att_ref[...], wa_ref[...])
    bm = _dot(hm_ref[...], wm_ref[...])
    y = _sigmoid(ga_ref[...].astype(F32)) * a + _sigmoid(gm_ref[...].astype(F32)) * bm
    o_ref[...] += _dot(y.astype(BF16), wo_ref[...])


def _merge(att2d, hm2d, z2d, x2d, wa, wm, wo, tm=512, tn=1024):
    t = x2d.shape[0]
    nj = D_MODEL // tn
    return pl.pallas_call(
        _merge_kernel,
        grid=(t // tm, nj),
        in_specs=[
            pl.BlockSpec((tm, ATTN_WIDTH), lambda i, j: (i, 0)),
            pl.BlockSpec((tm, MLSTM_WIDTH), lambda i, j: (i, 0)),
            pl.BlockSpec((tm, tn), lambda i, j: (i, COL_GA // tn + j)),
            pl.BlockSpec((tm, tn), lambda i, j: (i, COL_GM // tn + j)),
            pl.BlockSpec((tm, D_MODEL), lambda i, j: (i, 0)),
            pl.BlockSpec((ATTN_WIDTH, tn), lambda i, j: (0, j)),
            pl.BlockSpec((MLSTM_WIDTH, tn), lambda i, j: (0, j)),
            pl.BlockSpec((tn, D_MODEL), lambda i, j: (j, 0)),
        ],
        out_specs=pl.BlockSpec((tm, D_MODEL), lambda i, j: (i, 0)),
        out_shape=jax.ShapeDtypeStruct((t, D_MODEL), F32),
        compiler_params=pltpu.CompilerParams(
            dimension_semantics=("parallel", "arbitrary"), vmem_limit_bytes=VMEM_LIMIT),
        name="merge",
    )(att2d, hm2d, z2d, z2d, x2d, wa, wm, wo)


def _cross_kernel(x_ref, g_ref, wq_ref, kv_ref, wo_ref, o_ref):
    x = x_ref[...]
    hc = _rms(x, g_ref[...]).astype(BF16)
    cq = (_dot(hc, wq_ref[...]) * (CROSS_HEAD_DIM ** -0.5)).astype(BF16)
    outs = []
    for hh in range(CROSS_HEADS):
        lo = hh * CROSS_HEAD_DIM
        qh = cq[:, lo:lo + CROSS_HEAD_DIM]
        kh = kv_ref[0, :, lo:lo + CROSS_HEAD_DIM]
        vh = kv_ref[0, :, CROSS_WIDTH + lo:CROSS_WIDTH + lo + CROSS_HEAD_DIM]
        s = _dot_nt(qh, kh)
        p = jnp.exp(s - jnp.max(s, axis=-1, keepdims=True))
        l = jnp.sum(p, axis=-1, keepdims=True)
        outs.append((_dot(p.astype(BF16), vh) / l).astype(BF16))
    co = jnp.concatenate(outs, axis=1)
    o_ref[...] = x + _dot(co, wo_ref[...])


def _cross(x2d, g, wq, ckv, wo, seq, tm=512):
    t = x2d.shape[0]
    n_mem = ckv.shape[1]
    per_batch = seq // tm
    return pl.pallas_call(
        _cross_kernel,
        grid=(t // tm,),
        in_specs=[
            pl.BlockSpec((tm, D_MODEL), lambda i: (i, 0)),
            pl.BlockSpec((1, D_MODEL), lambda i: (0, 0)),
            pl.BlockSpec((D_MODEL, CROSS_WIDTH), lambda i: (0, 0)),
            pl.BlockSpec((1, n_mem, 2 * CROSS_WIDTH), lambda i: (i // per_batch, 0, 0)),
            pl.BlockSpec((CROSS_WIDTH, D_MODEL), lambda i: (0, 0)),
        ],
        out_specs=pl.BlockSpec((tm, D_MODEL), lambda i: (i, 0)),
        out_shape=jax.ShapeDtypeStruct((t, D_MODEL), F32),
        compiler_params=pltpu.CompilerParams(
            dimension_semantics=("parallel",), vmem_limit_bytes=VMEM_LIMIT),
        name="cross",
    )(x2d, g, wq, ckv, wo)


def _mlp_kernel(x_ref, g_ref, wu_ref, wd_ref, gf_ref, o_ref, h_scr, acc, *, final_norm):
    j = pl.program_id(1)

    @pl.when(j == 0)
    def _():
        x = x_ref[...]
        h_scr[...] = _rms(x, g_ref[...]).astype(BF16)
        acc[...] = x

    u = jnp.square(jnp.maximum(_dot(h_scr[...], wu_ref[...]), 0.0)).astype(BF16)
    acc[...] += _dot(u, wd_ref[...])

    @pl.when(j == pl.num_programs(1) - 1)
    def _():
        if final_norm:
            o_ref[...] = _rms(acc[...], gf_ref[...])
        else:
            o_ref[...] = acc[...]


def _mlp(x2d, g, wu, wd, gf, final_norm, tm=512, tf=1024):
    t = x2d.shape[0]
    kern = functools.partial(_mlp_kernel, final_norm=final_norm)
    return pl.pallas_call(
        kern,
        grid=(t // tm, D_FF // tf),
        in_specs=[
            pl.BlockSpec((tm, D_MODEL), lambda i, j: (i, 0)),
            pl.BlockSpec((1, D_MODEL), lambda i, j: (0, 0)),
            pl.BlockSpec((D_MODEL, tf), lambda i, j: (0, j)),
            pl.BlockSpec((tf, D_MODEL), lambda i, j: (j, 0)),
            pl.BlockSpec((1, D_MODEL), lambda i, j: (0, 0)),
        ],
        out_specs=pl.BlockSpec((tm, D_MODEL), lambda i, j: (i, 0)),
        out_shape=jax.ShapeDtypeStruct((t, D_MODEL), F32),
        scratch_shapes=[pltpu.VMEM((tm, D_MODEL), BF16), pltpu.VMEM((tm, D_MODEL), F32)],
        compiler_params=pltpu.CompilerParams(
            dimension_semantics=("parallel", "arbitrary"), vmem_limit_bytes=VMEM_LIMIT),
        name="mlp",
    )(x2d, g, wu, wd, gf)


def _alibi_terms():
    slopes = 2.0 ** (-8.0 * np.arange(1, ATTN_HEADS + 1, dtype=np.float64) / ATTN_HEADS)
    rem = slopes * LOG2E
    terms = []
    for _ in range(ALIBI_TERMS):
        t = rem.astype(np.float32).astype(ml_dtypes.bfloat16).astype(np.float64)
        terms.append(t)
        rem = rem - t
    return np.stack(terms, axis=1).reshape(-1).astype(np.float32)


def _in_col_scale():
    scale = np.ones((GATE_LO, 1), np.float32)
    scale[COL_AQ:COL_AK] = ATTN_QK_DIM ** -0.5 * LOG2E
    return scale


def _pad_rows(a, rows):
    return jnp.pad(a, ((0, rows - a.shape[0]), (0, 0)))


def kernel(x, mem, norm_mix, w_in, b_igate, b_fgate, conv_w, conv_b, lam_q1, lam_k1, lam_q2, lam_k2, attn_norm, mlstm_norm, w_attn_br, w_mlstm_br, w_out, norm_cross, norm_mem, w_cq, w_ckv, w_co, norm_mlp, w_up, w_down, norm_final):
    b, s, _ = x.shape
    t = b * s
    tq = 256
    cs_terms = jnp.asarray(_alibi_terms())
    x2d = x.reshape(t, D_MODEL)
    for l in range(DEPTH):
        lam_init = 0.8 - 0.6 * math.exp(-0.3 * l)
        w = w_in[l]
        wt = w.T
        w_a = (wt[:GATE_LO] * jnp.asarray(_in_col_scale())).astype(BF16)
        w_b = wt[GATE_HI:].astype(BF16)
        w_gate = _pad_rows(wt[GATE_LO:GATE_HI], LANES).astype(BF16)

        z2d, gates, w_up_b, w_down_b = _inproj(x2d, norm_mix[l][None, :], w_a, w_b, w_gate,
                                               w_up[l], w_down[l])
        z3 = z2d.reshape(b, s, IN_MAIN)

        vt5 = (z3[:, :, COL_AV:COL_AV + ATTN_WIDTH]
               .reshape(b, s // tq, tq, ATTN_HEADS, ATTN_V_DIM).transpose(0, 3, 1, 4, 2))
        lamv = _pad_rows(jnp.pad(jnp.stack([lam_q1[l], lam_k1[l], lam_q2[l], lam_k2[l]]),
                                 ((0, 0), (0, LANES - ATTN_QK_DIM))), SUBLANES)
        att = _attn(z3, vt5, cs_terms, lamv, attn_norm[l][None, :], lam_init, tq=tq)

        gates_t = gates[:, :SUBLANES].reshape(b, s, SUBLANES).transpose(0, 2, 1)
        gate_bias = jnp.concatenate([b_igate[l], b_fgate[l]])
        bias_col = jnp.pad(gate_bias, (0, LANES - gate_bias.shape[0]))[None, :]
        bias_row = jnp.broadcast_to(gate_bias[:, None], (SUBLANES, LANES))
        hm = _mlstm(z3, gates, gates_t, bias_col, bias_row, _pad_rows(conv_w[l], SUBLANES),
                    conv_b[l][None, :], mlstm_norm[l][None, :])

        x2d = _merge(att.reshape(t, ATTN_WIDTH), hm.reshape(t, MLSTM_WIDTH), z2d, x2d,
                     w_attn_br[l].astype(BF16), w_mlstm_br[l].astype(BF16), w_out[l].astype(BF16))

        ckv = _memkv(mem, norm_mem[l][None, :], w_ckv[l])
        x2d = _cross(x2d, norm_cross[l][None, :], w_cq[l].astype(BF16), ckv,
                     w_co[l].astype(BF16), s)

        x2d = _mlp(x2d, norm_mlp[l][None, :], w_up_b, w_down_b,
                   norm_final[None, :], final_norm=(l == DEPTH - 1))
    return x2d.reshape(b, s, D_MODEL)
```

```python
import functools
import math

import ml_dtypes
import numpy as np
import jax
import jax.numpy as jnp
from jax import lax
from jax.experimental import pallas as pl
from jax.experimental.pallas import tpu as pltpu

F32 = jnp.float32
BF16 = jnp.bfloat16

D_MODEL = 2048
DEPTH = 1
ATTN_HEADS = 8
ATTN_QK_DIM = 64
ATTN_V_DIM = 128
ATTN_WIDTH = ATTN_HEADS * ATTN_V_DIM
MLSTM_HEADS = 4
MLSTM_QK_DIM = 128
MLSTM_V_DIM = 256
MLSTM_QK_WIDTH = MLSTM_HEADS * MLSTM_QK_DIM
MLSTM_WIDTH = MLSTM_HEADS * MLSTM_V_DIM
CONV_WIDTH = 4
CROSS_HEADS = 4
CROSS_HEAD_DIM = 128
CROSS_WIDTH = CROSS_HEADS * CROSS_HEAD_DIM
D_FF = 4 * D_MODEL
EPS = 1e-6
LANES = 128
SUBLANES = 8
NEG_BIG = -1e30
LOG2E = 1.4426950408889634
ALIBI_TERMS = 3

COL_AQ = 0
COL_AK = 1024
COL_MQ = 2048
COL_MK = 2560
COL_MV = 3072
COL_MO = 4096
COL_GA = 5120
COL_GM = 7168
IN_MAIN = 9216
W_AQ_LO, W_AQ_HI = 0, 1024
W_AV_LO = 2048
GATE_LO = 6144
GATE_HI = 6152

VMEM_LIMIT = 56 * 1024 * 1024


def _rms(x, g):
    ms = jnp.mean(x * x, axis=-1, keepdims=True)
    return x * lax.rsqrt(ms + EPS) * g


def _sigmoid(x):
    return 1.0 / (1.0 + jnp.exp(-x))


def _log_sigmoid(x):
    return jnp.minimum(x, 0.0) - jnp.log(1.0 + jnp.exp(-jnp.abs(x)))


def _dot(a, b):
    return jnp.dot(a, b, preferred_element_type=F32)


def _dot_nt(a, b):
    return lax.dot_general(a, b, (((1,), (1,)), ((), ())), preferred_element_type=F32)


def _memkv_kernel(mem_ref, g_ref, w_ref, o_ref):
    mn = _rms(mem_ref[0], g_ref[...]).astype(BF16)
    o_ref[0] = _dot(mn, w_ref[...].astype(BF16)).astype(BF16)


def _memkv(mem, g, w_ckv):
    b, n_mem, _ = mem.shape
    return pl.pallas_call(
        _memkv_kernel,
        grid=(b,),
        in_specs=[
            pl.BlockSpec((1, n_mem, D_MODEL), lambda i: (i, 0, 0)),
            pl.BlockSpec((1, D_MODEL), lambda i: (0, 0)),
            pl.BlockSpec((D_MODEL, 2 * CROSS_WIDTH), lambda i: (0, 0)),
        ],
        out_specs=pl.BlockSpec((1, n_mem, 2 * CROSS_WIDTH), lambda i: (i, 0, 0)),
        out_shape=jax.ShapeDtypeStruct((b, n_mem, 2 * CROSS_WIDTH), BF16),
        compiler_params=pltpu.CompilerParams(
            dimension_semantics=("arbitrary",), vmem_limit_bytes=VMEM_LIMIT),
        name="memkv",
    )(mem, g, w_ckv)


def _inproj_kernel(x_ref, g_ref, wa_ref, wb_ref, wg_ref, wu_ref, wd_ref,
                   z_ref, gate_ref, vt_ref, wu_out, wd_out, h_scr, *, na, jv, ncast):
    j = pl.program_id(1)

    @pl.when(j == 0)
    def _():
        h = _rms(x_ref[...], g_ref[...]).astype(BF16)
        h_scr[...] = h
        gate_ref[...] = _dot_nt(h, wg_ref[...])

    @pl.when((j < na) & (j != jv))
    def _():
        z_ref[...] = _dot_nt(h_scr[...], wa_ref[...]).astype(BF16)

    @pl.when(j == jv)
    def _():
        vt = _dot_nt(wa_ref[...], h_scr[...]).astype(BF16)
        tkv = vt_ref.shape[2]
        for n in range(vt_ref.shape[0]):
            vt_ref[n] = vt[:, n * tkv:(n + 1) * tkv]

    @pl.when(j >= na)
    def _():
        z_ref[...] = _dot_nt(h_scr[...], wb_ref[...]).astype(BF16)

    @pl.when(j < ncast)
    def _():
        wu_out[...] = wu_ref[...].astype(BF16)
        wd_out[...] = wd_ref[...].astype(BF16)


def _inproj(x2d, g, w_a, w_b, w_gate, w_up, w_down, tkv, tm=1024, tn=1024):
    t = x2d.shape[0]
    na = w_a.shape[0] // tn
    n = w_a.shape[0] + w_b.shape[0]
    ni, nj = t // tm, n // tn
    assert tn == ATTN_WIDTH and W_AV_LO % tn == 0
    jv = W_AV_LO // tn
    ncast = 8
    assert ncast <= nj
    ub = (w_up.shape[0] // ni, w_up.shape[1] // ncast)
    db = (w_down.shape[0] // ni, w_down.shape[1] // ncast)
    cast_map = lambda i, j: (i, jnp.minimum(j, ncast - 1))
    return pl.pallas_call(
        functools.partial(_inproj_kernel, na=na, jv=jv, ncast=ncast),
        grid=(ni, nj),
        in_specs=[
            pl.BlockSpec((tm, D_MODEL), lambda i, j: (i, 0)),
            pl.BlockSpec((1, D_MODEL), lambda i, j: (0, 0)),
            pl.BlockSpec((tn, D_MODEL), lambda i, j: (jnp.minimum(j, na - 1), 0)),
            pl.BlockSpec((tn, D_MODEL), lambda i, j: (jnp.maximum(j - na, 0), 0)),
            pl.BlockSpec((LANES, D_MODEL), lambda i, j: (0, 0)),
            pl.BlockSpec(ub, cast_map),
            pl.BlockSpec(db, cast_map),
        ],
        out_specs=[
            pl.BlockSpec((tm, tn), lambda i, j: (i, j - (j >= jv).astype(jnp.int32))),
            pl.BlockSpec((tm, LANES), lambda i, j: (i, 0)),
            pl.BlockSpec((tm // tkv, ATTN_WIDTH, tkv), lambda i, j: (i, 0, 0)),
            pl.BlockSpec(ub, cast_map),
            pl.BlockSpec(db, cast_map),
        ],
        out_shape=[
            jax.ShapeDtypeStruct((t, n - tn), BF16),
            jax.ShapeDtypeStruct((t, LANES), F32),
            jax.ShapeDtypeStruct((t // tkv, ATTN_WIDTH, tkv), BF16),
            jax.ShapeDtypeStruct(w_up.shape, BF16),
            jax.ShapeDtypeStruct(w_down.shape, BF16),
        ],
        scratch_shapes=[pltpu.VMEM((tm, D_MODEL), BF16)],
        compiler_params=pltpu.CompilerParams(
            dimension_semantics=("arbitrary", "arbitrary"), vmem_limit_bytes=VMEM_LIMIT),
        name="inproj",
    )(x2d, g, w_a, w_b, w_gate, w_up, w_down)


def _attn_kernel(cs_ref, lam_ref, q_ref, k_ref, vt_ref, gain_ref, wo_ref, wa_ref, wm_ref,
                 o_ref, wo_out, wa_out, wm_out, qs_ref, kf_ref, acc_ref, m_ref, *, tq, lam_init):
    tk = tq
    nfeat = ALIBI_TERMS
    dq = 2 * ATTN_QK_DIM
    dv = ATTN_V_DIM
    i = pl.program_id(1)
    cs = [[cs_ref[nfeat * h + t] for t in range(nfeat)] for h in range(ATTN_HEADS)]
    cs_tot = [sum(c[1:], c[0]) for c in cs]

    @pl.when(i == 0)
    def _():
        klane = lax.broadcasted_iota(jnp.int32, (tk, LANES), 1)
        krow = lax.broadcasted_iota(jnp.int32, (tk, LANES), 0).astype(F32)
        frow = lax.broadcasted_iota(jnp.int32, (LANES, 2 * tq), 0)
        fcol = lax.broadcasted_iota(jnp.int32, (LANES, 2 * tq), 1)
        fcol = jnp.where(fcol >= tq, fcol - tq, fcol).astype(F32)
        for h in range(ATTN_HEADS):
            kf = jnp.where(klane < nfeat, krow, 0.0)
            qf = jnp.where((frow >= nfeat) & (frow < 2 * nfeat), -fcol, 0.0)
            for t in range(nfeat):
                kf = jnp.where(klane == nfeat + t, cs[h][t], kf)
                qf = jnp.where(frow == t, cs[h][t], qf)
            kf_ref[h] = kf.astype(BF16)
            qs_ref[h, pl.ds(dq, LANES), :] = qf.astype(BF16)

    row = lax.broadcasted_iota(jnp.int32, (dq, tq), 0)
    for h in range(ATTN_HEADS):
        qt = q_ref[0, :, pl.ds(h * dq, dq)].astype(F32).T
        qs_ref[h, pl.ds(0, dq), pl.ds(0, tq)] = jnp.where(row < ATTN_QK_DIM, qt, 0.0).astype(BF16)
        qs_ref[h, pl.ds(0, dq), pl.ds(tq, tq)] = jnp.where(row >= ATTN_QK_DIM, qt, 0.0).astype(BF16)
    acc_ref[...] = jnp.zeros_like(acc_ref)
    m_ref[...] = jnp.full_like(m_ref, NEG_BIG)

    orow = lax.broadcasted_iota(jnp.int32, (2 * SUBLANES, tk), 0)
    ones_blk = jnp.where(orow == 0, 1.0, 0.0).astype(BF16)

    def scores(h, j):
        k = k_ref[0, pl.ds(pl.multiple_of(j * tk, tk), tk), pl.ds(h * dq, dq)]
        kx = jnp.concatenate([k, kf_ref[h]], axis=1)
        return _dot(kx, qs_ref[h])

    def softmax_pv(h, j, t, diagonal):
        if diagonal:
            krow = lax.broadcasted_iota(jnp.int32, (tk, 2 * tq), 0)
            qcol = lax.broadcasted_iota(jnp.int32, (tk, 2 * tq), 1)
            qcol = jnp.where(qcol >= tq, qcol - tq, qcol)
            t = jnp.where(krow <= qcol, t, NEG_BIG)
        soff = cs_tot[h] * ((j - i) * tk).astype(F32)
        m = m_ref[h]
        m_new = jnp.maximum(m, jnp.max(t, axis=0, keepdims=True) + soff)
        alpha = jnp.exp2(m - m_new)
        p = jnp.exp2(t - (m_new - soff)).astype(BF16)
        vx = jnp.concatenate([vt_ref[0, j, h], ones_blk], axis=0)
        acc_ref[h] = alpha * acc_ref[h] + _dot(vx, p)
        m_ref[h] = m_new

    def run(units):
        ahead = 2
        pending = [scores(*u[:2]) for u in units[:ahead]]
        for n, (h, j, diagonal) in enumerate(units):
            if n + ahead < len(units):
                pending.append(scores(*units[n + ahead][:2]))
            softmax_pv(h, j, pending.pop(0), diagonal)

    def tile_units(j, diagonal):
        return [(h, j, diagonal) for h in range(ATTN_HEADS)]

    def body(jj, carry):
        run(tile_units(2 * jj, False) + tile_units(2 * jj + 1, False))
        return carry

    lax.fori_loop(0, i // 2, body, 0)

    @pl.when(i % 2 == 0)
    def _():
        run(tile_units(i, True))

    @pl.when(i % 2 == 1)
    def _():
        run(tile_units(i - 1, False) + tile_units(i, True))

    lv = lam_ref[...]
    d1 = jnp.sum(lv[0:1] * lv[1:2], axis=-1, keepdims=True)
    d2 = jnp.sum(lv[2:3] * lv[3:4], axis=-1, keepdims=True)
    lam = jnp.exp(d1) - jnp.exp(d2) + lam_init
    for h in range(ATTN_HEADS):
        l = acc_ref[h, pl.ds(dv, 1), :]
        out = (acc_ref[h, pl.ds(0, dv), pl.ds(0, tq)] / l[:, :tq]
               - lam * (acc_ref[h, pl.ds(0, dv), pl.ds(tq, tq)] / l[:, tq:]))
        ms = jnp.mean(out * out, axis=0, keepdims=True)
        on = out * lax.rsqrt(ms + EPS)
        o_ref[0, :, pl.ds(h * dv, dv)] = (on.T * gain_ref[...] * (1.0 - lam_init)).astype(BF16)

    wo_out[...] = wo_ref[...].astype(BF16)
    wa_out[...] = wa_ref[...].astype(BF16)
    wm_out[...] = wm_ref[...].astype(BF16)


def _attn(z3, vt5, cs_terms, lamv, gain, w_out, w_abr, w_mbr, lam_init, tq=256):
    b, s, _ = z3.shape
    nq = s // tq
    steps = b * nq
    cast_specs = [pl.BlockSpec((w.shape[0] // steps, w.shape[1]), lambda bi, i: (bi * nq + i, 0))
                  for w in (w_out, w_abr, w_mbr)]
    kern = functools.partial(_attn_kernel, tq=tq, lam_init=lam_init)
    width = ATTN_HEADS * 2 * ATTN_QK_DIM
    return pl.pallas_call(
        kern,
        grid=(b, nq),
        in_specs=[
            pl.BlockSpec(memory_space=pltpu.SMEM),
            pl.BlockSpec((SUBLANES, LANES), lambda bi, i: (0, 0)),
            pl.BlockSpec((1, tq, width), lambda bi, i: (bi, i, COL_AQ // width)),
            pl.BlockSpec((1, s, width), lambda bi, i: (bi, 0, COL_AK // width)),
            pl.BlockSpec((1, nq, ATTN_HEADS, ATTN_V_DIM, tq), lambda bi, i: (bi, 0, 0, 0, 0)),
            pl.BlockSpec((1, ATTN_V_DIM), lambda bi, i: (0, 0)),
        ] + cast_specs,
        out_specs=[pl.BlockSpec((1, tq, ATTN_WIDTH), lambda bi, i: (bi, i, 0))] + cast_specs,
        out_shape=[jax.ShapeDtypeStruct((b, s, ATTN_WIDTH), BF16)]
        + [jax.ShapeDtypeStruct(w.shape, BF16) for w in (w_out, w_abr, w_mbr)],
        scratch_shapes=[
            pltpu.VMEM((ATTN_HEADS, 2 * ATTN_QK_DIM + LANES, 2 * tq), BF16),
            pltpu.VMEM((ATTN_HEADS, tq, LANES), BF16),
            pltpu.VMEM((ATTN_HEADS, ATTN_V_DIM + 2 * SUBLANES, 2 * tq), F32),
            pltpu.VMEM((ATTN_HEADS, 1, 2 * tq), F32),
        ],
        compiler_params=pltpu.CompilerParams(
            dimension_semantics=("parallel", "arbitrary"),
            vmem_limit_bytes=VMEM_LIMIT),
        name="attn",
    )(cs_terms, lamv, z3, z3, vt5, gain, w_out, w_abr, w_mbr)


def _split3(x):
    hi = x.astype(BF16)
    r = x - hi.astype(F32)
    mid = r.astype(BF16)
    lo = (r - mid.astype(F32)).astype(BF16)
    return hi, mid, lo


def _mlstm_kernel(uq_ref, uk_ref, v_ref, mo_ref, gcol_ref, grow_ref, bcol_ref, brow_ref,
                  cw_ref, cb_ref, gn_ref, o_ref, extq, extk, c_scr, n_scr, m_scr, *, chunk):
    L = chunk
    nh = MLSTM_HEADS
    dk = MLSTM_QK_DIM
    dv = MLSTM_V_DIM
    heads = range(nh)

    @pl.when(pl.program_id(1) == 0)
    def _():
        extq[pl.ds(0, SUBLANES), :] = jnp.zeros((SUBLANES, MLSTM_QK_WIDTH), F32)
        extk[pl.ds(0, SUBLANES), :] = jnp.zeros((SUBLANES, MLSTM_QK_WIDTH), F32)
        c_scr[...] = jnp.zeros_like(c_scr)
        n_scr[...] = jnp.zeros_like(n_scr)
        m_scr[...] = jnp.zeros_like(m_scr)

    def conv_silu(u_ref, ext, col0):
        ext[pl.ds(SUBLANES, L), :] = u_ref[0].astype(F32)
        w = cw_ref[:, pl.ds(col0, MLSTM_QK_WIDTH)]
        y = cb_ref[:, pl.ds(col0, MLSTM_QK_WIDTH)]
        for tap in range(CONV_WIDTH):
            off = SUBLANES - (CONV_WIDTH - 1) + tap
            y = y + ext[pl.ds(off, L), :] * w[tap:tap + 1]
        ext[pl.ds(0, SUBLANES), :] = ext[pl.ds(L, SUBLANES), :]
        return y * _sigmoid(y)

    q_all = conv_silu(uq_ref, extq, 0)
    k_all = conv_silu(uk_ref, extk, MLSTM_QK_WIDTH) * (dk ** -0.5)
    q = [q_all[:, h * dk:(h + 1) * dk] for h in heads]
    k = [k_all[:, h * dk:(h + 1) * dk] for h in heads]
    qb = [x.astype(BF16) for x in q]
    kb = [x.astype(BF16) for x in k]
    vb = [v_ref[0, :, pl.ds(h * dv, dv)] for h in heads]

    g_c = gcol_ref[...] + bcol_ref[...]
    g_r = grow_ref[0] + brow_ref[:, 0:1]
    lf_c = _log_sigmoid(g_c)
    lf_r = _log_sigmoid(g_r)

    r_i = lax.broadcasted_iota(jnp.int32, (L, L), 0)
    c_i = lax.broadcasted_iota(jnp.int32, (L, L), 1)
    causal = c_i <= r_i
    tri = jnp.where(causal, 1.0, 0.0).astype(BF16)
    tri_t = jnp.where(r_i <= c_i, 1.0, 0.0).astype(BF16)
    hi, mid, lo = _split3(lf_c)
    b_c = _dot(tri, hi) + _dot(tri, mid) + _dot(tri, lo)
    hi, mid, lo = _split3(lf_r)
    b_r = _dot(hi, tri_t) + _dot(mid, tri_t) + _dot(lo, tri_t)
    g_sum = jnp.sum(lf_r, axis=1, keepdims=True)

    lane = lax.broadcasted_iota(jnp.int32, (L, LANES), 1)

    def col(x, idx):
        return jnp.sum(jnp.where(lane == idx, x, 0.0), axis=1, keepdims=True)

    i_col = [col(g_c, h) for h in heads]
    b_col = [col(b_c, nh + h) for h in heads]
    i_row = [g_r[h:h + 1, :] for h in heads]
    b_row = [b_r[nh + h:nh + h + 1, :] for h in heads]
    g_tot = [g_sum[nh + h:nh + h + 1, :] for h in heads]

    c_prev = [c_scr[h] for h in heads]
    n_prev = [n_scr[h] for h in heads]
    m_prev = [m_scr[h][:, 0:1] for h in heads]

    s_qk = [_dot_nt(qb[h], kb[h]) for h in heads]
    inter = [_dot(qb[h], c_prev[h].astype(BF16)) for h in heads]
    m_loc = [jnp.max(g_tot[h] - b_row[h] + i_row[h], axis=1, keepdims=True) for h in heads]
    kw = [k[h] * jnp.exp(g_tot[h] - b_col[h] + i_col[h] - m_loc[h]) for h in heads]
    c_loc = [_dot(kw[h].T.astype(BF16), vb[h]) for h in heads]

    for h in heads:
        d = jnp.where(causal, b_col[h] - b_row[h] + i_row[h], NEG_BIG)
        m_inter = b_col[h] + m_prev[h]
        m_j = jnp.maximum(m_inter, jnp.max(d, axis=1, keepdims=True))
        w_inter = jnp.exp(m_inter - m_j)
        p = s_qk[h] * jnp.exp(d - m_j)
        num = w_inter * inter[h] + _dot(p.astype(BF16), vb[h])
        den = (w_inter * jnp.sum(q[h] * n_prev[h], axis=1, keepdims=True)
               + jnp.sum(p, axis=1, keepdims=True))
        hval = num / jnp.maximum(jnp.abs(den), jnp.exp(-m_j))
        hn = _rms(hval, gn_ref[:, pl.ds(h * dv, dv)])
        gate = _sigmoid(mo_ref[0, :, pl.ds(h * dv, dv)].astype(F32))
        o_ref[0, :, pl.ds(h * dv, dv)] = (hn * gate).astype(BF16)

    for h in heads:
        n_loc = jnp.sum(kw[h], axis=0, keepdims=True)
        m_new = jnp.maximum(g_tot[h] + m_prev[h], m_loc[h])
        a_old = jnp.exp(g_tot[h] + m_prev[h] - m_new)
        a_new = jnp.exp(m_loc[h] - m_new)
        c_scr[h] = a_old * c_prev[h] + a_new * c_loc[h]
        n_scr[h] = a_old * n_prev[h] + a_new * n_loc
        m_scr[h] = jnp.broadcast_to(m_new, (1, LANES))


def _mlstm(z3, gates, gates_t, bias_col, bias_row, conv_w8, conv_b, gnorm, chunk=256):
    b, s, _ = z3.shape
    nc = s // chunk
    kern = functools.partial(_mlstm_kernel, chunk=chunk)
    qw = MLSTM_QK_WIDTH
    vw = MLSTM_WIDTH
    return pl.pallas_call(
        kern,
        grid=(b, nc),
        in_specs=[
            pl.BlockSpec((1, chunk, qw), lambda bi, c: (bi, c, COL_MQ // qw)),
            pl.BlockSpec((1, chunk, qw), lambda bi, c: (bi, c, COL_MK // qw)),
            pl.BlockSpec((1, chunk, vw), lambda bi, c: (bi, c, COL_MV // vw)),
            pl.BlockSpec((1, chunk, vw), lambda bi, c: (bi, c, COL_MO // vw)),
            pl.BlockSpec((chunk, LANES), lambda bi, c: (bi * nc + c, 0)),
            pl.BlockSpec((1, SUBLANES, chunk), lambda bi, c: (bi, 0, c)),
            pl.BlockSpec((1, LANES), lambda bi, c: (0, 0)),
            pl.BlockSpec((SUBLANES, LANES), lambda bi, c: (0, 0)),
            pl.BlockSpec((SUBLANES, 2 * qw), lambda bi, c: (0, 0)),
            pl.BlockSpec((1, 2 * qw), lambda bi, c: (0, 0)),
            pl.BlockSpec((1, vw), lambda bi, c: (0, 0)),
        ],
        out_specs=pl.BlockSpec((1, chunk, vw), lambda bi, c: (bi, c, 0)),
        out_shape=jax.ShapeDtypeStruct((b, s, vw), BF16),
        scratch_shapes=[
            pltpu.VMEM((chunk + 2 * SUBLANES, qw), F32),
            pltpu.VMEM((chunk + 2 * SUBLANES, qw), F32),
            pltpu.VMEM((MLSTM_HEADS, MLSTM_QK_DIM, MLSTM_V_DIM), F32),
            pltpu.VMEM((MLSTM_HEADS, 1, MLSTM_QK_DIM), F32),
            pltpu.VMEM((MLSTM_HEADS, 1, LANES), F32),
        ],
        compiler_params=pltpu.CompilerParams(
            dimension_semantics=("parallel", "arbitrary"),
            vmem_limit_bytes=VMEM_LIMIT),
        name="mlstm",
    )(z3, z3, z3, z3, gates, gates_t, bias_col, bias_row, conv_w8, conv_b, gnorm)


def _merge_kernel(att_ref, hm_ref, ga_ref, gm_ref, x_ref, wa_ref, wm_ref, wo_ref, o_ref):
    @pl.when(pl.program_id(1) == 0)
    def _():
        o_ref[...] = x_ref[...]

    a = _dot(att_ref[...], wa_ref[...])
    bm = _dot(hm_ref[...], wm_ref[...])
    y = _sigmoid(ga_ref[...].astype(F32)) * a + _sigmoid(gm_ref[...].astype(F32)) * bm
    o_ref[...] += _dot(y.astype(BF16), wo_ref[...])


def _merge(att2d, hm2d, z2d, x2d, wa, wm, wo, tm=512, tn=1024):
    t = x2d.shape[0]
    nj = D_MODEL // tn
    return pl.pallas_call(
        _merge_kernel,
        grid=(t // tm, nj),
        in_specs=[
            pl.BlockSpec((tm, ATTN_WIDTH), lambda i, j: (i, 0)),
            pl.BlockSpec((tm, MLSTM_WIDTH), lambda i, j: (i, 0)),
            pl.BlockSpec((tm, tn), lambda i, j: (i, COL_GA // tn + j)),
            pl.BlockSpec((tm, tn), lambda i, j: (i, COL_GM // tn + j)),
            pl.BlockSpec((tm, D_MODEL), lambda i, j: (i, 0)),
            pl.BlockSpec((ATTN_WIDTH, tn), lambda i, j: (0, j)),
            pl.BlockSpec((MLSTM_WIDTH, tn), lambda i, j: (0, j)),
            pl.BlockSpec((tn, D_MODEL), lambda i, j: (j, 0)),
        ],
        out_specs=pl.BlockSpec((tm, D_MODEL), lambda i, j: (i, 0)),
        out_shape=jax.ShapeDtypeStruct((t, D_MODEL), F32),
        compiler_params=pltpu.CompilerParams(
            dimension_semantics=("parallel", "arbitrary"), vmem_limit_bytes=VMEM_LIMIT),
        name="merge",
    )(att2d, hm2d, z2d, z2d, x2d, wa, wm, wo)


def _cross_kernel(x_ref, g_ref, wq_ref, kv_ref, wo_ref, o_ref):
    x = x_ref[...]
    hc = _rms(x, g_ref[...]).astype(BF16)
    cq = (_dot(hc, wq_ref[...]) * (CROSS_HEAD_DIM ** -0.5)).astype(BF16)
    outs = []
    for hh in range(CROSS_HEADS):
        lo = hh * CROSS_HEAD_DIM
        qh = cq[:, lo:lo + CROSS_HEAD_DIM]
        kh = kv_ref[0, :, lo:lo + CROSS_HEAD_DIM]
        vh = kv_ref[0, :, CROSS_WIDTH + lo:CROSS_WIDTH + lo + CROSS_HEAD_DIM]
        s = _dot_nt(qh, kh)
        p = jnp.exp(s - jnp.max(s, axis=-1, keepdims=True))
        l = jnp.sum(p, axis=-1, keepdims=True)
        outs.append((_dot(p.astype(BF16), vh) / l).astype(BF16))
    co = jnp.concatenate(outs, axis=1)
    o_ref[...] = x + _dot(co, wo_ref[...])


def _cross(x2d, g, wq, ckv, wo, seq, tm=512):
    t = x2d.shape[0]
    n_mem = ckv.shape[1]
    per_batch = seq // tm
    return pl.pallas_call(
        _cross_kernel,
        grid=(t // tm,),
        in_specs=[
            pl.BlockSpec((tm, D_MODEL), lambda i: (i, 0)),
            pl.BlockSpec((1, D_MODEL), lambda i: (0, 0)),
            pl.BlockSpec((D_MODEL, CROSS_WIDTH), lambda i: (0, 0)),
            pl.BlockSpec((1, n_mem, 2 * CROSS_WIDTH), lambda i: (i // per_batch, 0, 0)),
            pl.BlockSpec((CROSS_WIDTH, D_MODEL), lambda i: (0, 0)),
        ],
        out_specs=pl.BlockSpec((tm, D_MODEL), lambda i: (i, 0)),
        out_shape=jax.ShapeDtypeStruct((t, D_MODEL), F32),
        compiler_params=pltpu.CompilerParams(
            dimension_semantics=("parallel",), vmem_limit_bytes=VMEM_LIMIT),
        name="cross",
    )(x2d, g, wq, ckv, wo)


def _mlp_kernel(x_ref, g_ref, wu_ref, wd_ref, gf_ref, o_ref, h_scr, acc, *, final_norm):
    j = pl.program_id(1)

    @pl.when(j == 0)
    def _():
        x = x_ref[...]
        h_scr[...] = _rms(x, g_ref[...]).astype(BF16)
        acc[...] = x

    u = jnp.square(jnp.maximum(_dot(h_scr[...], wu_ref[...]), 0.0)).astype(BF16)
    acc[...] += _dot(u, wd_ref[...])

    @pl.when(j == pl.num_programs(1) - 1)
    def _():
        if final_norm:
            o_ref[...] = _rms(acc[...], gf_ref[...])
        else:
            o_ref[...] = acc[...]


def _mlp(x2d, g, wu, wd, gf, final_norm, tm=512, tf=1024):
    t = x2d.shape[0]
    kern = functools.partial(_mlp_kernel, final_norm=final_norm)
    return pl.pallas_call(
        kern,
        grid=(t // tm, D_FF // tf),
        in_specs=[
            pl.BlockSpec((tm, D_MODEL), lambda i, j: (i, 0)),
            pl.BlockSpec((1, D_MODEL), lambda i, j: (0, 0)),
            pl.BlockSpec((D_MODEL, tf), lambda i, j: (0, j)),
            pl.BlockSpec((tf, D_MODEL), lambda i, j: (j, 0)),
            pl.BlockSpec((1, D_MODEL), lambda i, j: (0, 0)),
        ],
        out_specs=pl.BlockSpec((tm, D_MODEL), lambda i, j: (i, 0)),
        out_shape=jax.ShapeDtypeStruct((t, D_MODEL), F32),
        scratch_shapes=[pltpu.VMEM((tm, D_MODEL), BF16), pltpu.VMEM((tm, D_MODEL), F32)],
        compiler_params=pltpu.CompilerParams(
            dimension_semantics=("parallel", "arbitrary"), vmem_limit_bytes=VMEM_LIMIT),
        name="mlp",
    )(x2d, g, wu, wd, gf)


def _alibi_terms():
    slopes = 2.0 ** (-8.0 * np.arange(1, ATTN_HEADS + 1, dtype=np.float64) / ATTN_HEADS)
    rem = slopes * LOG2E
    terms = []
    for _ in range(ALIBI_TERMS):
        t = rem.astype(np.float32).astype(ml_dtypes.bfloat16).astype(np.float64)
        terms.append(t)
        rem = rem - t
    return np.stack(terms, axis=1).reshape(-1).astype(np.float32)


def _in_col_scale():
    scale = np.ones((GATE_LO, 1), np.float32)
    scale[W_AQ_LO:W_AQ_HI] = ATTN_QK_DIM ** -0.5 * LOG2E
    return scale


def _pad_rows(a, rows):
    return jnp.pad(a, ((0, rows - a.shape[0]), (0, 0)))


def kernel(x, mem, norm_mix, w_in, b_igate, b_fgate, conv_w, conv_b, lam_q1, lam_k1, lam_q2, lam_k2, attn_norm, mlstm_norm, w_attn_br, w_mlstm_br, w_out, norm_cross, norm_mem, w_cq, w_ckv, w_co, norm_mlp, w_up, w_down, norm_final):
    b, s, _ = x.shape
    t = b * s
    tq = 256
    cs_terms = jnp.asarray(_alibi_terms())
    x2d = x.reshape(t, D_MODEL)
    for l in range(DEPTH):
        lam_init = 0.8 - 0.6 * math.exp(-0.3 * l)
        w = w_in[l]
        wt = w.T
        w_a = (wt[:GATE_LO] * jnp.asarray(_in_col_scale())).astype(BF16)
        w_b = wt[GATE_HI:].astype(BF16)
        w_gate = _pad_rows(wt[GATE_LO:GATE_HI], LANES).astype(BF16)

        z2d, gates, vt, w_up_b, w_down_b = _inproj(x2d, norm_mix[l][None, :], w_a, w_b, w_gate,
                                                   w_up[l], w_down[l], tq)
        z3 = z2d.reshape(b, s, IN_MAIN)

        vt5 = vt.reshape(b, s // tq, ATTN_HEADS, ATTN_V_DIM, tq)
        lamv = _pad_rows(jnp.pad(jnp.stack([lam_q1[l], lam_k1[l], lam_q2[l], lam_k2[l]]),
                                 ((0, 0), (0, LANES - ATTN_QK_DIM))), SUBLANES)
        att, w_out_b, w_abr_b, w_mbr_b = _attn(z3, vt5, cs_terms, lamv, attn_norm[l][None, :],
                                               w_out[l], w_attn_br[l], w_mlstm_br[l], lam_init, tq=tq)

        gates_t = gates[:, :SUBLANES].reshape(b, s, SUBLANES).transpose(0, 2, 1)
        gate_bias = jnp.concatenate([b_igate[l], b_fgate[l]])
        bias_col = jnp.pad(gate_bias, (0, LANES - gate_bias.shape[0]))[None, :]
        bias_row = jnp.broadcast_to(gate_bias[:, None], (SUBLANES, LANES))
        hm = _mlstm(z3, gates, gates_t, bias_col, bias_row, _pad_rows(conv_w[l], SUBLANES),
                    conv_b[l][None, :], mlstm_norm[l][None, :])

        x2d = _merge(att.reshape(t, ATTN_WIDTH), hm.reshape(t, MLSTM_WIDTH), z2d, x2d,
                     w_abr_b, w_mbr_b, w_out_b)

        ckv = _memkv(mem, norm_mem[l][None, :], w_ckv[l])
        x2d = _cross(x2d, norm_cross[l][None, :], w_cq[l].astype(BF16), ckv,
                     w_co[l].astype(BF16), s)

        x2d = _mlp(x2d, norm_mlp[l][None, :], w_up_b, w_down_b,
                   norm_final[None, :], final_norm=(l == DEPTH - 1))
    return x2d.reshape(b, s, D_MODEL)
```

```python
import functools
import math

import ml_dtypes
import numpy as np
import jax
import jax.numpy as jnp
from jax import lax
from jax.experimental import pallas as pl
from jax.experimental.pallas import tpu as pltpu

F32 = jnp.float32
BF16 = jnp.bfloat16

D_MODEL = 2048
DEPTH = 1
ATTN_HEADS = 8
ATTN_QK_DIM = 64
ATTN_V_DIM = 128
ATTN_WIDTH = ATTN_HEADS * ATTN_V_DIM
MLSTM_HEADS = 4
MLSTM_QK_DIM = 128
MLSTM_V_DIM = 256
MLSTM_QK_WIDTH = MLSTM_HEADS * MLSTM_QK_DIM
MLSTM_WIDTH = MLSTM_HEADS * MLSTM_V_DIM
CONV_WIDTH = 4
CROSS_HEADS = 4
CROSS_HEAD_DIM = 128
CROSS_WIDTH = CROSS_HEADS * CROSS_HEAD_DIM
D_FF = 4 * D_MODEL
EPS = 1e-6
LANES = 128
SUBLANES = 8
NEG_BIG = -1e30
LOG2E = 1.4426950408889634
ALIBI_TERMS = 3

COL_AQ = 0
COL_AK = 1024
COL_MQ = 2048
COL_MK = 2560
COL_MV = 3072
COL_MO = 4096
COL_GA = 5120
COL_GM = 7168
IN_MAIN = 9216
W_AQ_LO, W_AQ_HI = 0, 1024
W_AV_LO = 2048
GATE_LO = 6144
GATE_HI = 6152

VMEM_LIMIT = 56 * 1024 * 1024


def _rms(x, g):
    ms = jnp.mean(x * x, axis=-1, keepdims=True)
    return x * lax.rsqrt(ms + EPS) * g


def _sigmoid(x):
    return 1.0 / (1.0 + jnp.exp(-x))


def _log_sigmoid(x):
    return jnp.minimum(x, 0.0) - jnp.log(1.0 + jnp.exp(-jnp.abs(x)))


def _dot(a, b):
    return jnp.dot(a, b, preferred_element_type=F32)


def _dot_nt(a, b):
    return lax.dot_general(a, b, (((1,), (1,)), ((), ())), preferred_element_type=F32)


def _memkv_kernel(mem_ref, g_ref, w_ref, o_ref):
    mn = _rms(mem_ref[0], g_ref[...]).astype(BF16)
    o_ref[0] = _dot(mn, w_ref[...].astype(BF16)).astype(BF16)


def _memkv(mem, g, w_ckv):
    b, n_mem, _ = mem.shape
    return pl.pallas_call(
        _memkv_kernel,
        grid=(b,),
        in_specs=[
            pl.BlockSpec((1, n_mem, D_MODEL), lambda i: (i, 0, 0)),
            pl.BlockSpec((1, D_MODEL), lambda i: (0, 0)),
            pl.BlockSpec((D_MODEL, 2 * CROSS_WIDTH), lambda i: (0, 0)),
        ],
        out_specs=pl.BlockSpec((1, n_mem, 2 * CROSS_WIDTH), lambda i: (i, 0, 0)),
        out_shape=jax.ShapeDtypeStruct((b, n_mem, 2 * CROSS_WIDTH), BF16),
        compiler_params=pltpu.CompilerParams(
            dimension_semantics=("arbitrary",), vmem_limit_bytes=VMEM_LIMIT),
        name="memkv",
    )(mem, g, w_ckv)


def _wprep_kernel(a_ref, nxt_ref, wa_out, wb_out, wg_out, *, na):
    k = pl.program_id(0)
    ng = GATE_HI - GATE_LO

    @pl.when(k < na)
    def _():
        scale = jnp.where(k == W_AQ_LO // a_ref.shape[0], ATTN_QK_DIM ** -0.5 * LOG2E, 1.0)
        wa_out[...] = (a_ref[...] * scale).astype(BF16)

    @pl.when(k >= na)
    def _():
        wb_out[...] = jnp.concatenate([a_ref[pl.ds(ng, a_ref.shape[0] - ng), :], nxt_ref[0]],
                                      axis=0).astype(BF16)

    @pl.when(k == na)
    def _():
        pad = jnp.zeros((LANES - ng, D_MODEL), F32)
        wg_out[...] = jnp.concatenate([a_ref[pl.ds(0, ng), :], pad], axis=0).astype(BF16)


def _wprep(wt, tn=1024):
    rows = wt.shape[0]
    ng = GATE_HI - GATE_LO
    assert ng == SUBLANES and GATE_LO % tn == 0 and (rows - GATE_HI) % tn == 0
    assert W_AQ_LO == 0 and W_AQ_HI == tn
    na = GATE_LO // tn
    nb = (rows - GATE_HI) // tn
    groups = wt.reshape(rows // ng, ng, D_MODEL)
    per_block = tn // ng
    return pl.pallas_call(
        functools.partial(_wprep_kernel, na=na),
        grid=(na + nb,),
        in_specs=[
            pl.BlockSpec((tn, D_MODEL), lambda k: (k, 0)),
            pl.BlockSpec((1, ng, D_MODEL), lambda k: ((k + 1) * per_block, 0, 0)),
        ],
        out_specs=[
            pl.BlockSpec((tn, D_MODEL), lambda k: (jnp.minimum(k, na - 1), 0)),
            pl.BlockSpec((tn, D_MODEL), lambda k: (jnp.maximum(k - na, 0), 0)),
            pl.BlockSpec((LANES, D_MODEL), lambda k: (0, 0)),
        ],
        out_shape=[
            jax.ShapeDtypeStruct((GATE_LO, D_MODEL), BF16),
            jax.ShapeDtypeStruct((rows - GATE_HI, D_MODEL), BF16),
            jax.ShapeDtypeStruct((LANES, D_MODEL), BF16),
        ],
        compiler_params=pltpu.CompilerParams(
            dimension_semantics=("arbitrary",), vmem_limit_bytes=VMEM_LIMIT),
        name="wprep",
    )(wt, groups)


def _inproj_kernel(x_ref, g_ref, wa_ref, wb_ref, wg_ref, wu_ref, wd_ref,
                   z_ref, gate_ref, vt_ref, wu_out, wd_out, h_scr, *, na, jv, ncast):
    j = pl.program_id(1)

    @pl.when(j == 0)
    def _():
        h = _rms(x_ref[...], g_ref[...]).astype(BF16)
        h_scr[...] = h
        gate_ref[...] = _dot_nt(h, wg_ref[...])

    @pl.when((j < na) & (j != jv))
    def _():
        z_ref[...] = _dot_nt(h_scr[...], wa_ref[...]).astype(BF16)

    @pl.when(j == jv)
    def _():
        vt = _dot_nt(wa_ref[...], h_scr[...]).astype(BF16)
        tkv = vt_ref.shape[2]
        for n in range(vt_ref.shape[0]):
            vt_ref[n] = vt[:, n * tkv:(n + 1) * tkv]

    @pl.when(j >= na)
    def _():
        z_ref[...] = _dot_nt(h_scr[...], wb_ref[...]).astype(BF16)

    @pl.when(j < ncast)
    def _():
        wu_out[...] = wu_ref[...].astype(BF16)
        wd_out[...] = wd_ref[...].astype(BF16)


def _inproj(x2d, g, w_a, w_b, w_gate, w_up, w_down, tkv, tm=1024, tn=1024):
    t = x2d.shape[0]
    na = w_a.shape[0] // tn
    n = w_a.shape[0] + w_b.shape[0]
    ni, nj = t // tm, n // tn
    assert tn == ATTN_WIDTH and W_AV_LO % tn == 0
    jv = W_AV_LO // tn
    ncast = 8
    assert ncast <= nj
    ub = (w_up.shape[0] // ni, w_up.shape[1] // ncast)
    db = (w_down.shape[0] // ni, w_down.shape[1] // ncast)
    cast_map = lambda i, j: (i, jnp.minimum(j, ncast - 1))
    return pl.pallas_call(
        functools.partial(_inproj_kernel, na=na, jv=jv, ncast=ncast),
        grid=(ni, nj),
        in_specs=[
            pl.BlockSpec((tm, D_MODEL), lambda i, j: (i, 0)),
            pl.BlockSpec((1, D_MODEL), lambda i, j: (0, 0)),
            pl.BlockSpec((tn, D_MODEL), lambda i, j: (jnp.minimum(j, na - 1), 0)),
            pl.BlockSpec((tn, D_MODEL), lambda i, j: (jnp.maximum(j - na, 0), 0)),
            pl.BlockSpec((LANES, D_MODEL), lambda i, j: (0, 0)),
            pl.BlockSpec(ub, cast_map),
            pl.BlockSpec(db, cast_map),
        ],
        out_specs=[
            pl.BlockSpec((tm, tn), lambda i, j: (i, j - (j >= jv).astype(jnp.int32))),
            pl.BlockSpec((tm, LANES), lambda i, j: (i, 0)),
            pl.BlockSpec((tm // tkv, ATTN_WIDTH, tkv), lambda i, j: (i, 0, 0)),
            pl.BlockSpec(ub, cast_map),
            pl.BlockSpec(db, cast_map),
        ],
        out_shape=[
            jax.ShapeDtypeStruct((t, n - tn), BF16),
            jax.ShapeDtypeStruct((t, LANES), F32),
            jax.ShapeDtypeStruct((t // tkv, ATTN_WIDTH, tkv), BF16),
            jax.ShapeDtypeStruct(w_up.shape, BF16),
            jax.ShapeDtypeStruct(w_down.shape, BF16),
        ],
        scratch_shapes=[pltpu.VMEM((tm, D_MODEL), BF16)],
        compiler_params=pltpu.CompilerParams(
            dimension_semantics=("arbitrary", "arbitrary"), vmem_limit_bytes=VMEM_LIMIT),
        name="inproj",
    )(x2d, g, w_a, w_b, w_gate, w_up, w_down)


def _attn_kernel(cs_ref, lam_ref, q_ref, k_ref, vt_ref, gain_ref, wo_ref, wa_ref, wm_ref,
                 o_ref, wo_out, wa_out, wm_out, qs_ref, kf_ref, acc_ref, m_ref, *, tq, lam_init):
    tk = tq
    nfeat = ALIBI_TERMS
    dq = 2 * ATTN_QK_DIM
    dv = ATTN_V_DIM
    i = pl.program_id(1)
    cs = [[cs_ref[nfeat * h + t] for t in range(nfeat)] for h in range(ATTN_HEADS)]
    cs_tot = [sum(c[1:], c[0]) for c in cs]

    @pl.when(i == 0)
    def _():
        klane = lax.broadcasted_iota(jnp.int32, (tk, LANES), 1)
        krow = lax.broadcasted_iota(jnp.int32, (tk, LANES), 0).astype(F32)
        frow = lax.broadcasted_iota(jnp.int32, (LANES, 2 * tq), 0)
        fcol = lax.broadcasted_iota(jnp.int32, (LANES, 2 * tq), 1)
        fcol = jnp.where(fcol >= tq, fcol - tq, fcol).astype(F32)
        for h in range(ATTN_HEADS):
            kf = jnp.where(klane < nfeat, krow, 0.0)
            qf = jnp.where((frow >= nfeat) & (frow < 2 * nfeat), -fcol, 0.0)
            for t in range(nfeat):
                kf = jnp.where(klane == nfeat + t, cs[h][t], kf)
                qf = jnp.where(frow == t, cs[h][t], qf)
            kf_ref[h] = kf.astype(BF16)
            qs_ref[h, pl.ds(dq, LANES), :] = qf.astype(BF16)

    row = lax.broadcasted_iota(jnp.int32, (dq, tq), 0)
    for h in range(ATTN_HEADS):
        qt = q_ref[0, :, pl.ds(h * dq, dq)].astype(F32).T
        qs_ref[h, pl.ds(0, dq), pl.ds(0, tq)] = jnp.where(row < ATTN_QK_DIM, qt, 0.0).astype(BF16)
        qs_ref[h, pl.ds(0, dq), pl.ds(tq, tq)] = jnp.where(row >= ATTN_QK_DIM, qt, 0.0).astype(BF16)
    acc_ref[...] = jnp.zeros_like(acc_ref)
    m_ref[...] = jnp.full_like(m_ref, NEG_BIG)

    orow = lax.broadcasted_iota(jnp.int32, (2 * SUBLANES, tk), 0)
    ones_blk = jnp.where(orow == 0, 1.0, 0.0).astype(BF16)

    def scores(h, j):
        k = k_ref[0, pl.ds(pl.multiple_of(j * tk, tk), tk), pl.ds(h * dq, dq)]
        kx = jnp.concatenate([k, kf_ref[h]], axis=1)
        return _dot(kx, qs_ref[h])

    def softmax_pv(h, j, t, diagonal):
        if diagonal:
            krow = lax.broadcasted_iota(jnp.int32, (tk, 2 * tq), 0)
            qcol = lax.broadcasted_iota(jnp.int32, (tk, 2 * tq), 1)
            qcol = jnp.where(qcol >= tq, qcol - tq, qcol)
            t = jnp.where(krow <= qcol, t, NEG_BIG)
        soff = cs_tot[h] * ((j - i) * tk).astype(F32)
        m = m_ref[h]
        m_new = jnp.maximum(m, jnp.max(t, axis=0, keepdims=True) + soff)
        alpha = jnp.exp2(m - m_new)
        p = jnp.exp2(t - (m_new - soff)).astype(BF16)
        vx = jnp.concatenate([vt_ref[0, j, h], ones_blk], axis=0)
        acc_ref[h] = alpha * acc_ref[h] + _dot(vx, p)
        m_ref[h] = m_new

    def run(units):
        ahead = 2
        pending = [scores(*u[:2]) for u in units[:ahead]]
        for n, (h, j, diagonal) in enumerate(units):
            if n + ahead < len(units):
                pending.append(scores(*units[n + ahead][:2]))
            softmax_pv(h, j, pending.pop(0), diagonal)

    def tile_units(j, diagonal):
        return [(h, j, diagonal) for h in range(ATTN_HEADS)]

    def body(jj, carry):
        run(tile_units(2 * jj, False) + tile_units(2 * jj + 1, False))
        return carry

    lax.fori_loop(0, i // 2, body, 0)

    @pl.when(i % 2 == 0)
    def _():
        run(tile_units(i, True))

    @pl.when(i % 2 == 1)
    def _():
        run(tile_units(i - 1, False) + tile_units(i, True))

    lv = lam_ref[...]
    d1 = jnp.sum(lv[0:1] * lv[1:2], axis=-1, keepdims=True)
    d2 = jnp.sum(lv[2:3] * lv[3:4], axis=-1, keepdims=True)
    lam = jnp.exp(d1) - jnp.exp(d2) + lam_init
    for h in range(ATTN_HEADS):
        l = acc_ref[h, pl.ds(dv, 1), :]
        out = (acc_ref[h, pl.ds(0, dv), pl.ds(0, tq)] / l[:, :tq]
               - lam * (acc_ref[h, pl.ds(0, dv), pl.ds(tq, tq)] / l[:, tq:]))
        ms = jnp.mean(out * out, axis=0, keepdims=True)
        on = out * lax.rsqrt(ms + EPS)
        o_ref[0, :, pl.ds(h * dv, dv)] = (on.T * gain_ref[...] * (1.0 - lam_init)).astype(BF16)

    wo_out[...] = wo_ref[...].astype(BF16)
    wa_out[...] = wa_ref[...].astype(BF16)
    wm_out[...] = wm_ref[...].astype(BF16)


def _attn(z3, vt5, cs_terms, lamv, gain, w_out, w_abr, w_mbr, lam_init, tq=256):
    b, s, _ = z3.shape
    nq = s // tq
    steps = b * nq
    cast_specs = [pl.BlockSpec((w.shape[0] // steps, w.shape[1]), lambda bi, i: (bi * nq + i, 0))
                  for w in (w_out, w_abr, w_mbr)]
    kern = functools.partial(_attn_kernel, tq=tq, lam_init=lam_init)
    width = ATTN_HEADS * 2 * ATTN_QK_DIM
    return pl.pallas_call(
        kern,
        grid=(b, nq),
        in_specs=[
            pl.BlockSpec(memory_space=pltpu.SMEM),
            pl.BlockSpec((SUBLANES, LANES), lambda bi, i: (0, 0)),
            pl.BlockSpec((1, tq, width), lambda bi, i: (bi, i, COL_AQ // width)),
            pl.BlockSpec((1, s, width), lambda bi, i: (bi, 0, COL_AK // width)),
            pl.BlockSpec((1, nq, ATTN_HEADS, ATTN_V_DIM, tq), lambda bi, i: (bi, 0, 0, 0, 0)),
            pl.BlockSpec((1, ATTN_V_DIM), lambda bi, i: (0, 0)),
        ] + cast_specs,
        out_specs=[pl.BlockSpec((1, tq, ATTN_WIDTH), lambda bi, i: (bi, i, 0))] + cast_specs,
        out_shape=[jax.ShapeDtypeStruct((b, s, ATTN_WIDTH), BF16)]
        + [jax.ShapeDtypeStruct(w.shape, BF16) for w in (w_out, w_abr, w_mbr)],
        scratch_shapes=[
            pltpu.VMEM((ATTN_HEADS, 2 * ATTN_QK_DIM + LANES, 2 * tq), BF16),
            pltpu.VMEM((ATTN_HEADS, tq, LANES), BF16),
            pltpu.VMEM((ATTN_HEADS, ATTN_V_DIM + 2 * SUBLANES, 2 * tq), F32),
            pltpu.VMEM((ATTN_HEADS, 1, 2 * tq), F32),
        ],
        compiler_params=pltpu.CompilerParams(
            dimension_semantics=("parallel", "arbitrary"),
            vmem_limit_bytes=VMEM_LIMIT),
        name="attn",
    )(cs_terms, lamv, z3, z3, vt5, gain, w_out, w_abr, w_mbr)


def _split3(x):
    hi = x.astype(BF16)
    r = x - hi.astype(F32)
    mid = r.astype(BF16)
    lo = (r - mid.astype(F32)).astype(BF16)
    return hi, mid, lo


def _mlstm_kernel(uq_ref, uk_ref, v_ref, mo_ref, gcol_ref, grow_ref, bcol_ref, brow_ref,
                  cw_ref, cb_ref, gn_ref, o_ref, extq, extk, c_scr, n_scr, m_scr, *, chunk):
    L = chunk
    nh = MLSTM_HEADS
    dk = MLSTM_QK_DIM
    dv = MLSTM_V_DIM
    heads = range(nh)

    @pl.when(pl.program_id(1) == 0)
    def _():
        extq[pl.ds(0, SUBLANES), :] = jnp.zeros((SUBLANES, MLSTM_QK_WIDTH), F32)
        extk[pl.ds(0, SUBLANES), :] = jnp.zeros((SUBLANES, MLSTM_QK_WIDTH), F32)
        c_scr[...] = jnp.zeros_like(c_scr)
        n_scr[...] = jnp.zeros_like(n_scr)
        m_scr[...] = jnp.zeros_like(m_scr)

    def conv_silu(u_ref, ext, col0):
        ext[pl.ds(SUBLANES, L), :] = u_ref[0].astype(F32)
        w = cw_ref[:, pl.ds(col0, MLSTM_QK_WIDTH)]
        y = cb_ref[:, pl.ds(col0, MLSTM_QK_WIDTH)]
        for tap in range(CONV_WIDTH):
            off = SUBLANES - (CONV_WIDTH - 1) + tap
            y = y + ext[pl.ds(off, L), :] * w[tap:tap + 1]
        ext[pl.ds(0, SUBLANES), :] = ext[pl.ds(L, SUBLANES), :]
        return y * _sigmoid(y)

    q_all = conv_silu(uq_ref, extq, 0)
    k_all = conv_silu(uk_ref, extk, MLSTM_QK_WIDTH) * (dk ** -0.5)
    q = [q_all[:, h * dk:(h + 1) * dk] for h in heads]
    k = [k_all[:, h * dk:(h + 1) * dk] for h in heads]
    qb = [x.astype(BF16) for x in q]
    kb = [x.astype(BF16) for x in k]
    vb = [v_ref[0, :, pl.ds(h * dv, dv)] for h in heads]

    g_c = gcol_ref[...] + bcol_ref[...]
    g_r = grow_ref[0] + brow_ref[:, 0:1]
    lf_c = _log_sigmoid(g_c)
    lf_r = _log_sigmoid(g_r)

    r_i = lax.broadcasted_iota(jnp.int32, (L, L), 0)
    c_i = lax.broadcasted_iota(jnp.int32, (L, L), 1)
    causal = c_i <= r_i
    tri = jnp.where(causal, 1.0, 0.0).astype(BF16)
    tri_t = jnp.where(r_i <= c_i, 1.0, 0.0).astype(BF16)
    hi, mid, lo = _split3(lf_c)
    b_c = _dot(tri, hi) + _dot(tri, mid) + _dot(tri, lo)
    hi, mid, lo = _split3(lf_r)
    b_r = _dot(hi, tri_t) + _dot(mid, tri_t) + _dot(lo, tri_t)
    g_sum = jnp.sum(lf_r, axis=1, keepdims=True)

    lane = lax.broadcasted_iota(jnp.int32, (L, LANES), 1)

    def col(x, idx):
        return jnp.sum(jnp.where(lane == idx, x, 0.0), axis=1, keepdims=True)

    i_col = [col(g_c, h) for h in heads]
    b_col = [col(b_c, nh + h) for h in heads]
    i_row = [g_r[h:h + 1, :] for h in heads]
    b_row = [b_r[nh + h:nh + h + 1, :] for h in heads]
    g_tot = [g_sum[nh + h:nh + h + 1, :] for h in heads]

    c_prev = [c_scr[h] for h in heads]
    n_prev = [n_scr[h] for h in heads]
    m_prev = [m_scr[h][:, 0:1] for h in heads]

    s_qk = [_dot_nt(qb[h], kb[h]) for h in heads]
    inter = [_dot(qb[h], c_prev[h].astype(BF16)) for h in heads]
    m_loc = [jnp.max(g_tot[h] - b_row[h] + i_row[h], axis=1, keepdims=True) for h in heads]
    kw = [k[h] * jnp.exp(g_tot[h] - b_col[h] + i_col[h] - m_loc[h]) for h in heads]
    c_loc = [_dot(kw[h].T.astype(BF16), vb[h]) for h in heads]

    for h in heads:
        d = jnp.where(causal, b_col[h] - b_row[h] + i_row[h], NEG_BIG)
        m_inter = b_col[h] + m_prev[h]
        m_j = jnp.maximum(m_inter, jnp.max(d, axis=1, keepdims=True))
        w_inter = jnp.exp(m_inter - m_j)
        p = s_qk[h] * jnp.exp(d - m_j)
        num = w_inter * inter[h] + _dot(p.astype(BF16), vb[h])
        den = (w_inter * jnp.sum(q[h] * n_prev[h], axis=1, keepdims=True)
               + jnp.sum(p, axis=1, keepdims=True))
        hval = num / jnp.maximum(jnp.abs(den), jnp.exp(-m_j))
        hn = _rms(hval, gn_ref[:, pl.ds(h * dv, dv)])
        gate = _sigmoid(mo_ref[0, :, pl.ds(h * dv, dv)].astype(F32))
        o_ref[0, :, pl.ds(h * dv, dv)] = (hn * gate).astype(BF16)

    for h in heads:
        n_loc = jnp.sum(kw[h], axis=0, keepdims=True)
        m_new = jnp.maximum(g_tot[h] + m_prev[h], m_loc[h])
        a_old = jnp.exp(g_tot[h] + m_prev[h] - m_new)
        a_new = jnp.exp(m_loc[h] - m_new)
        c_scr[h] = a_old * c_prev[h] + a_new * c_loc[h]
        n_scr[h] = a_old * n_prev[h] + a_new * n_loc
        m_scr[h] = jnp.broadcast_to(m_new, (1, LANES))


def _mlstm(z3, gates, gates_t, bias_col, bias_row, conv_w8, conv_b, gnorm, chunk=256):
    b, s, _ = z3.shape
    nc = s // chunk
    kern = functools.partial(_mlstm_kernel, chunk=chunk)
    qw = MLSTM_QK_WIDTH
    vw = MLSTM_WIDTH
    return pl.pallas_call(
        kern,
        grid=(b, nc),
        in_specs=[
            pl.BlockSpec((1, chunk, qw), lambda bi, c: (bi, c, COL_MQ // qw)),
            pl.BlockSpec((1, chunk, qw), lambda bi, c: (bi, c, COL_MK // qw)),
            pl.BlockSpec((1, chunk, vw), lambda bi, c: (bi, c, COL_MV // vw)),
            pl.BlockSpec((1, chunk, vw), lambda bi, c: (bi, c, COL_MO // vw)),
            pl.BlockSpec((chunk, LANES), lambda bi, c: (bi * nc + c, 0)),
            pl.BlockSpec((1, SUBLANES, chunk), lambda bi, c: (bi, 0, c)),
            pl.BlockSpec((1, LANES), lambda bi, c: (0, 0)),
            pl.BlockSpec((SUBLANES, LANES), lambda bi, c: (0, 0)),
            pl.BlockSpec((SUBLANES, 2 * qw), lambda bi, c: (0, 0)),
            pl.BlockSpec((1, 2 * qw), lambda bi, c: (0, 0)),
            pl.BlockSpec((1, vw), lambda bi, c: (0, 0)),
        ],
        out_specs=pl.BlockSpec((1, chunk, vw), lambda bi, c: (bi, c, 0)),
        out_shape=jax.ShapeDtypeStruct((b, s, vw), BF16),
        scratch_shapes=[
            pltpu.VMEM((chunk + 2 * SUBLANES, qw), F32),
            pltpu.VMEM((chunk + 2 * SUBLANES, qw), F32),
            pltpu.VMEM((MLSTM_HEADS, MLSTM_QK_DIM, MLSTM_V_DIM), F32),
            pltpu.VMEM((MLSTM_HEADS, 1, MLSTM_QK_DIM), F32),
            pltpu.VMEM((MLSTM_HEADS, 1, LANES), F32),
        ],
        compiler_params=pltpu.CompilerParams(
            dimension_semantics=("parallel", "arbitrary"),
            vmem_limit_bytes=VMEM_LIMIT),
        name="mlstm",
    )(z3, z3, z3, z3, gates, gates_t, bias_col, bias_row, conv_w8, conv_b, gnorm)


def _merge_kernel(att_ref, hm_ref, ga_ref, gm_ref, x_ref, wa_ref, wm_ref, wo_ref, o_ref):
    @pl.when(pl.program_id(1) == 0)
    def _():
        o_ref[...] = x_ref[...]

    a = _dot(att_ref[...], wa_ref[...])
    bm = _dot(hm_ref[...], wm_ref[...])
    y = _sigmoid(ga_ref[...].astype(F32)) * a + _sigmoid(gm_ref[...].astype(F32)) * bm
    o_ref[...] += _dot(y.astype(BF16), wo_ref[...])


def _merge(att2d, hm2d, z2d, x2d, wa, wm, wo, tm=512, tn=1024):
    t = x2d.shape[0]
    nj = D_MODEL // tn
    return pl.pallas_call(
        _merge_kernel,
        grid=(t // tm, nj),
        in_specs=[
            pl.BlockSpec((tm, ATTN_WIDTH), lambda i, j: (i, 0)),
            pl.BlockSpec((tm, MLSTM_WIDTH), lambda i, j: (i, 0)),
            pl.BlockSpec((tm, tn), lambda i, j: (i, COL_GA // tn + j)),
            pl.BlockSpec((tm, tn), lambda i, j: (i, COL_GM // tn + j)),
            pl.BlockSpec((tm, D_MODEL), lambda i, j: (i, 0)),
            pl.BlockSpec((ATTN_WIDTH, tn), lambda i, j: (0, j)),
            pl.BlockSpec((MLSTM_WIDTH, tn), lambda i, j: (0, j)),
            pl.BlockSpec((tn, D_MODEL), lambda i, j: (j, 0)),
        ],
        out_specs=pl.BlockSpec((tm, D_MODEL), lambda i, j: (i, 0)),
        out_shape=jax.ShapeDtypeStruct((t, D_MODEL), F32),
        compiler_params=pltpu.CompilerParams(
            dimension_semantics=("parallel", "arbitrary"), vmem_limit_bytes=VMEM_LIMIT),
        name="merge",
    )(att2d, hm2d, z2d, z2d, x2d, wa, wm, wo)


def _cross_kernel(x_ref, g_ref, wq_ref, kv_ref, wo_ref, o_ref):
    x = x_ref[...]
    hc = _rms(x, g_ref[...]).astype(BF16)
    cq = (_dot(hc, wq_ref[...]) * (CROSS_HEAD_DIM ** -0.5)).astype(BF16)
    outs = []
    for hh in range(CROSS_HEADS):
        lo = hh * CROSS_HEAD_DIM
        qh = cq[:, lo:lo + CROSS_HEAD_DIM]
        kh = kv_ref[0, :, lo:lo + CROSS_HEAD_DIM]
        vh = kv_ref[0, :, CROSS_WIDTH + lo:CROSS_WIDTH + lo + CROSS_HEAD_DIM]
        s = _dot_nt(qh, kh)
        p = jnp.exp(s - jnp.max(s, axis=-1, keepdims=True))
        l = jnp.sum(p, axis=-1, keepdims=True)
        outs.append((_dot(p.astype(BF16), vh) / l).astype(BF16))
    co = jnp.concatenate(outs, axis=1)
    o_ref[...] = x + _dot(co, wo_ref[...])


def _cross(x2d, g, wq, ckv, wo, seq, tm=512):
    t = x2d.shape[0]
    n_mem = ckv.shape[1]
    per_batch = seq // tm
    return pl.pallas_call(
        _cross_kernel,
        grid=(t // tm,),
        in_specs=[
            pl.BlockSpec((tm, D_MODEL), lambda i: (i, 0)),
            pl.BlockSpec((1, D_MODEL), lambda i: (0, 0)),
            pl.BlockSpec((D_MODEL, CROSS_WIDTH), lambda i: (0, 0)),
            pl.BlockSpec((1, n_mem, 2 * CROSS_WIDTH), lambda i: (i // per_batch, 0, 0)),
            pl.BlockSpec((CROSS_WIDTH, D_MODEL), lambda i: (0, 0)),
        ],
        out_specs=pl.BlockSpec((tm, D_MODEL), lambda i: (i, 0)),
        out_shape=jax.ShapeDtypeStruct((t, D_MODEL), F32),
        compiler_params=pltpu.CompilerParams(
            dimension_semantics=("parallel",), vmem_limit_bytes=VMEM_LIMIT),
        name="cross",
    )(x2d, g, wq, ckv, wo)


def _mlp_kernel(x_ref, g_ref, wu_ref, wd_ref, gf_ref, o_ref, h_scr, acc, *, final_norm):
    j = pl.program_id(1)

    @pl.when(j == 0)
    def _():
        x = x_ref[...]
        h_scr[...] = _rms(x, g_ref[...]).astype(BF16)
        acc[...] = x

    u = jnp.square(jnp.maximum(_dot(h_scr[...], wu_ref[...]), 0.0)).astype(BF16)
    acc[...] += _dot(u, wd_ref[...])

    @pl.when(j == pl.num_programs(1) - 1)
    def _():
        if final_norm:
            o_ref[...] = _rms(acc[...], gf_ref[...])
        else:
            o_ref[...] = acc[...]


def _mlp(x2d, g, wu, wd, gf, final_norm, tm=512, tf=1024):
    t = x2d.shape[0]
    kern = functools.partial(_mlp_kernel, final_norm=final_norm)
    return pl.pallas_call(
        kern,
        grid=(t // tm, D_FF // tf),
        in_specs=[
            pl.BlockSpec((tm, D_MODEL), lambda i, j: (i, 0)),
            pl.BlockSpec((1, D_MODEL), lambda i, j: (0, 0)),
            pl.BlockSpec((D_MODEL, tf), lambda i, j: (0, j)),
            pl.BlockSpec((tf, D_MODEL), lambda i, j: (j, 0)),
            pl.BlockSpec((1, D_MODEL), lambda i, j: (0, 0)),
        ],
        out_specs=pl.BlockSpec((tm, D_MODEL), lambda i, j: (i, 0)),
        out_shape=jax.ShapeDtypeStruct((t, D_MODEL), F32),
        scratch_shapes=[pltpu.VMEM((tm, D_MODEL), BF16), pltpu.VMEM((tm, D_MODEL), F32)],
        compiler_params=pltpu.CompilerParams(
            dimension_semantics=("parallel", "arbitrary"), vmem_limit_bytes=VMEM_LIMIT),
        name="mlp",
    )(x2d, g, wu, wd, gf)


def _alibi_terms():
    slopes = 2.0 ** (-8.0 * np.arange(1, ATTN_HEADS + 1, dtype=np.float64) / ATTN_HEADS)
    rem = slopes * LOG2E
    terms = []
    for _ in range(ALIBI_TERMS):
        t = rem.astype(np.float32).astype(ml_dtypes.bfloat16).astype(np.float64)
        terms.append(t)
        rem = rem - t
    return np.stack(terms, axis=1).reshape(-1).astype(np.float32)


def _pad_rows(a, rows):
    return jnp.pad(a, ((0, rows - a.shape[0]), (0, 0)))


def kernel(x, mem, norm_mix, w_in, b_igate, b_fgate, conv_w, conv_b, lam_q1, lam_k1, lam_q2, lam_k2, attn_norm, mlstm_norm, w_attn_br, w_mlstm_br, w_out, norm_cross, norm_mem, w_cq, w_ckv, w_co, norm_mlp, w_up, w_down, norm_final):
    b, s, _ = x.shape
    t = b * s
    tq = 256
    cs_terms = jnp.asarray(_alibi_terms())
    x2d = x.reshape(t, D_MODEL)
    for l in range(DEPTH):
        lam_init = 0.8 - 0.6 * math.exp(-0.3 * l)
        w = w_in[l]
        w_a, w_b, w_gate = _wprep(w.T)

        z2d, gates, vt, w_up_b, w_down_b = _inproj(x2d, norm_mix[l][None, :], w_a, w_b, w_gate,
                                                   w_up[l], w_down[l], tq)
        z3 = z2d.reshape(b, s, IN_MAIN)

        vt5 = vt.reshape(b, s // tq, ATTN_HEADS, ATTN_V_DIM, tq)
        lamv = _pad_rows(jnp.pad(jnp.stack([lam_q1[l], lam_k1[l], lam_q2[l], lam_k2[l]]),
                                 ((0, 0), (0, LANES - ATTN_QK_DIM))), SUBLANES)
        att, w_out_b, w_abr_b, w_mbr_b = _attn(z3, vt5, cs_terms, lamv, attn_norm[l][None, :],
                                               w_out[l], w_attn_br[l], w_mlstm_br[l], lam_init, tq=tq)

        gates_t = gates[:, :SUBLANES].reshape(b, s, SUBLANES).transpose(0, 2, 1)
        gate_bias = jnp.concatenate([b_igate[l], b_fgate[l]])
        bias_col = jnp.pad(gate_bias, (0, LANES - gate_bias.shape[0]))[None, :]
        bias_row = jnp.broadcast_to(gate_bias[:, None], (SUBLANES, LANES))
        hm = _mlstm(z3, gates, gates_t, bias_col, bias_row, _pad_rows(conv_w[l], SUBLANES),
                    conv_b[l][None, :], mlstm_norm[l][None, :])

        x2d = _merge(att.reshape(t, ATTN_WIDTH), hm.reshape(t, MLSTM_WIDTH), z2d, x2d,
                     w_abr_b, w_mbr_b, w_out_b)

        ckv = _memkv(mem, norm_mem[l][None, :], w_ckv[l])
        x2d = _cross(x2d, norm_cross[l][None, :], w_cq[l].astype(BF16), ckv,
                     w_co[l].astype(BF16), s)

        x2d = _mlp(x2d, norm_mlp[l][None, :], w_up_b, w_down_b,
                   norm_final[None, :], final_norm=(l == DEPTH - 1))
    return x2d.reshape(b, s, D_MODEL)
```

```python
import functools
import math

import ml_dtypes
import numpy as np
import jax
import jax.numpy as jnp
from jax import lax
from jax.experimental import pallas as pl
from jax.experimental.pallas import tpu as pltpu

F32 = jnp.float32
BF16 = jnp.bfloat16

D_MODEL = 2048
DEPTH = 1
ATTN_HEADS = 8
ATTN_QK_DIM = 64
ATTN_V_DIM = 128
ATTN_WIDTH = ATTN_HEADS * ATTN_V_DIM
MLSTM_HEADS = 4
MLSTM_QK_DIM = 128
MLSTM_V_DIM = 256
MLSTM_QK_WIDTH = MLSTM_HEADS * MLSTM_QK_DIM
MLSTM_WIDTH = MLSTM_HEADS * MLSTM_V_DIM
CONV_WIDTH = 4
CROSS_HEADS = 4
CROSS_HEAD_DIM = 128
CROSS_WIDTH = CROSS_HEADS * CROSS_HEAD_DIM
D_FF = 4 * D_MODEL
EPS = 1e-6
LANES = 128
SUBLANES = 8
NEG_BIG = -1e30
LOG2E = 1.4426950408889634
ALIBI_TERMS = 3

COL_AQ = 0
COL_AK = 1024
COL_MQ = 2048
COL_MK = 2560
COL_MV = 3072
COL_MO = 4096
COL_GA = 5120
COL_GM = 7168
IN_MAIN = 9216
W_AQ_LO, W_AQ_HI = 0, 1024
W_AV_LO = 2048
GATE_LO = 6144
GATE_HI = 6152

VMEM_LIMIT = 56 * 1024 * 1024


def _rms(x, g):
    ms = jnp.mean(x * x, axis=-1, keepdims=True)
    return x * lax.rsqrt(ms + EPS) * g


def _sigmoid(x):
    return 0.5 * jnp.tanh(0.5 * x) + 0.5


def _log_sigmoid(x):
    return jnp.minimum(x, 0.0) - jnp.log(1.0 + jnp.exp(-jnp.abs(x)))


def _dot(a, b):
    return jnp.dot(a, b, preferred_element_type=F32)


def _dot_nt(a, b):
    return lax.dot_general(a, b, (((1,), (1,)), ((), ())), preferred_element_type=F32)


def _memkv_kernel(mem_ref, g_ref, w_ref, o_ref):
    mn = _rms(mem_ref[0], g_ref[...]).astype(BF16)
    o_ref[0] = _dot(mn, w_ref[...].astype(BF16)).astype(BF16)


def _memkv(mem, g, w_ckv):
    b, n_mem, _ = mem.shape
    return pl.pallas_call(
        _memkv_kernel,
        grid=(b,),
        in_specs=[
            pl.BlockSpec((1, n_mem, D_MODEL), lambda i: (i, 0, 0)),
            pl.BlockSpec((1, D_MODEL), lambda i: (0, 0)),
            pl.BlockSpec((D_MODEL, 2 * CROSS_WIDTH), lambda i: (0, 0)),
        ],
        out_specs=pl.BlockSpec((1, n_mem, 2 * CROSS_WIDTH), lambda i: (i, 0, 0)),
        out_shape=jax.ShapeDtypeStruct((b, n_mem, 2 * CROSS_WIDTH), BF16),
        compiler_params=pltpu.CompilerParams(
            dimension_semantics=("arbitrary",), vmem_limit_bytes=VMEM_LIMIT),
        name="memkv",
    )(mem, g, w_ckv)


def _wprep_kernel(a_ref, nxt_ref, wa_out, wb_out, wg_out, *, na):
    k = pl.program_id(0)
    ng = GATE_HI - GATE_LO

    @pl.when(k < na)
    def _():
        scale = jnp.where(k == W_AQ_LO // a_ref.shape[0], ATTN_QK_DIM ** -0.5 * LOG2E, 1.0)
        wa_out[...] = (a_ref[...] * scale).astype(BF16)

    @pl.when(k >= na)
    def _():
        wb_out[...] = jnp.concatenate([a_ref[pl.ds(ng, a_ref.shape[0] - ng), :], nxt_ref[0]],
                                      axis=0).astype(BF16)

    @pl.when(k == na)
    def _():
        pad = jnp.zeros((LANES - ng, D_MODEL), F32)
        wg_out[...] = jnp.concatenate([a_ref[pl.ds(0, ng), :], pad], axis=0).astype(BF16)


def _wprep(wt, tn=1024):
    rows = wt.shape[0]
    ng = GATE_HI - GATE_LO
    assert ng == SUBLANES and GATE_LO % tn == 0 and (rows - GATE_HI) % tn == 0
    assert W_AQ_LO == 0 and W_AQ_HI == tn
    na = GATE_LO // tn
    nb = (rows - GATE_HI) // tn
    groups = wt.reshape(rows // ng, ng, D_MODEL)
    per_block = tn // ng
    return pl.pallas_call(
        functools.partial(_wprep_kernel, na=na),
        grid=(na + nb,),
        in_specs=[
            pl.BlockSpec((tn, D_MODEL), lambda k: (k, 0)),
            pl.BlockSpec((1, ng, D_MODEL), lambda k: ((k + 1) * per_block, 0, 0)),
        ],
        out_specs=[
            pl.BlockSpec((tn, D_MODEL), lambda k: (jnp.minimum(k, na - 1), 0)),
            pl.BlockSpec((tn, D_MODEL), lambda k: (jnp.maximum(k - na, 0), 0)),
            pl.BlockSpec((LANES, D_MODEL), lambda k: (0, 0)),
        ],
        out_shape=[
            jax.ShapeDtypeStruct((GATE_LO, D_MODEL), BF16),
            jax.ShapeDtypeStruct((rows - GATE_HI, D_MODEL), BF16),
            jax.ShapeDtypeStruct((LANES, D_MODEL), BF16),
        ],
        compiler_params=pltpu.CompilerParams(
            dimension_semantics=("arbitrary",), vmem_limit_bytes=VMEM_LIMIT),
        name="wprep",
    )(wt, groups)


def _inproj_kernel(x_ref, g_ref, wa_ref, wb_ref, wg_ref, wu_ref, wd_ref,
                   z_ref, gate_ref, vt_ref, wu_out, wd_out, h_scr, *, na, jv, ncast):
    j = pl.program_id(1)

    @pl.when(j == 0)
    def _():
        h = _rms(x_ref[...], g_ref[...]).astype(BF16)
        h_scr[...] = h
        gate_ref[...] = _dot_nt(h, wg_ref[...])

    @pl.when((j < na) & (j != jv))
    def _():
        z_ref[...] = _dot_nt(h_scr[...], wa_ref[...]).astype(BF16)

    @pl.when(j == jv)
    def _():
        vt = _dot_nt(wa_ref[...], h_scr[...]).astype(BF16)
        tkv = vt_ref.shape[2]
        for n in range(vt_ref.shape[0]):
            vt_ref[n] = vt[:, n * tkv:(n + 1) * tkv]

    @pl.when(j >= na)
    def _():
        z_ref[...] = _dot_nt(h_scr[...], wb_ref[...]).astype(BF16)

    @pl.when(j < ncast)
    def _():
        wu_out[...] = wu_ref[...].astype(BF16)
        wd_out[...] = wd_ref[...].astype(BF16)


def _inproj(x2d, g, w_a, w_b, w_gate, w_up, w_down, tkv, tm=1024, tn=1024):
    t = x2d.shape[0]
    na = w_a.shape[0] // tn
    n = w_a.shape[0] + w_b.shape[0]
    ni, nj = t // tm, n // tn
    assert tn == ATTN_WIDTH and W_AV_LO % tn == 0
    jv = W_AV_LO // tn
    ncast = 8
    assert ncast <= nj
    ub = (w_up.shape[0] // ni, w_up.shape[1] // ncast)
    db = (w_down.shape[0] // ni, w_down.shape[1] // ncast)
    cast_map = lambda i, j: (i, jnp.minimum(j, ncast - 1))
    return pl.pallas_call(
        functools.partial(_inproj_kernel, na=na, jv=jv, ncast=ncast),
        grid=(ni, nj),
        in_specs=[
            pl.BlockSpec((tm, D_MODEL), lambda i, j: (i, 0)),
            pl.BlockSpec((1, D_MODEL), lambda i, j: (0, 0)),
            pl.BlockSpec((tn, D_MODEL), lambda i, j: (jnp.minimum(j, na - 1), 0)),
            pl.BlockSpec((tn, D_MODEL), lambda i, j: (jnp.maximum(j - na, 0), 0)),
            pl.BlockSpec((LANES, D_MODEL), lambda i, j: (0, 0)),
            pl.BlockSpec(ub, cast_map),
            pl.BlockSpec(db, cast_map),
        ],
        out_specs=[
            pl.BlockSpec((tm, tn), lambda i, j: (i, j - (j >= jv).astype(jnp.int32))),
            pl.BlockSpec((tm, LANES), lambda i, j: (i, 0)),
            pl.BlockSpec((tm // tkv, ATTN_WIDTH, tkv), lambda i, j: (i, 0, 0)),
            pl.BlockSpec(ub, cast_map),
            pl.BlockSpec(db, cast_map),
        ],
        out_shape=[
            jax.ShapeDtypeStruct((t, n - tn), BF16),
            jax.ShapeDtypeStruct((t, LANES), F32),
            jax.ShapeDtypeStruct((t // tkv, ATTN_WIDTH, tkv), BF16),
            jax.ShapeDtypeStruct(w_up.shape, BF16),
            jax.ShapeDtypeStruct(w_down.shape, BF16),
        ],
        scratch_shapes=[pltpu.VMEM((tm, D_MODEL), BF16)],
        compiler_params=pltpu.CompilerParams(
            dimension_semantics=("arbitrary", "arbitrary"), vmem_limit_bytes=VMEM_LIMIT),
        name="inproj",
    )(x2d, g, w_a, w_b, w_gate, w_up, w_down)


def _attn_kernel(cs_ref, lam_ref, q_ref, k_ref, vt_ref, gain_ref, wo_ref, wa_ref, wm_ref,
                 o_ref, wo_out, wa_out, wm_out, qs_ref, kf_ref, acc_ref, m_ref, *, tq, lam_init):
    tk = tq
    nfeat = ALIBI_TERMS
    dq = 2 * ATTN_QK_DIM
    dv = ATTN_V_DIM
    i = pl.program_id(1)
    cs = [[cs_ref[nfeat * h + t] for t in range(nfeat)] for h in range(ATTN_HEADS)]
    cs_tot = [sum(c[1:], c[0]) for c in cs]

    @pl.when(i == 0)
    def _():
        klane = lax.broadcasted_iota(jnp.int32, (tk, LANES), 1)
        krow = lax.broadcasted_iota(jnp.int32, (tk, LANES), 0).astype(F32)
        frow = lax.broadcasted_iota(jnp.int32, (LANES, 2 * tq), 0)
        fcol = lax.broadcasted_iota(jnp.int32, (LANES, 2 * tq), 1)
        fcol = jnp.where(fcol >= tq, fcol - tq, fcol).astype(F32)
        for h in range(ATTN_HEADS):
            kf = jnp.where(klane < nfeat, krow, 0.0)
            qf = jnp.where((frow >= nfeat) & (frow < 2 * nfeat), -fcol, 0.0)
            for t in range(nfeat):
                kf = jnp.where(klane == nfeat + t, cs[h][t], kf)
                qf = jnp.where(frow == t, cs[h][t], qf)
            kf_ref[h] = kf.astype(BF16)
            qs_ref[h, pl.ds(dq, LANES), :] = qf.astype(BF16)

    row = lax.broadcasted_iota(jnp.int32, (dq, tq), 0)
    for h in range(ATTN_HEADS):
        qt = q_ref[0, :, pl.ds(h * dq, dq)].astype(F32).T
        qs_ref[h, pl.ds(0, dq), pl.ds(0, tq)] = jnp.where(row < ATTN_QK_DIM, qt, 0.0).astype(BF16)
        qs_ref[h, pl.ds(0, dq), pl.ds(tq, tq)] = jnp.where(row >= ATTN_QK_DIM, qt, 0.0).astype(BF16)
    acc_ref[...] = jnp.zeros_like(acc_ref)
    m_ref[...] = jnp.full_like(m_ref, NEG_BIG)

    orow = lax.broadcasted_iota(jnp.int32, (2 * SUBLANES, tk), 0)
    ones_blk = jnp.where(orow == 0, 1.0, 0.0).astype(BF16)

    def scores(h, j):
        k = k_ref[0, pl.ds(pl.multiple_of(j * tk, tk), tk), pl.ds(h * dq, dq)]
        kx = jnp.concatenate([k, kf_ref[h]], axis=1)
        return _dot(kx, qs_ref[h])

    def softmax_pv(h, j, t, diagonal):
        if diagonal:
            krow = lax.broadcasted_iota(jnp.int32, (tk, 2 * tq), 0)
            qcol = lax.broadcasted_iota(jnp.int32, (tk, 2 * tq), 1)
            qcol = jnp.where(qcol >= tq, qcol - tq, qcol)
            t = jnp.where(krow <= qcol, t, NEG_BIG)
        soff = cs_tot[h] * ((j - i) * tk).astype(F32)
        m = m_ref[h]
        m_new = jnp.maximum(m, jnp.max(t, axis=0, keepdims=True) + soff)
        alpha = jnp.exp2(m - m_new)
        p = jnp.exp2(t - (m_new - soff)).astype(BF16)
        vx = jnp.concatenate([vt_ref[0, j, h], ones_blk], axis=0)
        acc_ref[h] = alpha * acc_ref[h] + _dot(vx, p)
        m_ref[h] = m_new

    def run(units):
        ahead = 2
        pending = [scores(*u[:2]) for u in units[:ahead]]
        for n, (h, j, diagonal) in enumerate(units):
            if n + ahead < len(units):
                pending.append(scores(*units[n + ahead][:2]))
            softmax_pv(h, j, pending.pop(0), diagonal)

    def tile_units(j, diagonal):
        return [(h, j, diagonal) for h in range(ATTN_HEADS)]

    def body(jj, carry):
        run(tile_units(2 * jj, False) + tile_units(2 * jj + 1, False))
        return carry

    lax.fori_loop(0, i // 2, body, 0)

    @pl.when(i % 2 == 0)
    def _():
        run(tile_units(i, True))

    @pl.when(i % 2 == 1)
    def _():
        run(tile_units(i - 1, False) + tile_units(i, True))

    lv = lam_ref[...]
    d1 = jnp.sum(lv[0:1] * lv[1:2], axis=-1, keepdims=True)
    d2 = jnp.sum(lv[2:3] * lv[3:4], axis=-1, keepdims=True)
    lam = jnp.exp(d1) - jnp.exp(d2) + lam_init
    for h in range(ATTN_HEADS):
        l = acc_ref[h, pl.ds(dv, 1), :]
        rl = 1.0 / l
        out = (acc_ref[h, pl.ds(0, dv), pl.ds(0, tq)] * rl[:, :tq]
               - lam * (acc_ref[h, pl.ds(0, dv), pl.ds(tq, tq)] * rl[:, tq:]))
        ms = jnp.mean(out * out, axis=0, keepdims=True)
        on = out * lax.rsqrt(ms + EPS)
        o_ref[0, :, pl.ds(h * dv, dv)] = (on.T * gain_ref[...] * (1.0 - lam_init)).astype(BF16)

    wo_out[...] = wo_ref[...].astype(BF16)
    wa_out[...] = wa_ref[...].astype(BF16)
    wm_out[...] = wm_ref[...].astype(BF16)


def _attn(z3, vt5, cs_terms, lamv, gain, w_out, w_abr, w_mbr, lam_init, tq=256):
    b, s, _ = z3.shape
    nq = s // tq
    steps = b * nq
    cast_specs = [pl.BlockSpec((w.shape[0] // steps, w.shape[1]), lambda bi, i: (bi * nq + i, 0))
                  for w in (w_out, w_abr, w_mbr)]
    kern = functools.partial(_attn_kernel, tq=tq, lam_init=lam_init)
    width = ATTN_HEADS * 2 * ATTN_QK_DIM
    return pl.pallas_call(
        kern,
        grid=(b, nq),
        in_specs=[
            pl.BlockSpec(memory_space=pltpu.SMEM),
            pl.BlockSpec((SUBLANES, LANES), lambda bi, i: (0, 0)),
            pl.BlockSpec((1, tq, width), lambda bi, i: (bi, i, COL_AQ // width)),
            pl.BlockSpec((1, s, width), lambda bi, i: (bi, 0, COL_AK // width)),
            pl.BlockSpec((1, nq, ATTN_HEADS, ATTN_V_DIM, tq), lambda bi, i: (bi, 0, 0, 0, 0)),
            pl.BlockSpec((1, ATTN_V_DIM), lambda bi, i: (0, 0)),
        ] + cast_specs,
        out_specs=[pl.BlockSpec((1, tq, ATTN_WIDTH), lambda bi, i: (bi, i, 0))] + cast_specs,
        out_shape=[jax.ShapeDtypeStruct((b, s, ATTN_WIDTH), BF16)]
        + [jax.ShapeDtypeStruct(w.shape, BF16) for w in (w_out, w_abr, w_mbr)],
        scratch_shapes=[
            pltpu.VMEM((ATTN_HEADS, 2 * ATTN_QK_DIM + LANES, 2 * tq), BF16),
            pltpu.VMEM((ATTN_HEADS, tq, LANES), BF16),
            pltpu.VMEM((ATTN_HEADS, ATTN_V_DIM + 2 * SUBLANES, 2 * tq), F32),
            pltpu.VMEM((ATTN_HEADS, 1, 2 * tq), F32),
        ],
        compiler_params=pltpu.CompilerParams(
            dimension_semantics=("parallel", "arbitrary"),
            vmem_limit_bytes=VMEM_LIMIT),
        name="attn",
    )(cs_terms, lamv, z3, z3, vt5, gain, w_out, w_abr, w_mbr)


def _split3(x):
    hi = x.astype(BF16)
    r = x - hi.astype(F32)
    mid = r.astype(BF16)
    lo = (r - mid.astype(F32)).astype(BF16)
    return hi, mid, lo


def _mlstm_kernel(uq_ref, uk_ref, v_ref, mo_ref, gcol_ref, grow_ref, bcol_ref, brow_ref,
                  cw_ref, cb_ref, gn_ref, o_ref, extq, extk, dstq, dstk, c_scr, n_scr, m_scr,
                  *, chunk):
    L = chunk
    nh = MLSTM_HEADS
    dk = MLSTM_QK_DIM
    dv = MLSTM_V_DIM
    heads = range(nh)

    @pl.when(pl.program_id(1) == 0)
    def _():
        extq[:, pl.ds(0, SUBLANES), :] = jnp.zeros((nh, SUBLANES, dk), F32)
        extk[:, pl.ds(0, SUBLANES), :] = jnp.zeros((nh, SUBLANES, dk), F32)
        c_scr[...] = jnp.zeros_like(c_scr)
        n_scr[...] = jnp.zeros_like(n_scr)
        m_scr[...] = jnp.zeros_like(m_scr)

    def conv_silu(u_ref, ext, dst, col0, scale):
        n8 = L // SUBLANES
        first = SUBLANES - (CONV_WIDTH - 1)
        outs = []
        for h in heads:
            cols = pl.ds(col0 + h * dk, dk)
            ext[h, pl.ds(SUBLANES, L), :] = u_ref[0, :, pl.ds(h * dk, dk)].astype(F32)
            w = cw_ref[:, cols]
            bias = cb_ref[:, cols]
            rows = [ext[h, pl.ds(first + s, n8, stride=SUBLANES), :]
                    for s in range(SUBLANES + CONV_WIDTH - 1)]
            for r in range(SUBLANES):
                y = bias
                for tap in range(CONV_WIDTH):
                    y = y + rows[r + tap] * w[tap:tap + 1]
                y = y * _sigmoid(y)
                dst[h, pl.ds(r, n8, stride=SUBLANES), :] = y if scale is None else y * scale
            ext[h, pl.ds(0, SUBLANES), :] = ext[h, pl.ds(L, SUBLANES), :]
            outs.append(dst[h])
        return outs

    q = conv_silu(uq_ref, extq, dstq, 0, None)
    k = conv_silu(uk_ref, extk, dstk, MLSTM_QK_WIDTH, dk ** -0.5)
    qb = [x.astype(BF16) for x in q]
    kb = [x.astype(BF16) for x in k]
    vb = [v_ref[0, :, pl.ds(h * dv, dv)] for h in heads]

    g_c = gcol_ref[...] + bcol_ref[...]
    g_r = grow_ref[0] + brow_ref[:, 0:1]
    lf_c = _log_sigmoid(g_c)
    lf_r = _log_sigmoid(g_r)

    r_i = lax.broadcasted_iota(jnp.int32, (L, L), 0)
    c_i = lax.broadcasted_iota(jnp.int32, (L, L), 1)
    causal = c_i <= r_i
    tri = jnp.where(causal, 1.0, 0.0).astype(BF16)
    tri_t = jnp.where(r_i <= c_i, 1.0, 0.0).astype(BF16)
    hi, mid, lo = _split3(lf_c)
    b_c = _dot(tri, hi) + _dot(tri, mid) + _dot(tri, lo)
    hi, mid, lo = _split3(lf_r)
    b_r = _dot(hi, tri_t) + _dot(mid, tri_t) + _dot(lo, tri_t)
    g_sum = jnp.sum(lf_r, axis=1, keepdims=True)

    lane = lax.broadcasted_iota(jnp.int32, (L, LANES), 1)

    def col(x, idx):
        return jnp.sum(jnp.where(lane == idx, x, 0.0), axis=1, keepdims=True)

    i_col = [col(g_c, h) for h in heads]
    b_col = [col(b_c, nh + h) for h in heads]
    i_row = [g_r[h:h + 1, :] for h in heads]
    b_row = [b_r[nh + h:nh + h + 1, :] for h in heads]
    g_tot = [g_sum[nh + h:nh + h + 1, :] for h in heads]

    c_prev = [c_scr[h] for h in heads]
    n_prev = [n_scr[h] for h in heads]
    m_prev = [m_scr[h][:, 0:1] for h in heads]

    s_qk = [_dot_nt(qb[h], kb[h]) for h in heads]
    inter = [_dot(qb[h], c_prev[h].astype(BF16)) for h in heads]
    m_loc = [jnp.max(g_tot[h] - b_row[h] + i_row[h], axis=1, keepdims=True) for h in heads]
    kw = [k[h] * jnp.exp(g_tot[h] - b_col[h] + i_col[h] - m_loc[h]) for h in heads]
    c_loc = [_dot(kw[h].T.astype(BF16), vb[h]) for h in heads]

    for h in heads:
        d = jnp.where(causal, b_col[h] - b_row[h] + i_row[h], NEG_BIG)
        m_inter = b_col[h] + m_prev[h]
        m_j = jnp.maximum(m_inter, jnp.max(d, axis=1, keepdims=True))
        w_inter = jnp.exp(m_inter - m_j)
        p = s_qk[h] * jnp.exp(d - m_j)
        num = w_inter * inter[h] + _dot(p.astype(BF16), vb[h])
        den = (w_inter * jnp.sum(q[h] * n_prev[h], axis=1, keepdims=True)
               + jnp.sum(p, axis=1, keepdims=True))
        hval = num * (1.0 / jnp.maximum(jnp.abs(den), jnp.exp(-m_j)))
        hn = _rms(hval, gn_ref[:, pl.ds(h * dv, dv)])
        gate = _sigmoid(mo_ref[0, :, pl.ds(h * dv, dv)].astype(F32))
        o_ref[0, :, pl.ds(h * dv, dv)] = (hn * gate).astype(BF16)

    for h in heads:
        n_loc = jnp.sum(kw[h], axis=0, keepdims=True)
        m_new = jnp.maximum(g_tot[h] + m_prev[h], m_loc[h])
        a_old = jnp.exp(g_tot[h] + m_prev[h] - m_new)
        a_new = jnp.exp(m_loc[h] - m_new)
        c_scr[h] = a_old * c_prev[h] + a_new * c_loc[h]
        n_scr[h] = a_old * n_prev[h] + a_new * n_loc
        m_scr[h] = jnp.broadcast_to(m_new, (1, LANES))


def _mlstm(z3, gates, gates_t, bias_col, bias_row, conv_w8, conv_b, gnorm, chunk=256):
    b, s, _ = z3.shape
    nc = s // chunk
    kern = functools.partial(_mlstm_kernel, chunk=chunk)
    qw = MLSTM_QK_WIDTH
    vw = MLSTM_WIDTH
    return pl.pallas_call(
        kern,
        grid=(b, nc),
        in_specs=[
            pl.BlockSpec((1, chunk, qw), lambda bi, c: (bi, c, COL_MQ // qw)),
            pl.BlockSpec((1, chunk, qw), lambda bi, c: (bi, c, COL_MK // qw)),
            pl.BlockSpec((1, chunk, vw), lambda bi, c: (bi, c, COL_MV // vw)),
            pl.BlockSpec((1, chunk, vw), lambda bi, c: (bi, c, COL_MO // vw)),
            pl.BlockSpec((chunk, LANES), lambda bi, c: (bi * nc + c, 0)),
            pl.BlockSpec((1, SUBLANES, chunk), lambda bi, c: (bi, 0, c)),
            pl.BlockSpec((1, LANES), lambda bi, c: (0, 0)),
            pl.BlockSpec((SUBLANES, LANES), lambda bi, c: (0, 0)),
            pl.BlockSpec((SUBLANES, 2 * qw), lambda bi, c: (0, 0)),
            pl.BlockSpec((1, 2 * qw), lambda bi, c: (0, 0)),
            pl.BlockSpec((1, vw), lambda bi, c: (0, 0)),
        ],
        out_specs=pl.BlockSpec((1, chunk, vw), lambda bi, c: (bi, c, 0)),
        out_shape=jax.ShapeDtypeStruct((b, s, vw), BF16),
        scratch_shapes=[
            pltpu.VMEM((MLSTM_HEADS, chunk + 2 * SUBLANES, MLSTM_QK_DIM), F32),
            pltpu.VMEM((MLSTM_HEADS, chunk + 2 * SUBLANES, MLSTM_QK_DIM), F32),
            pltpu.VMEM((MLSTM_HEADS, chunk, MLSTM_QK_DIM), F32),
            pltpu.VMEM((MLSTM_HEADS, chunk, MLSTM_QK_DIM), F32),
            pltpu.VMEM((MLSTM_HEADS, MLSTM_QK_DIM, MLSTM_V_DIM), F32),
            pltpu.VMEM((MLSTM_HEADS, 1, MLSTM_QK_DIM), F32),
            pltpu.VMEM((MLSTM_HEADS, 1, LANES), F32),
        ],
        compiler_params=pltpu.CompilerParams(
            dimension_semantics=("parallel", "arbitrary"),
            vmem_limit_bytes=VMEM_LIMIT),
        name="mlstm",
    )(z3, z3, z3, z3, gates, gates_t, bias_col, bias_row, conv_w8, conv_b, gnorm)


def _merge_kernel(att_ref, hm_ref, ga_ref, gm_ref, x_ref, wa_ref, wm_ref, wo_ref, o_ref):
    @pl.when(pl.program_id(1) == 0)
    def _():
        o_ref[...] = x_ref[...]

    a = _dot(att_ref[...], wa_ref[...])
    bm = _dot(hm_ref[...], wm_ref[...])
    y = _sigmoid(ga_ref[...].astype(F32)) * a + _sigmoid(gm_ref[...].astype(F32)) * bm
    o_ref[...] += _dot(y.astype(BF16), wo_ref[...])


def _merge(att2d, hm2d, z2d, x2d, wa, wm, wo, tm=512, tn=1024):
    t = x2d.shape[0]
    nj = D_MODEL // tn
    return pl.pallas_call(
        _merge_kernel,
        grid=(t // tm, nj),
        in_specs=[
            pl.BlockSpec((tm, ATTN_WIDTH), lambda i, j: (i, 0)),
            pl.BlockSpec((tm, MLSTM_WIDTH), lambda i, j: (i, 0)),
            pl.BlockSpec((tm, tn), lambda i, j: (i, COL_GA // tn + j)),
            pl.BlockSpec((tm, tn), lambda i, j: (i, COL_GM // tn + j)),
            pl.BlockSpec((tm, D_MODEL), lambda i, j: (i, 0)),
            pl.BlockSpec((ATTN_WIDTH, tn), lambda i, j: (0, j)),
            pl.BlockSpec((MLSTM_WIDTH, tn), lambda i, j: (0, j)),
            pl.BlockSpec((tn, D_MODEL), lambda i, j: (j, 0)),
        ],
        out_specs=pl.BlockSpec((tm, D_MODEL), lambda i, j: (i, 0)),
        out_shape=jax.ShapeDtypeStruct((t, D_MODEL), F32),
        compiler_params=pltpu.CompilerParams(
            dimension_semantics=("parallel", "arbitrary"), vmem_limit_bytes=VMEM_LIMIT),
        name="merge",
    )(att2d, hm2d, z2d, z2d, x2d, wa, wm, wo)


def _cross_kernel(x_ref, g_ref, wq_ref, kv_ref, wo_ref, o_ref):
    x = x_ref[...]
    hc = _rms(x, g_ref[...]).astype(BF16)
    cq = (_dot(hc, wq_ref[...]) * (CROSS_HEAD_DIM ** -0.5)).astype(BF16)
    outs = []
    for hh in range(CROSS_HEADS):
        lo = hh * CROSS_HEAD_DIM
        qh = cq[:, lo:lo + CROSS_HEAD_DIM]
        kh = kv_ref[0, :, lo:lo + CROSS_HEAD_DIM]
        vh = kv_ref[0, :, CROSS_WIDTH + lo:CROSS_WIDTH + lo + CROSS_HEAD_DIM]
        s = _dot_nt(qh, kh)
        p = jnp.exp(s - jnp.max(s, axis=-1, keepdims=True))
        l = jnp.sum(p, axis=-1, keepdims=True)
        outs.append((_dot(p.astype(BF16), vh) * (1.0 / l)).astype(BF16))
    co = jnp.concatenate(outs, axis=1)
    o_ref[...] = x + _dot(co, wo_ref[...])


def _cross(x2d, g, wq, ckv, wo, seq, tm=512):
    t = x2d.shape[0]
    n_mem = ckv.shape[1]
    per_batch = seq // tm
    return pl.pallas_call(
        _cross_kernel,
        grid=(t // tm,),
        in_specs=[
            pl.BlockSpec((tm, D_MODEL), lambda i: (i, 0)),
            pl.BlockSpec((1, D_MODEL), lambda i: (0, 0)),
            pl.BlockSpec((D_MODEL, CROSS_WIDTH), lambda i: (0, 0)),
            pl.BlockSpec((1, n_mem, 2 * CROSS_WIDTH), lambda i: (i // per_batch, 0, 0)),
            pl.BlockSpec((CROSS_WIDTH, D_MODEL), lambda i: (0, 0)),
        ],
        out_specs=pl.BlockSpec((tm, D_MODEL), lambda i: (i, 0)),
        out_shape=jax.ShapeDtypeStruct((t, D_MODEL), F32),
        compiler_params=pltpu.CompilerParams(
            dimension_semantics=("parallel",), vmem_limit_bytes=VMEM_LIMIT),
        name="cross",
    )(x2d, g, wq, ckv, wo)


def _mlp_kernel(x_ref, g_ref, wu_ref, wd_ref, gf_ref, o_ref, h_scr, acc, *, final_norm):
    j = pl.program_id(1)

    @pl.when(j == 0)
    def _():
        x = x_ref[...]
        h_scr[...] = _rms(x, g_ref[...]).astype(BF16)
        acc[...] = x

    u = jnp.square(jnp.maximum(_dot(h_scr[...], wu_ref[...]), 0.0)).astype(BF16)
    acc[...] += _dot(u, wd_ref[...])

    @pl.when(j == pl.num_programs(1) - 1)
    def _():
        if final_norm:
            o_ref[...] = _rms(acc[...], gf_ref[...])
        else:
            o_ref[...] = acc[...]


def _mlp(x2d, g, wu, wd, gf, final_norm, tm=512, tf=1024):
    t = x2d.shape[0]
    kern = functools.partial(_mlp_kernel, final_norm=final_norm)
    return pl.pallas_call(
        kern,
        grid=(t // tm, D_FF // tf),
        in_specs=[
            pl.BlockSpec((tm, D_MODEL), lambda i, j: (i, 0)),
            pl.BlockSpec((1, D_MODEL), lambda i, j: (0, 0)),
            pl.BlockSpec((D_MODEL, tf), lambda i, j: (0, j)),
            pl.BlockSpec((tf, D_MODEL), lambda i, j: (j, 0)),
            pl.BlockSpec((1, D_MODEL), lambda i, j: (0, 0)),
        ],
        out_specs=pl.BlockSpec((tm, D_MODEL), lambda i, j: (i, 0)),
        out_shape=jax.ShapeDtypeStruct((t, D_MODEL), F32),
        scratch_shapes=[pltpu.VMEM((tm, D_MODEL), BF16), pltpu.VMEM((tm, D_MODEL), F32)],
        compiler_params=pltpu.CompilerParams(
            dimension_semantics=("parallel", "arbitrary"), vmem_limit_bytes=VMEM_LIMIT),
        name="mlp",
    )(x2d, g, wu, wd, gf)


def _alibi_terms():
    slopes = 2.0 ** (-8.0 * np.arange(1, ATTN_HEADS + 1, dtype=np.float64) / ATTN_HEADS)
    rem = slopes * LOG2E
    terms = []
    for _ in range(ALIBI_TERMS):
        t = rem.astype(np.float32).astype(ml_dtypes.bfloat16).astype(np.float64)
        terms.append(t)
        rem = rem - t
    return np.stack(terms, axis=1).reshape(-1).astype(np.float32)


def _pad_rows(a, rows):
    return jnp.pad(a, ((0, rows - a.shape[0]), (0, 0)))


def kernel(x, mem, norm_mix, w_in, b_igate, b_fgate, conv_w, conv_b, lam_q1, lam_k1, lam_q2, lam_k2, attn_norm, mlstm_norm, w_attn_br, w_mlstm_br, w_out, norm_cross, norm_mem, w_cq, w_ckv, w_co, norm_mlp, w_up, w_down, norm_final):
    b, s, _ = x.shape
    t = b * s
    tq = 256
    cs_terms = jnp.asarray(_alibi_terms())
    x2d = x.reshape(t, D_MODEL)
    for l in range(DEPTH):
        lam_init = 0.8 - 0.6 * math.exp(-0.3 * l)
        w = w_in[l]
        w_a, w_b, w_gate = _wprep(w.T)

        z2d, gates, vt, w_up_b, w_down_b = _inproj(x2d, norm_mix[l][None, :], w_a, w_b, w_gate,
                                                   w_up[l], w_down[l], tq)
        z3 = z2d.reshape(b, s, IN_MAIN)

        vt5 = vt.reshape(b, s // tq, ATTN_HEADS, ATTN_V_DIM, tq)
        lamv = _pad_rows(jnp.pad(jnp.stack([lam_q1[l], lam_k1[l], lam_q2[l], lam_k2[l]]),
                                 ((0, 0), (0, LANES - ATTN_QK_DIM))), SUBLANES)
        att, w_out_b, w_abr_b, w_mbr_b = _attn(z3, vt5, cs_terms, lamv, attn_norm[l][None, :],
                                               w_out[l], w_attn_br[l], w_mlstm_br[l], lam_init, tq=tq)

        gates_t = gates[:, :SUBLANES].reshape(b, s, SUBLANES).transpose(0, 2, 1)
        gate_bias = jnp.concatenate([b_igate[l], b_fgate[l]])
        bias_col = jnp.pad(gate_bias, (0, LANES - gate_bias.shape[0]))[None, :]
        bias_row = jnp.broadcast_to(gate_bias[:, None], (SUBLANES, LANES))
        hm = _mlstm(z3, gates, gates_t, bias_col, bias_row, _pad_rows(conv_w[l], SUBLANES),
                    conv_b[l][None, :], mlstm_norm[l][None, :])

        x2d = _merge(att.reshape(t, ATTN_WIDTH), hm.reshape(t, MLSTM_WIDTH), z2d, x2d,
                     w_abr_b, w_mbr_b, w_out_b)

        ckv = _memkv(mem, norm_mem[l][None, :], w_ckv[l])
        x2d = _cross(x2d, norm_cross[l][None, :], w_cq[l].astype(BF16), ckv,
                     w_co[l].astype(BF16), s)

        x2d = _mlp(x2d, norm_mlp[l][None, :], w_up_b, w_down_b,
                   norm_final[None, :], final_norm=(l == DEPTH - 1))
    return x2d.reshape(b, s, D_MODEL)
```

```python
import functools
import math

import ml_dtypes
import numpy as np
import jax
import jax.numpy as jnp
from jax import lax
from jax.experimental import pallas as pl
from jax.experimental.pallas import tpu as pltpu

F32 = jnp.float32
BF16 = jnp.bfloat16

D_MODEL = 2048
DEPTH = 1
ATTN_HEADS = 8
ATTN_QK_DIM = 64
ATTN_V_DIM = 128
ATTN_WIDTH = ATTN_HEADS * ATTN_V_DIM
MLSTM_HEADS = 4
MLSTM_QK_DIM = 128
MLSTM_V_DIM = 256
MLSTM_QK_WIDTH = MLSTM_HEADS * MLSTM_QK_DIM
MLSTM_WIDTH = MLSTM_HEADS * MLSTM_V_DIM
CONV_WIDTH = 4
CROSS_HEADS = 4
CROSS_HEAD_DIM = 128
CROSS_WIDTH = CROSS_HEADS * CROSS_HEAD_DIM
D_FF = 4 * D_MODEL
EPS = 1e-6
LANES = 128
SUBLANES = 8
NEG_BIG = -1e30
LOG2E = 1.4426950408889634
ALIBI_TERMS = 3

COL_AQ = 0
COL_AK = 1024
COL_MQ = 2048
COL_MK = 2560
COL_MV = 3072
COL_MO = 4096
COL_GA = 5120
COL_GM = 7168
IN_MAIN = 9216
W_AQ_LO, W_AQ_HI = 0, 1024
W_AV_LO = 2048
GATE_LO = 6144
GATE_HI = 6152

VMEM_LIMIT = 56 * 1024 * 1024


def _rms(x, g):
    ms = jnp.mean(x * x, axis=-1, keepdims=True)
    return x * lax.rsqrt(ms + EPS) * g


def _sigmoid(x):
    return 0.5 * jnp.tanh(0.5 * x) + 0.5


def _log_sigmoid(x):
    return jnp.minimum(x, 0.0) - jnp.log(1.0 + jnp.exp(-jnp.abs(x)))


def _dot(a, b):
    return jnp.dot(a, b, preferred_element_type=F32)


def _dot_nt(a, b):
    return lax.dot_general(a, b, (((1,), (1,)), ((), ())), preferred_element_type=F32)


def _memkv_kernel(mem_ref, g_ref, w_ref, o_ref):
    mn = _rms(mem_ref[0], g_ref[...]).astype(BF16)
    o_ref[0] = _dot(mn, w_ref[...].astype(BF16)).astype(BF16)


def _memkv(mem, g, w_ckv):
    b, n_mem, _ = mem.shape
    return pl.pallas_call(
        _memkv_kernel,
        grid=(b,),
        in_specs=[
            pl.BlockSpec((1, n_mem, D_MODEL), lambda i: (i, 0, 0)),
            pl.BlockSpec((1, D_MODEL), lambda i: (0, 0)),
            pl.BlockSpec((D_MODEL, 2 * CROSS_WIDTH), lambda i: (0, 0)),
        ],
        out_specs=pl.BlockSpec((1, n_mem, 2 * CROSS_WIDTH), lambda i: (i, 0, 0)),
        out_shape=jax.ShapeDtypeStruct((b, n_mem, 2 * CROSS_WIDTH), BF16),
        compiler_params=pltpu.CompilerParams(
            dimension_semantics=("arbitrary",), vmem_limit_bytes=VMEM_LIMIT),
        name="memkv",
    )(mem, g, w_ckv)


def _wprep_kernel(a_ref, nxt_ref, wa_out, wb_out, wg_out, *, na):
    k = pl.program_id(0)
    ng = GATE_HI - GATE_LO

    @pl.when(k < na)
    def _():
        scale = jnp.where(k == W_AQ_LO // a_ref.shape[0], ATTN_QK_DIM ** -0.5 * LOG2E, 1.0)
        wa_out[...] = (a_ref[...] * scale).astype(BF16)

    @pl.when(k >= na)
    def _():
        wb_out[...] = jnp.concatenate([a_ref[pl.ds(ng, a_ref.shape[0] - ng), :], nxt_ref[0]],
                                      axis=0).astype(BF16)

    @pl.when(k == na)
    def _():
        pad = jnp.zeros((LANES - ng, D_MODEL), F32)
        wg_out[...] = jnp.concatenate([a_ref[pl.ds(0, ng), :], pad], axis=0).astype(BF16)


def _wprep(wt, tn=1024):
    rows = wt.shape[0]
    ng = GATE_HI - GATE_LO
    assert ng == SUBLANES and GATE_LO % tn == 0 and (rows - GATE_HI) % tn == 0
    assert W_AQ_LO == 0 and W_AQ_HI == tn
    na = GATE_LO // tn
    nb = (rows - GATE_HI) // tn
    groups = wt.reshape(rows // ng, ng, D_MODEL)
    per_block = tn // ng
    return pl.pallas_call(
        functools.partial(_wprep_kernel, na=na),
        grid=(na + nb,),
        in_specs=[
            pl.BlockSpec((tn, D_MODEL), lambda k: (k, 0)),
            pl.BlockSpec((1, ng, D_MODEL), lambda k: ((k + 1) * per_block, 0, 0)),
        ],
        out_specs=[
            pl.BlockSpec((tn, D_MODEL), lambda k: (jnp.minimum(k, na - 1), 0)),
            pl.BlockSpec((tn, D_MODEL), lambda k: (jnp.maximum(k - na, 0), 0)),
            pl.BlockSpec((LANES, D_MODEL), lambda k: (0, 0)),
        ],
        out_shape=[
            jax.ShapeDtypeStruct((GATE_LO, D_MODEL), BF16),
            jax.ShapeDtypeStruct((rows - GATE_HI, D_MODEL), BF16),
            jax.ShapeDtypeStruct((LANES, D_MODEL), BF16),
        ],
        compiler_params=pltpu.CompilerParams(
            dimension_semantics=("arbitrary",), vmem_limit_bytes=VMEM_LIMIT),
        name="wprep",
    )(wt, groups)


def _inproj_kernel(x_ref, g_ref, wa_ref, wb_ref, wg_ref, wu_ref, wd_ref,
                   z_ref, gate_ref, vt_ref, wu_out, wd_out, h_scr, *, na, jv, ncast):
    j = pl.program_id(1)

    @pl.when(j == 0)
    def _():
        h = _rms(x_ref[...], g_ref[...]).astype(BF16)
        h_scr[...] = h
        gate_ref[...] = _dot_nt(h, wg_ref[...])

    @pl.when((j < na) & (j != jv))
    def _():
        z_ref[...] = _dot_nt(h_scr[...], wa_ref[...]).astype(BF16)

    @pl.when(j == jv)
    def _():
        vt = _dot_nt(wa_ref[...], h_scr[...]).astype(BF16)
        tkv = vt_ref.shape[2]
        for n in range(vt_ref.shape[0]):
            vt_ref[n] = vt[:, n * tkv:(n + 1) * tkv]

    @pl.when(j >= na)
    def _():
        z_ref[...] = _dot_nt(h_scr[...], wb_ref[...]).astype(BF16)

    @pl.when(j < ncast)
    def _():
        wu_out[...] = wu_ref[...].astype(BF16)
        wd_out[...] = wd_ref[...].astype(BF16)


def _inproj(x2d, g, w_a, w_b, w_gate, w_up, w_down, tkv, tm=1024, tn=1024):
    t = x2d.shape[0]
    na = w_a.shape[0] // tn
    n = w_a.shape[0] + w_b.shape[0]
    ni, nj = t // tm, n // tn
    assert tn == ATTN_WIDTH and W_AV_LO % tn == 0
    jv = W_AV_LO // tn
    ncast = 8
    assert ncast <= nj
    ub = (w_up.shape[0] // ni, w_up.shape[1] // ncast)
    db = (w_down.shape[0] // ni, w_down.shape[1] // ncast)
    cast_map = lambda i, j: (i, jnp.minimum(j, ncast - 1))
    return pl.pallas_call(
        functools.partial(_inproj_kernel, na=na, jv=jv, ncast=ncast),
        grid=(ni, nj),
        in_specs=[
            pl.BlockSpec((tm, D_MODEL), lambda i, j: (i, 0)),
            pl.BlockSpec((1, D_MODEL), lambda i, j: (0, 0)),
            pl.BlockSpec((tn, D_MODEL), lambda i, j: (jnp.minimum(j, na - 1), 0)),
            pl.BlockSpec((tn, D_MODEL), lambda i, j: (jnp.maximum(j - na, 0), 0)),
            pl.BlockSpec((LANES, D_MODEL), lambda i, j: (0, 0)),
            pl.BlockSpec(ub, cast_map),
            pl.BlockSpec(db, cast_map),
        ],
        out_specs=[
            pl.BlockSpec((tm, tn), lambda i, j: (i, j - (j >= jv).astype(jnp.int32))),
            pl.BlockSpec((tm, LANES), lambda i, j: (i, 0)),
            pl.BlockSpec((tm // tkv, ATTN_WIDTH, tkv), lambda i, j: (i, 0, 0)),
            pl.BlockSpec(ub, cast_map),
            pl.BlockSpec(db, cast_map),
        ],
        out_shape=[
            jax.ShapeDtypeStruct((t, n - tn), BF16),
            jax.ShapeDtypeStruct((t, LANES), F32),
            jax.ShapeDtypeStruct((t // tkv, ATTN_WIDTH, tkv), BF16),
            jax.ShapeDtypeStruct(w_up.shape, BF16),
            jax.ShapeDtypeStruct(w_down.shape, BF16),
        ],
        scratch_shapes=[pltpu.VMEM((tm, D_MODEL), BF16)],
        compiler_params=pltpu.CompilerParams(
            dimension_semantics=("arbitrary", "arbitrary"), vmem_limit_bytes=VMEM_LIMIT),
        name="inproj",
    )(x2d, g, w_a, w_b, w_gate, w_up, w_down)


def _attn_kernel(cs_ref, lam_ref, q_ref, k_ref, vt_ref, gain_ref, wo_ref, wa_ref, wm_ref,
                 o_ref, wo_out, wa_out, wm_out, qs_ref, kf_ref, acc_ref, m_ref, *, tq, lam_init):
    tk = tq
    nfeat = ALIBI_TERMS
    dq = 2 * ATTN_QK_DIM
    dv = ATTN_V_DIM
    i = pl.program_id(1)
    cs = [[cs_ref[nfeat * h + t] for t in range(nfeat)] for h in range(ATTN_HEADS)]
    cs_tot = [sum(c[1:], c[0]) for c in cs]

    @pl.when(i == 0)
    def _():
        klane = lax.broadcasted_iota(jnp.int32, (tk, LANES), 1)
        krow = lax.broadcasted_iota(jnp.int32, (tk, LANES), 0).astype(F32)
        frow = lax.broadcasted_iota(jnp.int32, (LANES, 2 * tq), 0)
        fcol = lax.broadcasted_iota(jnp.int32, (LANES, 2 * tq), 1)
        fcol = jnp.where(fcol >= tq, fcol - tq, fcol).astype(F32)
        for h in range(ATTN_HEADS):
            kf = jnp.where(klane < nfeat, krow, 0.0)
            qf = jnp.where((frow >= nfeat) & (frow < 2 * nfeat), -fcol, 0.0)
            for t in range(nfeat):
                kf = jnp.where(klane == nfeat + t, cs[h][t], kf)
                qf = jnp.where(frow == t, cs[h][t], qf)
            kf_ref[h] = kf.astype(BF16)
            qs_ref[h, pl.ds(dq, LANES), :] = qf.astype(BF16)

    row = lax.broadcasted_iota(jnp.int32, (dq, tq), 0)
    for h in range(ATTN_HEADS):
        qt = q_ref[0, :, pl.ds(h * dq, dq)].astype(F32).T
        qs_ref[h, pl.ds(0, dq), pl.ds(0, tq)] = jnp.where(row < ATTN_QK_DIM, qt, 0.0).astype(BF16)
        qs_ref[h, pl.ds(0, dq), pl.ds(tq, tq)] = jnp.where(row >= ATTN_QK_DIM, qt, 0.0).astype(BF16)
    acc_ref[...] = jnp.zeros_like(acc_ref)
    m_ref[...] = jnp.full_like(m_ref, NEG_BIG)

    orow = lax.broadcasted_iota(jnp.int32, (2 * SUBLANES, tk), 0)
    ones_blk = jnp.where(orow == 0, 1.0, 0.0).astype(BF16)

    def scores(h, j):
        k = k_ref[0, pl.ds(pl.multiple_of(j * tk, tk), tk), pl.ds(h * dq, dq)]
        kx = jnp.concatenate([k, kf_ref[h]], axis=1)
        return _dot(kx, qs_ref[h])

    def softmax_pv(h, j, t, diagonal):
        if diagonal:
            krow = lax.broadcasted_iota(jnp.int32, (tk, 2 * tq), 0)
            qcol = lax.broadcasted_iota(jnp.int32, (tk, 2 * tq), 1)
            qcol = jnp.where(qcol >= tq, qcol - tq, qcol)
            t = jnp.where(krow <= qcol, t, NEG_BIG)
        soff = cs_tot[h] * ((j - i) * tk).astype(F32)
        m = m_ref[h]
        m_new = jnp.maximum(m, jnp.max(t, axis=0, keepdims=True) + soff)
        alpha = jnp.exp2(m - m_new)
        p = jnp.exp2(t - (m_new - soff)).astype(BF16)
        vx = jnp.concatenate([vt_ref[0, j, h], ones_blk], axis=0)
        acc_ref[h] = alpha * acc_ref[h] + _dot(vx, p)
        m_ref[h] = m_new

    def run(units):
        ahead = 2
        pending = [scores(*u[:2]) for u in units[:ahead]]
        for n, (h, j, diagonal) in enumerate(units):
            if n + ahead < len(units):
                pending.append(scores(*units[n + ahead][:2]))
            softmax_pv(h, j, pending.pop(0), diagonal)

    def tile_units(j, diagonal):
        return [(h, j, diagonal) for h in range(ATTN_HEADS)]

    def body(jj, carry):
        run(tile_units(2 * jj, False) + tile_units(2 * jj + 1, False))
        return carry

    lax.fori_loop(0, i // 2, body, 0)

    @pl.when(i % 2 == 0)
    def _():
        run(tile_units(i, True))

    @pl.when(i % 2 == 1)
    def _():
        run(tile_units(i - 1, False) + tile_units(i, True))

    lv = lam_ref[...]
    d1 = jnp.sum(lv[0:1] * lv[1:2], axis=-1, keepdims=True)
    d2 = jnp.sum(lv[2:3] * lv[3:4], axis=-1, keepdims=True)
    lam = jnp.exp(d1) - jnp.exp(d2) + lam_init
    for h in range(ATTN_HEADS):
        l = acc_ref[h, pl.ds(dv, 1), :]
        rl = 1.0 / l
        out = (acc_ref[h, pl.ds(0, dv), pl.ds(0, tq)] * rl[:, :tq]
               - lam * (acc_ref[h, pl.ds(0, dv), pl.ds(tq, tq)] * rl[:, tq:]))
        ms = jnp.mean(out * out, axis=0, keepdims=True)
        on = out * lax.rsqrt(ms + EPS)
        o_ref[0, :, pl.ds(h * dv, dv)] = (on.T * gain_ref[...] * (1.0 - lam_init)).astype(BF16)

    wo_out[...] = wo_ref[...].astype(BF16)
    wa_out[...] = wa_ref[...].astype(BF16)
    wm_out[...] = wm_ref[...].astype(BF16)


def _attn(z3, vt5, cs_terms, lamv, gain, w_out, w_abr, w_mbr, lam_init, tq=256):
    b, s, _ = z3.shape
    nq = s // tq
    steps = b * nq
    cast_specs = [pl.BlockSpec((w.shape[0] // steps, w.shape[1]), lambda bi, i: (bi * nq + i, 0))
                  for w in (w_out, w_abr, w_mbr)]
    kern = functools.partial(_attn_kernel, tq=tq, lam_init=lam_init)
    width = ATTN_HEADS * 2 * ATTN_QK_DIM
    return pl.pallas_call(
        kern,
        grid=(b, nq),
        in_specs=[
            pl.BlockSpec(memory_space=pltpu.SMEM),
            pl.BlockSpec((SUBLANES, LANES), lambda bi, i: (0, 0)),
            pl.BlockSpec((1, tq, width), lambda bi, i: (bi, i, COL_AQ // width)),
            pl.BlockSpec((1, s, width), lambda bi, i: (bi, 0, COL_AK // width)),
            pl.BlockSpec((1, nq, ATTN_HEADS, ATTN_V_DIM, tq), lambda bi, i: (bi, 0, 0, 0, 0)),
            pl.BlockSpec((1, ATTN_V_DIM), lambda bi, i: (0, 0)),
        ] + cast_specs,
        out_specs=[pl.BlockSpec((1, tq, ATTN_WIDTH), lambda bi, i: (bi, i, 0))] + cast_specs,
        out_shape=[jax.ShapeDtypeStruct((b, s, ATTN_WIDTH), BF16)]
        + [jax.ShapeDtypeStruct(w.shape, BF16) for w in (w_out, w_abr, w_mbr)],
        scratch_shapes=[
            pltpu.VMEM((ATTN_HEADS, 2 * ATTN_QK_DIM + LANES, 2 * tq), BF16),
            pltpu.VMEM((ATTN_HEADS, tq, LANES), BF16),
            pltpu.VMEM((ATTN_HEADS, ATTN_V_DIM + 2 * SUBLANES, 2 * tq), F32),
            pltpu.VMEM((ATTN_HEADS, 1, 2 * tq), F32),
        ],
        compiler_params=pltpu.CompilerParams(
            dimension_semantics=("parallel", "arbitrary"),
            vmem_limit_bytes=VMEM_LIMIT),
        name="attn",
    )(cs_terms, lamv, z3, z3, vt5, gain, w_out, w_abr, w_mbr)


def _split3(x):
    hi = x.astype(BF16)
    r = x - hi.astype(F32)
    mid = r.astype(BF16)
    lo = (r - mid.astype(F32)).astype(BF16)
    return hi, mid, lo


def _mlstm_kernel(uq_ref, uk_ref, v_ref, mo_ref, gcol_ref, grow_ref, bcol_ref, brow_ref,
                  cw_ref, cb_ref, gn_ref, o_ref, extq, extk, dstq, dstk, c_scr, n_scr, m_scr,
                  *, chunk):
    L = chunk
    nh = MLSTM_HEADS
    dk = MLSTM_QK_DIM
    dv = MLSTM_V_DIM
    heads = range(nh)

    @pl.when(pl.program_id(1) == 0)
    def _():
        extq[:, pl.ds(0, SUBLANES), :] = jnp.zeros((nh, SUBLANES, dk), F32)
        extk[:, pl.ds(0, SUBLANES), :] = jnp.zeros((nh, SUBLANES, dk), F32)
        c_scr[...] = jnp.zeros_like(c_scr)
        n_scr[...] = jnp.zeros_like(n_scr)
        m_scr[...] = jnp.zeros_like(m_scr)

    def conv_silu(u_ref, ext, dst, col0, scale):
        n8 = L // SUBLANES
        first = SUBLANES - (CONV_WIDTH - 1)
        outs = []
        for h in heads:
            cols = pl.ds(col0 + h * dk, dk)
            ext[h, pl.ds(SUBLANES, L), :] = u_ref[0, :, pl.ds(h * dk, dk)].astype(F32)
            w = cw_ref[:, cols]
            bias = cb_ref[:, cols]
            rows = [ext[h, pl.ds(first + s, n8, stride=SUBLANES), :]
                    for s in range(SUBLANES + CONV_WIDTH - 1)]
            for r in range(SUBLANES):
                y = bias
                for tap in range(CONV_WIDTH):
                    y = y + rows[r + tap] * w[tap:tap + 1]
                y = y * _sigmoid(y)
                dst[h, pl.ds(r, n8, stride=SUBLANES), :] = y if scale is None else y * scale
            ext[h, pl.ds(0, SUBLANES), :] = ext[h, pl.ds(L, SUBLANES), :]
            outs.append(dst[h])
        return outs

    q = conv_silu(uq_ref, extq, dstq, 0, None)
    k = conv_silu(uk_ref, extk, dstk, MLSTM_QK_WIDTH, dk ** -0.5)
    qb = [x.astype(BF16) for x in q]
    kb = [x.astype(BF16) for x in k]
    vb = [v_ref[0, :, pl.ds(h * dv, dv)] for h in heads]

    g_c = gcol_ref[...] + bcol_ref[...]
    g_r = grow_ref[0] + brow_ref[:, 0:1]
    lf_c = _log_sigmoid(g_c)
    lf_r = _log_sigmoid(g_r)

    r_i = lax.broadcasted_iota(jnp.int32, (L, L), 0)
    c_i = lax.broadcasted_iota(jnp.int32, (L, L), 1)
    causal = c_i <= r_i
    tri = jnp.where(causal, 1.0, 0.0).astype(BF16)
    tri_t = jnp.where(r_i <= c_i, 1.0, 0.0).astype(BF16)
    hi, mid, lo = _split3(lf_c)
    b_c = _dot(tri, hi) + _dot(tri, mid) + _dot(tri, lo)
    hi, mid, lo = _split3(lf_r)
    b_r = _dot(hi, tri_t) + _dot(mid, tri_t) + _dot(lo, tri_t)
    g_sum = jnp.sum(lf_r, axis=1, keepdims=True)

    lane = lax.broadcasted_iota(jnp.int32, (L, LANES), 1)

    def col(x, idx):
        return jnp.sum(jnp.where(lane == idx, x, 0.0), axis=1, keepdims=True)

    i_col = [col(g_c, h) for h in heads]
    b_col = [col(b_c, nh + h) for h in heads]
    i_row = [g_r[h:h + 1, :] for h in heads]
    b_row = [b_r[nh + h:nh + h + 1, :] for h in heads]
    g_tot = [g_sum[nh + h:nh + h + 1, :] for h in heads]

    c_prev = [c_scr[h] for h in heads]
    n_prev = [n_scr[h] for h in heads]
    m_prev = [m_scr[h][:, 0:1] for h in heads]

    s_qk = [_dot_nt(qb[h], kb[h]) for h in heads]
    inter = [_dot(qb[h], c_prev[h].astype(BF16)) for h in heads]
    m_loc = [jnp.max(g_tot[h] - b_row[h] + i_row[h], axis=1, keepdims=True) for h in heads]
    kw = [k[h] * jnp.exp(g_tot[h] - b_col[h] + i_col[h] - m_loc[h]) for h in heads]
    c_loc = [_dot(kw[h].T.astype(BF16), vb[h]) for h in heads]

    for h in heads:
        d = jnp.where(causal, b_col[h] - b_row[h] + i_row[h], NEG_BIG)
        m_inter = b_col[h] + m_prev[h]
        m_j = jnp.maximum(m_inter, jnp.max(d, axis=1, keepdims=True))
        w_inter = jnp.exp(m_inter - m_j)
        p = s_qk[h] * jnp.exp(d - m_j)
        num = w_inter * inter[h] + _dot(p.astype(BF16), vb[h])
        den = (w_inter * jnp.sum(q[h] * n_prev[h], axis=1, keepdims=True)
               + jnp.sum(p, axis=1, keepdims=True))
        hval = num * (1.0 / jnp.maximum(jnp.abs(den), jnp.exp(-m_j)))
        hn = _rms(hval, gn_ref[:, pl.ds(h * dv, dv)])
        gate = _sigmoid(mo_ref[0, :, pl.ds(h * dv, dv)].astype(F32))
        o_ref[0, :, pl.ds(h * dv, dv)] = (hn * gate).astype(BF16)

    for h in heads:
        n_loc = jnp.sum(kw[h], axis=0, keepdims=True)
        m_new = jnp.maximum(g_tot[h] + m_prev[h], m_loc[h])
        a_old = jnp.exp(g_tot[h] + m_prev[h] - m_new)
        a_new = jnp.exp(m_loc[h] - m_new)
        c_scr[h] = a_old * c_prev[h] + a_new * c_loc[h]
        n_scr[h] = a_old * n_prev[h] + a_new * n_loc
        m_scr[h] = jnp.broadcast_to(m_new, (1, LANES))


def _mlstm(z3, gates, gates_t, bias_col, bias_row, conv_w8, conv_b, gnorm, chunk=256):
    b, s, _ = z3.shape
    nc = s // chunk
    kern = functools.partial(_mlstm_kernel, chunk=chunk)
    qw = MLSTM_QK_WIDTH
    vw = MLSTM_WIDTH
    return pl.pallas_call(
        kern,
        grid=(b, nc),
        in_specs=[
            pl.BlockSpec((1, chunk, qw), lambda bi, c: (bi, c, COL_MQ // qw)),
            pl.BlockSpec((1, chunk, qw), lambda bi, c: (bi, c, COL_MK // qw)),
            pl.BlockSpec((1, chunk, vw), lambda bi, c: (bi, c, COL_MV // vw)),
            pl.BlockSpec((1, chunk, vw), lambda bi, c: (bi, c, COL_MO // vw)),
            pl.BlockSpec((chunk, LANES), lambda bi, c: (bi * nc + c, 0)),
            pl.BlockSpec((1, SUBLANES, chunk), lambda bi, c: (bi, 0, c)),
            pl.BlockSpec((1, LANES), lambda bi, c: (0, 0)),
            pl.BlockSpec((SUBLANES, LANES), lambda bi, c: (0, 0)),
            pl.BlockSpec((SUBLANES, 2 * qw), lambda bi, c: (0, 0)),
            pl.BlockSpec((1, 2 * qw), lambda bi, c: (0, 0)),
            pl.BlockSpec((1, vw), lambda bi, c: (0, 0)),
        ],
        out_specs=pl.BlockSpec((1, chunk, vw), lambda bi, c: (bi, c, 0)),
        out_shape=jax.ShapeDtypeStruct((b, s, vw), BF16),
        scratch_shapes=[
            pltpu.VMEM((MLSTM_HEADS, chunk + 2 * SUBLANES, MLSTM_QK_DIM), F32),
            pltpu.VMEM((MLSTM_HEADS, chunk + 2 * SUBLANES, MLSTM_QK_DIM), F32),
            pltpu.VMEM((MLSTM_HEADS, chunk, MLSTM_QK_DIM), F32),
            pltpu.VMEM((MLSTM_HEADS, chunk, MLSTM_QK_DIM), F32),
            pltpu.VMEM((MLSTM_HEADS, MLSTM_QK_DIM, MLSTM_V_DIM), F32),
            pltpu.VMEM((MLSTM_HEADS, 1, MLSTM_QK_DIM), F32),
            pltpu.VMEM((MLSTM_HEADS, 1, LANES), F32),
        ],
        compiler_params=pltpu.CompilerParams(
            dimension_semantics=("parallel", "arbitrary"),
            vmem_limit_bytes=VMEM_LIMIT),
        name="mlstm",
    )(z3, z3, z3, z3, gates, gates_t, bias_col, bias_row, conv_w8, conv_b, gnorm)


def _cross_delta(x, g_ref, wq_ref, kv_ref, wo_ref):
    hc = _rms(x, g_ref[...]).astype(BF16)
    cq = (_dot(hc, wq_ref[...]) * (CROSS_HEAD_DIM ** -0.5)).astype(BF16)
    outs = []
    for hh in range(CROSS_HEADS):
        lo = hh * CROSS_HEAD_DIM
        qh = cq[:, lo:lo + CROSS_HEAD_DIM]
        kh = kv_ref[0, :, lo:lo + CROSS_HEAD_DIM]
        vh = kv_ref[0, :, CROSS_WIDTH + lo:CROSS_WIDTH + lo + CROSS_HEAD_DIM]
        s = _dot_nt(qh, kh)
        p = jnp.exp(s - jnp.max(s, axis=-1, keepdims=True))
        l = jnp.sum(p, axis=-1, keepdims=True)
        outs.append((_dot(p.astype(BF16), vh) * (1.0 / l)).astype(BF16))
    co = jnp.concatenate(outs, axis=1)
    return _dot(co, wo_ref[...])


def _merge_kernel(att_ref, hm_ref, ga_ref, gm_ref, x_ref, wa_ref, wm_ref, wo_ref,
                  gc_ref, wcq_ref, kv_ref, wco_ref, o_ref):
    j = pl.program_id(1)

    @pl.when(j == 0)
    def _():
        o_ref[...] = x_ref[...]

    a = _dot(att_ref[...], wa_ref[...])
    bm = _dot(hm_ref[...], wm_ref[...])
    y = _sigmoid(ga_ref[...].astype(F32)) * a + _sigmoid(gm_ref[...].astype(F32)) * bm
    o_ref[...] += _dot(y.astype(BF16), wo_ref[...])

    @pl.when(j == pl.num_programs(1) - 1)
    def _():
        x1 = o_ref[...]
        o_ref[...] = x1 + _cross_delta(x1, gc_ref, wcq_ref, kv_ref, wco_ref)


def _merge(att2d, hm2d, z2d, x2d, wa, wm, wo, gc, wcq, ckv, wco, seq, tm=512, tn=1024):
    t = x2d.shape[0]
    nj = D_MODEL // tn
    n_mem = ckv.shape[1]
    per_batch = seq // tm
    return pl.pallas_call(
        _merge_kernel,
        grid=(t // tm, nj),
        in_specs=[
            pl.BlockSpec((tm, ATTN_WIDTH), lambda i, j: (i, 0)),
            pl.BlockSpec((tm, MLSTM_WIDTH), lambda i, j: (i, 0)),
            pl.BlockSpec((tm, tn), lambda i, j: (i, COL_GA // tn + j)),
            pl.BlockSpec((tm, tn), lambda i, j: (i, COL_GM // tn + j)),
            pl.BlockSpec((tm, D_MODEL), lambda i, j: (i, 0)),
            pl.BlockSpec((ATTN_WIDTH, tn), lambda i, j: (0, j)),
            pl.BlockSpec((MLSTM_WIDTH, tn), lambda i, j: (0, j)),
            pl.BlockSpec((tn, D_MODEL), lambda i, j: (j, 0)),
            pl.BlockSpec((1, D_MODEL), lambda i, j: (0, 0)),
            pl.BlockSpec((D_MODEL, CROSS_WIDTH), lambda i, j: (0, 0)),
            pl.BlockSpec((1, n_mem, 2 * CROSS_WIDTH), lambda i, j: (i // per_batch, 0, 0)),
            pl.BlockSpec((CROSS_WIDTH, D_MODEL), lambda i, j: (0, 0)),
        ],
        out_specs=pl.BlockSpec((tm, D_MODEL), lambda i, j: (i, 0)),
        out_shape=jax.ShapeDtypeStruct((t, D_MODEL), F32),
        compiler_params=pltpu.CompilerParams(
            dimension_semantics=("parallel", "arbitrary"), vmem_limit_bytes=VMEM_LIMIT),
        name="merge",
    )(att2d, hm2d, z2d, z2d, x2d, wa, wm, wo, gc, wcq, ckv, wco)


def _mlp_kernel(x_ref, g_ref, wu_ref, wd_ref, gf_ref, o_ref, h_scr, acc, *, final_norm):
    j = pl.program_id(1)

    @pl.when(j == 0)
    def _():
        x = x_ref[...]
        h_scr[...] = _rms(x, g_ref[...]).astype(BF16)
        acc[...] = x

    u = jnp.square(jnp.maximum(_dot(h_scr[...], wu_ref[...]), 0.0)).astype(BF16)
    acc[...] += _dot(u, wd_ref[...])

    @pl.when(j == pl.num_programs(1) - 1)
    def _():
        if final_norm:
            o_ref[...] = _rms(acc[...], gf_ref[...])
        else:
            o_ref[...] = acc[...]


def _mlp(x2d, g, wu, wd, gf, final_norm, tm=512, tf=1024):
    t = x2d.shape[0]
    kern = functools.partial(_mlp_kernel, final_norm=final_norm)
    return pl.pallas_call(
        kern,
        grid=(t // tm, D_FF // tf),
        in_specs=[
            pl.BlockSpec((tm, D_MODEL), lambda i, j: (i, 0)),
            pl.BlockSpec((1, D_MODEL), lambda i, j: (0, 0)),
            pl.BlockSpec((D_MODEL, tf), lambda i, j: (0, j)),
            pl.BlockSpec((tf, D_MODEL), lambda i, j: (j, 0)),
            pl.BlockSpec((1, D_MODEL), lambda i, j: (0, 0)),
        ],
        out_specs=pl.BlockSpec((tm, D_MODEL), lambda i, j: (i, 0)),
        out_shape=jax.ShapeDtypeStruct((t, D_MODEL), F32),
        scratch_shapes=[pltpu.VMEM((tm, D_MODEL), BF16), pltpu.VMEM((tm, D_MODEL), F32)],
        compiler_params=pltpu.CompilerParams(
            dimension_semantics=("parallel", "arbitrary"), vmem_limit_bytes=VMEM_LIMIT),
        name="mlp",
    )(x2d, g, wu, wd, gf)


def _alibi_terms():
    slopes = 2.0 ** (-8.0 * np.arange(1, ATTN_HEADS + 1, dtype=np.float64) / ATTN_HEADS)
    rem = slopes * LOG2E
    terms = []
    for _ in range(ALIBI_TERMS):
        t = rem.astype(np.float32).astype(ml_dtypes.bfloat16).astype(np.float64)
        terms.append(t)
        rem = rem - t
    return np.stack(terms, axis=1).reshape(-1).astype(np.float32)


def _pad_rows(a, rows):
    return jnp.pad(a, ((0, rows - a.shape[0]), (0, 0)))


def kernel(x, mem, norm_mix, w_in, b_igate, b_fgate, conv_w, conv_b, lam_q1, lam_k1, lam_q2, lam_k2, attn_norm, mlstm_norm, w_attn_br, w_mlstm_br, w_out, norm_cross, norm_mem, w_cq, w_ckv, w_co, norm_mlp, w_up, w_down, norm_final):
    b, s, _ = x.shape
    t = b * s
    tq = 256
    cs_terms = jnp.asarray(_alibi_terms())
    x2d = x.reshape(t, D_MODEL)
    for l in range(DEPTH):
        lam_init = 0.8 - 0.6 * math.exp(-0.3 * l)
        w = w_in[l]
        w_a, w_b, w_gate = _wprep(w.T)

        z2d, gates, vt, w_up_b, w_down_b = _inproj(x2d, norm_mix[l][None, :], w_a, w_b, w_gate,
                                                   w_up[l], w_down[l], tq)
        z3 = z2d.reshape(b, s, IN_MAIN)

        vt5 = vt.reshape(b, s // tq, ATTN_HEADS, ATTN_V_DIM, tq)
        lamv = _pad_rows(jnp.pad(jnp.stack([lam_q1[l], lam_k1[l], lam_q2[l], lam_k2[l]]),
                                 ((0, 0), (0, LANES - ATTN_QK_DIM))), SUBLANES)
        att, w_out_b, w_abr_b, w_mbr_b = _attn(z3, vt5, cs_terms, lamv, attn_norm[l][None, :],
                                               w_out[l], w_attn_br[l], w_mlstm_br[l], lam_init, tq=tq)

        gates_t = gates[:, :SUBLANES].reshape(b, s, SUBLANES).transpose(0, 2, 1)
        gate_bias = jnp.concatenate([b_igate[l], b_fgate[l]])
        bias_col = jnp.pad(gate_bias, (0, LANES - gate_bias.shape[0]))[None, :]
        bias_row = jnp.broadcast_to(gate_bias[:, None], (SUBLANES, LANES))
        hm = _mlstm(z3, gates, gates_t, bias_col, bias_row, _pad_rows(conv_w[l], SUBLANES),
                    conv_b[l][None, :], mlstm_norm[l][None, :])

        ckv = _memkv(mem, norm_mem[l][None, :], w_ckv[l])
        x2d = _merge(att.reshape(t, ATTN_WIDTH), hm.reshape(t, MLSTM_WIDTH), z2d, x2d,
                     w_abr_b, w_mbr_b, w_out_b, norm_cross[l][None, :], w_cq[l].astype(BF16), ckv,
                     w_co[l].astype(BF16), s)

        x2d = _mlp(x2d, norm_mlp[l][None, :], w_up_b, w_down_b,
                   norm_final[None, :], final_norm=(l == DEPTH - 1))
    return x2d.reshape(b, s, D_MODEL)
```

```python
import functools
import math

import ml_dtypes
import numpy as np
import jax
import jax.numpy as jnp
from jax import lax
from jax.experimental import pallas as pl
from jax.experimental.pallas import tpu as pltpu

F32 = jnp.float32
BF16 = jnp.bfloat16

D_MODEL = 2048
DEPTH = 1
ATTN_HEADS = 8
ATTN_QK_DIM = 64
ATTN_V_DIM = 128
ATTN_WIDTH = ATTN_HEADS * ATTN_V_DIM
MLSTM_HEADS = 4
MLSTM_QK_DIM = 128
MLSTM_V_DIM = 256
MLSTM_QK_WIDTH = MLSTM_HEADS * MLSTM_QK_DIM
MLSTM_WIDTH = MLSTM_HEADS * MLSTM_V_DIM
CONV_WIDTH = 4
CROSS_HEADS = 4
CROSS_HEAD_DIM = 128
CROSS_WIDTH = CROSS_HEADS * CROSS_HEAD_DIM
D_FF = 4 * D_MODEL
EPS = 1e-6
LANES = 128
SUBLANES = 8
NEG_BIG = -1e30
LOG2E = 1.4426950408889634
ALIBI_TERMS = 3

COL_AQ = 0
COL_AK = 1024
COL_MQ = 2048
COL_MK = 2560
COL_MV = 3072
COL_MO = 4096
COL_GA = 5120
COL_GM = 7168
IN_MAIN = 9216
W_AQ_LO, W_AQ_HI = 0, 1024
W_AV_LO = 2048
GATE_LO = 6144
GATE_HI = 6152

VMEM_LIMIT = 56 * 1024 * 1024


def _rms(x, g):
    ms = jnp.mean(x * x, axis=-1, keepdims=True)
    return x * lax.rsqrt(ms + EPS) * g


def _sigmoid(x):
    return 0.5 * jnp.tanh(0.5 * x) + 0.5


def _log_sigmoid(x):
    return jnp.minimum(x, 0.0) - jnp.log(1.0 + jnp.exp(-jnp.abs(x)))


def _dot(a, b):
    return jnp.dot(a, b, preferred_element_type=F32)


def _dot_nt(a, b):
    return lax.dot_general(a, b, (((1,), (1,)), ((), ())), preferred_element_type=F32)


def _memkv_kernel(mem_ref, g_ref, w_ref, o_ref):
    mn = _rms(mem_ref[0], g_ref[...]).astype(BF16)
    o_ref[0] = _dot(mn, w_ref[...].astype(BF16)).astype(BF16)


def _memkv(mem, g, w_ckv):
    b, n_mem, _ = mem.shape
    return pl.pallas_call(
        _memkv_kernel,
        grid=(b,),
        in_specs=[
            pl.BlockSpec((1, n_mem, D_MODEL), lambda i: (i, 0, 0)),
            pl.BlockSpec((1, D_MODEL), lambda i: (0, 0)),
            pl.BlockSpec((D_MODEL, 2 * CROSS_WIDTH), lambda i: (0, 0)),
        ],
        out_specs=pl.BlockSpec((1, n_mem, 2 * CROSS_WIDTH), lambda i: (i, 0, 0)),
        out_shape=jax.ShapeDtypeStruct((b, n_mem, 2 * CROSS_WIDTH), BF16),
        compiler_params=pltpu.CompilerParams(
            dimension_semantics=("arbitrary",), vmem_limit_bytes=VMEM_LIMIT),
        name="memkv",
    )(mem, g, w_ckv)


def _wprep_kernel(a_ref, nxt_ref, wa_out, wb_out, wg_out, *, na):
    k = pl.program_id(0)
    ng = GATE_HI - GATE_LO

    @pl.when(k < na)
    def _():
        scale = jnp.where(k == W_AQ_LO // a_ref.shape[0], ATTN_QK_DIM ** -0.5 * LOG2E, 1.0)
        wa_out[...] = (a_ref[...] * scale).astype(BF16)

    @pl.when(k >= na)
    def _():
        wb_out[...] = jnp.concatenate([a_ref[pl.ds(ng, a_ref.shape[0] - ng), :], nxt_ref[0]],
                                      axis=0).astype(BF16)

    @pl.when(k == na)
    def _():
        pad = jnp.zeros((LANES - ng, D_MODEL), F32)
        wg_out[...] = jnp.concatenate([a_ref[pl.ds(0, ng), :], pad], axis=0).astype(BF16)


def _wprep(wt, tn=1024):
    rows = wt.shape[0]
    ng = GATE_HI - GATE_LO
    assert ng == SUBLANES and GATE_LO % tn == 0 and (rows - GATE_HI) % tn == 0
    assert W_AQ_LO == 0 and W_AQ_HI == tn
    na = GATE_LO // tn
    nb = (rows - GATE_HI) // tn
    groups = wt.reshape(rows // ng, ng, D_MODEL)
    per_block = tn // ng
    return pl.pallas_call(
        functools.partial(_wprep_kernel, na=na),
        grid=(na + nb,),
        in_specs=[
            pl.BlockSpec((tn, D_MODEL), lambda k: (k, 0)),
            pl.BlockSpec((1, ng, D_MODEL), lambda k: ((k + 1) * per_block, 0, 0)),
        ],
        out_specs=[
            pl.BlockSpec((tn, D_MODEL), lambda k: (jnp.minimum(k, na - 1), 0)),
            pl.BlockSpec((tn, D_MODEL), lambda k: (jnp.maximum(k - na, 0), 0)),
            pl.BlockSpec((LANES, D_MODEL), lambda k: (0, 0)),
        ],
        out_shape=[
            jax.ShapeDtypeStruct((GATE_LO, D_MODEL), BF16),
            jax.ShapeDtypeStruct((rows - GATE_HI, D_MODEL), BF16),
            jax.ShapeDtypeStruct((LANES, D_MODEL), BF16),
        ],
        compiler_params=pltpu.CompilerParams(
            dimension_semantics=("arbitrary",), vmem_limit_bytes=VMEM_LIMIT),
        name="wprep",
    )(wt, groups)


def _inproj_kernel(x_ref, g_ref, wa_ref, wb_ref, wg_ref, z_ref, gate_ref, vt_ref, h_scr, *, na, jv):
    j = pl.program_id(1)

    @pl.when(j == 0)
    def _():
        h = _rms(x_ref[...], g_ref[...]).astype(BF16)
        h_scr[...] = h
        gate_ref[...] = _dot_nt(h, wg_ref[...])

    @pl.when((j < na) & (j != jv))
    def _():
        z_ref[...] = _dot_nt(h_scr[...], wa_ref[...]).astype(BF16)

    @pl.when(j == jv)
    def _():
        vt = _dot_nt(wa_ref[...], h_scr[...]).astype(BF16)
        tkv = vt_ref.shape[2]
        for n in range(vt_ref.shape[0]):
            vt_ref[n] = vt[:, n * tkv:(n + 1) * tkv]

    @pl.when(j >= na)
    def _():
        z_ref[...] = _dot_nt(h_scr[...], wb_ref[...]).astype(BF16)


def _inproj(x2d, g, w_a, w_b, w_gate, tkv, tm=1024, tn=1024):
    t = x2d.shape[0]
    na = w_a.shape[0] // tn
    n = w_a.shape[0] + w_b.shape[0]
    ni, nj = t // tm, n // tn
    assert tn == ATTN_WIDTH and W_AV_LO % tn == 0
    jv = W_AV_LO // tn
    return pl.pallas_call(
        functools.partial(_inproj_kernel, na=na, jv=jv),
        grid=(ni, nj),
        in_specs=[
            pl.BlockSpec((tm, D_MODEL), lambda i, j: (i, 0)),
            pl.BlockSpec((1, D_MODEL), lambda i, j: (0, 0)),
            pl.BlockSpec((tn, D_MODEL), lambda i, j: (jnp.minimum(j, na - 1), 0)),
            pl.BlockSpec((tn, D_MODEL), lambda i, j: (jnp.maximum(j - na, 0), 0)),
            pl.BlockSpec((LANES, D_MODEL), lambda i, j: (0, 0)),
        ],
        out_specs=[
            pl.BlockSpec((tm, tn), lambda i, j: (i, j - (j >= jv).astype(jnp.int32))),
            pl.BlockSpec((tm, LANES), lambda i, j: (i, 0)),
            pl.BlockSpec((tm // tkv, ATTN_WIDTH, tkv), lambda i, j: (i, 0, 0)),
        ],
        out_shape=[
            jax.ShapeDtypeStruct((t, n - tn), BF16),
            jax.ShapeDtypeStruct((t, LANES), F32),
            jax.ShapeDtypeStruct((t // tkv, ATTN_WIDTH, tkv), BF16),
        ],
        scratch_shapes=[pltpu.VMEM((tm, D_MODEL), BF16)],
        compiler_params=pltpu.CompilerParams(
            dimension_semantics=("arbitrary", "arbitrary"), vmem_limit_bytes=VMEM_LIMIT),
        name="inproj",
    )(x2d, g, w_a, w_b, w_gate)


def _attn_kernel(cs_ref, lam_ref, q_ref, k_ref, vt_ref, gain_ref, *refs, ncast, tq, lam_init):
    cast_in = refs[:ncast]
    o_ref = refs[ncast]
    cast_out = refs[ncast + 1:2 * ncast + 1]
    qs_ref, kf_ref, acc_ref, m_ref = refs[2 * ncast + 1:]
    tk = tq
    nfeat = ALIBI_TERMS
    dq = 2 * ATTN_QK_DIM
    dv = ATTN_V_DIM
    i = pl.program_id(1)
    cs = [[cs_ref[nfeat * h + t] for t in range(nfeat)] for h in range(ATTN_HEADS)]
    cs_tot = [sum(c[1:], c[0]) for c in cs]

    @pl.when(i == 0)
    def _():
        klane = lax.broadcasted_iota(jnp.int32, (tk, LANES), 1)
        krow = lax.broadcasted_iota(jnp.int32, (tk, LANES), 0).astype(F32)
        frow = lax.broadcasted_iota(jnp.int32, (LANES, 2 * tq), 0)
        fcol = lax.broadcasted_iota(jnp.int32, (LANES, 2 * tq), 1)
        fcol = jnp.where(fcol >= tq, fcol - tq, fcol).astype(F32)
        for h in range(ATTN_HEADS):
            kf = jnp.where(klane < nfeat, krow, 0.0)
            qf = jnp.where((frow >= nfeat) & (frow < 2 * nfeat), -fcol, 0.0)
            for t in range(nfeat):
                kf = jnp.where(klane == nfeat + t, cs[h][t], kf)
                qf = jnp.where(frow == t, cs[h][t], qf)
            kf_ref[h] = kf.astype(BF16)
            qs_ref[h, pl.ds(dq, LANES), :] = qf.astype(BF16)

    row = lax.broadcasted_iota(jnp.int32, (dq, tq), 0)
    for h in range(ATTN_HEADS):
        qt = q_ref[0, :, pl.ds(h * dq, dq)].astype(F32).T
        qs_ref[h, pl.ds(0, dq), pl.ds(0, tq)] = jnp.where(row < ATTN_QK_DIM, qt, 0.0).astype(BF16)
        qs_ref[h, pl.ds(0, dq), pl.ds(tq, tq)] = jnp.where(row >= ATTN_QK_DIM, qt, 0.0).astype(BF16)
    acc_ref[...] = jnp.zeros_like(acc_ref)
    m_ref[...] = jnp.full_like(m_ref, NEG_BIG)

    orow = lax.broadcasted_iota(jnp.int32, (2 * SUBLANES, tk), 0)
    ones_blk = jnp.where(orow == 0, 1.0, 0.0).astype(BF16)

    def scores(h, j):
        k = k_ref[0, pl.ds(pl.multiple_of(j * tk, tk), tk), pl.ds(h * dq, dq)]
        kx = jnp.concatenate([k, kf_ref[h]], axis=1)
        return _dot(kx, qs_ref[h])

    def softmax_pv(h, j, t, diagonal):
        if diagonal:
            krow = lax.broadcasted_iota(jnp.int32, (tk, 2 * tq), 0)
            qcol = lax.broadcasted_iota(jnp.int32, (tk, 2 * tq), 1)
            qcol = jnp.where(qcol >= tq, qcol - tq, qcol)
            t = jnp.where(krow <= qcol, t, NEG_BIG)
        soff = cs_tot[h] * ((j - i) * tk).astype(F32)
        m = m_ref[h]
        m_new = jnp.maximum(m, jnp.max(t, axis=0, keepdims=True) + soff)
        alpha = jnp.exp2(m - m_new)
        p = jnp.exp2(t - (m_new - soff)).astype(BF16)
        vx = jnp.concatenate([vt_ref[0, j, h], ones_blk], axis=0)
        acc_ref[h] = alpha * acc_ref[h] + _dot(vx, p)
        m_ref[h] = m_new

    def run(units):
        ahead = 2
        pending = [scores(*u[:2]) for u in units[:ahead]]
        for n, (h, j, diagonal) in enumerate(units):
            if n + ahead < len(units):
                pending.append(scores(*units[n + ahead][:2]))
            softmax_pv(h, j, pending.pop(0), diagonal)

    def tile_units(j, diagonal):
        return [(h, j, diagonal) for h in range(ATTN_HEADS)]

    def body(jj, carry):
        run(tile_units(2 * jj, False) + tile_units(2 * jj + 1, False))
        return carry

    lax.fori_loop(0, i // 2, body, 0)

    @pl.when(i % 2 == 0)
    def _():
        run(tile_units(i, True))

    @pl.when(i % 2 == 1)
    def _():
        run(tile_units(i - 1, False) + tile_units(i, True))

    lv = lam_ref[...]
    d1 = jnp.sum(lv[0:1] * lv[1:2], axis=-1, keepdims=True)
    d2 = jnp.sum(lv[2:3] * lv[3:4], axis=-1, keepdims=True)
    lam = jnp.exp(d1) - jnp.exp(d2) + lam_init
    for h in range(ATTN_HEADS):
        l = acc_ref[h, pl.ds(dv, 1), :]
        rl = 1.0 / l
        out = (acc_ref[h, pl.ds(0, dv), pl.ds(0, tq)] * rl[:, :tq]
               - lam * (acc_ref[h, pl.ds(0, dv), pl.ds(tq, tq)] * rl[:, tq:]))
        ms = jnp.mean(out * out, axis=0, keepdims=True)
        on = out * lax.rsqrt(ms + EPS)
        o_ref[0, :, pl.ds(h * dv, dv)] = (on.T * gain_ref[...] * (1.0 - lam_init)).astype(BF16)

    for src, dst in zip(cast_in, cast_out):
        dst[...] = src[...].astype(BF16)


def _attn(z3, vt5, cs_terms, lamv, gain, cast_weights, lam_init, tq=256):
    b, s, _ = z3.shape
    nq = s // tq
    steps = b * nq
    cast_specs = [pl.BlockSpec((w.shape[0] // steps, w.shape[1]), lambda bi, i: (bi * nq + i, 0))
                  for w in cast_weights]
    kern = functools.partial(_attn_kernel, ncast=len(cast_weights), tq=tq, lam_init=lam_init)
    width = ATTN_HEADS * 2 * ATTN_QK_DIM
    return pl.pallas_call(
        kern,
        grid=(b, nq),
        in_specs=[
            pl.BlockSpec(memory_space=pltpu.SMEM),
            pl.BlockSpec((SUBLANES, LANES), lambda bi, i: (0, 0)),
            pl.BlockSpec((1, tq, width), lambda bi, i: (bi, i, COL_AQ // width)),
            pl.BlockSpec((1, s, width), lambda bi, i: (bi, 0, COL_AK // width)),
            pl.BlockSpec((1, nq, ATTN_HEADS, ATTN_V_DIM, tq), lambda bi, i: (bi, 0, 0, 0, 0)),
            pl.BlockSpec((1, ATTN_V_DIM), lambda bi, i: (0, 0)),
        ] + cast_specs,
        out_specs=[pl.BlockSpec((1, tq, ATTN_WIDTH), lambda bi, i: (bi, i, 0))] + cast_specs,
        out_shape=[jax.ShapeDtypeStruct((b, s, ATTN_WIDTH), BF16)]
        + [jax.ShapeDtypeStruct(w.shape, BF16) for w in cast_weights],
        scratch_shapes=[
            pltpu.VMEM((ATTN_HEADS, 2 * ATTN_QK_DIM + LANES, 2 * tq), BF16),
            pltpu.VMEM((ATTN_HEADS, tq, LANES), BF16),
            pltpu.VMEM((ATTN_HEADS, ATTN_V_DIM + 2 * SUBLANES, 2 * tq), F32),
            pltpu.VMEM((ATTN_HEADS, 1, 2 * tq), F32),
        ],
        compiler_params=pltpu.CompilerParams(
            dimension_semantics=("parallel", "arbitrary"),
            vmem_limit_bytes=VMEM_LIMIT),
        name="attn",
    )(cs_terms, lamv, z3, z3, vt5, gain, *cast_weights)


def _split3(x):
    hi = x.astype(BF16)
    r = x - hi.astype(F32)
    mid = r.astype(BF16)
    lo = (r - mid.astype(F32)).astype(BF16)
    return hi, mid, lo


def _mlstm_kernel(uq_ref, uk_ref, v_ref, mo_ref, gcol_ref, grow_ref, bcol_ref, brow_ref,
                  cw_ref, cb_ref, gn_ref, o_ref, extq, extk, dstq, dstk, c_scr, n_scr, m_scr,
                  *, chunk):
    L = chunk
    nh = MLSTM_HEADS
    dk = MLSTM_QK_DIM
    dv = MLSTM_V_DIM
    heads = range(nh)

    @pl.when(pl.program_id(1) == 0)
    def _():
        extq[:, pl.ds(0, SUBLANES), :] = jnp.zeros((nh, SUBLANES, dk), F32)
        extk[:, pl.ds(0, SUBLANES), :] = jnp.zeros((nh, SUBLANES, dk), F32)
        c_scr[...] = jnp.zeros_like(c_scr)
        n_scr[...] = jnp.zeros_like(n_scr)
        m_scr[...] = jnp.zeros_like(m_scr)

    def conv_silu(u_ref, ext, dst, col0, scale):
        n8 = L // SUBLANES
        first = SUBLANES - (CONV_WIDTH - 1)
        outs = []
        for h in heads:
            cols = pl.ds(col0 + h * dk, dk)
            ext[h, pl.ds(SUBLANES, L), :] = u_ref[0, :, pl.ds(h * dk, dk)].astype(F32)
            w = cw_ref[:, cols]
            bias = cb_ref[:, cols]
            rows = [ext[h, pl.ds(first + s, n8, stride=SUBLANES), :]
                    for s in range(SUBLANES + CONV_WIDTH - 1)]
            for r in range(SUBLANES):
                y = bias
                for tap in range(CONV_WIDTH):
                    y = y + rows[r + tap] * w[tap:tap + 1]
                y = y * _sigmoid(y)
                dst[h, pl.ds(r, n8, stride=SUBLANES), :] = y if scale is None else y * scale
            ext[h, pl.ds(0, SUBLANES), :] = ext[h, pl.ds(L, SUBLANES), :]
            outs.append(dst[h])
        return outs

    q = conv_silu(uq_ref, extq, dstq, 0, None)
    k = conv_silu(uk_ref, extk, dstk, MLSTM_QK_WIDTH, dk ** -0.5)
    qb = [x.astype(BF16) for x in q]
    kb = [x.astype(BF16) for x in k]
    vb = [v_ref[0, :, pl.ds(h * dv, dv)] for h in heads]

    g_c = gcol_ref[...] + bcol_ref[...]
    g_r = grow_ref[0] + brow_ref[:, 0:1]
    lf_c = _log_sigmoid(g_c)
    lf_r = _log_sigmoid(g_r)

    r_i = lax.broadcasted_iota(jnp.int32, (L, L), 0)
    c_i = lax.broadcasted_iota(jnp.int32, (L, L), 1)
    causal = c_i <= r_i
    tri = jnp.where(causal, 1.0, 0.0).astype(BF16)
    tri_t = jnp.where(r_i <= c_i, 1.0, 0.0).astype(BF16)
    hi, mid, lo = _split3(lf_c)
    b_c = _dot(tri, hi) + _dot(tri, mid) + _dot(tri, lo)
    hi, mid, lo = _split3(lf_r)
    b_r = _dot(hi, tri_t) + _dot(mid, tri_t) + _dot(lo, tri_t)
    g_sum = jnp.sum(lf_r, axis=1, keepdims=True)

    lane = lax.broadcasted_iota(jnp.int32, (L, LANES), 1)

    def col(x, idx):
        return jnp.sum(jnp.where(lane == idx, x, 0.0), axis=1, keepdims=True)

    i_col = [col(g_c, h) for h in heads]
    b_col = [col(b_c, nh + h) for h in heads]
    i_row = [g_r[h:h + 1, :] for h in heads]
    b_row = [b_r[nh + h:nh + h + 1, :] for h in heads]
    g_tot = [g_sum[nh + h:nh + h + 1, :] for h in heads]

    c_prev = [c_scr[h] for h in heads]
    n_prev = [n_scr[h] for h in heads]
    m_prev = [m_scr[h][:, 0:1] for h in heads]

    s_qk = [_dot_nt(qb[h], kb[h]) for h in heads]
    inter = [_dot(qb[h], c_prev[h].astype(BF16)) for h in heads]
    m_loc = [jnp.max(g_tot[h] - b_row[h] + i_row[h], axis=1, keepdims=True) for h in heads]
    kw = [k[h] * jnp.exp(g_tot[h] - b_col[h] + i_col[h] - m_loc[h]) for h in heads]
    c_loc = [_dot(kw[h].T.astype(BF16), vb[h]) for h in heads]

    for h in heads:
        d = jnp.where(causal, b_col[h] - b_row[h] + i_row[h], NEG_BIG)
        m_inter = b_col[h] + m_prev[h]
        m_j = jnp.maximum(m_inter, jnp.max(d, axis=1, keepdims=True))
        w_inter = jnp.exp(m_inter - m_j)
        p = s_qk[h] * jnp.exp(d - m_j)
        num = w_inter * inter[h] + _dot(p.astype(BF16), vb[h])
        den = (w_inter * jnp.sum(q[h] * n_prev[h], axis=1, keepdims=True)
               + jnp.sum(p, axis=1, keepdims=True))
        hval = num * (1.0 / jnp.maximum(jnp.abs(den), jnp.exp(-m_j)))
        hn = _rms(hval, gn_ref[:, pl.ds(h * dv, dv)])
        gate = _sigmoid(mo_ref[0, :, pl.ds(h * dv, dv)].astype(F32))
        o_ref[0, :, pl.ds(h * dv, dv)] = (hn * gate).astype(BF16)

    for h in heads:
        n_loc = jnp.sum(kw[h], axis=0, keepdims=True)
        m_new = jnp.maximum(g_tot[h] + m_prev[h], m_loc[h])
        a_old = jnp.exp(g_tot[h] + m_prev[h] - m_new)
        a_new = jnp.exp(m_loc[h] - m_new)
        c_scr[h] = a_old * c_prev[h] + a_new * c_loc[h]
        n_scr[h] = a_old * n_prev[h] + a_new * n_loc
        m_scr[h] = jnp.broadcast_to(m_new, (1, LANES))


def _mlstm(z3, gates, gates_t, bias_col, bias_row, conv_w8, conv_b, gnorm, chunk=256):
    b, s, _ = z3.shape
    nc = s // chunk
    kern = functools.partial(_mlstm_kernel, chunk=chunk)
    qw = MLSTM_QK_WIDTH
    vw = MLSTM_WIDTH
    return pl.pallas_call(
        kern,
        grid=(b, nc),
        in_specs=[
            pl.BlockSpec((1, chunk, qw), lambda bi, c: (bi, c, COL_MQ // qw)),
            pl.BlockSpec((1, chunk, qw), lambda bi, c: (bi, c, COL_MK // qw)),
            pl.BlockSpec((1, chunk, vw), lambda bi, c: (bi, c, COL_MV // vw)),
            pl.BlockSpec((1, chunk, vw), lambda bi, c: (bi, c, COL_MO // vw)),
            pl.BlockSpec((chunk, LANES), lambda bi, c: (bi * nc + c, 0)),
            pl.BlockSpec((1, SUBLANES, chunk), lambda bi, c: (bi, 0, c)),
            pl.BlockSpec((1, LANES), lambda bi, c: (0, 0)),
            pl.BlockSpec((SUBLANES, LANES), lambda bi, c: (0, 0)),
            pl.BlockSpec((SUBLANES, 2 * qw), lambda bi, c: (0, 0)),
            pl.BlockSpec((1, 2 * qw), lambda bi, c: (0, 0)),
            pl.BlockSpec((1, vw), lambda bi, c: (0, 0)),
        ],
        out_specs=pl.BlockSpec((1, chunk, vw), lambda bi, c: (bi, c, 0)),
        out_shape=jax.ShapeDtypeStruct((b, s, vw), BF16),
        scratch_shapes=[
            pltpu.VMEM((MLSTM_HEADS, chunk + 2 * SUBLANES, MLSTM_QK_DIM), F32),
            pltpu.VMEM((MLSTM_HEADS, chunk + 2 * SUBLANES, MLSTM_QK_DIM), F32),
            pltpu.VMEM((MLSTM_HEADS, chunk, MLSTM_QK_DIM), F32),
            pltpu.VMEM((MLSTM_HEADS, chunk, MLSTM_QK_DIM), F32),
            pltpu.VMEM((MLSTM_HEADS, MLSTM_QK_DIM, MLSTM_V_DIM), F32),
            pltpu.VMEM((MLSTM_HEADS, 1, MLSTM_QK_DIM), F32),
            pltpu.VMEM((MLSTM_HEADS, 1, LANES), F32),
        ],
        compiler_params=pltpu.CompilerParams(
            dimension_semantics=("parallel", "arbitrary"),
            vmem_limit_bytes=VMEM_LIMIT),
        name="mlstm",
    )(z3, z3, z3, z3, gates, gates_t, bias_col, bias_row, conv_w8, conv_b, gnorm)


def _cross_delta(x, g_ref, wq_ref, kv_ref, wo_ref):
    hc = _rms(x, g_ref[...]).astype(BF16)
    cq = (_dot(hc, wq_ref[...]) * (CROSS_HEAD_DIM ** -0.5)).astype(BF16)
    outs = []
    for hh in range(CROSS_HEADS):
        lo = hh * CROSS_HEAD_DIM
        qh = cq[:, lo:lo + CROSS_HEAD_DIM]
        kh = kv_ref[0, :, lo:lo + CROSS_HEAD_DIM]
        vh = kv_ref[0, :, CROSS_WIDTH + lo:CROSS_WIDTH + lo + CROSS_HEAD_DIM]
        s = _dot_nt(qh, kh)
        p = jnp.exp(s - jnp.max(s, axis=-1, keepdims=True))
        l = jnp.sum(p, axis=-1, keepdims=True)
        outs.append((_dot(p.astype(BF16), vh) * (1.0 / l)).astype(BF16))
    co = jnp.concatenate(outs, axis=1)
    return _dot(co, wo_ref[...])


def _merge_kernel(att_ref, hm_ref, ga_ref, gm_ref, x_ref, wa_ref, wm_ref, wo_ref,
                  gc_ref, wcq_ref, kv_ref, wco_ref, o_ref):
    j = pl.program_id(1)

    @pl.when(j == 0)
    def _():
        o_ref[...] = x_ref[...]

    a = _dot(att_ref[...], wa_ref[...])
    bm = _dot(hm_ref[...], wm_ref[...])
    y = _sigmoid(ga_ref[...].astype(F32)) * a + _sigmoid(gm_ref[...].astype(F32)) * bm
    o_ref[...] += _dot(y.astype(BF16), wo_ref[...])

    @pl.when(j == pl.num_programs(1) - 1)
    def _():
        x1 = o_ref[...]
        o_ref[...] = x1 + _cross_delta(x1, gc_ref, wcq_ref, kv_ref, wco_ref)


def _merge(att2d, hm2d, z2d, x2d, wa, wm, wo, gc, wcq, ckv, wco, seq, tm=512, tn=1024):
    t = x2d.shape[0]
    nj = D_MODEL // tn
    n_mem = ckv.shape[1]
    per_batch = seq // tm
    return pl.pallas_call(
        _merge_kernel,
        grid=(t // tm, nj),
        in_specs=[
            pl.BlockSpec((tm, ATTN_WIDTH), lambda i, j: (i, 0)),
            pl.BlockSpec((tm, MLSTM_WIDTH), lambda i, j: (i, 0)),
            pl.BlockSpec((tm, tn), lambda i, j: (i, COL_GA // tn + j)),
            pl.BlockSpec((tm, tn), lambda i, j: (i, COL_GM // tn + j)),
            pl.BlockSpec((tm, D_MODEL), lambda i, j: (i, 0)),
            pl.BlockSpec((ATTN_WIDTH, tn), lambda i, j: (0, j)),
            pl.BlockSpec((MLSTM_WIDTH, tn), lambda i, j: (0, j)),
            pl.BlockSpec((tn, D_MODEL), lambda i, j: (j, 0)),
            pl.BlockSpec((1, D_MODEL), lambda i, j: (0, 0)),
            pl.BlockSpec((D_MODEL, CROSS_WIDTH), lambda i, j: (0, 0)),
            pl.BlockSpec((1, n_mem, 2 * CROSS_WIDTH), lambda i, j: (i // per_batch, 0, 0)),
            pl.BlockSpec((CROSS_WIDTH, D_MODEL), lambda i, j: (0, 0)),
        ],
        out_specs=pl.BlockSpec((tm, D_MODEL), lambda i, j: (i, 0)),
        out_shape=jax.ShapeDtypeStruct((t, D_MODEL), F32),
        compiler_params=pltpu.CompilerParams(
            dimension_semantics=("parallel", "arbitrary"), vmem_limit_bytes=VMEM_LIMIT),
        name="merge",
    )(att2d, hm2d, z2d, z2d, x2d, wa, wm, wo, gc, wcq, ckv, wco)


def _mlp_kernel(x_ref, g_ref, wu_ref, wd_ref, gf_ref, o_ref, h_scr, acc, *, final_norm):
    j = pl.program_id(1)

    @pl.when(j == 0)
    def _():
        x = x_ref[...]
        h_scr[...] = _rms(x, g_ref[...]).astype(BF16)
        acc[...] = x

    u = jnp.square(jnp.maximum(_dot(h_scr[...], wu_ref[...]), 0.0)).astype(BF16)
    acc[...] += _dot(u, wd_ref[...])

    @pl.when(j == pl.num_programs(1) - 1)
    def _():
        if final_norm:
            o_ref[...] = _rms(acc[...], gf_ref[...])
        else:
            o_ref[...] = acc[...]


def _mlp(x2d, g, wu, wd, gf, final_norm, tm=512, tf=1024):
    t = x2d.shape[0]
    kern = functools.partial(_mlp_kernel, final_norm=final_norm)
    return pl.pallas_call(
        kern,
        grid=(t // tm, D_FF // tf),
        in_specs=[
            pl.BlockSpec((tm, D_MODEL), lambda i, j: (i, 0)),
            pl.BlockSpec((1, D_MODEL), lambda i, j: (0, 0)),
            pl.BlockSpec((D_MODEL, tf), lambda i, j: (0, j)),
            pl.BlockSpec((tf, D_MODEL), lambda i, j: (j, 0)),
            pl.BlockSpec((1, D_MODEL), lambda i, j: (0, 0)),
        ],
        out_specs=pl.BlockSpec((tm, D_MODEL), lambda i, j: (i, 0)),
        out_shape=jax.ShapeDtypeStruct((t, D_MODEL), F32),
        scratch_shapes=[pltpu.VMEM((tm, D_MODEL), BF16), pltpu.VMEM((tm, D_MODEL), F32)],
        compiler_params=pltpu.CompilerParams(
            dimension_semantics=("parallel", "arbitrary"), vmem_limit_bytes=VMEM_LIMIT),
        name="mlp",
    )(x2d, g, wu, wd, gf)


def _alibi_terms():
    slopes = 2.0 ** (-8.0 * np.arange(1, ATTN_HEADS + 1, dtype=np.float64) / ATTN_HEADS)
    rem = slopes * LOG2E
    terms = []
    for _ in range(ALIBI_TERMS):
        t = rem.astype(np.float32).astype(ml_dtypes.bfloat16).astype(np.float64)
        terms.append(t)
        rem = rem - t
    return np.stack(terms, axis=1).reshape(-1).astype(np.float32)


def _pad_rows(a, rows):
    return jnp.pad(a, ((0, rows - a.shape[0]), (0, 0)))


def kernel(x, mem, norm_mix, w_in, b_igate, b_fgate, conv_w, conv_b, lam_q1, lam_k1, lam_q2, lam_k2, attn_norm, mlstm_norm, w_attn_br, w_mlstm_br, w_out, norm_cross, norm_mem, w_cq, w_ckv, w_co, norm_mlp, w_up, w_down, norm_final):
    b, s, _ = x.shape
    t = b * s
    tq = 256
    cs_terms = jnp.asarray(_alibi_terms())
    x2d = x.reshape(t, D_MODEL)
    for l in range(DEPTH):
        lam_init = 0.8 - 0.6 * math.exp(-0.3 * l)
        w = w_in[l]
        w_a, w_b, w_gate = _wprep(w.T)

        z2d, gates, vt = _inproj(x2d, norm_mix[l][None, :], w_a, w_b, w_gate, tq)
        z3 = z2d.reshape(b, s, IN_MAIN)

        vt5 = vt.reshape(b, s // tq, ATTN_HEADS, ATTN_V_DIM, tq)
        lamv = _pad_rows(jnp.pad(jnp.stack([lam_q1[l], lam_k1[l], lam_q2[l], lam_k2[l]]),
                                 ((0, 0), (0, LANES - ATTN_QK_DIM))), SUBLANES)
        att, w_out_b, w_abr_b, w_mbr_b, w_up_b, w_down_b = _attn(
            z3, vt5, cs_terms, lamv, attn_norm[l][None, :],
            (w_out[l], w_attn_br[l], w_mlstm_br[l], w_up[l], w_down[l]), lam_init, tq=tq)

        gates_t = gates[:, :SUBLANES].reshape(b, s, SUBLANES).transpose(0, 2, 1)
        gate_bias = jnp.concatenate([b_igate[l], b_fgate[l]])
        bias_col = jnp.pad(gate_bias, (0, LANES - gate_bias.shape[0]))[None, :]
        bias_row = jnp.broadcast_to(gate_bias[:, None], (SUBLANES, LANES))
        hm = _mlstm(z3, gates, gates_t, bias_col, bias_row, _pad_rows(conv_w[l], SUBLANES),
                    conv_b[l][None, :], mlstm_norm[l][None, :])

        ckv = _memkv(mem, norm_mem[l][None, :], w_ckv[l])
        x2d = _merge(att.reshape(t, ATTN_WIDTH), hm.reshape(t, MLSTM_WIDTH), z2d, x2d,
                     w_abr_b, w_mbr_b, w_out_b, norm_cross[l][None, :], w_cq[l].astype(BF16), ckv,
                     w_co[l].astype(BF16), s)

        x2d = _mlp(x2d, norm_mlp[l][None, :], w_up_b, w_down_b,
                   norm_final[None, :], final_norm=(l == DEPTH - 1))
    return x2d.reshape(b, s, D_MODEL)
```

```python
import functools
import math

import ml_dtypes
import numpy as np
import jax
import jax.numpy as jnp
from jax import lax
from jax.experimental import pallas as pl
from jax.experimental.pallas import tpu as pltpu

F32 = jnp.float32
BF16 = jnp.bfloat16

D_MODEL = 2048
DEPTH = 1
ATTN_HEADS = 8
ATTN_QK_DIM = 64
ATTN_V_DIM = 128
ATTN_WIDTH = ATTN_HEADS * ATTN_V_DIM
MLSTM_HEADS = 4
MLSTM_QK_DIM = 128
MLSTM_V_DIM = 256
MLSTM_QK_WIDTH = MLSTM_HEADS * MLSTM_QK_DIM
MLSTM_WIDTH = MLSTM_HEADS * MLSTM_V_DIM
CONV_WIDTH = 4
CROSS_HEADS = 4
CROSS_HEAD_DIM = 128
CROSS_WIDTH = CROSS_HEADS * CROSS_HEAD_DIM
D_FF = 4 * D_MODEL
EPS = 1e-6
LANES = 128
SUBLANES = 8
NEG_BIG = -1e30
LOG2E = 1.4426950408889634
ALIBI_TERMS = 3

COL_AK = 0
COL_MQ = 1024
COL_MK = 1536
COL_MV = 2048
COL_MO = 3072
COL_GA = 4096
COL_GM = 6144
IN_MAIN = 8192
W_AQ_LO, W_AQ_HI = 0, 1024
W_AV_LO = 2048
GATE_LO = 6144
GATE_HI = 6152

VMEM_LIMIT = 56 * 1024 * 1024


def _rms(x, g):
    ms = jnp.mean(x * x, axis=-1, keepdims=True)
    return x * lax.rsqrt(ms + EPS) * g


def _sigmoid(x):
    return 0.5 * jnp.tanh(0.5 * x) + 0.5


def _log_sigmoid(x):
    return jnp.minimum(x, 0.0) - jnp.log(1.0 + jnp.exp(-jnp.abs(x)))


def _dot(a, b):
    return jnp.dot(a, b, preferred_element_type=F32)


def _dot_nt(a, b):
    return lax.dot_general(a, b, (((1,), (1,)), ((), ())), preferred_element_type=F32)


def _memkv_kernel(mem_ref, g_ref, w_ref, o_ref):
    mn = _rms(mem_ref[0], g_ref[...]).astype(BF16)
    o_ref[0] = _dot(mn, w_ref[...].astype(BF16)).astype(BF16)


def _memkv(mem, g, w_ckv):
    b, n_mem, _ = mem.shape
    return pl.pallas_call(
        _memkv_kernel,
        grid=(b,),
        in_specs=[
            pl.BlockSpec((1, n_mem, D_MODEL), lambda i: (i, 0, 0)),
            pl.BlockSpec((1, D_MODEL), lambda i: (0, 0)),
            pl.BlockSpec((D_MODEL, 2 * CROSS_WIDTH), lambda i: (0, 0)),
        ],
        out_specs=pl.BlockSpec((1, n_mem, 2 * CROSS_WIDTH), lambda i: (i, 0, 0)),
        out_shape=jax.ShapeDtypeStruct((b, n_mem, 2 * CROSS_WIDTH), BF16),
        compiler_params=pltpu.CompilerParams(
            dimension_semantics=("arbitrary",), vmem_limit_bytes=VMEM_LIMIT),
        name="memkv",
    )(mem, g, w_ckv)


def _wprep_kernel(a_ref, nxt_ref, wa_out, wb_out, wg_out, *, na):
    k = pl.program_id(0)
    ng = GATE_HI - GATE_LO

    @pl.when(k < na)
    def _():
        scale = jnp.where(k == W_AQ_LO // a_ref.shape[0], ATTN_QK_DIM ** -0.5 * LOG2E, 1.0)
        wa_out[...] = (a_ref[...] * scale).astype(BF16)

    @pl.when(k >= na)
    def _():
        wb_out[...] = jnp.concatenate([a_ref[pl.ds(ng, a_ref.shape[0] - ng), :], nxt_ref[0]],
                                      axis=0).astype(BF16)

    @pl.when(k == na)
    def _():
        pad = jnp.zeros((LANES - ng, D_MODEL), F32)
        wg_out[...] = jnp.concatenate([a_ref[pl.ds(0, ng), :], pad], axis=0).astype(BF16)


def _wprep(wt, tn=1024):
    rows = wt.shape[0]
    ng = GATE_HI - GATE_LO
    assert ng == SUBLANES and GATE_LO % tn == 0 and (rows - GATE_HI) % tn == 0
    assert W_AQ_LO == 0 and W_AQ_HI == tn
    na = GATE_LO // tn
    nb = (rows - GATE_HI) // tn
    groups = wt.reshape(rows // ng, ng, D_MODEL)
    per_block = tn // ng
    return pl.pallas_call(
        functools.partial(_wprep_kernel, na=na),
        grid=(na + nb,),
        in_specs=[
            pl.BlockSpec((tn, D_MODEL), lambda k: (k, 0)),
            pl.BlockSpec((1, ng, D_MODEL), lambda k: ((k + 1) * per_block, 0, 0)),
        ],
        out_specs=[
            pl.BlockSpec((tn, D_MODEL), lambda k: (jnp.minimum(k, na - 1), 0)),
            pl.BlockSpec((tn, D_MODEL), lambda k: (jnp.maximum(k - na, 0), 0)),
            pl.BlockSpec((LANES, D_MODEL), lambda k: (0, 0)),
        ],
        out_shape=[
            jax.ShapeDtypeStruct((GATE_LO, D_MODEL), BF16),
            jax.ShapeDtypeStruct((rows - GATE_HI, D_MODEL), BF16),
            jax.ShapeDtypeStruct((LANES, D_MODEL), BF16),
        ],
        compiler_params=pltpu.CompilerParams(
            dimension_semantics=("arbitrary",), vmem_limit_bytes=VMEM_LIMIT),
        name="wprep",
    )(wt, groups)


def _inproj_kernel(x_ref, g_ref, wa_ref, wb_ref, wg_ref, z_ref, gate_ref, qt_ref, vt_ref, h_scr,
                   *, na, jq, jv):
    j = pl.program_id(1)

    @pl.when(j == 0)
    def _():
        h = _rms(x_ref[...], g_ref[...]).astype(BF16)
        h_scr[...] = h
        gate_ref[...] = _dot_nt(h, wg_ref[...])

    @pl.when((j < na) & (j != jq) & (j != jv))
    def _():
        z_ref[...] = _dot_nt(h_scr[...], wa_ref[...]).astype(BF16)

    def transposed(out_ref):
        xt = _dot_nt(wa_ref[...], h_scr[...]).astype(BF16)
        tile = out_ref.shape[2]
        for n in range(out_ref.shape[0]):
            out_ref[n] = xt[:, n * tile:(n + 1) * tile]

    @pl.when(j == jq)
    def _():
        transposed(qt_ref)

    @pl.when(j == jv)
    def _():
        transposed(vt_ref)

    @pl.when(j >= na)
    def _():
        z_ref[...] = _dot_nt(h_scr[...], wb_ref[...]).astype(BF16)


def _inproj(x2d, g, w_a, w_b, w_gate, tkv, tm=1024, tn=1024):
    t = x2d.shape[0]
    na = w_a.shape[0] // tn
    n = w_a.shape[0] + w_b.shape[0]
    ni, nj = t // tm, n // tn
    assert tn == ATTN_WIDTH and W_AQ_LO % tn == 0 and W_AQ_HI - W_AQ_LO == tn and W_AV_LO % tn == 0
    jq, jv = W_AQ_LO // tn, W_AV_LO // tn
    assert jq < jv

    def z_tile(i, j):
        skipped = (j >= jq).astype(jnp.int32) + (j >= jv).astype(jnp.int32)
        return i, jnp.maximum(j - skipped, 0)

    return pl.pallas_call(
        functools.partial(_inproj_kernel, na=na, jq=jq, jv=jv),
        grid=(ni, nj),
        in_specs=[
            pl.BlockSpec((tm, D_MODEL), lambda i, j: (i, 0)),
            pl.BlockSpec((1, D_MODEL), lambda i, j: (0, 0)),
            pl.BlockSpec((tn, D_MODEL), lambda i, j: (jnp.minimum(j, na - 1), 0)),
            pl.BlockSpec((tn, D_MODEL), lambda i, j: (jnp.maximum(j - na, 0), 0)),
            pl.BlockSpec((LANES, D_MODEL), lambda i, j: (0, 0)),
        ],
        out_specs=[
            pl.BlockSpec((tm, tn), z_tile),
            pl.BlockSpec((tm, LANES), lambda i, j: (i, 0)),
            pl.BlockSpec((tm // tkv, ATTN_WIDTH, tkv), lambda i, j: (i, 0, 0)),
            pl.BlockSpec((tm // tkv, ATTN_WIDTH, tkv), lambda i, j: (i, 0, 0)),
        ],
        out_shape=[
            jax.ShapeDtypeStruct((t, n - 2 * tn), BF16),
            jax.ShapeDtypeStruct((t, LANES), F32),
            jax.ShapeDtypeStruct((t // tkv, ATTN_WIDTH, tkv), BF16),
            jax.ShapeDtypeStruct((t // tkv, ATTN_WIDTH, tkv), BF16),
        ],
        scratch_shapes=[pltpu.VMEM((tm, D_MODEL), BF16)],
        compiler_params=pltpu.CompilerParams(
            dimension_semantics=("arbitrary", "arbitrary"), vmem_limit_bytes=VMEM_LIMIT),
        name="inproj",
    )(x2d, g, w_a, w_b, w_gate)


def _attn_kernel(cs_ref, lam_ref, qt_ref, k_ref, vt_ref, gain_ref, o_ref,
                 qs_ref, kf_ref, acc_ref, m_ref, *, tq, lam_init):
    tk = tq
    nfeat = ALIBI_TERMS
    dq = 2 * ATTN_QK_DIM
    dv = ATTN_V_DIM
    i = pl.program_id(1)
    cs = [[cs_ref[nfeat * h + t] for t in range(nfeat)] for h in range(ATTN_HEADS)]
    cs_tot = [sum(c[1:], c[0]) for c in cs]

    @pl.when(i == 0)
    def _():
        klane = lax.broadcasted_iota(jnp.int32, (tk, LANES), 1)
        krow = lax.broadcasted_iota(jnp.int32, (tk, LANES), 0).astype(F32)
        frow = lax.broadcasted_iota(jnp.int32, (LANES, 2 * tq), 0)
        fcol = lax.broadcasted_iota(jnp.int32, (LANES, 2 * tq), 1)
        fcol = jnp.where(fcol >= tq, fcol - tq, fcol).astype(F32)
        for h in range(ATTN_HEADS):
            kf = jnp.where(klane < nfeat, krow, 0.0)
            qf = jnp.where((frow >= nfeat) & (frow < 2 * nfeat), -fcol, 0.0)
            for t in range(nfeat):
                kf = jnp.where(klane == nfeat + t, cs[h][t], kf)
                qf = jnp.where(frow == t, cs[h][t], qf)
            kf_ref[h] = kf.astype(BF16)
            qs_ref[h, pl.ds(dq, LANES), :] = qf.astype(BF16)

    row = lax.broadcasted_iota(jnp.int32, (dq, tq), 0)
    for h in range(ATTN_HEADS):
        qt = qt_ref[0, 0, h]
        zero = jnp.zeros_like(qt)
        qs_ref[h, pl.ds(0, dq), pl.ds(0, tq)] = jnp.where(row < ATTN_QK_DIM, qt, zero)
        qs_ref[h, pl.ds(0, dq), pl.ds(tq, tq)] = jnp.where(row >= ATTN_QK_DIM, qt, zero)
    acc_ref[...] = jnp.zeros_like(acc_ref)
    m_ref[...] = jnp.full_like(m_ref, NEG_BIG)

    orow = lax.broadcasted_iota(jnp.int32, (2 * SUBLANES, tk), 0)
    ones_blk = jnp.where(orow == 0, 1.0, 0.0).astype(BF16)

    def scores(h, j):
        k = k_ref[0, pl.ds(pl.multiple_of(j * tk, tk), tk), pl.ds(h * dq, dq)]
        kx = jnp.concatenate([k, kf_ref[h]], axis=1)
        return _dot(kx, qs_ref[h])

    def softmax_pv(h, j, t, diagonal):
        if diagonal:
            krow = lax.broadcasted_iota(jnp.int32, (tk, 2 * tq), 0)
            qcol = lax.broadcasted_iota(jnp.int32, (tk, 2 * tq), 1)
            qcol = jnp.where(qcol >= tq, qcol - tq, qcol)
            t = jnp.where(krow <= qcol, t, NEG_BIG)
        soff = cs_tot[h] * ((j - i) * tk).astype(F32)
        m = m_ref[h]
        m_new = jnp.maximum(m, jnp.max(t, axis=0, keepdims=True) + soff)
        alpha = jnp.exp2(m - m_new)
        p = jnp.exp2(t - (m_new - soff)).astype(BF16)
        vx = jnp.concatenate([vt_ref[0, j, h], ones_blk], axis=0)
        acc_ref[h] = alpha * acc_ref[h] + _dot(vx, p)
        m_ref[h] = m_new

    def run(units):
        ahead = 2
        pending = [scores(*u[:2]) for u in units[:ahead]]
        for n, (h, j, diagonal) in enumerate(units):
            if n + ahead < len(units):
                pending.append(scores(*units[n + ahead][:2]))
            softmax_pv(h, j, pending.pop(0), diagonal)

    def tile_units(j, diagonal):
        return [(h, j, diagonal) for h in range(ATTN_HEADS)]

    def body(jj, carry):
        run(tile_units(2 * jj, False) + tile_units(2 * jj + 1, False))
        return carry

    lax.fori_loop(0, i // 2, body, 0)

    @pl.when(i % 2 == 0)
    def _():
        run(tile_units(i, True))

    @pl.when(i % 2 == 1)
    def _():
        run(tile_units(i - 1, False) + tile_units(i, True))

    lv = lam_ref[...]
    d1 = jnp.sum(lv[0:1] * lv[1:2], axis=-1, keepdims=True)
    d2 = jnp.sum(lv[2:3] * lv[3:4], axis=-1, keepdims=True)
    lam = jnp.exp(d1) - jnp.exp(d2) + lam_init
    for h in range(ATTN_HEADS):
        l = acc_ref[h, pl.ds(dv, 1), :]
        rl = 1.0 / l
        out = (acc_ref[h, pl.ds(0, dv), pl.ds(0, tq)] * rl[:, :tq]
               - lam * (acc_ref[h, pl.ds(0, dv), pl.ds(tq, tq)] * rl[:, tq:]))
        ms = jnp.mean(out * out, axis=0, keepdims=True)
        on = out * lax.rsqrt(ms + EPS)
        o_ref[0, :, pl.ds(h * dv, dv)] = (on.T * gain_ref[...] * (1.0 - lam_init)).astype(BF16)


def _attn(z3, qt5, vt5, cs_terms, lamv, gain, lam_init, tq=256):
    b, s, _ = z3.shape
    nq = s // tq
    kern = functools.partial(_attn_kernel, tq=tq, lam_init=lam_init)
    width = ATTN_HEADS * 2 * ATTN_QK_DIM
    return pl.pallas_call(
        kern,
        grid=(b, nq),
        in_specs=[
            pl.BlockSpec(memory_space=pltpu.SMEM),
            pl.BlockSpec((SUBLANES, LANES), lambda bi, i: (0, 0)),
            pl.BlockSpec((1, 1, ATTN_HEADS, 2 * ATTN_QK_DIM, tq), lambda bi, i: (bi, i, 0, 0, 0)),
            pl.BlockSpec((1, s, width), lambda bi, i: (bi, 0, COL_AK // width)),
            pl.BlockSpec((1, nq, ATTN_HEADS, ATTN_V_DIM, tq), lambda bi, i: (bi, 0, 0, 0, 0)),
            pl.BlockSpec((1, ATTN_V_DIM), lambda bi, i: (0, 0)),
        ],
        out_specs=pl.BlockSpec((1, tq, ATTN_WIDTH), lambda bi, i: (bi, i, 0)),
        out_shape=jax.ShapeDtypeStruct((b, s, ATTN_WIDTH), BF16),
        scratch_shapes=[
            pltpu.VMEM((ATTN_HEADS, 2 * ATTN_QK_DIM + LANES, 2 * tq), BF16),
            pltpu.VMEM((ATTN_HEADS, tq, LANES), BF16),
            pltpu.VMEM((ATTN_HEADS, ATTN_V_DIM + 2 * SUBLANES, 2 * tq), F32),
            pltpu.VMEM((ATTN_HEADS, 1, 2 * tq), F32),
        ],
        compiler_params=pltpu.CompilerParams(
            dimension_semantics=("parallel", "arbitrary"),
            vmem_limit_bytes=VMEM_LIMIT),
        name="attn",
    )(cs_terms, lamv, qt5, z3, vt5, gain)


def _split3(x):
    hi = x.astype(BF16)
    r = x - hi.astype(F32)
    mid = r.astype(BF16)
    lo = (r - mid.astype(F32)).astype(BF16)
    return hi, mid, lo


def _mlstm_kernel(uq_ref, uk_ref, v_ref, mo_ref, gcol_ref, grow_ref, bcol_ref, brow_ref,
                  cw_ref, cb_ref, gn_ref, *refs, ncast, chunk):
    cast_in = refs[:ncast]
    o_ref = refs[ncast]
    cast_out = refs[ncast + 1:2 * ncast + 1]
    extq, extk, dstq, dstk, c_scr, n_scr, m_scr = refs[2 * ncast + 1:]
    L = chunk
    nh = MLSTM_HEADS
    dk = MLSTM_QK_DIM
    dv = MLSTM_V_DIM
    heads = range(nh)

    @pl.when(pl.program_id(1) == 0)
    def _():
        extq[:, pl.ds(0, SUBLANES), :] = jnp.zeros((nh, SUBLANES, dk), F32)
        extk[:, pl.ds(0, SUBLANES), :] = jnp.zeros((nh, SUBLANES, dk), F32)
        c_scr[...] = jnp.zeros_like(c_scr)
        n_scr[...] = jnp.zeros_like(n_scr)
        m_scr[...] = jnp.zeros_like(m_scr)

    def conv_silu(u_ref, ext, dst, col0, scale):
        n8 = L // SUBLANES
        first = SUBLANES - (CONV_WIDTH - 1)
        outs = []
        for h in heads:
            cols = pl.ds(col0 + h * dk, dk)
            ext[h, pl.ds(SUBLANES, L), :] = u_ref[0, :, pl.ds(h * dk, dk)].astype(F32)
            w = cw_ref[:, cols]
            bias = cb_ref[:, cols]
            rows = [ext[h, pl.ds(first + s, n8, stride=SUBLANES), :]
                    for s in range(SUBLANES + CONV_WIDTH - 1)]
            for r in range(SUBLANES):
                y = bias
                for tap in range(CONV_WIDTH):
                    y = y + rows[r + tap] * w[tap:tap + 1]
                y = y * _sigmoid(y)
                dst[h, pl.ds(r, n8, stride=SUBLANES), :] = y if scale is None else y * scale
            ext[h, pl.ds(0, SUBLANES), :] = ext[h, pl.ds(L, SUBLANES), :]
            outs.append(dst[h])
        return outs

    q = conv_silu(uq_ref, extq, dstq, 0, None)
    k = conv_silu(uk_ref, extk, dstk, MLSTM_QK_WIDTH, dk ** -0.5)
    qb = [x.astype(BF16) for x in q]
    kb = [x.astype(BF16) for x in k]
    vb = [v_ref[0, :, pl.ds(h * dv, dv)] for h in heads]

    g_c = gcol_ref[...] + bcol_ref[...]
    g_r = grow_ref[0] + brow_ref[:, 0:1]
    lf_c = _log_sigmoid(g_c)
    lf_r = _log_sigmoid(g_r)

    r_i = lax.broadcasted_iota(jnp.int32, (L, L), 0)
    c_i = lax.broadcasted_iota(jnp.int32, (L, L), 1)
    causal = c_i <= r_i
    tri = jnp.where(causal, 1.0, 0.0).astype(BF16)
    tri_t = jnp.where(r_i <= c_i, 1.0, 0.0).astype(BF16)
    hi, mid, lo = _split3(lf_c)
    b_c = _dot(tri, hi) + _dot(tri, mid) + _dot(tri, lo)
    hi, mid, lo = _split3(lf_r)
    b_r = _dot(hi, tri_t) + _dot(mid, tri_t) + _dot(lo, tri_t)
    g_sum = jnp.sum(lf_r, axis=1, keepdims=True)

    lane = lax.broadcasted_iota(jnp.int32, (L, LANES), 1)

    def col(x, idx):
        return jnp.sum(jnp.where(lane == idx, x, 0.0), axis=1, keepdims=True)

    i_col = [col(g_c, h) for h in heads]
    b_col = [col(b_c, nh + h) for h in heads]
    i_row = [g_r[h:h + 1, :] for h in heads]
    b_row = [b_r[nh + h:nh + h + 1, :] for h in heads]
    g_tot = [g_sum[nh + h:nh + h + 1, :] for h in heads]

    c_prev = [c_scr[h] for h in heads]
    n_prev = [n_scr[h] for h in heads]
    m_prev = [m_scr[h][:, 0:1] for h in heads]

    s_qk = [_dot_nt(qb[h], kb[h]) for h in heads]
    inter = [_dot(qb[h], c_prev[h].astype(BF16)) for h in heads]
    m_loc = [jnp.max(g_tot[h] - b_row[h] + i_row[h], axis=1, keepdims=True) for h in heads]
    kw = [k[h] * jnp.exp(g_tot[h] - b_col[h] + i_col[h] - m_loc[h]) for h in heads]
    c_loc = [_dot(kw[h].T.astype(BF16), vb[h]) for h in heads]

    for h in heads:
        d = jnp.where(causal, b_col[h] - b_row[h] + i_row[h], NEG_BIG)
        m_inter = b_col[h] + m_prev[h]
        m_j = jnp.maximum(m_inter, jnp.max(d, axis=1, keepdims=True))
        w_inter = jnp.exp(m_inter - m_j)
        p = s_qk[h] * jnp.exp(d - m_j)
        num = w_inter * inter[h] + _dot(p.astype(BF16), vb[h])
        den = (w_inter * jnp.sum(q[h] * n_prev[h], axis=1, keepdims=True)
               + jnp.sum(p, axis=1, keepdims=True))
        hval = num * (1.0 / jnp.maximum(jnp.abs(den), jnp.exp(-m_j)))
        hn = _rms(hval, gn_ref[:, pl.ds(h * dv, dv)])
        gate = _sigmoid(mo_ref[0, :, pl.ds(h * dv, dv)].astype(F32))
        o_ref[0, :, pl.ds(h * dv, dv)] = (hn * gate).astype(BF16)

    for h in heads:
        n_loc = jnp.sum(kw[h], axis=0, keepdims=True)
        m_new = jnp.maximum(g_tot[h] + m_prev[h], m_loc[h])
        a_old = jnp.exp(g_tot[h] + m_prev[h] - m_new)
        a_new = jnp.exp(m_loc[h] - m_new)
        c_scr[h] = a_old * c_prev[h] + a_new * c_loc[h]
        n_scr[h] = a_old * n_prev[h] + a_new * n_loc
        m_scr[h] = jnp.broadcast_to(m_new, (1, LANES))

    for src, dst in zip(cast_in, cast_out):
        dst[...] = src[...].astype(BF16)


def _mlstm(z3, gates, gates_t, bias_col, bias_row, conv_w8, conv_b, gnorm, cast_weights, chunk=256):
    b, s, _ = z3.shape
    nc = s // chunk
    steps = b * nc
    cast_specs = [pl.BlockSpec((w.shape[0] // steps, w.shape[1]), lambda bi, c: (bi * nc + c, 0))
                  for w in cast_weights]
    kern = functools.partial(_mlstm_kernel, ncast=len(cast_weights), chunk=chunk)
    qw = MLSTM_QK_WIDTH
    vw = MLSTM_WIDTH
    return pl.pallas_call(
        kern,
        grid=(b, nc),
        in_specs=[
            pl.BlockSpec((1, chunk, qw), lambda bi, c: (bi, c, COL_MQ // qw)),
            pl.BlockSpec((1, chunk, qw), lambda bi, c: (bi, c, COL_MK // qw)),
            pl.BlockSpec((1, chunk, vw), lambda bi, c: (bi, c, COL_MV // vw)),
            pl.BlockSpec((1, chunk, vw), lambda bi, c: (bi, c, COL_MO // vw)),
            pl.BlockSpec((chunk, LANES), lambda bi, c: (bi * nc + c, 0)),
            pl.BlockSpec((1, SUBLANES, chunk), lambda bi, c: (bi, 0, c)),
            pl.BlockSpec((1, LANES), lambda bi, c: (0, 0)),
            pl.BlockSpec((SUBLANES, LANES), lambda bi, c: (0, 0)),
            pl.BlockSpec((SUBLANES, 2 * qw), lambda bi, c: (0, 0)),
            pl.BlockSpec((1, 2 * qw), lambda bi, c: (0, 0)),
            pl.BlockSpec((1, vw), lambda bi, c: (0, 0)),
        ] + cast_specs,
        out_specs=[pl.BlockSpec((1, chunk, vw), lambda bi, c: (bi, c, 0))] + cast_specs,
        out_shape=[jax.ShapeDtypeStruct((b, s, vw), BF16)]
        + [jax.ShapeDtypeStruct(w.shape, BF16) for w in cast_weights],
        scratch_shapes=[
            pltpu.VMEM((MLSTM_HEADS, chunk + 2 * SUBLANES, MLSTM_QK_DIM), F32),
            pltpu.VMEM((MLSTM_HEADS, chunk + 2 * SUBLANES, MLSTM_QK_DIM), F32),
            pltpu.VMEM((MLSTM_HEADS, chunk, MLSTM_QK_DIM), F32),
            pltpu.VMEM((MLSTM_HEADS, chunk, MLSTM_QK_DIM), F32),
            pltpu.VMEM((MLSTM_HEADS, MLSTM_QK_DIM, MLSTM_V_DIM), F32),
            pltpu.VMEM((MLSTM_HEADS, 1, MLSTM_QK_DIM), F32),
            pltpu.VMEM((MLSTM_HEADS, 1, LANES), F32),
        ],
        compiler_params=pltpu.CompilerParams(
            dimension_semantics=("parallel", "arbitrary"),
            vmem_limit_bytes=VMEM_LIMIT),
        name="mlstm",
    )(z3, z3, z3, z3, gates, gates_t, bias_col, bias_row, conv_w8, conv_b, gnorm, *cast_weights)


def _cross_delta(x, g_ref, wq_ref, kv_ref, wo_ref):
    hc = _rms(x, g_ref[...]).astype(BF16)
    cq = (_dot(hc, wq_ref[...]) * (CROSS_HEAD_DIM ** -0.5)).astype(BF16)
    outs = []
    for hh in range(CROSS_HEADS):
        lo = hh * CROSS_HEAD_DIM
        qh = cq[:, lo:lo + CROSS_HEAD_DIM]
        kh = kv_ref[0, :, lo:lo + CROSS_HEAD_DIM]
        vh = kv_ref[0, :, CROSS_WIDTH + lo:CROSS_WIDTH + lo + CROSS_HEAD_DIM]
        s = _dot_nt(qh, kh)
        p = jnp.exp(s - jnp.max(s, axis=-1, keepdims=True))
        l = jnp.sum(p, axis=-1, keepdims=True)
        outs.append((_dot(p.astype(BF16), vh) * (1.0 / l)).astype(BF16))
    co = jnp.concatenate(outs, axis=1)
    return _dot(co, wo_ref[...])


def _merge_kernel(att_ref, hm_ref, ga_ref, gm_ref, x_ref, wa_ref, wm_ref, wo_ref,
                  gc_ref, wcq_ref, kv_ref, wco_ref, o_ref):
    j = pl.program_id(1)

    @pl.when(j == 0)
    def _():
        o_ref[...] = x_ref[...]

    a = _dot(att_ref[...], wa_ref[...])
    bm = _dot(hm_ref[...], wm_ref[...])
    y = _sigmoid(ga_ref[...].astype(F32)) * a + _sigmoid(gm_ref[...].astype(F32)) * bm
    o_ref[...] += _dot(y.astype(BF16), wo_ref[...])

    @pl.when(j == pl.num_programs(1) - 1)
    def _():
        x1 = o_ref[...]
        o_ref[...] = x1 + _cross_delta(x1, gc_ref, wcq_ref, kv_ref, wco_ref)


def _merge(att2d, hm2d, z2d, x2d, wa, wm, wo, gc, wcq, ckv, wco, seq, tm=512, tn=1024):
    t = x2d.shape[0]
    nj = D_MODEL // tn
    n_mem = ckv.shape[1]
    per_batch = seq // tm
    return pl.pallas_call(
        _merge_kernel,
        grid=(t // tm, nj),
        in_specs=[
            pl.BlockSpec((tm, ATTN_WIDTH), lambda i, j: (i, 0)),
            pl.BlockSpec((tm, MLSTM_WIDTH), lambda i, j: (i, 0)),
            pl.BlockSpec((tm, tn), lambda i, j: (i, COL_GA // tn + j)),
            pl.BlockSpec((tm, tn), lambda i, j: (i, COL_GM // tn + j)),
            pl.BlockSpec((tm, D_MODEL), lambda i, j: (i, 0)),
            pl.BlockSpec((ATTN_WIDTH, tn), lambda i, j: (0, j)),
            pl.BlockSpec((MLSTM_WIDTH, tn), lambda i, j: (0, j)),
            pl.BlockSpec((tn, D_MODEL), lambda i, j: (j, 0)),
            pl.BlockSpec((1, D_MODEL), lambda i, j: (0, 0)),
            pl.BlockSpec((D_MODEL, CROSS_WIDTH), lambda i, j: (0, 0)),
            pl.BlockSpec((1, n_mem, 2 * CROSS_WIDTH), lambda i, j: (i // per_batch, 0, 0)),
            pl.BlockSpec((CROSS_WIDTH, D_MODEL), lambda i, j: (0, 0)),
        ],
        out_specs=pl.BlockSpec((tm, D_MODEL), lambda i, j: (i, 0)),
        out_shape=jax.ShapeDtypeStruct((t, D_MODEL), F32),
        compiler_params=pltpu.CompilerParams(
            dimension_semantics=("parallel", "arbitrary"), vmem_limit_bytes=VMEM_LIMIT),
        name="merge",
    )(att2d, hm2d, z2d, z2d, x2d, wa, wm, wo, gc, wcq, ckv, wco)


def _mlp_kernel(x_ref, g_ref, wu_ref, wd_ref, gf_ref, o_ref, h_scr, acc, *, final_norm):
    j = pl.program_id(1)

    @pl.when(j == 0)
    def _():
        x = x_ref[...]
        h_scr[...] = _rms(x, g_ref[...]).astype(BF16)
        acc[...] = x

    u = jnp.square(jnp.maximum(_dot(h_scr[...], wu_ref[...]), 0.0)).astype(BF16)
    acc[...] += _dot(u, wd_ref[...])

    @pl.when(j == pl.num_programs(1) - 1)
    def _():
        if final_norm:
            o_ref[...] = _rms(acc[...], gf_ref[...])
        else:
            o_ref[...] = acc[...]


def _mlp(x2d, g, wu, wd, gf, final_norm, tm=512, tf=1024):
    t = x2d.shape[0]
    kern = functools.partial(_mlp_kernel, final_norm=final_norm)
    return pl.pallas_call(
        kern,
        grid=(t // tm, D_FF // tf),
        in_specs=[
            pl.BlockSpec((tm, D_MODEL), lambda i, j: (i, 0)),
            pl.BlockSpec((1, D_MODEL), lambda i, j: (0, 0)),
            pl.BlockSpec((D_MODEL, tf), lambda i, j: (0, j)),
            pl.BlockSpec((tf, D_MODEL), lambda i, j: (j, 0)),
            pl.BlockSpec((1, D_MODEL), lambda i, j: (0, 0)),
        ],
        out_specs=pl.BlockSpec((tm, D_MODEL), lambda i, j: (i, 0)),
        out_shape=jax.ShapeDtypeStruct((t, D_MODEL), F32),
        scratch_shapes=[pltpu.VMEM((tm, D_MODEL), BF16), pltpu.VMEM((tm, D_MODEL), F32)],
        compiler_params=pltpu.CompilerParams(
            dimension_semantics=("parallel", "arbitrary"), vmem_limit_bytes=VMEM_LIMIT),
        name="mlp",
    )(x2d, g, wu, wd, gf)


def _alibi_terms():
    slopes = 2.0 ** (-8.0 * np.arange(1, ATTN_HEADS + 1, dtype=np.float64) / ATTN_HEADS)
    rem = slopes * LOG2E
    terms = []
    for _ in range(ALIBI_TERMS):
        t = rem.astype(np.float32).astype(ml_dtypes.bfloat16).astype(np.float64)
        terms.append(t)
        rem = rem - t
    return np.stack(terms, axis=1).reshape(-1).astype(np.float32)


def _pad_rows(a, rows):
    return jnp.pad(a, ((0, rows - a.shape[0]), (0, 0)))


def kernel(x, mem, norm_mix, w_in, b_igate, b_fgate, conv_w, conv_b, lam_q1, lam_k1, lam_q2, lam_k2, attn_norm, mlstm_norm, w_attn_br, w_mlstm_br, w_out, norm_cross, norm_mem, w_cq, w_ckv, w_co, norm_mlp, w_up, w_down, norm_final):
    b, s, _ = x.shape
    t = b * s
    tq = 256
    cs_terms = jnp.asarray(_alibi_terms())
    x2d = x.reshape(t, D_MODEL)
    for l in range(DEPTH):
        lam_init = 0.8 - 0.6 * math.exp(-0.3 * l)
        w = w_in[l]
        w_a, w_b, w_gate = _wprep(w.T)

        z2d, gates, qt, vt = _inproj(x2d, norm_mix[l][None, :], w_a, w_b, w_gate, tq)
        z3 = z2d.reshape(b, s, IN_MAIN)

        qt5 = qt.reshape(b, s // tq, ATTN_HEADS, 2 * ATTN_QK_DIM, tq)
        vt5 = vt.reshape(b, s // tq, ATTN_HEADS, ATTN_V_DIM, tq)
        lamv = _pad_rows(jnp.pad(jnp.stack([lam_q1[l], lam_k1[l], lam_q2[l], lam_k2[l]]),
                                 ((0, 0), (0, LANES - ATTN_QK_DIM))), SUBLANES)
        att = _attn(z3, qt5, vt5, cs_terms, lamv, attn_norm[l][None, :], lam_init, tq=tq)

        gates_t = gates[:, :SUBLANES].reshape(b, s, SUBLANES).transpose(0, 2, 1)
        gate_bias = jnp.concatenate([b_igate[l], b_fgate[l]])
        bias_col = jnp.pad(gate_bias, (0, LANES - gate_bias.shape[0]))[None, :]
        bias_row = jnp.broadcast_to(gate_bias[:, None], (SUBLANES, LANES))
        hm, w_out_b, w_abr_b, w_mbr_b, w_up_b, w_down_b = _mlstm(
            z3, gates, gates_t, bias_col, bias_row, _pad_rows(conv_w[l], SUBLANES),
            conv_b[l][None, :], mlstm_norm[l][None, :],
            (w_out[l], w_attn_br[l], w_mlstm_br[l], w_up[l], w_down[l]))

        ckv = _memkv(mem, norm_mem[l][None, :], w_ckv[l])
        x2d = _merge(att.reshape(t, ATTN_WIDTH), hm.reshape(t, MLSTM_WIDTH), z2d, x2d,
                     w_abr_b, w_mbr_b, w_out_b, norm_cross[l][None, :], w_cq[l].astype(BF16), ckv,
                     w_co[l].astype(BF16), s)

        x2d = _mlp(x2d, norm_mlp[l][None, :], w_up_b, w_down_b,
                   norm_final[None, :], final_norm=(l == DEPTH - 1))
    return x2d.reshape(b, s, D_MODEL)
```

```python
import functools
import math

import ml_dtypes
import numpy as np
import jax
import jax.numpy as jnp
from jax import lax
from jax.experimental import pallas as pl
from jax.experimental.pallas import tpu as pltpu

F32 = jnp.float32
BF16 = jnp.bfloat16

D_MODEL = 2048
DEPTH = 1
ATTN_HEADS = 8
ATTN_QK_DIM = 64
ATTN_V_DIM = 128
ATTN_WIDTH = ATTN_HEADS * ATTN_V_DIM
MLSTM_HEADS = 4
MLSTM_QK_DIM = 128
MLSTM_V_DIM = 256
MLSTM_QK_WIDTH = MLSTM_HEADS * MLSTM_QK_DIM
MLSTM_WIDTH = MLSTM_HEADS * MLSTM_V_DIM
CONV_WIDTH = 4
CROSS_HEADS = 4
CROSS_HEAD_DIM = 128
CROSS_WIDTH = CROSS_HEADS * CROSS_HEAD_DIM
D_FF = 4 * D_MODEL
EPS = 1e-6
LANES = 128
SUBLANES = 8
NEG_BIG = -1e30
LOG2E = 1.4426950408889634
ALIBI_TERMS = 3

COL_AK = 0
COL_MQ = 1024
COL_MK = 1536
COL_MV = 2048
COL_MO = 3072
COL_GA = 4096
COL_GM = 6144
IN_MAIN = 8192
W_AQ_LO, W_AQ_HI = 0, 1024
W_AV_LO = 2048
GATE_LO = 6144
GATE_HI = 6152

VMEM_LIMIT = 56 * 1024 * 1024


def _rms(x, g):
    ms = jnp.mean(x * x, axis=-1, keepdims=True)
    return x * lax.rsqrt(ms + EPS) * g


def _sigmoid(x):
    return 0.5 * jnp.tanh(0.5 * x) + 0.5


def _log_sigmoid(x):
    return jnp.minimum(x, 0.0) - jnp.log(1.0 + jnp.exp(-jnp.abs(x)))


def _dot(a, b):
    return jnp.dot(a, b, preferred_element_type=F32)


def _dot_nt(a, b):
    return lax.dot_general(a, b, (((1,), (1,)), ((), ())), preferred_element_type=F32)


def _memkv_kernel(mem_ref, g_ref, w_ref, o_ref):
    mn = _rms(mem_ref[0], g_ref[...]).astype(BF16)
    o_ref[0] = _dot(mn, w_ref[...].astype(BF16)).astype(BF16)


def _memkv(mem, g, w_ckv):
    b, n_mem, _ = mem.shape
    return pl.pallas_call(
        _memkv_kernel,
        grid=(b,),
        in_specs=[
            pl.BlockSpec((1, n_mem, D_MODEL), lambda i: (i, 0, 0)),
            pl.BlockSpec((1, D_MODEL), lambda i: (0, 0)),
            pl.BlockSpec((D_MODEL, 2 * CROSS_WIDTH), lambda i: (0, 0)),
        ],
        out_specs=pl.BlockSpec((1, n_mem, 2 * CROSS_WIDTH), lambda i: (i, 0, 0)),
        out_shape=jax.ShapeDtypeStruct((b, n_mem, 2 * CROSS_WIDTH), BF16),
        compiler_params=pltpu.CompilerParams(
            dimension_semantics=("arbitrary",), vmem_limit_bytes=VMEM_LIMIT),
        name="memkv",
    )(mem, g, w_ckv)


def _wprep_kernel(a_ref, nxt_ref, wa_out, wb_out, wg_out, *, na):
    k = pl.program_id(0)
    ng = GATE_HI - GATE_LO

    @pl.when(k < na)
    def _():
        scale = jnp.where(k == W_AQ_LO // a_ref.shape[0], ATTN_QK_DIM ** -0.5 * LOG2E, 1.0)
        wa_out[...] = (a_ref[...] * scale).astype(BF16)

    @pl.when(k >= na)
    def _():
        wb_out[...] = jnp.concatenate([a_ref[pl.ds(ng, a_ref.shape[0] - ng), :], nxt_ref[0]],
                                      axis=0).astype(BF16)

    @pl.when(k == na)
    def _():
        pad = jnp.zeros((LANES - ng, D_MODEL), F32)
        wg_out[...] = jnp.concatenate([a_ref[pl.ds(0, ng), :], pad], axis=0).astype(BF16)


def _wprep(wt, tn=1024):
    rows = wt.shape[0]
    ng = GATE_HI - GATE_LO
    assert ng == SUBLANES and GATE_LO % tn == 0 and (rows - GATE_HI) % tn == 0
    assert W_AQ_LO == 0 and W_AQ_HI == tn
    na = GATE_LO // tn
    nb = (rows - GATE_HI) // tn
    groups = wt.reshape(rows // ng, ng, D_MODEL)
    per_block = tn // ng
    return pl.pallas_call(
        functools.partial(_wprep_kernel, na=na),
        grid=(na + nb,),
        in_specs=[
            pl.BlockSpec((tn, D_MODEL), lambda k: (k, 0)),
            pl.BlockSpec((1, ng, D_MODEL), lambda k: ((k + 1) * per_block, 0, 0)),
        ],
        out_specs=[
            pl.BlockSpec((tn, D_MODEL), lambda k: (jnp.minimum(k, na - 1), 0)),
            pl.BlockSpec((tn, D_MODEL), lambda k: (jnp.maximum(k - na, 0), 0)),
            pl.BlockSpec((LANES, D_MODEL), lambda k: (0, 0)),
        ],
        out_shape=[
            jax.ShapeDtypeStruct((GATE_LO, D_MODEL), BF16),
            jax.ShapeDtypeStruct((rows - GATE_HI, D_MODEL), BF16),
            jax.ShapeDtypeStruct((LANES, D_MODEL), BF16),
        ],
        compiler_params=pltpu.CompilerParams(
            dimension_semantics=("arbitrary",), vmem_limit_bytes=VMEM_LIMIT),
        name="wprep",
    )(wt, groups)


def _inproj_kernel(x_ref, g_ref, wa_ref, wb_ref, wg_ref, z_ref, gate_ref, qt_ref, vt_ref, h_scr,
                   *, na, jq, jv):
    j = pl.program_id(1)

    @pl.when(j == 0)
    def _():
        h = _rms(x_ref[...], g_ref[...]).astype(BF16)
        h_scr[...] = h
        gate_ref[...] = _dot_nt(h, wg_ref[...])

    @pl.when((j < na) & (j != jq) & (j != jv))
    def _():
        z_ref[...] = _dot_nt(h_scr[...], wa_ref[...]).astype(BF16)

    def transposed(out_ref):
        xt = _dot_nt(wa_ref[...], h_scr[...]).astype(BF16)
        tile = out_ref.shape[2]
        for n in range(out_ref.shape[0]):
            out_ref[n] = xt[:, n * tile:(n + 1) * tile]

    @pl.when(j == jq)
    def _():
        transposed(qt_ref)

    @pl.when(j == jv)
    def _():
        transposed(vt_ref)

    @pl.when(j >= na)
    def _():
        z_ref[...] = _dot_nt(h_scr[...], wb_ref[...]).astype(BF16)


def _inproj(x2d, g, w_a, w_b, w_gate, tkv, tm=1024, tn=1024):
    t = x2d.shape[0]
    na = w_a.shape[0] // tn
    n = w_a.shape[0] + w_b.shape[0]
    ni, nj = t // tm, n // tn
    assert tn == ATTN_WIDTH and W_AQ_LO % tn == 0 and W_AQ_HI - W_AQ_LO == tn and W_AV_LO % tn == 0
    jq, jv = W_AQ_LO // tn, W_AV_LO // tn
    assert jq < jv

    def z_tile(i, j):
        skipped = (j >= jq).astype(jnp.int32) + (j >= jv).astype(jnp.int32)
        return i, jnp.maximum(j - skipped, 0)

    return pl.pallas_call(
        functools.partial(_inproj_kernel, na=na, jq=jq, jv=jv),
        grid=(ni, nj),
        in_specs=[
            pl.BlockSpec((tm, D_MODEL), lambda i, j: (i, 0)),
            pl.BlockSpec((1, D_MODEL), lambda i, j: (0, 0)),
            pl.BlockSpec((tn, D_MODEL), lambda i, j: (jnp.minimum(j, na - 1), 0)),
            pl.BlockSpec((tn, D_MODEL), lambda i, j: (jnp.maximum(j - na, 0), 0)),
            pl.BlockSpec((LANES, D_MODEL), lambda i, j: (0, 0)),
        ],
        out_specs=[
            pl.BlockSpec((tm, tn), z_tile),
            pl.BlockSpec((tm, LANES), lambda i, j: (i, 0)),
            pl.BlockSpec((tm // tkv, ATTN_WIDTH, tkv), lambda i, j: (i, 0, 0)),
            pl.BlockSpec((tm // tkv, ATTN_WIDTH, tkv), lambda i, j: (i, 0, 0)),
        ],
        out_shape=[
            jax.ShapeDtypeStruct((t, n - 2 * tn), BF16),
            jax.ShapeDtypeStruct((t, LANES), F32),
            jax.ShapeDtypeStruct((t // tkv, ATTN_WIDTH, tkv), BF16),
            jax.ShapeDtypeStruct((t // tkv, ATTN_WIDTH, tkv), BF16),
        ],
        scratch_shapes=[pltpu.VMEM((tm, D_MODEL), BF16)],
        compiler_params=pltpu.CompilerParams(
            dimension_semantics=("arbitrary", "arbitrary"), vmem_limit_bytes=VMEM_LIMIT),
        name="inproj",
    )(x2d, g, w_a, w_b, w_gate)


def _attn_kernel(cs_ref, lam_ref, qt_ref, k_ref, vt_ref, gain_ref, *refs, ncast, tq, lam_init):
    cast_in = refs[:ncast]
    o_ref = refs[ncast]
    cast_out = refs[ncast + 1:2 * ncast + 1]
    qs_ref, kf_ref, acc_ref, m_ref = refs[2 * ncast + 1:]
    tk = tq
    nfeat = ALIBI_TERMS
    dq = 2 * ATTN_QK_DIM
    dv = ATTN_V_DIM
    i = pl.program_id(1)
    cs = [[cs_ref[nfeat * h + t] for t in range(nfeat)] for h in range(ATTN_HEADS)]
    cs_tot = [sum(c[1:], c[0]) for c in cs]

    @pl.when(i == 0)
    def _():
        klane = lax.broadcasted_iota(jnp.int32, (tk, LANES), 1)
        krow = lax.broadcasted_iota(jnp.int32, (tk, LANES), 0).astype(F32)
        frow = lax.broadcasted_iota(jnp.int32, (LANES, 2 * tq), 0)
        fcol = lax.broadcasted_iota(jnp.int32, (LANES, 2 * tq), 1)
        fcol = jnp.where(fcol >= tq, fcol - tq, fcol).astype(F32)
        for h in range(ATTN_HEADS):
            kf = jnp.where(klane < nfeat, krow, 0.0)
            qf = jnp.where((frow >= nfeat) & (frow < 2 * nfeat), -fcol, 0.0)
            for t in range(nfeat):
                kf = jnp.where(klane == nfeat + t, cs[h][t], kf)
                qf = jnp.where(frow == t, cs[h][t], qf)
            kf_ref[h] = kf.astype(BF16)
            qs_ref[h, pl.ds(dq, LANES), :] = qf.astype(BF16)

    row = lax.broadcasted_iota(jnp.int32, (dq, tq), 0)
    for h in range(ATTN_HEADS):
        qt = qt_ref[0, 0, h]
        zero = jnp.zeros_like(qt)
        qs_ref[h, pl.ds(0, dq), pl.ds(0, tq)] = jnp.where(row < ATTN_QK_DIM, qt, zero)
        qs_ref[h, pl.ds(0, dq), pl.ds(tq, tq)] = jnp.where(row >= ATTN_QK_DIM, qt, zero)
    acc_ref[...] = jnp.zeros_like(acc_ref)
    m_ref[...] = jnp.full_like(m_ref, NEG_BIG)

    orow = lax.broadcasted_iota(jnp.int32, (2 * SUBLANES, tk), 0)
    ones_blk = jnp.where(orow == 0, 1.0, 0.0).astype(BF16)

    def scores(h, j):
        k = k_ref[0, pl.ds(pl.multiple_of(j * tk, tk), tk), pl.ds(h * dq, dq)]
        kx = jnp.concatenate([k, kf_ref[h]], axis=1)
        return _dot(kx, qs_ref[h])

    def softmax_pv(h, j, t, diagonal):
        if diagonal:
            krow = lax.broadcasted_iota(jnp.int32, (tk, 2 * tq), 0)
            qcol = lax.broadcasted_iota(jnp.int32, (tk, 2 * tq), 1)
            qcol = jnp.where(qcol >= tq, qcol - tq, qcol)
            t = jnp.where(krow <= qcol, t, NEG_BIG)
        soff = cs_tot[h] * ((j - i) * tk).astype(F32)
        m = m_ref[h]
        m_new = jnp.maximum(m, jnp.max(t, axis=0, keepdims=True) + soff)
        alpha = jnp.exp2(m - m_new)
        p = jnp.exp2(t - (m_new - soff)).astype(BF16)
        vx = jnp.concatenate([vt_ref[0, j, h], ones_blk], axis=0)
        acc_ref[h] = alpha * acc_ref[h] + _dot(vx, p)
        m_ref[h] = m_new

    def run(units):
        ahead = 2
        pending = [scores(*u[:2]) for u in units[:ahead]]
        for n, (h, j, diagonal) in enumerate(units):
            if n + ahead < len(units):
                pending.append(scores(*units[n + ahead][:2]))
            softmax_pv(h, j, pending.pop(0), diagonal)

    def tile_units(j, diagonal):
        return [(h, j, diagonal) for h in range(ATTN_HEADS)]

    def body(jj, carry):
        run(tile_units(2 * jj, False) + tile_units(2 * jj + 1, False))
        return carry

    lax.fori_loop(0, i // 2, body, 0)

    @pl.when(i % 2 == 0)
    def _():
        run(tile_units(i, True))

    @pl.when(i % 2 == 1)
    def _():
        run(tile_units(i - 1, False) + tile_units(i, True))

    lv = lam_ref[...]
    d1 = jnp.sum(lv[0:1] * lv[1:2], axis=-1, keepdims=True)
    d2 = jnp.sum(lv[2:3] * lv[3:4], axis=-1, keepdims=True)
    lam = jnp.exp(d1) - jnp.exp(d2) + lam_init
    for h in range(ATTN_HEADS):
        l = acc_ref[h, pl.ds(dv, 1), :]
        rl = 1.0 / l
        out = (acc_ref[h, pl.ds(0, dv), pl.ds(0, tq)] * rl[:, :tq]
               - lam * (acc_ref[h, pl.ds(0, dv), pl.ds(tq, tq)] * rl[:, tq:]))
        ms = jnp.mean(out * out, axis=0, keepdims=True)
        on = out * lax.rsqrt(ms + EPS)
        o_ref[0, :, pl.ds(h * dv, dv)] = (on.T * gain_ref[...] * (1.0 - lam_init)).astype(BF16)

    for src, dst in zip(cast_in, cast_out):
        dst[...] = src[...].astype(BF16)


def _attn(z3, qt5, vt5, cs_terms, lamv, gain, cast_weights, lam_init, tq=256):
    b, s, _ = z3.shape
    nq = s // tq
    steps = b * nq
    cast_specs = [pl.BlockSpec((w.shape[0] // steps, w.shape[1]), lambda bi, i: (bi * nq + i, 0))
                  for w in cast_weights]
    kern = functools.partial(_attn_kernel, ncast=len(cast_weights), tq=tq, lam_init=lam_init)
    width = ATTN_HEADS * 2 * ATTN_QK_DIM
    return pl.pallas_call(
        kern,
        grid=(b, nq),
        in_specs=[
            pl.BlockSpec(memory_space=pltpu.SMEM),
            pl.BlockSpec((SUBLANES, LANES), lambda bi, i: (0, 0)),
            pl.BlockSpec((1, 1, ATTN_HEADS, 2 * ATTN_QK_DIM, tq), lambda bi, i: (bi, i, 0, 0, 0)),
            pl.BlockSpec((1, s, width), lambda bi, i: (bi, 0, COL_AK // width)),
            pl.BlockSpec((1, nq, ATTN_HEADS, ATTN_V_DIM, tq), lambda bi, i: (bi, 0, 0, 0, 0)),
            pl.BlockSpec((1, ATTN_V_DIM), lambda bi, i: (0, 0)),
        ] + cast_specs,
        out_specs=[pl.BlockSpec((1, tq, ATTN_WIDTH), lambda bi, i: (bi, i, 0))] + cast_specs,
        out_shape=[jax.ShapeDtypeStruct((b, s, ATTN_WIDTH), BF16)]
        + [jax.ShapeDtypeStruct(w.shape, BF16) for w in cast_weights],
        scratch_shapes=[
            pltpu.VMEM((ATTN_HEADS, 2 * ATTN_QK_DIM + LANES, 2 * tq), BF16),
            pltpu.VMEM((ATTN_HEADS, tq, LANES), BF16),
            pltpu.VMEM((ATTN_HEADS, ATTN_V_DIM + 2 * SUBLANES, 2 * tq), F32),
            pltpu.VMEM((ATTN_HEADS, 1, 2 * tq), F32),
        ],
        compiler_params=pltpu.CompilerParams(
            dimension_semantics=("parallel", "arbitrary"),
            vmem_limit_bytes=VMEM_LIMIT),
        name="attn",
    )(cs_terms, lamv, qt5, z3, vt5, gain, *cast_weights)


def _split3(x):
    hi = x.astype(BF16)
    r = x - hi.astype(F32)
    mid = r.astype(BF16)
    lo = (r - mid.astype(F32)).astype(BF16)
    return hi, mid, lo


def _mlstm_kernel(uq_ref, uk_ref, v_ref, mo_ref, gcol_ref, grow_ref, bcol_ref, brow_ref,
                  cw_ref, cb_ref, gn_ref, o_ref, extq, extk, dstq, dstk, c_scr, n_scr, m_scr,
                  *, chunk):
    L = chunk
    nh = MLSTM_HEADS
    dk = MLSTM_QK_DIM
    dv = MLSTM_V_DIM
    heads = range(nh)

    @pl.when(pl.program_id(1) == 0)
    def _():
        extq[:, pl.ds(0, SUBLANES), :] = jnp.zeros((nh, SUBLANES, dk), F32)
        extk[:, pl.ds(0, SUBLANES), :] = jnp.zeros((nh, SUBLANES, dk), F32)
        c_scr[...] = jnp.zeros_like(c_scr)
        n_scr[...] = jnp.zeros_like(n_scr)
        m_scr[...] = jnp.zeros_like(m_scr)

    def conv_silu(u_ref, ext, dst, col0, scale):
        n8 = L // SUBLANES
        first = SUBLANES - (CONV_WIDTH - 1)
        outs = []
        for h in heads:
            cols = pl.ds(col0 + h * dk, dk)
            ext[h, pl.ds(SUBLANES, L), :] = u_ref[0, :, pl.ds(h * dk, dk)].astype(F32)
            w = cw_ref[:, cols]
            bias = cb_ref[:, cols]
            rows = [ext[h, pl.ds(first + s, n8, stride=SUBLANES), :]
                    for s in range(SUBLANES + CONV_WIDTH - 1)]
            for r in range(SUBLANES):
                y = bias
                for tap in range(CONV_WIDTH):
                    y = y + rows[r + tap] * w[tap:tap + 1]
                y = y * _sigmoid(y)
                dst[h, pl.ds(r, n8, stride=SUBLANES), :] = y if scale is None else y * scale
            ext[h, pl.ds(0, SUBLANES), :] = ext[h, pl.ds(L, SUBLANES), :]
            outs.append(dst[h])
        return outs

    q = conv_silu(uq_ref, extq, dstq, 0, None)
    k = conv_silu(uk_ref, extk, dstk, MLSTM_QK_WIDTH, dk ** -0.5)
    qb = [x.astype(BF16) for x in q]
    kb = [x.astype(BF16) for x in k]
    vb = [v_ref[0, :, pl.ds(h * dv, dv)] for h in heads]

    g_c = gcol_ref[...] + bcol_ref[...]
    g_r = grow_ref[0] + brow_ref[:, 0:1]
    lf_c = _log_sigmoid(g_c)
    lf_r = _log_sigmoid(g_r)

    r_i = lax.broadcasted_iota(jnp.int32, (L, L), 0)
    c_i = lax.broadcasted_iota(jnp.int32, (L, L), 1)
    causal = c_i <= r_i
    tri = jnp.where(causal, 1.0, 0.0).astype(BF16)
    tri_t = jnp.where(r_i <= c_i, 1.0, 0.0).astype(BF16)
    hi, mid, lo = _split3(lf_c)
    b_c = _dot(tri, hi) + _dot(tri, mid) + _dot(tri, lo)
    hi, mid, lo = _split3(lf_r)
    b_r = _dot(hi, tri_t) + _dot(mid, tri_t) + _dot(lo, tri_t)
    g_sum = jnp.sum(lf_r, axis=1, keepdims=True)

    lane = lax.broadcasted_iota(jnp.int32, (L, LANES), 1)

    def col(x, idx):
        return jnp.sum(jnp.where(lane == idx, x, 0.0), axis=1, keepdims=True)

    i_col = [col(g_c, h) for h in heads]
    b_col = [col(b_c, nh + h) for h in heads]
    i_row = [g_r[h:h + 1, :] for h in heads]
    b_row = [b_r[nh + h:nh + h + 1, :] for h in heads]
    g_tot = [g_sum[nh + h:nh + h + 1, :] for h in heads]

    c_prev = [c_scr[h] for h in heads]
    n_prev = [n_scr[h] for h in heads]
    m_prev = [m_scr[h][:, 0:1] for h in heads]

    s_qk = [_dot_nt(qb[h], kb[h]) for h in heads]
    inter = [_dot(qb[h], c_prev[h].astype(BF16)) for h in heads]
    m_loc = [jnp.max(g_tot[h] - b_row[h] + i_row[h], axis=1, keepdims=True) for h in heads]
    kw = [k[h] * jnp.exp(g_tot[h] - b_col[h] + i_col[h] - m_loc[h]) for h in heads]
    c_loc = [_dot(kw[h].T.astype(BF16), vb[h]) for h in heads]

    for h in heads:
        d = jnp.where(causal, b_col[h] - b_row[h] + i_row[h], NEG_BIG)
        m_inter = b_col[h] + m_prev[h]
        m_j = jnp.maximum(m_inter, jnp.max(d, axis=1, keepdims=True))
        w_inter = jnp.exp(m_inter - m_j)
        p = s_qk[h] * jnp.exp(d - m_j)
        num = w_inter * inter[h] + _dot(p.astype(BF16), vb[h])
        den = (w_inter * jnp.sum(q[h] * n_prev[h], axis=1, keepdims=True)
               + jnp.sum(p, axis=1, keepdims=True))
        hval = num * (1.0 / jnp.maximum(jnp.abs(den), jnp.exp(-m_j)))
        hn = _rms(hval, gn_ref[:, pl.ds(h * dv, dv)])
        gate = _sigmoid(mo_ref[0, :, pl.ds(h * dv, dv)].astype(F32))
        o_ref[0, :, pl.ds(h * dv, dv)] = (hn * gate).astype(BF16)

    for h in heads:
        n_loc = jnp.sum(kw[h], axis=0, keepdims=True)
        m_new = jnp.maximum(g_tot[h] + m_prev[h], m_loc[h])
        a_old = jnp.exp(g_tot[h] + m_prev[h] - m_new)
        a_new = jnp.exp(m_loc[h] - m_new)
        c_scr[h] = a_old * c_prev[h] + a_new * c_loc[h]
        n_scr[h] = a_old * n_prev[h] + a_new * n_loc
        m_scr[h] = jnp.broadcast_to(m_new, (1, LANES))


def _mlstm(z3, gates, gates_t, bias_col, bias_row, conv_w8, conv_b, gnorm, chunk=256):
    b, s, _ = z3.shape
    nc = s // chunk
    kern = functools.partial(_mlstm_kernel, chunk=chunk)
    qw = MLSTM_QK_WIDTH
    vw = MLSTM_WIDTH
    return pl.pallas_call(
        kern,
        grid=(b, nc),
        in_specs=[
            pl.BlockSpec((1, chunk, qw), lambda bi, c: (bi, c, COL_MQ // qw)),
            pl.BlockSpec((1, chunk, qw), lambda bi, c: (bi, c, COL_MK // qw)),
            pl.BlockSpec((1, chunk, vw), lambda bi, c: (bi, c, COL_MV // vw)),
            pl.BlockSpec((1, chunk, vw), lambda bi, c: (bi, c, COL_MO // vw)),
            pl.BlockSpec((chunk, LANES), lambda bi, c: (bi * nc + c, 0)),
            pl.BlockSpec((1, SUBLANES, chunk), lambda bi, c: (bi, 0, c)),
            pl.BlockSpec((1, LANES), lambda bi, c: (0, 0)),
            pl.BlockSpec((SUBLANES, LANES), lambda bi, c: (0, 0)),
            pl.BlockSpec((SUBLANES, 2 * qw), lambda bi, c: (0, 0)),
            pl.BlockSpec((1, 2 * qw), lambda bi, c: (0, 0)),
            pl.BlockSpec((1, vw), lambda bi, c: (0, 0)),
        ],
        out_specs=pl.BlockSpec((1, chunk, vw), lambda bi, c: (bi, c, 0)),
        out_shape=jax.ShapeDtypeStruct((b, s, vw), BF16),
        scratch_shapes=[
            pltpu.VMEM((MLSTM_HEADS, chunk + 2 * SUBLANES, MLSTM_QK_DIM), F32),
            pltpu.VMEM((MLSTM_HEADS, chunk + 2 * SUBLANES, MLSTM_QK_DIM), F32),
            pltpu.VMEM((MLSTM_HEADS, chunk, MLSTM_QK_DIM), F32),
            pltpu.VMEM((MLSTM_HEADS, chunk, MLSTM_QK_DIM), F32),
            pltpu.VMEM((MLSTM_HEADS, MLSTM_QK_DIM, MLSTM_V_DIM), F32),
            pltpu.VMEM((MLSTM_HEADS, 1, MLSTM_QK_DIM), F32),
            pltpu.VMEM((MLSTM_HEADS, 1, LANES), F32),
        ],
        compiler_params=pltpu.CompilerParams(
            dimension_semantics=("parallel", "arbitrary"),
            vmem_limit_bytes=VMEM_LIMIT),
        name="mlstm",
    )(z3, z3, z3, z3, gates, gates_t, bias_col, bias_row, conv_w8, conv_b, gnorm)


def _cross_delta(x, g_ref, wq_ref, kv_ref, wo_ref):
    hc = _rms(x, g_ref[...]).astype(BF16)
    cq = (_dot(hc, wq_ref[...]) * (CROSS_HEAD_DIM ** -0.5)).astype(BF16)
    outs = []
    for hh in range(CROSS_HEADS):
        lo = hh * CROSS_HEAD_DIM
        qh = cq[:, lo:lo + CROSS_HEAD_DIM]
        kh = kv_ref[0, :, lo:lo + CROSS_HEAD_DIM]
        vh = kv_ref[0, :, CROSS_WIDTH + lo:CROSS_WIDTH + lo + CROSS_HEAD_DIM]
        s = _dot_nt(qh, kh)
        p = jnp.exp(s - jnp.max(s, axis=-1, keepdims=True))
        l = jnp.sum(p, axis=-1, keepdims=True)
        outs.append((_dot(p.astype(BF16), vh) * (1.0 / l)).astype(BF16))
    co = jnp.concatenate(outs, axis=1)
    return _dot(co, wo_ref[...])


def _merge_kernel(att_ref, hm_ref, ga_ref, gm_ref, x_ref, wa_ref, wm_ref, wo_ref,
                  gc_ref, wcq_ref, kv_ref, wco_ref, o_ref):
    j = pl.program_id(1)

    @pl.when(j == 0)
    def _():
        o_ref[...] = x_ref[...]

    a = _dot(att_ref[...], wa_ref[...])
    bm = _dot(hm_ref[...], wm_ref[...])
    y = _sigmoid(ga_ref[...].astype(F32)) * a + _sigmoid(gm_ref[...].astype(F32)) * bm
    o_ref[...] += _dot(y.astype(BF16), wo_ref[...])

    @pl.when(j == pl.num_programs(1) - 1)
    def _():
        x1 = o_ref[...]
        o_ref[...] = x1 + _cross_delta(x1, gc_ref, wcq_ref, kv_ref, wco_ref)


def _merge(att2d, hm2d, z2d, x2d, wa, wm, wo, gc, wcq, ckv, wco, seq, tm=512, tn=1024):
    t = x2d.shape[0]
    nj = D_MODEL // tn
    n_mem = ckv.shape[1]
    per_batch = seq // tm
    return pl.pallas_call(
        _merge_kernel,
        grid=(t // tm, nj),
        in_specs=[
            pl.BlockSpec((tm, ATTN_WIDTH), lambda i, j: (i, 0)),
            pl.BlockSpec((tm, MLSTM_WIDTH), lambda i, j: (i, 0)),
            pl.BlockSpec((tm, tn), lambda i, j: (i, COL_GA // tn + j)),
            pl.BlockSpec((tm, tn), lambda i, j: (i, COL_GM // tn + j)),
            pl.BlockSpec((tm, D_MODEL), lambda i, j: (i, 0)),
            pl.BlockSpec((ATTN_WIDTH, tn), lambda i, j: (0, j)),
            pl.BlockSpec((MLSTM_WIDTH, tn), lambda i, j: (0, j)),
            pl.BlockSpec((tn, D_MODEL), lambda i, j: (j, 0)),
            pl.BlockSpec((1, D_MODEL), lambda i, j: (0, 0)),
            pl.BlockSpec((D_MODEL, CROSS_WIDTH), lambda i, j: (0, 0)),
            pl.BlockSpec((1, n_mem, 2 * CROSS_WIDTH), lambda i, j: (i // per_batch, 0, 0)),
            pl.BlockSpec((CROSS_WIDTH, D_MODEL), lambda i, j: (0, 0)),
        ],
        out_specs=pl.BlockSpec((tm, D_MODEL), lambda i, j: (i, 0)),
        out_shape=jax.ShapeDtypeStruct((t, D_MODEL), F32),
        compiler_params=pltpu.CompilerParams(
            dimension_semantics=("parallel", "arbitrary"), vmem_limit_bytes=VMEM_LIMIT),
        name="merge",
    )(att2d, hm2d, z2d, z2d, x2d, wa, wm, wo, gc, wcq, ckv, wco)


def _mlp_kernel(x_ref, g_ref, wu_ref, wd_ref, gf_ref, o_ref, h_scr, acc, *, final_norm):
    j = pl.program_id(1)

    @pl.when(j == 0)
    def _():
        x = x_ref[...]
        h_scr[...] = _rms(x, g_ref[...]).astype(BF16)
        acc[...] = x

    u = jnp.square(jnp.maximum(_dot(h_scr[...], wu_ref[...]), 0.0)).astype(BF16)
    acc[...] += _dot(u, wd_ref[...])

    @pl.when(j == pl.num_programs(1) - 1)
    def _():
        if final_norm:
            o_ref[...] = _rms(acc[...], gf_ref[...])
        else:
            o_ref[...] = acc[...]


def _mlp(x2d, g, wu, wd, gf, final_norm, tm=512, tf=1024):
    t = x2d.shape[0]
    kern = functools.partial(_mlp_kernel, final_norm=final_norm)
    return pl.pallas_call(
        kern,
        grid=(t // tm, D_FF // tf),
        in_specs=[
            pl.BlockSpec((tm, D_MODEL), lambda i, j: (i, 0)),
            pl.BlockSpec((1, D_MODEL), lambda i, j: (0, 0)),
            pl.BlockSpec((D_MODEL, tf), lambda i, j: (0, j)),
            pl.BlockSpec((tf, D_MODEL), lambda i, j: (j, 0)),
            pl.BlockSpec((1, D_MODEL), lambda i, j: (0, 0)),
        ],
        out_specs=pl.BlockSpec((tm, D_MODEL), lambda i, j: (i, 0)),
        out_shape=jax.ShapeDtypeStruct((t, D_MODEL), F32),
        scratch_shapes=[pltpu.VMEM((tm, D_MODEL), BF16), pltpu.VMEM((tm, D_MODEL), F32)],
        compiler_params=pltpu.CompilerParams(
            dimension_semantics=("parallel", "arbitrary"), vmem_limit_bytes=VMEM_LIMIT),
        name="mlp",
    )(x2d, g, wu, wd, gf)


def _alibi_terms():
    slopes = 2.0 ** (-8.0 * np.arange(1, ATTN_HEADS + 1, dtype=np.float64) / ATTN_HEADS)
    rem = slopes * LOG2E
    terms = []
    for _ in range(ALIBI_TERMS):
        t = rem.astype(np.float32).astype(ml_dtypes.bfloat16).astype(np.float64)
        terms.append(t)
        rem = rem - t
    return np.stack(terms, axis=1).reshape(-1).astype(np.float32)


def _pad_rows(a, rows):
    return jnp.pad(a, ((0, rows - a.shape[0]), (0, 0)))


def kernel(x, mem, norm_mix, w_in, b_igate, b_fgate, conv_w, conv_b, lam_q1, lam_k1, lam_q2, lam_k2, attn_norm, mlstm_norm, w_attn_br, w_mlstm_br, w_out, norm_cross, norm_mem, w_cq, w_ckv, w_co, norm_mlp, w_up, w_down, norm_final):
    b, s, _ = x.shape
    t = b * s
    tq = 256
    cs_terms = jnp.asarray(_alibi_terms())
    x2d = x.reshape(t, D_MODEL)
    for l in range(DEPTH):
        lam_init = 0.8 - 0.6 * math.exp(-0.3 * l)
        w = w_in[l]
        w_a, w_b, w_gate = _wprep(w.T)

        z2d, gates, qt, vt = _inproj(x2d, norm_mix[l][None, :], w_a, w_b, w_gate, tq)
        z3 = z2d.reshape(b, s, IN_MAIN)

        qt5 = qt.reshape(b, s // tq, ATTN_HEADS, 2 * ATTN_QK_DIM, tq)
        vt5 = vt.reshape(b, s // tq, ATTN_HEADS, ATTN_V_DIM, tq)
        lamv = _pad_rows(jnp.pad(jnp.stack([lam_q1[l], lam_k1[l], lam_q2[l], lam_k2[l]]),
                                 ((0, 0), (0, LANES - ATTN_QK_DIM))), SUBLANES)
        att, w_out_b, w_abr_b, w_mbr_b, w_up_b, w_down_b = _attn(
            z3, qt5, vt5, cs_terms, lamv, attn_norm[l][None, :],
            (w_out[l], w_attn_br[l], w_mlstm_br[l], w_up[l], w_down[l]), lam_init, tq=tq)

        gates_t = gates[:, :SUBLANES].reshape(b, s, SUBLANES).transpose(0, 2, 1)
        gate_bias = jnp.concatenate([b_igate[l], b_fgate[l]])
        bias_col = jnp.pad(gate_bias, (0, LANES - gate_bias.shape[0]))[None, :]
        bias_row = jnp.broadcast_to(gate_bias[:, None], (SUBLANES, LANES))
        hm = _mlstm(z3, gates, gates_t, bias_col, bias_row, _pad_rows(conv_w[l], SUBLANES),
                    conv_b[l][None, :], mlstm_norm[l][None, :])

        ckv = _memkv(mem, norm_mem[l][None, :], w_ckv[l])
        x2d = _merge(att.reshape(t, ATTN_WIDTH), hm.reshape(t, MLSTM_WIDTH), z2d, x2d,
                     w_abr_b, w_mbr_b, w_out_b, norm_cross[l][None, :], w_cq[l].astype(BF16), ckv,
                     w_co[l].astype(BF16), s)

        x2d = _mlp(x2d, norm_mlp[l][None, :], w_up_b, w_down_b,
                   norm_final[None, :], final_norm=(l == DEPTH - 1))
    return x2d.reshape(b, s, D_MODEL)
```

```python
import functools
import math

import ml_dtypes
import numpy as np
import jax
import jax.numpy as jnp
from jax import lax
from jax.experimental import pallas as pl
from jax.experimental.pallas import tpu as pltpu

F32 = jnp.float32
BF16 = jnp.bfloat16

D_MODEL = 2048
DEPTH = 1
ATTN_HEADS = 8
ATTN_QK_DIM = 64
ATTN_V_DIM = 128
ATTN_WIDTH = ATTN_HEADS * ATTN_V_DIM
MLSTM_HEADS = 4
MLSTM_QK_DIM = 128
MLSTM_V_DIM = 256
MLSTM_QK_WIDTH = MLSTM_HEADS * MLSTM_QK_DIM
MLSTM_WIDTH = MLSTM_HEADS * MLSTM_V_DIM
CONV_WIDTH = 4
CROSS_HEADS = 4
CROSS_HEAD_DIM = 128
CROSS_WIDTH = CROSS_HEADS * CROSS_HEAD_DIM
D_FF = 4 * D_MODEL
EPS = 1e-6
LANES = 128
SUBLANES = 8
NEG_BIG = -1e30
LOG2E = 1.4426950408889634
ALIBI_TERMS = 3

COL_AK = 0
COL_MQ = 1024
COL_MK = 1536
COL_MV = 2048
COL_MO = 3072
COL_GA = 4096
COL_GM = 6144
IN_MAIN = 8192
W_AQ_LO, W_AQ_HI = 0, 1024
W_AV_LO = 2048
GATE_LO = 6144
GATE_HI = 6152

VMEM_LIMIT = 56 * 1024 * 1024


def _rms(x, g):
    ms = jnp.mean(x * x, axis=-1, keepdims=True)
    return x * lax.rsqrt(ms + EPS) * g


def _sigmoid(x):
    return 0.5 * jnp.tanh(0.5 * x) + 0.5


def _log_sigmoid(x):
    return jnp.minimum(x, 0.0) - jnp.log(1.0 + jnp.exp(-jnp.abs(x)))


def _dot(a, b):
    return jnp.dot(a, b, preferred_element_type=F32)


def _dot_nt(a, b):
    return lax.dot_general(a, b, (((1,), (1,)), ((), ())), preferred_element_type=F32)


def _memkv_kernel(mem_ref, g_ref, w_ref, o_ref):
    mn = _rms(mem_ref[0], g_ref[...]).astype(BF16)
    o_ref[0] = _dot(mn, w_ref[...].astype(BF16)).astype(BF16)


def _memkv(mem, g, w_ckv):
    b, n_mem, _ = mem.shape
    return pl.pallas_call(
        _memkv_kernel,
        grid=(b,),
        in_specs=[
            pl.BlockSpec((1, n_mem, D_MODEL), lambda i: (i, 0, 0)),
            pl.BlockSpec((1, D_MODEL), lambda i: (0, 0)),
            pl.BlockSpec((D_MODEL, 2 * CROSS_WIDTH), lambda i: (0, 0)),
        ],
        out_specs=pl.BlockSpec((1, n_mem, 2 * CROSS_WIDTH), lambda i: (i, 0, 0)),
        out_shape=jax.ShapeDtypeStruct((b, n_mem, 2 * CROSS_WIDTH), BF16),
        compiler_params=pltpu.CompilerParams(
            dimension_semantics=("arbitrary",), vmem_limit_bytes=VMEM_LIMIT),
        name="memkv",
    )(mem, g, w_ckv)


def _wprep_kernel(a_ref, nxt_ref, wa_out, wb_out, wg_out, *, na):
    k = pl.program_id(0)
    ng = GATE_HI - GATE_LO

    @pl.when(k < na)
    def _():
        scale = jnp.where(k == W_AQ_LO // a_ref.shape[0], ATTN_QK_DIM ** -0.5 * LOG2E, 1.0)
        wa_out[...] = (a_ref[...] * scale).astype(BF16)

    @pl.when(k >= na)
    def _():
        wb_out[...] = jnp.concatenate([a_ref[pl.ds(ng, a_ref.shape[0] - ng), :], nxt_ref[0]],
                                      axis=0).astype(BF16)

    @pl.when(k == na)
    def _():
        pad = jnp.zeros((LANES - ng, D_MODEL), F32)
        wg_out[...] = jnp.concatenate([a_ref[pl.ds(0, ng), :], pad], axis=0).astype(BF16)


def _wprep(wt, tn=1024):
    rows = wt.shape[0]
    ng = GATE_HI - GATE_LO
    assert ng == SUBLANES and GATE_LO % tn == 0 and (rows - GATE_HI) % tn == 0
    assert W_AQ_LO == 0 and W_AQ_HI == tn
    na = GATE_LO // tn
    nb = (rows - GATE_HI) // tn
    groups = wt.reshape(rows // ng, ng, D_MODEL)
    per_block = tn // ng
    return pl.pallas_call(
        functools.partial(_wprep_kernel, na=na),
        grid=(na + nb,),
        in_specs=[
            pl.BlockSpec((tn, D_MODEL), lambda k: (k, 0)),
            pl.BlockSpec((1, ng, D_MODEL), lambda k: ((k + 1) * per_block, 0, 0)),
        ],
        out_specs=[
            pl.BlockSpec((tn, D_MODEL), lambda k: (jnp.minimum(k, na - 1), 0)),
            pl.BlockSpec((tn, D_MODEL), lambda k: (jnp.maximum(k - na, 0), 0)),
            pl.BlockSpec((LANES, D_MODEL), lambda k: (0, 0)),
        ],
        out_shape=[
            jax.ShapeDtypeStruct((GATE_LO, D_MODEL), BF16),
            jax.ShapeDtypeStruct((rows - GATE_HI, D_MODEL), BF16),
            jax.ShapeDtypeStruct((LANES, D_MODEL), BF16),
        ],
        compiler_params=pltpu.CompilerParams(
            dimension_semantics=("arbitrary",), vmem_limit_bytes=VMEM_LIMIT),
        name="wprep",
    )(wt, groups)


def _inproj_kernel(x_ref, g_ref, wa_ref, wb_ref, wg_ref, z_ref, gate_ref, qt_ref, vt_ref, h_scr,
                   *, na, jq, jv):
    j = pl.program_id(1)

    @pl.when(j == 0)
    def _():
        h = _rms(x_ref[...], g_ref[...]).astype(BF16)
        h_scr[...] = h
        gate_ref[...] = _dot_nt(h, wg_ref[...])

    @pl.when((j < na) & (j != jq) & (j != jv))
    def _():
        z_ref[...] = _dot_nt(h_scr[...], wa_ref[...]).astype(BF16)

    def transposed(out_ref):
        xt = _dot_nt(wa_ref[...], h_scr[...]).astype(BF16)
        tile = out_ref.shape[2]
        for n in range(out_ref.shape[0]):
            out_ref[n] = xt[:, n * tile:(n + 1) * tile]

    @pl.when(j == jq)
    def _():
        transposed(qt_ref)

    @pl.when(j == jv)
    def _():
        transposed(vt_ref)

    @pl.when(j >= na)
    def _():
        z_ref[...] = _dot_nt(h_scr[...], wb_ref[...]).astype(BF16)


def _inproj(x2d, g, w_a, w_b, w_gate, tkv, tm=1024, tn=1024):
    t = x2d.shape[0]
    na = w_a.shape[0] // tn
    n = w_a.shape[0] + w_b.shape[0]
    ni, nj = t // tm, n // tn
    assert tn == ATTN_WIDTH and W_AQ_LO % tn == 0 and W_AQ_HI - W_AQ_LO == tn and W_AV_LO % tn == 0
    jq, jv = W_AQ_LO // tn, W_AV_LO // tn
    assert jq < jv

    def z_tile(i, j):
        skipped = (j >= jq).astype(jnp.int32) + (j >= jv).astype(jnp.int32)
        return i, jnp.maximum(j - skipped, 0)

    return pl.pallas_call(
        functools.partial(_inproj_kernel, na=na, jq=jq, jv=jv),
        grid=(ni, nj),
        in_specs=[
            pl.BlockSpec((tm, D_MODEL), lambda i, j: (i, 0)),
            pl.BlockSpec((1, D_MODEL), lambda i, j: (0, 0)),
            pl.BlockSpec((tn, D_MODEL), lambda i, j: (jnp.minimum(j, na - 1), 0)),
            pl.BlockSpec((tn, D_MODEL), lambda i, j: (jnp.maximum(j - na, 0), 0)),
            pl.BlockSpec((LANES, D_MODEL), lambda i, j: (0, 0)),
        ],
        out_specs=[
            pl.BlockSpec((tm, tn), z_tile),
            pl.BlockSpec((tm, LANES), lambda i, j: (i, 0)),
            pl.BlockSpec((tm // tkv, ATTN_WIDTH, tkv), lambda i, j: (i, 0, 0)),
            pl.BlockSpec((tm // tkv, ATTN_WIDTH, tkv), lambda i, j: (i, 0, 0)),
        ],
        out_shape=[
            jax.ShapeDtypeStruct((t, n - 2 * tn), BF16),
            jax.ShapeDtypeStruct((t, LANES), F32),
            jax.ShapeDtypeStruct((t // tkv, ATTN_WIDTH, tkv), BF16),
            jax.ShapeDtypeStruct((t // tkv, ATTN_WIDTH, tkv), BF16),
        ],
        scratch_shapes=[pltpu.VMEM((tm, D_MODEL), BF16)],
        compiler_params=pltpu.CompilerParams(
            dimension_semantics=("arbitrary", "arbitrary"), vmem_limit_bytes=VMEM_LIMIT),
        name="inproj",
    )(x2d, g, w_a, w_b, w_gate)


def _attn_kernel(cs_ref, lam_ref, qt_ref, k_ref, vt_ref, gain_ref, *refs, ncast, tq, lam_init):
    cast_in = refs[:ncast]
    o_ref = refs[ncast]
    cast_out = refs[ncast + 1:2 * ncast + 1]
    qs_ref, kf_ref, acc_ref, m_ref = refs[2 * ncast + 1:]
    tk = tq
    nfeat = ALIBI_TERMS
    dq = 2 * ATTN_QK_DIM
    dv = ATTN_V_DIM
    i = pl.program_id(1)
    cs = [[cs_ref[nfeat * h + t] for t in range(nfeat)] for h in range(ATTN_HEADS)]
    cs_tot = [sum(c[1:], c[0]) for c in cs]

    @pl.when(i == 0)
    def _():
        klane = lax.broadcasted_iota(jnp.int32, (tk, LANES), 1)
        krow = lax.broadcasted_iota(jnp.int32, (tk, LANES), 0).astype(F32)
        frow = lax.broadcasted_iota(jnp.int32, (LANES, 2 * tq), 0)
        fcol = lax.broadcasted_iota(jnp.int32, (LANES, 2 * tq), 1)
        fcol = jnp.where(fcol >= tq, fcol - tq, fcol).astype(F32)
        for h in range(ATTN_HEADS):
            kf = jnp.where(klane < nfeat, krow, 0.0)
            qf = jnp.where((frow >= nfeat) & (frow < 2 * nfeat), -fcol, 0.0)
            for t in range(nfeat):
                kf = jnp.where(klane == nfeat + t, cs[h][t], kf)
                qf = jnp.where(frow == t, cs[h][t], qf)
            kf_ref[h] = kf.astype(BF16)
            qs_ref[h, pl.ds(dq, LANES), :] = qf.astype(BF16)

    row = lax.broadcasted_iota(jnp.int32, (dq, tq), 0)
    for h in range(ATTN_HEADS):
        qt = qt_ref[0, 0, h]
        zero = jnp.zeros_like(qt)
        qs_ref[h, pl.ds(0, dq), pl.ds(0, tq)] = jnp.where(row < ATTN_QK_DIM, qt, zero)
        qs_ref[h, pl.ds(0, dq), pl.ds(tq, tq)] = jnp.where(row >= ATTN_QK_DIM, qt, zero)
    acc_ref[...] = jnp.zeros_like(acc_ref)
    m_ref[...] = jnp.full_like(m_ref, NEG_BIG)

    orow = lax.broadcasted_iota(jnp.int32, (2 * SUBLANES, tk), 0)
    ones_blk = jnp.where(orow == 0, 1.0, 0.0).astype(BF16)

    def scores(h, j):
        k = k_ref[0, pl.ds(pl.multiple_of(j * tk, tk), tk), pl.ds(h * dq, dq)]
        kx = jnp.concatenate([k, kf_ref[h]], axis=1)
        return _dot(kx, qs_ref[h])

    def softmax_pv(h, j, t, diagonal):
        if diagonal:
            krow = lax.broadcasted_iota(jnp.int32, (tk, 2 * tq), 0)
            qcol = lax.broadcasted_iota(jnp.int32, (tk, 2 * tq), 1)
            qcol = jnp.where(qcol >= tq, qcol - tq, qcol)
            t = jnp.where(krow <= qcol, t, NEG_BIG)
        soff = cs_tot[h] * ((j - i) * tk).astype(F32)
        m = m_ref[h]
        m_new = jnp.maximum(m, jnp.max(t, axis=0, keepdims=True) + soff)
        alpha = jnp.exp2(m - m_new)
        p = jnp.exp2(t - (m_new - soff)).astype(BF16)
        vx = jnp.concatenate([vt_ref[0, j, h], ones_blk], axis=0)
        acc_ref[h] = alpha * acc_ref[h] + _dot(vx, p)
        m_ref[h] = m_new

    def run(units):
        ahead = 2
        pending = [scores(*u[:2]) for u in units[:ahead]]
        for n, (h, j, diagonal) in enumerate(units):
            if n + ahead < len(units):
                pending.append(scores(*units[n + ahead][:2]))
            softmax_pv(h, j, pending.pop(0), diagonal)

    def tile_units(j, diagonal):
        return [(h, j, diagonal) for h in range(ATTN_HEADS)]

    def body(jj, carry):
        run(tile_units(2 * jj, False) + tile_units(2 * jj + 1, False))
        return carry

    lax.fori_loop(0, i // 2, body, 0)

    @pl.when(i % 2 == 0)
    def _():
        run(tile_units(i, True))

    @pl.when(i % 2 == 1)
    def _():
        run(tile_units(i - 1, False) + tile_units(i, True))

    lv = lam_ref[...]
    d1 = jnp.sum(lv[0:1] * lv[1:2], axis=-1, keepdims=True)
    d2 = jnp.sum(lv[2:3] * lv[3:4], axis=-1, keepdims=True)
    lam = jnp.exp(d1) - jnp.exp(d2) + lam_init
    for h in range(ATTN_HEADS):
        l = acc_ref[h, pl.ds(dv, 1), :]
        rl = 1.0 / l
        out = (acc_ref[h, pl.ds(0, dv), pl.ds(0, tq)] * rl[:, :tq]
               - lam * (acc_ref[h, pl.ds(0, dv), pl.ds(tq, tq)] * rl[:, tq:]))
        ms = jnp.mean(out * out, axis=0, keepdims=True)
        on = out * lax.rsqrt(ms + EPS)
        o_ref[0, :, pl.ds(h * dv, dv)] = (on.T * gain_ref[...] * (1.0 - lam_init)).astype(BF16)

    for src, dst in zip(cast_in, cast_out):
        dst[...] = src[...].astype(BF16)


def _attn(z3, qt5, vt5, cs_terms, lamv, gain, cast_weights, lam_init, tq=256):
    b, s, _ = z3.shape
    nq = s // tq
    steps = b * nq
    cast_specs = [pl.BlockSpec((w.shape[0] // steps, w.shape[1]), lambda bi, i: (bi * nq + i, 0))
                  for w in cast_weights]
    kern = functools.partial(_attn_kernel, ncast=len(cast_weights), tq=tq, lam_init=lam_init)
    width = ATTN_HEADS * 2 * ATTN_QK_DIM
    return pl.pallas_call(
        kern,
        grid=(b, nq),
        in_specs=[
            pl.BlockSpec(memory_space=pltpu.SMEM),
            pl.BlockSpec((SUBLANES, LANES), lambda bi, i: (0, 0)),
            pl.BlockSpec((1, 1, ATTN_HEADS, 2 * ATTN_QK_DIM, tq), lambda bi, i: (bi, i, 0, 0, 0)),
            pl.BlockSpec((1, s, width), lambda bi, i: (bi, 0, COL_AK // width)),
            pl.BlockSpec((1, nq, ATTN_HEADS, ATTN_V_DIM, tq), lambda bi, i: (bi, 0, 0, 0, 0)),
            pl.BlockSpec((1, ATTN_V_DIM), lambda bi, i: (0, 0)),
        ] + cast_specs,
        out_specs=[pl.BlockSpec((1, tq, ATTN_WIDTH), lambda bi, i: (bi, i, 0))] + cast_specs,
        out_shape=[jax.ShapeDtypeStruct((b, s, ATTN_WIDTH), BF16)]
        + [jax.ShapeDtypeStruct(w.shape, BF16) for w in cast_weights],
        scratch_shapes=[
            pltpu.VMEM((ATTN_HEADS, 2 * ATTN_QK_DIM + LANES, 2 * tq), BF16),
            pltpu.VMEM((ATTN_HEADS, tq, LANES), BF16),
            pltpu.VMEM((ATTN_HEADS, ATTN_V_DIM + 2 * SUBLANES, 2 * tq), F32),
            pltpu.VMEM((ATTN_HEADS, 1, 2 * tq), F32),
        ],
        compiler_params=pltpu.CompilerParams(
            dimension_semantics=("parallel", "arbitrary"),
            vmem_limit_bytes=VMEM_LIMIT),
        name="attn",
    )(cs_terms, lamv, qt5, z3, vt5, gain, *cast_weights)


def _split3(x):
    hi = x.astype(BF16)
    r = x - hi.astype(F32)
    mid = r.astype(BF16)
    lo = (r - mid.astype(F32)).astype(BF16)
    return hi, mid, lo


def _mlstm_kernel(uq_ref, uk_ref, v_ref, mo_ref, gcol_ref, grow_ref, bcol_ref, brow_ref,
                  cw_ref, cb_ref, gn_ref, o_ref, extq, extk, dstq, dstk, c_scr, n_scr, m_scr,
                  *, chunk):
    L = chunk
    nh = MLSTM_HEADS
    dk = MLSTM_QK_DIM
    dv = MLSTM_V_DIM
    heads = range(nh)

    @pl.when(pl.program_id(1) == 0)
    def _():
        extq[:, pl.ds(0, SUBLANES), :] = jnp.zeros((nh, SUBLANES, dk), F32)
        extk[:, pl.ds(0, SUBLANES), :] = jnp.zeros((nh, SUBLANES, dk), F32)
        c_scr[...] = jnp.zeros_like(c_scr)
        n_scr[...] = jnp.zeros_like(n_scr)
        m_scr[...] = jnp.zeros_like(m_scr)

    def conv_silu(u_ref, ext, dst, col0, scale):
        n8 = L // SUBLANES
        first = SUBLANES - (CONV_WIDTH - 1)
        outs = []
        for h in heads:
            cols = pl.ds(col0 + h * dk, dk)
            ext[h, pl.ds(SUBLANES, L), :] = u_ref[0, :, pl.ds(h * dk, dk)].astype(F32)
            w = cw_ref[:, cols]
            bias = cb_ref[:, cols]
            rows = [ext[h, pl.ds(first + s, n8, stride=SUBLANES), :]
                    for s in range(SUBLANES + CONV_WIDTH - 1)]
            for r in range(SUBLANES):
                y = bias
                for tap in range(CONV_WIDTH):
                    y = y + rows[r + tap] * w[tap:tap + 1]
                y = y * _sigmoid(y)
                dst[h, pl.ds(r, n8, stride=SUBLANES), :] = y if scale is None else y * scale
            ext[h, pl.ds(0, SUBLANES), :] = ext[h, pl.ds(L, SUBLANES), :]
            outs.append(dst[h])
        return outs

    q = conv_silu(uq_ref, extq, dstq, 0, None)
    k = conv_silu(uk_ref, extk, dstk, MLSTM_QK_WIDTH, dk ** -0.5)
    qb = [x.astype(BF16) for x in q]
    kb = [x.astype(BF16) for x in k]
    vb = [v_ref[0, :, pl.ds(h * dv, dv)] for h in heads]

    g_c = gcol_ref[...] + bcol_ref[...]
    g_r = grow_ref[0] + brow_ref[:, 0:1]
    lf_c = _log_sigmoid(g_c)
    lf_r = _log_sigmoid(g_r)

    r_i = lax.broadcasted_iota(jnp.int32, (L, L), 0)
    c_i = lax.broadcasted_iota(jnp.int32, (L, L), 1)
    causal = c_i <= r_i
    tri = jnp.where(causal, 1.0, 0.0).astype(BF16)
    tri_t = jnp.where(r_i <= c_i, 1.0, 0.0).astype(BF16)
    hi, mid, lo = _split3(lf_c)
    b_c = _dot(tri, hi) + _dot(tri, mid) + _dot(tri, lo)
    hi, mid, lo = _split3(lf_r)
    b_r = _dot(hi, tri_t) + _dot(mid, tri_t) + _dot(lo, tri_t)
    g_sum = jnp.sum(lf_r, axis=1, keepdims=True)

    lane = lax.broadcasted_iota(jnp.int32, (L, LANES), 1)

    def col(x, idx):
        return jnp.sum(jnp.where(lane == idx, x, 0.0), axis=1, keepdims=True)

    i_col = [col(g_c, h) for h in heads]
    b_col = [col(b_c, nh + h) for h in heads]
    i_row = [g_r[h:h + 1, :] for h in heads]
    b_row = [b_r[nh + h:nh + h + 1, :] for h in heads]
    g_tot = [g_sum[nh + h:nh + h + 1, :] for h in heads]

    c_prev = [c_scr[h] for h in heads]
    n_prev = [n_scr[h] for h in heads]
    m_prev = [m_scr[h][:, 0:1] for h in heads]

    s_qk = [_dot_nt(qb[h], kb[h]) for h in heads]
    inter = [_dot(qb[h], c_prev[h].astype(BF16)) for h in heads]
    m_loc = [jnp.max(g_tot[h] - b_row[h] + i_row[h], axis=1, keepdims=True) for h in heads]
    kw = [k[h] * jnp.exp(g_tot[h] - b_col[h] + i_col[h] - m_loc[h]) for h in heads]
    c_loc = [_dot(kw[h].T.astype(BF16), vb[h]) for h in heads]

    for h in heads:
        d = jnp.where(causal, b_col[h] - b_row[h] + i_row[h], NEG_BIG)
        m_inter = b_col[h] + m_prev[h]
        m_j = jnp.maximum(m_inter, jnp.max(d, axis=1, keepdims=True))
        w_inter = jnp.exp(m_inter - m_j)
        p = s_qk[h] * jnp.exp(d - m_j)
        num = w_inter * inter[h] + _dot(p.astype(BF16), vb[h])
        den = (w_inter * jnp.sum(q[h] * n_prev[h], axis=1, keepdims=True)
               + jnp.sum(p, axis=1, keepdims=True))
        hval = num * (1.0 / jnp.maximum(jnp.abs(den), jnp.exp(-m_j)))
        hn = _rms(hval, gn_ref[:, pl.ds(h * dv, dv)])
        gate = _sigmoid(mo_ref[0, :, pl.ds(h * dv, dv)].astype(F32))
        o_ref[0, :, pl.ds(h * dv, dv)] = (hn * gate).astype(BF16)

    for h in heads:
        n_loc = jnp.sum(kw[h], axis=0, keepdims=True)
        m_new = jnp.maximum(g_tot[h] + m_prev[h], m_loc[h])
        a_old = jnp.exp(g_tot[h] + m_prev[h] - m_new)
        a_new = jnp.exp(m_loc[h] - m_new)
        c_scr[h] = a_old * c_prev[h] + a_new * c_loc[h]
        n_scr[h] = a_old * n_prev[h] + a_new * n_loc
        m_scr[h] = jnp.broadcast_to(m_new, (1, LANES))


def _mlstm(z3, gates, gates_t, bias_col, bias_row, conv_w8, conv_b, gnorm, chunk=256):
    b, s, _ = z3.shape
    nc = s // chunk
    kern = functools.partial(_mlstm_kernel, chunk=chunk)
    qw = MLSTM_QK_WIDTH
    vw = MLSTM_WIDTH
    return pl.pallas_call(
        kern,
        grid=(b, nc),
        in_specs=[
            pl.BlockSpec((1, chunk, qw), lambda bi, c: (bi, c, COL_MQ // qw)),
            pl.BlockSpec((1, chunk, qw), lambda bi, c: (bi, c, COL_MK // qw)),
            pl.BlockSpec((1, chunk, vw), lambda bi, c: (bi, c, COL_MV // vw)),
            pl.BlockSpec((1, chunk, vw), lambda bi, c: (bi, c, COL_MO // vw)),
            pl.BlockSpec((chunk, LANES), lambda bi, c: (bi * nc + c, 0)),
            pl.BlockSpec((1, SUBLANES, chunk), lambda bi, c: (bi, 0, c)),
            pl.BlockSpec((1, LANES), lambda bi, c: (0, 0)),
            pl.BlockSpec((SUBLANES, LANES), lambda bi, c: (0, 0)),
            pl.BlockSpec((SUBLANES, 2 * qw), lambda bi, c: (0, 0)),
            pl.BlockSpec((1, 2 * qw), lambda bi, c: (0, 0)),
            pl.BlockSpec((1, vw), lambda bi, c: (0, 0)),
        ],
        out_specs=pl.BlockSpec((1, chunk, vw), lambda bi, c: (bi, c, 0)),
        out_shape=jax.ShapeDtypeStruct((b, s, vw), BF16),
        scratch_shapes=[
            pltpu.VMEM((MLSTM_HEADS, chunk + 2 * SUBLANES, MLSTM_QK_DIM), F32),
            pltpu.VMEM((MLSTM_HEADS, chunk + 2 * SUBLANES, MLSTM_QK_DIM), F32),
            pltpu.VMEM((MLSTM_HEADS, chunk, MLSTM_QK_DIM), F32),
            pltpu.VMEM((MLSTM_HEADS, chunk, MLSTM_QK_DIM), F32),
            pltpu.VMEM((MLSTM_HEADS, MLSTM_QK_DIM, MLSTM_V_DIM), F32),
            pltpu.VMEM((MLSTM_HEADS, 1, MLSTM_QK_DIM), F32),
            pltpu.VMEM((MLSTM_HEADS, 1, LANES), F32),
        ],
        compiler_params=pltpu.CompilerParams(
            dimension_semantics=("parallel", "arbitrary"),
            vmem_limit_bytes=VMEM_LIMIT),
        name="mlstm",
    )(z3, z3, z3, z3, gates, gates_t, bias_col, bias_row, conv_w8, conv_b, gnorm)


def _cross_delta(x, g_ref, wq_ref, kv_ref, wo_ref):
    hc = _rms(x, g_ref[...]).astype(BF16)
    cq = (_dot(hc, wq_ref[...]) * (CROSS_HEAD_DIM ** -0.5)).astype(BF16)
    outs = []
    for hh in range(CROSS_HEADS):
        lo = hh * CROSS_HEAD_DIM
        qh = cq[:, lo:lo + CROSS_HEAD_DIM]
        kh = kv_ref[0, :, lo:lo + CROSS_HEAD_DIM]
        vh = kv_ref[0, :, CROSS_WIDTH + lo:CROSS_WIDTH + lo + CROSS_HEAD_DIM]
        s = _dot_nt(qh, kh)
        p = jnp.exp(s - jnp.max(s, axis=-1, keepdims=True))
        l = jnp.sum(p, axis=-1, keepdims=True)
        outs.append((_dot(p.astype(BF16), vh) * (1.0 / l)).astype(BF16))
    co = jnp.concatenate(outs, axis=1)
    return _dot(co, wo_ref[...])


def _merge_kernel(att_ref, hm_ref, ga_ref, gm_ref, x_ref, wa_ref, wm_ref, wo_ref,
                  gc_ref, wcq_ref, kv_ref, wco_ref, o_ref):
    j = pl.program_id(1)

    @pl.when(j == 0)
    def _():
        o_ref[...] = x_ref[...]

    a = _dot(att_ref[...], wa_ref[...])
    bm = _dot(hm_ref[...], wm_ref[...])
    y = _sigmoid(ga_ref[...].astype(F32)) * a + _sigmoid(gm_ref[...].astype(F32)) * bm
    o_ref[...] += _dot(y.astype(BF16), wo_ref[...])

    @pl.when(j == pl.num_programs(1) - 1)
    def _():
        x1 = o_ref[...]
        o_ref[...] = x1 + _cross_delta(x1, gc_ref, wcq_ref, kv_ref, wco_ref)


def _merge(att2d, hm2d, z2d, x2d, wa, wm, wo, gc, wcq, ckv, wco, seq, tm=512, tn=1024):
    t = x2d.shape[0]
    nj = D_MODEL // tn
    n_mem = ckv.shape[1]
    per_batch = seq // tm
    return pl.pallas_call(
        _merge_kernel,
        grid=(t // tm, nj),
        in_specs=[
            pl.BlockSpec((tm, ATTN_WIDTH), lambda i, j: (i, 0)),
            pl.BlockSpec((tm, MLSTM_WIDTH), lambda i, j: (i, 0)),
            pl.BlockSpec((tm, tn), lambda i, j: (i, COL_GA // tn + j)),
            pl.BlockSpec((tm, tn), lambda i, j: (i, COL_GM // tn + j)),
            pl.BlockSpec((tm, D_MODEL), lambda i, j: (i, 0)),
            pl.BlockSpec((ATTN_WIDTH, tn), lambda i, j: (0, j)),
            pl.BlockSpec((MLSTM_WIDTH, tn), lambda i, j: (0, j)),
            pl.BlockSpec((tn, D_MODEL), lambda i, j: (j, 0)),
            pl.BlockSpec((1, D_MODEL), lambda i, j: (0, 0)),
            pl.BlockSpec((D_MODEL, CROSS_WIDTH), lambda i, j: (0, 0)),
            pl.BlockSpec((1, n_mem, 2 * CROSS_WIDTH), lambda i, j: (i // per_batch, 0, 0)),
            pl.BlockSpec((CROSS_WIDTH, D_MODEL), lambda i, j: (0, 0)),
        ],
        out_specs=pl.BlockSpec((tm, D_MODEL), lambda i, j: (i, 0)),
        out_shape=jax.ShapeDtypeStruct((t, D_MODEL), F32),
        compiler_params=pltpu.CompilerParams(
            dimension_semantics=("parallel", "arbitrary"), vmem_limit_bytes=VMEM_LIMIT),
        name="merge",
    )(att2d, hm2d, z2d, z2d, x2d, wa, wm, wo, gc, wcq, ckv, wco)


def _mlp_kernel(x_ref, g_ref, wu_ref, wd_ref, gf_ref, o_ref, h_scr, *, final_norm):
    j = pl.program_id(1)

    @pl.when(j == 0)
    def _():
        x = x_ref[...]
        h_scr[...] = _rms(x, g_ref[...]).astype(BF16)
        o_ref[...] = x

    u = jnp.square(jnp.maximum(_dot(h_scr[...], wu_ref[...]), 0.0)).astype(BF16)
    o_ref[...] += _dot(u, wd_ref[...])

    if final_norm:
        @pl.when(j == pl.num_programs(1) - 1)
        def _():
            o_ref[...] = _rms(o_ref[...], gf_ref[...])


def _mlp(x2d, g, wu, wd, gf, final_norm, tm=1024, tf=512):
    t = x2d.shape[0]
    kern = functools.partial(_mlp_kernel, final_norm=final_norm)
    return pl.pallas_call(
        kern,
        grid=(t // tm, D_FF // tf),
        in_specs=[
            pl.BlockSpec((tm, D_MODEL), lambda i, j: (i, 0)),
            pl.BlockSpec((1, D_MODEL), lambda i, j: (0, 0)),
            pl.BlockSpec((D_MODEL, tf), lambda i, j: (0, j)),
            pl.BlockSpec((tf, D_MODEL), lambda i, j: (j, 0)),
            pl.BlockSpec((1, D_MODEL), lambda i, j: (0, 0)),
        ],
        out_specs=pl.BlockSpec((tm, D_MODEL), lambda i, j: (i, 0)),
        out_shape=jax.ShapeDtypeStruct((t, D_MODEL), F32),
        scratch_shapes=[pltpu.VMEM((tm, D_MODEL), BF16)],
        compiler_params=pltpu.CompilerParams(
            dimension_semantics=("parallel", "arbitrary"), vmem_limit_bytes=VMEM_LIMIT),
        name="mlp",
    )(x2d, g, wu, wd, gf)


def _alibi_terms():
    slopes = 2.0 ** (-8.0 * np.arange(1, ATTN_HEADS + 1, dtype=np.float64) / ATTN_HEADS)
    rem = slopes * LOG2E
    terms = []
    for _ in range(ALIBI_TERMS):
        t = rem.astype(np.float32).astype(ml_dtypes.bfloat16).astype(np.float64)
        terms.append(t)
        rem = rem - t
    return np.stack(terms, axis=1).reshape(-1).astype(np.float32)


def _pad_rows(a, rows):
    return jnp.pad(a, ((0, rows - a.shape[0]), (0, 0)))


def kernel(x, mem, norm_mix, w_in, b_igate, b_fgate, conv_w, conv_b, lam_q1, lam_k1, lam_q2, lam_k2, attn_norm, mlstm_norm, w_attn_br, w_mlstm_br, w_out, norm_cross, norm_mem, w_cq, w_ckv, w_co, norm_mlp, w_up, w_down, norm_final):
    b, s, _ = x.shape
    t = b * s
    tq = 256
    cs_terms = jnp.asarray(_alibi_terms())
    x2d = x.reshape(t, D_MODEL)
    for l in range(DEPTH):
        lam_init = 0.8 - 0.6 * math.exp(-0.3 * l)
        w = w_in[l]
        w_a, w_b, w_gate = _wprep(w.T)

        z2d, gates, qt, vt = _inproj(x2d, norm_mix[l][None, :], w_a, w_b, w_gate, tq)
        z3 = z2d.reshape(b, s, IN_MAIN)

        qt5 = qt.reshape(b, s // tq, ATTN_HEADS, 2 * ATTN_QK_DIM, tq)
        vt5 = vt.reshape(b, s // tq, ATTN_HEADS, ATTN_V_DIM, tq)
        lamv = _pad_rows(jnp.pad(jnp.stack([lam_q1[l], lam_k1[l], lam_q2[l], lam_k2[l]]),
                                 ((0, 0), (0, LANES - ATTN_QK_DIM))), SUBLANES)
        att, w_out_b, w_abr_b, w_mbr_b, w_up_b, w_down_b = _attn(
            z3, qt5, vt5, cs_terms, lamv, attn_norm[l][None, :],
            (w_out[l], w_attn_br[l], w_mlstm_br[l], w_up[l], w_down[l]), lam_init, tq=tq)

        gates_t = gates[:, :SUBLANES].reshape(b, s, SUBLANES).transpose(0, 2, 1)
        gate_bias = jnp.concatenate([b_igate[l], b_fgate[l]])
        bias_col = jnp.pad(gate_bias, (0, LANES - gate_bias.shape[0]))[None, :]
        bias_row = jnp.broadcast_to(gate_bias[:, None], (SUBLANES, LANES))
        hm = _mlstm(z3, gates, gates_t, bias_col, bias_row, _pad_rows(conv_w[l], SUBLANES),
                    conv_b[l][None, :], mlstm_norm[l][None, :])

        ckv = _memkv(mem, norm_mem[l][None, :], w_ckv[l])
        x2d = _merge(att.reshape(t, ATTN_WIDTH), hm.reshape(t, MLSTM_WIDTH), z2d, x2d,
                     w_abr_b, w_mbr_b, w_out_b, norm_cross[l][None, :], w_cq[l].astype(BF16), ckv,
                     w_co[l].astype(BF16), s)

        x2d = _mlp(x2d, norm_mlp[l][None, :], w_up_b, w_down_b,
                   norm_final[None, :], final_norm=(l == DEPTH - 1))
    return x2d.reshape(b, s, D_MODEL)
```

```python
import functools
import math

import ml_dtypes
import numpy as np
import jax
import jax.numpy as jnp
from jax import lax
from jax.experimental import pallas as pl
from jax.experimental.pallas import tpu as pltpu

F32 = jnp.float32
BF16 = jnp.bfloat16

D_MODEL = 2048
DEPTH = 1
ATTN_HEADS = 8
ATTN_QK_DIM = 64
ATTN_V_DIM = 128
ATTN_WIDTH = ATTN_HEADS * ATTN_V_DIM
MLSTM_HEADS = 4
MLSTM_QK_DIM = 128
MLSTM_V_DIM = 256
MLSTM_QK_WIDTH = MLSTM_HEADS * MLSTM_QK_DIM
MLSTM_WIDTH = MLSTM_HEADS * MLSTM_V_DIM
CONV_WIDTH = 4
CROSS_HEADS = 4
CROSS_HEAD_DIM = 128
CROSS_WIDTH = CROSS_HEADS * CROSS_HEAD_DIM
D_FF = 4 * D_MODEL
EPS = 1e-6
LANES = 128
SUBLANES = 8
NEG_BIG = -1e30
LOG2E = 1.4426950408889634
ALIBI_TERMS = 3

COL_AK = 0
COL_MQ = 1024
COL_MK = 1536
COL_MV = 2048
COL_MO = 3072
COL_GA = 4096
COL_GM = 6144
IN_MAIN = 8192
W_AQ_LO, W_AQ_HI = 0, 1024
W_AV_LO = 2048
GATE_LO = 6144
GATE_HI = 6152

VMEM_LIMIT = 56 * 1024 * 1024


def _rms(x, g):
    ms = jnp.mean(x * x, axis=-1, keepdims=True)
    return x * lax.rsqrt(ms + EPS) * g


def _sigmoid(x):
    return 0.5 * jnp.tanh(0.5 * x) + 0.5


def _log_sigmoid(x):
    return jnp.minimum(x, 0.0) - jnp.log(1.0 + jnp.exp(-jnp.abs(x)))


def _dot(a, b):
    return jnp.dot(a, b, preferred_element_type=F32)


def _dot_nt(a, b):
    return lax.dot_general(a, b, (((1,), (1,)), ((), ())), preferred_element_type=F32)


def _memkv_kernel(mem_ref, g_ref, w_ref, o_ref):
    mn = _rms(mem_ref[0], g_ref[...]).astype(BF16)
    o_ref[0] = _dot(mn, w_ref[...].astype(BF16)).astype(BF16)


def _memkv(mem, g, w_ckv):
    b, n_mem, _ = mem.shape
    return pl.pallas_call(
        _memkv_kernel,
        grid=(b,),
        in_specs=[
            pl.BlockSpec((1, n_mem, D_MODEL), lambda i: (i, 0, 0)),
            pl.BlockSpec((1, D_MODEL), lambda i: (0, 0)),
            pl.BlockSpec((D_MODEL, 2 * CROSS_WIDTH), lambda i: (0, 0)),
        ],
        out_specs=pl.BlockSpec((1, n_mem, 2 * CROSS_WIDTH), lambda i: (i, 0, 0)),
        out_shape=jax.ShapeDtypeStruct((b, n_mem, 2 * CROSS_WIDTH), BF16),
        compiler_params=pltpu.CompilerParams(
            dimension_semantics=("arbitrary",), vmem_limit_bytes=VMEM_LIMIT),
        name="memkv",
    )(mem, g, w_ckv)


def _wprep_kernel(a_ref, nxt_ref, wa_out, wb_out, wg_out, *, na):
    k = pl.program_id(0)
    ng = GATE_HI - GATE_LO

    @pl.when(k < na)
    def _():
        scale = jnp.where(k == W_AQ_LO // a_ref.shape[0], ATTN_QK_DIM ** -0.5 * LOG2E, 1.0)
        wa_out[...] = (a_ref[...] * scale).astype(BF16)

    @pl.when(k >= na)
    def _():
        wb_out[...] = jnp.concatenate([a_ref[pl.ds(ng, a_ref.shape[0] - ng), :], nxt_ref[0]],
                                      axis=0).astype(BF16)

    @pl.when(k == na)
    def _():
        pad = jnp.zeros((LANES - ng, D_MODEL), F32)
        wg_out[...] = jnp.concatenate([a_ref[pl.ds(0, ng), :], pad], axis=0).astype(BF16)


def _wprep(wt, tn=1024):
    rows = wt.shape[0]
    ng = GATE_HI - GATE_LO
    assert ng == SUBLANES and GATE_LO % tn == 0 and (rows - GATE_HI) % tn == 0
    assert W_AQ_LO == 0 and W_AQ_HI == tn
    na = GATE_LO // tn
    nb = (rows - GATE_HI) // tn
    groups = wt.reshape(rows // ng, ng, D_MODEL)
    per_block = tn // ng
    return pl.pallas_call(
        functools.partial(_wprep_kernel, na=na),
        grid=(na + nb,),
        in_specs=[
            pl.BlockSpec((tn, D_MODEL), lambda k: (k, 0)),
            pl.BlockSpec((1, ng, D_MODEL), lambda k: ((k + 1) * per_block, 0, 0)),
        ],
        out_specs=[
            pl.BlockSpec((tn, D_MODEL), lambda k: (jnp.minimum(k, na - 1), 0)),
            pl.BlockSpec((tn, D_MODEL), lambda k: (jnp.maximum(k - na, 0), 0)),
            pl.BlockSpec((LANES, D_MODEL), lambda k: (0, 0)),
        ],
        out_shape=[
            jax.ShapeDtypeStruct((GATE_LO, D_MODEL), BF16),
            jax.ShapeDtypeStruct((rows - GATE_HI, D_MODEL), BF16),
            jax.ShapeDtypeStruct((LANES, D_MODEL), BF16),
        ],
        compiler_params=pltpu.CompilerParams(
            dimension_semantics=("arbitrary",), vmem_limit_bytes=VMEM_LIMIT),
        name="wprep",
    )(wt, groups)


def _inproj_kernel(x_ref, g_ref, wa_ref, wb_ref, wg_ref, z_ref, gate_ref, qt_ref, vt_ref, h_scr,
                   *, na, jq, jv):
    j = pl.program_id(1)

    @pl.when(j == 0)
    def _():
        h = _rms(x_ref[...], g_ref[...]).astype(BF16)
        h_scr[...] = h
        gate_ref[...] = _dot_nt(h, wg_ref[...])

    @pl.when((j < na) & (j != jq) & (j != jv))
    def _():
        z_ref[...] = _dot_nt(h_scr[...], wa_ref[...]).astype(BF16)

    def transposed(out_ref):
        xt = _dot_nt(wa_ref[...], h_scr[...]).astype(BF16)
        tile = out_ref.shape[2]
        for n in range(out_ref.shape[0]):
            out_ref[n] = xt[:, n * tile:(n + 1) * tile]

    @pl.when(j == jq)
    def _():
        transposed(qt_ref)

    @pl.when(j == jv)
    def _():
        transposed(vt_ref)

    @pl.when(j >= na)
    def _():
        z_ref[...] = _dot_nt(h_scr[...], wb_ref[...]).astype(BF16)


def _inproj(x2d, g, w_a, w_b, w_gate, tkv, tm=1024, tn=1024):
    t = x2d.shape[0]
    na = w_a.shape[0] // tn
    n = w_a.shape[0] + w_b.shape[0]
    ni, nj = t // tm, n // tn
    assert tn == ATTN_WIDTH and W_AQ_LO % tn == 0 and W_AQ_HI - W_AQ_LO == tn and W_AV_LO % tn == 0
    jq, jv = W_AQ_LO // tn, W_AV_LO // tn
    assert jq < jv

    def z_tile(i, j):
        skipped = (j >= jq).astype(jnp.int32) + (j >= jv).astype(jnp.int32)
        return i, jnp.maximum(j - skipped, 0)

    return pl.pallas_call(
        functools.partial(_inproj_kernel, na=na, jq=jq, jv=jv),
        grid=(ni, nj),
        in_specs=[
            pl.BlockSpec((tm, D_MODEL), lambda i, j: (i, 0)),
            pl.BlockSpec((1, D_MODEL), lambda i, j: (0, 0)),
            pl.BlockSpec((tn, D_MODEL), lambda i, j: (jnp.minimum(j, na - 1), 0)),
            pl.BlockSpec((tn, D_MODEL), lambda i, j: (jnp.maximum(j - na, 0), 0)),
            pl.BlockSpec((LANES, D_MODEL), lambda i, j: (0, 0)),
        ],
        out_specs=[
            pl.BlockSpec((tm, tn), z_tile),
            pl.BlockSpec((tm, LANES), lambda i, j: (i, 0)),
            pl.BlockSpec((tm // tkv, ATTN_WIDTH, tkv), lambda i, j: (i, 0, 0)),
            pl.BlockSpec((tm // tkv, ATTN_WIDTH, tkv), lambda i, j: (i, 0, 0)),
        ],
        out_shape=[
            jax.ShapeDtypeStruct((t, n - 2 * tn), BF16),
            jax.ShapeDtypeStruct((t, LANES), F32),
            jax.ShapeDtypeStruct((t // tkv, ATTN_WIDTH, tkv), BF16),
            jax.ShapeDtypeStruct((t // tkv, ATTN_WIDTH, tkv), BF16),
        ],
        scratch_shapes=[pltpu.VMEM((tm, D_MODEL), BF16)],
        compiler_params=pltpu.CompilerParams(
            dimension_semantics=("arbitrary", "arbitrary"), vmem_limit_bytes=VMEM_LIMIT),
        name="inproj",
    )(x2d, g, w_a, w_b, w_gate)


def _attn_kernel(cs_ref, lam_ref, qt_ref, k_ref, vt_ref, gain_ref, *refs, ncast, tq, lam_init):
    cast_in = refs[:ncast]
    o_ref = refs[ncast]
    cast_out = refs[ncast + 1:2 * ncast + 1]
    qs_ref, kf_ref, acc_ref, m_ref = refs[2 * ncast + 1:]
    tk = tq
    nfeat = ALIBI_TERMS
    dq = 2 * ATTN_QK_DIM
    dv = ATTN_V_DIM
    i = pl.program_id(1)
    cs = [[cs_ref[nfeat * h + t] for t in range(nfeat)] for h in range(ATTN_HEADS)]
    cs_tot = [sum(c[1:], c[0]) for c in cs]

    @pl.when(i == 0)
    def _():
        klane = lax.broadcasted_iota(jnp.int32, (tk, LANES), 1)
        krow = lax.broadcasted_iota(jnp.int32, (tk, LANES), 0).astype(F32)
        frow = lax.broadcasted_iota(jnp.int32, (LANES, 2 * tq), 0)
        fcol = lax.broadcasted_iota(jnp.int32, (LANES, 2 * tq), 1)
        fcol = jnp.where(fcol >= tq, fcol - tq, fcol).astype(F32)
        for h in range(ATTN_HEADS):
            kf = jnp.where(klane < nfeat, krow, 0.0)
            qf = jnp.where((frow >= nfeat) & (frow < 2 * nfeat), -fcol, 0.0)
            for t in range(nfeat):
                kf = jnp.where(klane == nfeat + t, cs[h][t], kf)
                qf = jnp.where(frow == t, cs[h][t], qf)
            kf_ref[h] = kf.astype(BF16)
            qs_ref[h, pl.ds(dq, LANES), :] = qf.astype(BF16)

    row = lax.broadcasted_iota(jnp.int32, (dq, tq), 0)
    for h in range(ATTN_HEADS):
        qt = qt_ref[0, 0, h]
        zero = jnp.zeros_like(qt)
        qs_ref[h, pl.ds(0, dq), pl.ds(0, tq)] = jnp.where(row < ATTN_QK_DIM, qt, zero)
        qs_ref[h, pl.ds(0, dq), pl.ds(tq, tq)] = jnp.where(row >= ATTN_QK_DIM, qt, zero)
    acc_ref[...] = jnp.zeros_like(acc_ref)
    m_ref[...] = jnp.full_like(m_ref, NEG_BIG)

    orow = lax.broadcasted_iota(jnp.int32, (2 * SUBLANES, tk), 0)
    ones_blk = jnp.where(orow == 0, 1.0, 0.0).astype(BF16)

    def scores(h, j):
        k = k_ref[0, pl.ds(pl.multiple_of(j * tk, tk), tk), pl.ds(h * dq, dq)]
        kx = jnp.concatenate([k, kf_ref[h]], axis=1)
        return _dot(kx, qs_ref[h])

    def softmax_pv(h, j, t, diagonal):
        if diagonal:
            krow = lax.broadcasted_iota(jnp.int32, (tk, 2 * tq), 0)
            qcol = lax.broadcasted_iota(jnp.int32, (tk, 2 * tq), 1)
            qcol = jnp.where(qcol >= tq, qcol - tq, qcol)
            t = jnp.where(krow <= qcol, t, NEG_BIG)
        soff = cs_tot[h] * ((j - i) * tk).astype(F32)
        m = m_ref[h]
        m_new = jnp.maximum(m, jnp.max(t, axis=0, keepdims=True) + soff)
        alpha = jnp.exp2(m - m_new)
        p = jnp.exp2(t - (m_new - soff)).astype(BF16)
        vx = jnp.concatenate([vt_ref[0, j, h], ones_blk], axis=0)
        acc_ref[h] = alpha * acc_ref[h] + _dot(vx, p)
        m_ref[h] = m_new

    def run(units):
        ahead = 2
        pending = [scores(*u[:2]) for u in units[:ahead]]
        for n, (h, j, diagonal) in enumerate(units):
            if n + ahead < len(units):
                pending.append(scores(*units[n + ahead][:2]))
            softmax_pv(h, j, pending.pop(0), diagonal)

    def tile_units(j, diagonal):
        return [(h, j, diagonal) for h in range(ATTN_HEADS)]

    group = 4

    def body(jj, carry):
        units = []
        for n in range(group):
            units += tile_units(group * jj + n, False)
        run(units)
        return carry

    lax.fori_loop(0, i // group, body, 0)

    for rest in range(group):
        @pl.when(i % group == rest)
        def _(rest=rest):
            units = []
            for n in range(rest):
                units += tile_units(i - rest + n, False)
            run(units + tile_units(i, True))

    lv = lam_ref[...]
    d1 = jnp.sum(lv[0:1] * lv[1:2], axis=-1, keepdims=True)
    d2 = jnp.sum(lv[2:3] * lv[3:4], axis=-1, keepdims=True)
    lam = jnp.exp(d1) - jnp.exp(d2) + lam_init
    for h in range(ATTN_HEADS):
        l = acc_ref[h, pl.ds(dv, 1), :]
        rl = 1.0 / l
        out = (acc_ref[h, pl.ds(0, dv), pl.ds(0, tq)] * rl[:, :tq]
               - lam * (acc_ref[h, pl.ds(0, dv), pl.ds(tq, tq)] * rl[:, tq:]))
        ms = jnp.mean(out * out, axis=0, keepdims=True)
        on = out * lax.rsqrt(ms + EPS)
        o_ref[0, :, pl.ds(h * dv, dv)] = (on.T * gain_ref[...] * (1.0 - lam_init)).astype(BF16)

    for src, dst in zip(cast_in, cast_out):
        dst[...] = src[...].astype(BF16)


def _attn(z3, qt5, vt5, cs_terms, lamv, gain, cast_weights, lam_init, tq=256):
    b, s, _ = z3.shape
    nq = s // tq
    steps = b * nq
    cast_specs = [pl.BlockSpec((w.shape[0] // steps, w.shape[1]), lambda bi, i: (bi * nq + i, 0))
                  for w in cast_weights]
    kern = functools.partial(_attn_kernel, ncast=len(cast_weights), tq=tq, lam_init=lam_init)
    width = ATTN_HEADS * 2 * ATTN_QK_DIM
    return pl.pallas_call(
        kern,
        grid=(b, nq),
        in_specs=[
            pl.BlockSpec(memory_space=pltpu.SMEM),
            pl.BlockSpec((SUBLANES, LANES), lambda bi, i: (0, 0)),
            pl.BlockSpec((1, 1, ATTN_HEADS, 2 * ATTN_QK_DIM, tq), lambda bi, i: (bi, i, 0, 0, 0)),
            pl.BlockSpec((1, s, width), lambda bi, i: (bi, 0, COL_AK // width)),
            pl.BlockSpec((1, nq, ATTN_HEADS, ATTN_V_DIM, tq), lambda bi, i: (bi, 0, 0, 0, 0)),
            pl.BlockSpec((1, ATTN_V_DIM), lambda bi, i: (0, 0)),
        ] + cast_specs,
        out_specs=[pl.BlockSpec((1, tq, ATTN_WIDTH), lambda bi, i: (bi, i, 0))] + cast_specs,
        out_shape=[jax.ShapeDtypeStruct((b, s, ATTN_WIDTH), BF16)]
        + [jax.ShapeDtypeStruct(w.shape, BF16) for w in cast_weights],
        scratch_shapes=[
            pltpu.VMEM((ATTN_HEADS, 2 * ATTN_QK_DIM + LANES, 2 * tq), BF16),
            pltpu.VMEM((ATTN_HEADS, tq, LANES), BF16),
            pltpu.VMEM((ATTN_HEADS, ATTN_V_DIM + 2 * SUBLANES, 2 * tq), F32),
            pltpu.VMEM((ATTN_HEADS, 1, 2 * tq), F32),
        ],
        compiler_params=pltpu.CompilerParams(
            dimension_semantics=("parallel", "arbitrary"),
            vmem_limit_bytes=VMEM_LIMIT),
        name="attn",
    )(cs_terms, lamv, qt5, z3, vt5, gain, *cast_weights)


def _split3(x):
    hi = x.astype(BF16)
    r = x - hi.astype(F32)
    mid = r.astype(BF16)
    lo = (r - mid.astype(F32)).astype(BF16)
    return hi, mid, lo


def _mlstm_kernel(uq_ref, uk_ref, v_ref, mo_ref, gcol_ref, grow_ref, bcol_ref, brow_ref,
                  cw_ref, cb_ref, gn_ref, o_ref, extq, extk, dstq, dstk, c_scr, n_scr, m_scr,
                  *, chunk):
    L = chunk
    nh = MLSTM_HEADS
    dk = MLSTM_QK_DIM
    dv = MLSTM_V_DIM
    heads = range(nh)

    @pl.when(pl.program_id(1) == 0)
    def _():
        extq[:, pl.ds(0, SUBLANES), :] = jnp.zeros((nh, SUBLANES, dk), F32)
        extk[:, pl.ds(0, SUBLANES), :] = jnp.zeros((nh, SUBLANES, dk), F32)
        c_scr[...] = jnp.zeros_like(c_scr)
        n_scr[...] = jnp.zeros_like(n_scr)
        m_scr[...] = jnp.zeros_like(m_scr)

    def conv_silu(u_ref, ext, dst, col0, scale):
        n8 = L // SUBLANES
        first = SUBLANES - (CONV_WIDTH - 1)
        outs = []
        for h in heads:
            cols = pl.ds(col0 + h * dk, dk)
            ext[h, pl.ds(SUBLANES, L), :] = u_ref[0, :, pl.ds(h * dk, dk)].astype(F32)
            w = cw_ref[:, cols]
            bias = cb_ref[:, cols]
            rows = [ext[h, pl.ds(first + s, n8, stride=SUBLANES), :]
                    for s in range(SUBLANES + CONV_WIDTH - 1)]
            for r in range(SUBLANES):
                y = bias
                for tap in range(CONV_WIDTH):
                    y = y + rows[r + tap] * w[tap:tap + 1]
                y = y * _sigmoid(y)
                dst[h, pl.ds(r, n8, stride=SUBLANES), :] = y if scale is None else y * scale
            ext[h, pl.ds(0, SUBLANES), :] = ext[h, pl.ds(L, SUBLANES), :]
            outs.append(dst[h])
        return outs

    q = conv_silu(uq_ref, extq, dstq, 0, None)
    k = conv_silu(uk_ref, extk, dstk, MLSTM_QK_WIDTH, dk ** -0.5)
    qb = [x.astype(BF16) for x in q]
    kb = [x.astype(BF16) for x in k]
    vb = [v_ref[0, :, pl.ds(h * dv, dv)] for h in heads]

    g_c = gcol_ref[...] + bcol_ref[...]
    g_r = grow_ref[0] + brow_ref[:, 0:1]
    lf_c = _log_sigmoid(g_c)
    lf_r = _log_sigmoid(g_r)

    r_i = lax.broadcasted_iota(jnp.int32, (L, L), 0)
    c_i = lax.broadcasted_iota(jnp.int32, (L, L), 1)
    causal = c_i <= r_i
    tri = jnp.where(causal, 1.0, 0.0).astype(BF16)
    tri_t = jnp.where(r_i <= c_i, 1.0, 0.0).astype(BF16)
    hi, mid, lo = _split3(lf_c)
    b_c = _dot(tri, hi) + _dot(tri, mid) + _dot(tri, lo)
    hi, mid, lo = _split3(lf_r)
    b_r = _dot(hi, tri_t) + _dot(mid, tri_t) + _dot(lo, tri_t)
    g_sum = jnp.sum(lf_r, axis=1, keepdims=True)

    lane = lax.broadcasted_iota(jnp.int32, (L, LANES), 1)

    def col(x, idx):
        return jnp.sum(jnp.where(lane == idx, x, 0.0), axis=1, keepdims=True)

    i_col = [col(g_c, h) for h in heads]
    b_col = [col(b_c, nh + h) for h in heads]
    i_row = [g_r[h:h + 1, :] for h in heads]
    b_row = [b_r[nh + h:nh + h + 1, :] for h in heads]
    g_tot = [g_sum[nh + h:nh + h + 1, :] for h in heads]

    c_prev = [c_scr[h] for h in heads]
    n_prev = [n_scr[h] for h in heads]
    m_prev = [m_scr[h][:, 0:1] for h in heads]

    s_qk = [_dot_nt(qb[h], kb[h]) for h in heads]
    inter = [_dot(qb[h], c_prev[h].astype(BF16)) for h in heads]
    m_loc = [jnp.max(g_tot[h] - b_row[h] + i_row[h], axis=1, keepdims=True) for h in heads]
    kw = [k[h] * jnp.exp(g_tot[h] - b_col[h] + i_col[h] - m_loc[h]) for h in heads]
    c_loc = [_dot(kw[h].T.astype(BF16), vb[h]) for h in heads]

    for h in heads:
        d = jnp.where(causal, b_col[h] - b_row[h] + i_row[h], NEG_BIG)
        m_inter = b_col[h] + m_prev[h]
        m_j = jnp.maximum(m_inter, jnp.max(d, axis=1, keepdims=True))
        w_inter = jnp.exp(m_inter - m_j)
        p = s_qk[h] * jnp.exp(d - m_j)
        num = w_inter * inter[h] + _dot(p.astype(BF16), vb[h])
        den = (w_inter * jnp.sum(q[h] * n_prev[h], axis=1, keepdims=True)
               + jnp.sum(p, axis=1, keepdims=True))
        hval = num * (1.0 / jnp.maximum(jnp.abs(den), jnp.exp(-m_j)))
        hn = _rms(hval, gn_ref[:, pl.ds(h * dv, dv)])
        gate = _sigmoid(mo_ref[0, :, pl.ds(h * dv, dv)].astype(F32))
        o_ref[0, :, pl.ds(h * dv, dv)] = (hn * gate).astype(BF16)

    for h in heads:
        n_loc = jnp.sum(kw[h], axis=0, keepdims=True)
        m_new = jnp.maximum(g_tot[h] + m_prev[h], m_loc[h])
        a_old = jnp.exp(g_tot[h] + m_prev[h] - m_new)
        a_new = jnp.exp(m_loc[h] - m_new)
        c_scr[h] = a_old * c_prev[h] + a_new * c_loc[h]
        n_scr[h] = a_old * n_prev[h] + a_new * n_loc
        m_scr[h] = jnp.broadcast_to(m_new, (1, LANES))


def _mlstm(z3, gates, gates_t, bias_col, bias_row, conv_w8, conv_b, gnorm, chunk=256):
    b, s, _ = z3.shape
    nc = s // chunk
    kern = functools.partial(_mlstm_kernel, chunk=chunk)
    qw = MLSTM_QK_WIDTH
    vw = MLSTM_WIDTH
    return pl.pallas_call(
        kern,
        grid=(b, nc),
        in_specs=[
            pl.BlockSpec((1, chunk, qw), lambda bi, c: (bi, c, COL_MQ // qw)),
            pl.BlockSpec((1, chunk, qw), lambda bi, c: (bi, c, COL_MK // qw)),
            pl.BlockSpec((1, chunk, vw), lambda bi, c: (bi, c, COL_MV // vw)),
            pl.BlockSpec((1, chunk, vw), lambda bi, c: (bi, c, COL_MO // vw)),
            pl.BlockSpec((chunk, LANES), lambda bi, c: (bi * nc + c, 0)),
            pl.BlockSpec((1, SUBLANES, chunk), lambda bi, c: (bi, 0, c)),
            pl.BlockSpec((1, LANES), lambda bi, c: (0, 0)),
            pl.BlockSpec((SUBLANES, LANES), lambda bi, c: (0, 0)),
            pl.BlockSpec((SUBLANES, 2 * qw), lambda bi, c: (0, 0)),
            pl.BlockSpec((1, 2 * qw), lambda bi, c: (0, 0)),
            pl.BlockSpec((1, vw), lambda bi, c: (0, 0)),
        ],
        out_specs=pl.BlockSpec((1, chunk, vw), lambda bi, c: (bi, c, 0)),
        out_shape=jax.ShapeDtypeStruct((b, s, vw), BF16),
        scratch_shapes=[
            pltpu.VMEM((MLSTM_HEADS, chunk + 2 * SUBLANES, MLSTM_QK_DIM), F32),
            pltpu.VMEM((MLSTM_HEADS, chunk + 2 * SUBLANES, MLSTM_QK_DIM), F32),
            pltpu.VMEM((MLSTM_HEADS, chunk, MLSTM_QK_DIM), F32),
            pltpu.VMEM((MLSTM_HEADS, chunk, MLSTM_QK_DIM), F32),
            pltpu.VMEM((MLSTM_HEADS, MLSTM_QK_DIM, MLSTM_V_DIM), F32),
            pltpu.VMEM((MLSTM_HEADS, 1, MLSTM_QK_DIM), F32),
            pltpu.VMEM((MLSTM_HEADS, 1, LANES), F32),
        ],
        compiler_params=pltpu.CompilerParams(
            dimension_semantics=("parallel", "arbitrary"),
            vmem_limit_bytes=VMEM_LIMIT),
        name="mlstm",
    )(z3, z3, z3, z3, gates, gates_t, bias_col, bias_row, conv_w8, conv_b, gnorm)


def _cross_delta(x, g_ref, wq_ref, kv_ref, wo_ref):
    hc = _rms(x, g_ref[...]).astype(BF16)
    cq = (_dot(hc, wq_ref[...]) * (CROSS_HEAD_DIM ** -0.5)).astype(BF16)
    outs = []
    for hh in range(CROSS_HEADS):
        lo = hh * CROSS_HEAD_DIM
        qh = cq[:, lo:lo + CROSS_HEAD_DIM]
        kh = kv_ref[0, :, lo:lo + CROSS_HEAD_DIM]
        vh = kv_ref[0, :, CROSS_WIDTH + lo:CROSS_WIDTH + lo + CROSS_HEAD_DIM]
        s = _dot_nt(qh, kh)
        p = jnp.exp(s - jnp.max(s, axis=-1, keepdims=True))
        l = jnp.sum(p, axis=-1, keepdims=True)
        outs.append((_dot(p.astype(BF16), vh) * (1.0 / l)).astype(BF16))
    co = jnp.concatenate(outs, axis=1)
    return _dot(co, wo_ref[...])


def _merge_kernel(att_ref, hm_ref, ga_ref, gm_ref, x_ref, wa_ref, wm_ref, wo_ref,
                  gc_ref, wcq_ref, kv_ref, wco_ref, o_ref):
    j = pl.program_id(1)

    @pl.when(j == 0)
    def _():
        o_ref[...] = x_ref[...]

    a = _dot(att_ref[...], wa_ref[...])
    bm = _dot(hm_ref[...], wm_ref[...])
    y = _sigmoid(ga_ref[...].astype(F32)) * a + _sigmoid(gm_ref[...].astype(F32)) * bm
    o_ref[...] += _dot(y.astype(BF16), wo_ref[...])

    @pl.when(j == pl.num_programs(1) - 1)
    def _():
        x1 = o_ref[...]
        o_ref[...] = x1 + _cross_delta(x1, gc_ref, wcq_ref, kv_ref, wco_ref)


def _merge(att2d, hm2d, z2d, x2d, wa, wm, wo, gc, wcq, ckv, wco, seq, tm=512, tn=1024):
    t = x2d.shape[0]
    nj = D_MODEL // tn
    n_mem = ckv.shape[1]
    per_batch = seq // tm
    return pl.pallas_call(
        _merge_kernel,
        grid=(t // tm, nj),
        in_specs=[
            pl.BlockSpec((tm, ATTN_WIDTH), lambda i, j: (i, 0)),
            pl.BlockSpec((tm, MLSTM_WIDTH), lambda i, j: (i, 0)),
            pl.BlockSpec((tm, tn), lambda i, j: (i, COL_GA // tn + j)),
            pl.BlockSpec((tm, tn), lambda i, j: (i, COL_GM // tn + j)),
            pl.BlockSpec((tm, D_MODEL), lambda i, j: (i, 0)),
            pl.BlockSpec((ATTN_WIDTH, tn), lambda i, j: (0, j)),
            pl.BlockSpec((MLSTM_WIDTH, tn), lambda i, j: (0, j)),
            pl.BlockSpec((tn, D_MODEL), lambda i, j: (j, 0)),
            pl.BlockSpec((1, D_MODEL), lambda i, j: (0, 0)),
            pl.BlockSpec((D_MODEL, CROSS_WIDTH), lambda i, j: (0, 0)),
            pl.BlockSpec((1, n_mem, 2 * CROSS_WIDTH), lambda i, j: (i // per_batch, 0, 0)),
            pl.BlockSpec((CROSS_WIDTH, D_MODEL), lambda i, j: (0, 0)),
        ],
        out_specs=pl.BlockSpec((tm, D_MODEL), lambda i, j: (i, 0)),
        out_shape=jax.ShapeDtypeStruct((t, D_MODEL), F32),
        compiler_params=pltpu.CompilerParams(
            dimension_semantics=("parallel", "arbitrary"), vmem_limit_bytes=VMEM_LIMIT),
        name="merge",
    )(att2d, hm2d, z2d, z2d, x2d, wa, wm, wo, gc, wcq, ckv, wco)


def _mlp_kernel(x_ref, g_ref, wu_ref, wd_ref, gf_ref, o_ref, h_scr, acc, *, final_norm):
    j = pl.program_id(1)

    @pl.when(j == 0)
    def _():
        x = x_ref[...]
        h_scr[...] = _rms(x, g_ref[...]).astype(BF16)
        acc[...] = x

    u = jnp.square(jnp.maximum(_dot(h_scr[...], wu_ref[...]), 0.0)).astype(BF16)
    acc[...] += _dot(u, wd_ref[...])

    @pl.when(j == pl.num_programs(1) - 1)
    def _():
        if final_norm:
            o_ref[...] = _rms(acc[...], gf_ref[...])
        else:
            o_ref[...] = acc[...]


def _mlp(x2d, g, wu, wd, gf, final_norm, tm=512, tf=1024):
    t = x2d.shape[0]
    kern = functools.partial(_mlp_kernel, final_norm=final_norm)
    return pl.pallas_call(
        kern,
        grid=(t // tm, D_FF // tf),
        in_specs=[
            pl.BlockSpec((tm, D_MODEL), lambda i, j: (i, 0)),
            pl.BlockSpec((1, D_MODEL), lambda i, j: (0, 0)),
            pl.BlockSpec((D_MODEL, tf), lambda i, j: (0, j)),
            pl.BlockSpec((tf, D_MODEL), lambda i, j: (j, 0)),
            pl.BlockSpec((1, D_MODEL), lambda i, j: (0, 0)),
        ],
        out_specs=pl.BlockSpec((tm, D_MODEL), lambda i, j: (i, 0)),
        out_shape=jax.ShapeDtypeStruct((t, D_MODEL), F32),
        scratch_shapes=[pltpu.VMEM((tm, D_MODEL), BF16), pltpu.VMEM((tm, D_MODEL), F32)],
        compiler_params=pltpu.CompilerParams(
            dimension_semantics=("parallel", "arbitrary"), vmem_limit_bytes=VMEM_LIMIT),
        name="mlp",
    )(x2d, g, wu, wd, gf)


def _alibi_terms():
    slopes = 2.0 ** (-8.0 * np.arange(1, ATTN_HEADS + 1, dtype=np.float64) / ATTN_HEADS)
    rem = slopes * LOG2E
    terms = []
    for _ in range(ALIBI_TERMS):
        t = rem.astype(np.float32).astype(ml_dtypes.bfloat16).astype(np.float64)
        terms.append(t)
        rem = rem - t
    return np.stack(terms, axis=1).reshape(-1).astype(np.float32)


def _pad_rows(a, rows):
    return jnp.pad(a, ((0, rows - a.shape[0]), (0, 0)))


def kernel(x, mem, norm_mix, w_in, b_igate, b_fgate, conv_w, conv_b, lam_q1, lam_k1, lam_q2, lam_k2, attn_norm, mlstm_norm, w_attn_br, w_mlstm_br, w_out, norm_cross, norm_mem, w_cq, w_ckv, w_co, norm_mlp, w_up, w_down, norm_final):
    b, s, _ = x.shape
    t = b * s
    tq = 256
    cs_terms = jnp.asarray(_alibi_terms())
    x2d = x.reshape(t, D_MODEL)
    for l in range(DEPTH):
        lam_init = 0.8 - 0.6 * math.exp(-0.3 * l)
        w = w_in[l]
        w_a, w_b, w_gate = _wprep(w.T)

        z2d, gates, qt, vt = _inproj(x2d, norm_mix[l][None, :], w_a, w_b, w_gate, tq)
        z3 = z2d.reshape(b, s, IN_MAIN)

        qt5 = qt.reshape(b, s // tq, ATTN_HEADS, 2 * ATTN_QK_DIM, tq)
        vt5 = vt.reshape(b, s // tq, ATTN_HEADS, ATTN_V_DIM, tq)
        lamv = _pad_rows(jnp.pad(jnp.stack([lam_q1[l], lam_k1[l], lam_q2[l], lam_k2[l]]),
                                 ((0, 0), (0, LANES - ATTN_QK_DIM))), SUBLANES)
        att, w_out_b, w_abr_b, w_mbr_b, w_up_b, w_down_b = _attn(
            z3, qt5, vt5, cs_terms, lamv, attn_norm[l][None, :],
            (w_out[l], w_attn_br[l], w_mlstm_br[l], w_up[l], w_down[l]), lam_init, tq=tq)

        gates_t = gates[:, :SUBLANES].reshape(b, s, SUBLANES).transpose(0, 2, 1)
        gate_bias = jnp.concatenate([b_igate[l], b_fgate[l]])
        bias_col = jnp.pad(gate_bias, (0, LANES - gate_bias.shape[0]))[None, :]
        bias_row = jnp.broadcast_to(gate_bias[:, None], (SUBLANES, LANES))
        hm = _mlstm(z3, gates, gates_t, bias_col, bias_row, _pad_rows(conv_w[l], SUBLANES),
                    conv_b[l][None, :], mlstm_norm[l][None, :])

        ckv = _memkv(mem, norm_mem[l][None, :], w_ckv[l])
        x2d = _merge(att.reshape(t, ATTN_WIDTH), hm.reshape(t, MLSTM_WIDTH), z2d, x2d,
                     w_abr_b, w_mbr_b, w_out_b, norm_cross[l][None, :], w_cq[l].astype(BF16), ckv,
                     w_co[l].astype(BF16), s)

        x2d = _mlp(x2d, norm_mlp[l][None, :], w_up_b, w_down_b,
                   norm_final[None, :], final_norm=(l == DEPTH - 1))
    return x2d.reshape(b, s, D_MODEL)
```

```python
import functools
import math

import ml_dtypes
import numpy as np
import jax
import jax.numpy as jnp
from jax import lax
from jax.experimental import pallas as pl
from jax.experimental.pallas import tpu as pltpu

F32 = jnp.float32
BF16 = jnp.bfloat16

D_MODEL = 2048
DEPTH = 1
ATTN_HEADS = 8
ATTN_QK_DIM = 64
ATTN_V_DIM = 128
ATTN_WIDTH = ATTN_HEADS * ATTN_V_DIM
MLSTM_HEADS = 4
MLSTM_QK_DIM = 128
MLSTM_V_DIM = 256
MLSTM_QK_WIDTH = MLSTM_HEADS * MLSTM_QK_DIM
MLSTM_WIDTH = MLSTM_HEADS * MLSTM_V_DIM
CONV_WIDTH = 4
CROSS_HEADS = 4
CROSS_HEAD_DIM = 128
CROSS_WIDTH = CROSS_HEADS * CROSS_HEAD_DIM
D_FF = 4 * D_MODEL
EPS = 1e-6
LANES = 128
SUBLANES = 8
NEG_BIG = -1e30
LOG2E = 1.4426950408889634
ALIBI_TERMS = 3

COL_AK = 0
COL_MQ = 1024
COL_MK = 1536
COL_MV = 2048
COL_MO = 3072
COL_GA = 4096
COL_GM = 6144
IN_MAIN = 8192
W_AQ_LO, W_AQ_HI = 0, 1024
W_AV_LO = 2048
GATE_LO = 6144
GATE_HI = 6152

VMEM_LIMIT = 56 * 1024 * 1024


def _rms(x, g):
    ms = jnp.mean(x * x, axis=-1, keepdims=True)
    return x * lax.rsqrt(ms + EPS) * g


def _sigmoid(x):
    return 0.5 * jnp.tanh(0.5 * x) + 0.5


def _log_sigmoid(x):
    return jnp.minimum(x, 0.0) - jnp.log(1.0 + jnp.exp(-jnp.abs(x)))


def _dot(a, b):
    return jnp.dot(a, b, preferred_element_type=F32)


def _dot_nt(a, b):
    return lax.dot_general(a, b, (((1,), (1,)), ((), ())), preferred_element_type=F32)


def _memkv_kernel(mem_ref, g_ref, w_ref, o_ref):
    mn = _rms(mem_ref[0], g_ref[...]).astype(BF16)
    o_ref[0] = _dot(mn, w_ref[...].astype(BF16)).astype(BF16)


def _memkv(mem, g, w_ckv):
    b, n_mem, _ = mem.shape
    return pl.pallas_call(
        _memkv_kernel,
        grid=(b,),
        in_specs=[
            pl.BlockSpec((1, n_mem, D_MODEL), lambda i: (i, 0, 0)),
            pl.BlockSpec((1, D_MODEL), lambda i: (0, 0)),
            pl.BlockSpec((D_MODEL, 2 * CROSS_WIDTH), lambda i: (0, 0)),
        ],
        out_specs=pl.BlockSpec((1, n_mem, 2 * CROSS_WIDTH), lambda i: (i, 0, 0)),
        out_shape=jax.ShapeDtypeStruct((b, n_mem, 2 * CROSS_WIDTH), BF16),
        compiler_params=pltpu.CompilerParams(
            dimension_semantics=("arbitrary",), vmem_limit_bytes=VMEM_LIMIT),
        name="memkv",
    )(mem, g, w_ckv)


def _wprep_kernel(a_ref, nxt_ref, wa_out, wb_out, wg_out, *, na):
    k = pl.program_id(0)
    ng = GATE_HI - GATE_LO

    @pl.when(k < na)
    def _():
        scale = jnp.where(k == W_AQ_LO // a_ref.shape[0], ATTN_QK_DIM ** -0.5 * LOG2E, 1.0)
        wa_out[...] = (a_ref[...] * scale).astype(BF16)

    @pl.when(k >= na)
    def _():
        wb_out[...] = jnp.concatenate([a_ref[pl.ds(ng, a_ref.shape[0] - ng), :], nxt_ref[0]],
                                      axis=0).astype(BF16)

    @pl.when(k == na)
    def _():
        pad = jnp.zeros((LANES - ng, D_MODEL), F32)
        wg_out[...] = jnp.concatenate([a_ref[pl.ds(0, ng), :], pad], axis=0).astype(BF16)


def _wprep(wt, tn=1024):
    rows = wt.shape[0]
    ng = GATE_HI - GATE_LO
    assert ng == SUBLANES and GATE_LO % tn == 0 and (rows - GATE_HI) % tn == 0
    assert W_AQ_LO == 0 and W_AQ_HI == tn
    na = GATE_LO // tn
    nb = (rows - GATE_HI) // tn
    groups = wt.reshape(rows // ng, ng, D_MODEL)
    per_block = tn // ng
    return pl.pallas_call(
        functools.partial(_wprep_kernel, na=na),
        grid=(na + nb,),
        in_specs=[
            pl.BlockSpec((tn, D_MODEL), lambda k: (k, 0)),
            pl.BlockSpec((1, ng, D_MODEL), lambda k: ((k + 1) * per_block, 0, 0)),
        ],
        out_specs=[
            pl.BlockSpec((tn, D_MODEL), lambda k: (jnp.minimum(k, na - 1), 0)),
            pl.BlockSpec((tn, D_MODEL), lambda k: (jnp.maximum(k - na, 0), 0)),
            pl.BlockSpec((LANES, D_MODEL), lambda k: (0, 0)),
        ],
        out_shape=[
            jax.ShapeDtypeStruct((GATE_LO, D_MODEL), BF16),
            jax.ShapeDtypeStruct((rows - GATE_HI, D_MODEL), BF16),
            jax.ShapeDtypeStruct((LANES, D_MODEL), BF16),
        ],
        compiler_params=pltpu.CompilerParams(
            dimension_semantics=("arbitrary",), vmem_limit_bytes=VMEM_LIMIT),
        name="wprep",
    )(wt, groups)


def _inproj_kernel(x_ref, g_ref, wa_ref, wb_ref, wg_ref, z_ref, gate_ref, qt_ref, vt_ref, h_scr,
                   *, na, jq, jv):
    j = pl.program_id(1)

    @pl.when(j == 0)
    def _():
        h = _rms(x_ref[...], g_ref[...]).astype(BF16)
        h_scr[...] = h
        gate_ref[...] = _dot_nt(h, wg_ref[...])

    @pl.when((j < na) & (j != jq) & (j != jv))
    def _():
        z_ref[...] = _dot_nt(h_scr[...], wa_ref[...]).astype(BF16)

    def transposed(out_ref):
        xt = _dot_nt(wa_ref[...], h_scr[...]).astype(BF16)
        tile = out_ref.shape[2]
        for n in range(out_ref.shape[0]):
            out_ref[n] = xt[:, n * tile:(n + 1) * tile]

    @pl.when(j == jq)
    def _():
        transposed(qt_ref)

    @pl.when(j == jv)
    def _():
        transposed(vt_ref)

    @pl.when(j >= na)
    def _():
        z_ref[...] = _dot_nt(h_scr[...], wb_ref[...]).astype(BF16)


def _inproj(x2d, g, w_a, w_b, w_gate, tkv, tm=1024, tn=1024):
    t = x2d.shape[0]
    na = w_a.shape[0] // tn
    n = w_a.shape[0] + w_b.shape[0]
    ni, nj = t // tm, n // tn
    assert tn == ATTN_WIDTH and W_AQ_LO % tn == 0 and W_AQ_HI - W_AQ_LO == tn and W_AV_LO % tn == 0
    jq, jv = W_AQ_LO // tn, W_AV_LO // tn
    assert jq < jv

    def z_tile(i, j):
        skipped = (j >= jq).astype(jnp.int32) + (j >= jv).astype(jnp.int32)
        return i, jnp.maximum(j - skipped, 0)

    return pl.pallas_call(
        functools.partial(_inproj_kernel, na=na, jq=jq, jv=jv),
        grid=(ni, nj),
        in_specs=[
            pl.BlockSpec((tm, D_MODEL), lambda i, j: (i, 0)),
            pl.BlockSpec((1, D_MODEL), lambda i, j: (0, 0)),
            pl.BlockSpec((tn, D_MODEL), lambda i, j: (jnp.minimum(j, na - 1), 0)),
            pl.BlockSpec((tn, D_MODEL), lambda i, j: (jnp.maximum(j - na, 0), 0)),
            pl.BlockSpec((LANES, D_MODEL), lambda i, j: (0, 0)),
        ],
        out_specs=[
            pl.BlockSpec((tm, tn), z_tile),
            pl.BlockSpec((tm, LANES), lambda i, j: (i, 0)),
            pl.BlockSpec((tm // tkv, ATTN_WIDTH, tkv), lambda i, j: (i, 0, 0)),
            pl.BlockSpec((tm // tkv, ATTN_WIDTH, tkv), lambda i, j: (i, 0, 0)),
        ],
        out_shape=[
            jax.ShapeDtypeStruct((t, n - 2 * tn), BF16),
            jax.ShapeDtypeStruct((t, LANES), F32),
            jax.ShapeDtypeStruct((t // tkv, ATTN_WIDTH, tkv), BF16),
            jax.ShapeDtypeStruct((t // tkv, ATTN_WIDTH, tkv), BF16),
        ],
        scratch_shapes=[pltpu.VMEM((tm, D_MODEL), BF16)],
        compiler_params=pltpu.CompilerParams(
            dimension_semantics=("arbitrary", "arbitrary"), vmem_limit_bytes=VMEM_LIMIT),
        name="inproj",
    )(x2d, g, w_a, w_b, w_gate)


def _attn_kernel(cs_ref, lam_ref, qt_ref, k_ref, vt_ref, gain_ref, *refs, ncast, tq, lam_init):
    cast_in = refs[:ncast]
    o_ref = refs[ncast]
    cast_out = refs[ncast + 1:2 * ncast + 1]
    qs_ref, kf_ref, acc_ref, m_ref = refs[2 * ncast + 1:]
    tk = tq
    nfeat = ALIBI_TERMS
    dq = 2 * ATTN_QK_DIM
    dv = ATTN_V_DIM
    i = pl.program_id(1)
    cs = [[cs_ref[nfeat * h + t] for t in range(nfeat)] for h in range(ATTN_HEADS)]
    cs_tot = [sum(c[1:], c[0]) for c in cs]

    @pl.when(i == 0)
    def _():
        klane = lax.broadcasted_iota(jnp.int32, (tk, LANES), 1)
        krow = lax.broadcasted_iota(jnp.int32, (tk, LANES), 0).astype(F32)
        frow = lax.broadcasted_iota(jnp.int32, (LANES, 2 * tq), 0)
        fcol = lax.broadcasted_iota(jnp.int32, (LANES, 2 * tq), 1)
        fcol = jnp.where(fcol >= tq, fcol - tq, fcol).astype(F32)
        for h in range(ATTN_HEADS):
            kf = jnp.where(klane < nfeat, krow, 0.0)
            qf = jnp.where((frow >= nfeat) & (frow < 2 * nfeat), -fcol, 0.0)
            for t in range(nfeat):
                kf = jnp.where(klane == nfeat + t, cs[h][t], kf)
                qf = jnp.where(frow == t, cs[h][t], qf)
            kf_ref[h] = kf.astype(BF16)
            qs_ref[h, pl.ds(dq, LANES), :] = qf.astype(BF16)

    row = lax.broadcasted_iota(jnp.int32, (dq, tq), 0)
    for h in range(ATTN_HEADS):
        qt = qt_ref[0, 0, h]
        zero = jnp.zeros_like(qt)
        qs_ref[h, pl.ds(0, dq), pl.ds(0, tq)] = jnp.where(row < ATTN_QK_DIM, qt, zero)
        qs_ref[h, pl.ds(0, dq), pl.ds(tq, tq)] = jnp.where(row >= ATTN_QK_DIM, qt, zero)
    acc_ref[...] = jnp.zeros_like(acc_ref)
    m_ref[...] = jnp.full_like(m_ref, NEG_BIG)

    orow = lax.broadcasted_iota(jnp.int32, (2 * SUBLANES, tk), 0)
    ones_blk = jnp.where(orow == 0, 1.0, 0.0).astype(BF16)

    def scores(h, j):
        k = k_ref[0, pl.ds(pl.multiple_of(j * tk, tk), tk), pl.ds(h * dq, dq)]
        kx = jnp.concatenate([k, kf_ref[h]], axis=1)
        return _dot(kx, qs_ref[h])

    def softmax_pv(h, j, t, diagonal):
        if diagonal:
            krow = lax.broadcasted_iota(jnp.int32, (tk, 2 * tq), 0)
            qcol = lax.broadcasted_iota(jnp.int32, (tk, 2 * tq), 1)
            qcol = jnp.where(qcol >= tq, qcol - tq, qcol)
            t = jnp.where(krow <= qcol, t, NEG_BIG)
        soff = cs_tot[h] * ((j - i) * tk).astype(F32)
        m = m_ref[h]
        m_new = jnp.maximum(m, jnp.max(t, axis=0, keepdims=True) + soff)
        alpha = jnp.exp2(m - m_new)
        p = jnp.exp2(t - (m_new - soff)).astype(BF16)
        vx = jnp.concatenate([vt_ref[0, j, h], ones_blk], axis=0)
        acc_ref[h] = alpha * acc_ref[h] + _dot(vx, p)
        m_ref[h] = m_new

    def run(units):
        ahead = 2
        pending = [scores(*u[:2]) for u in units[:ahead]]
        for n, (h, j, diagonal) in enumerate(units):
            if n + ahead < len(units):
                pending.append(scores(*units[n + ahead][:2]))
            softmax_pv(h, j, pending.pop(0), diagonal)

    def tile_units(j, diagonal):
        return [(h, j, diagonal) for h in range(ATTN_HEADS)]

    group = 4

    def body(jj, carry):
        units = []
        for n in range(group):
            units += tile_units(group * jj + n, False)
        run(units)
        return carry

    lax.fori_loop(0, i // group, body, 0)

    for rest in range(group):
        @pl.when(i % group == rest)
        def _(rest=rest):
            units = []
            for n in range(rest):
                units += tile_units(i - rest + n, False)
            run(units + tile_units(i, True))

    lv = lam_ref[...]
    d1 = jnp.sum(lv[0:1] * lv[1:2], axis=-1, keepdims=True)
    d2 = jnp.sum(lv[2:3] * lv[3:4], axis=-1, keepdims=True)
    lam = jnp.exp(d1) - jnp.exp(d2) + lam_init
    for h in range(ATTN_HEADS):
        l = acc_ref[h, pl.ds(dv, 1), :]
        rl = 1.0 / l
        out = (acc_ref[h, pl.ds(0, dv), pl.ds(0, tq)] * rl[:, :tq]
               - lam * (acc_ref[h, pl.ds(0, dv), pl.ds(tq, tq)] * rl[:, tq:]))
        ms = jnp.mean(out * out, axis=0, keepdims=True)
        on = out * lax.rsqrt(ms + EPS)
        o_ref[0, :, pl.ds(h * dv, dv)] = (on.T * gain_ref[...] * (1.0 - lam_init)).astype(BF16)

    for src, dst in zip(cast_in, cast_out):
        dst[...] = src[...].astype(BF16)


def _attn(z3, qt5, vt5, cs_terms, lamv, gain, cast_weights, lam_init, tq=256):
    b, s, _ = z3.shape
    nq = s // tq
    steps = b * nq
    cast_specs = [pl.BlockSpec((w.shape[0] // steps, w.shape[1]), lambda bi, i: (bi * nq + i, 0))
                  for w in cast_weights]
    kern = functools.partial(_attn_kernel, ncast=len(cast_weights), tq=tq, lam_init=lam_init)
    width = ATTN_HEADS * 2 * ATTN_QK_DIM
    return pl.pallas_call(
        kern,
        grid=(b, nq),
        in_specs=[
            pl.BlockSpec(memory_space=pltpu.SMEM),
            pl.BlockSpec((SUBLANES, LANES), lambda bi, i: (0, 0)),
            pl.BlockSpec((1, 1, ATTN_HEADS, 2 * ATTN_QK_DIM, tq), lambda bi, i: (bi, i, 0, 0, 0)),
            pl.BlockSpec((1, s, width), lambda bi, i: (bi, 0, COL_AK // width)),
            pl.BlockSpec((1, nq, ATTN_HEADS, ATTN_V_DIM, tq), lambda bi, i: (bi, 0, 0, 0, 0)),
            pl.BlockSpec((1, ATTN_V_DIM), lambda bi, i: (0, 0)),
        ] + cast_specs,
        out_specs=[pl.BlockSpec((1, tq, ATTN_WIDTH), lambda bi, i: (bi, i, 0))] + cast_specs,
        out_shape=[jax.ShapeDtypeStruct((b, s, ATTN_WIDTH), BF16)]
        + [jax.ShapeDtypeStruct(w.shape, BF16) for w in cast_weights],
        scratch_shapes=[
            pltpu.VMEM((ATTN_HEADS, 2 * ATTN_QK_DIM + LANES, 2 * tq), BF16),
            pltpu.VMEM((ATTN_HEADS, tq, LANES), BF16),
            pltpu.VMEM((ATTN_HEADS, ATTN_V_DIM + 2 * SUBLANES, 2 * tq), F32),
            pltpu.VMEM((ATTN_HEADS, 1, 2 * tq), F32),
        ],
        compiler_params=pltpu.CompilerParams(
            dimension_semantics=("parallel", "arbitrary"),
            vmem_limit_bytes=VMEM_LIMIT),
        name="attn",
    )(cs_terms, lamv, qt5, z3, vt5, gain, *cast_weights)


def _split3(x):
    hi = x.astype(BF16)
    r = x - hi.astype(F32)
    mid = r.astype(BF16)
    lo = (r - mid.astype(F32)).astype(BF16)
    return hi, mid, lo


def _mlstm_kernel(uq_ref, uk_ref, v_ref, mo_ref, gcol_ref, grow_ref, bcol_ref, brow_ref,
                  cw_ref, cb_ref, gn_ref, o_ref, extq, extk, dstq, dstk, c_scr, n_scr, m_scr,
                  *, chunk):
    L = chunk
    nh = MLSTM_HEADS
    dk = MLSTM_QK_DIM
    dv = MLSTM_V_DIM
    heads = range(nh)

    @pl.when(pl.program_id(1) == 0)
    def _():
        extq[:, pl.ds(0, SUBLANES), :] = jnp.zeros((nh, SUBLANES, dk), F32)
        extk[:, pl.ds(0, SUBLANES), :] = jnp.zeros((nh, SUBLANES, dk), F32)
        c_scr[...] = jnp.zeros_like(c_scr)
        n_scr[...] = jnp.zeros_like(n_scr)
        m_scr[...] = jnp.zeros_like(m_scr)

    def conv_silu(u_ref, ext, dst, col0, scale):
        n8 = L // SUBLANES
        first = SUBLANES - (CONV_WIDTH - 1)
        outs = []
        for h in heads:
            cols = pl.ds(col0 + h * dk, dk)
            ext[h, pl.ds(SUBLANES, L), :] = u_ref[0, :, pl.ds(h * dk, dk)].astype(F32)
            w = cw_ref[:, cols]
            bias = cb_ref[:, cols]
            rows = [ext[h, pl.ds(first + s, n8, stride=SUBLANES), :]
                    for s in range(SUBLANES + CONV_WIDTH - 1)]
            for r in range(SUBLANES):
                y = bias
                for tap in range(CONV_WIDTH):
                    y = y + rows[r + tap] * w[tap:tap + 1]
                y = y * _sigmoid(y)
                dst[h, pl.ds(r, n8, stride=SUBLANES), :] = y if scale is None else y * scale
            ext[h, pl.ds(0, SUBLANES), :] = ext[h, pl.ds(L, SUBLANES), :]
            outs.append(dst[h])
        return outs

    q = conv_silu(uq_ref, extq, dstq, 0, None)
    k = conv_silu(uk_ref, extk, dstk, MLSTM_QK_WIDTH, dk ** -0.5)
    qb = [x.astype(BF16) for x in q]
    kb = [x.astype(BF16) for x in k]
    vb = [v_ref[0, :, pl.ds(h * dv, dv)] for h in heads]

    g_c = gcol_ref[...] + bcol_ref[...]
    g_r = grow_ref[0] + brow_ref[:, 0:1]
    lf_c = _log_sigmoid(g_c)
    lf_r = _log_sigmoid(g_r)

    r_i = lax.broadcasted_iota(jnp.int32, (L, L), 0)
    c_i = lax.broadcasted_iota(jnp.int32, (L, L), 1)
    causal = c_i <= r_i
    tri = jnp.where(causal, 1.0, 0.0).astype(BF16)
    tri_t = jnp.where(r_i <= c_i, 1.0, 0.0).astype(BF16)
    hi, mid, lo = _split3(lf_c)
    b_c = _dot(tri, hi) + _dot(tri, mid) + _dot(tri, lo)
    hi, mid, lo = _split3(lf_r)
    b_r = _dot(hi, tri_t) + _dot(mid, tri_t) + _dot(lo, tri_t)
    g_sum = jnp.sum(lf_r, axis=1, keepdims=True)

    lane = lax.broadcasted_iota(jnp.int32, (L, LANES), 1)

    def col(x, idx):
        return jnp.sum(jnp.where(lane == idx, x, 0.0), axis=1, keepdims=True)

    i_col = [col(g_c, h) for h in heads]
    b_col = [col(b_c, nh + h) for h in heads]
    i_row = [g_r[h:h + 1, :] for h in heads]
    b_row = [b_r[nh + h:nh + h + 1, :] for h in heads]
    g_tot = [g_sum[nh + h:nh + h + 1, :] for h in heads]

    c_prev = [c_scr[h] for h in heads]
    n_prev = [n_scr[h] for h in heads]
    m_prev = [m_scr[h][:, 0:1] for h in heads]

    s_qk = [_dot_nt(qb[h], kb[h]) for h in heads]
    inter = [_dot(qb[h], c_prev[h].astype(BF16)) for h in heads]
    m_loc = [jnp.max(g_tot[h] - b_row[h] + i_row[h], axis=1, keepdims=True) for h in heads]
    kw = [k[h] * jnp.exp(g_tot[h] - b_col[h] + i_col[h] - m_loc[h]) for h in heads]
    c_loc = [_dot(kw[h].T.astype(BF16), vb[h]) for h in heads]

    d = [jnp.where(causal, b_col[h] - b_row[h] + i_row[h], NEG_BIG) for h in heads]
    qn = [jnp.sum(q[h] * n_prev[h], axis=1, keepdims=True) for h in heads]
    d_max = [jnp.max(d[h], axis=1, keepdims=True) for h in heads]
    m_inter = [b_col[h] + m_prev[h] for h in heads]
    m_j = [jnp.maximum(m_inter[h], d_max[h]) for h in heads]
    w_inter = [jnp.exp(m_inter[h] - m_j[h]) for h in heads]
    p = [s_qk[h] * jnp.exp(d[h] - m_j[h]) for h in heads]
    p_sum = [jnp.sum(p[h], axis=1, keepdims=True) for h in heads]
    pv = [_dot(p[h].astype(BF16), vb[h]) for h in heads]
    hval = []
    for h in heads:
        den = w_inter[h] * qn[h] + p_sum[h]
        num = w_inter[h] * inter[h] + pv[h]
        hval.append(num * (1.0 / jnp.maximum(jnp.abs(den), jnp.exp(-m_j[h]))))
    ms = [jnp.mean(hval[h] * hval[h], axis=-1, keepdims=True) for h in heads]
    for h in heads:
        hn = hval[h] * lax.rsqrt(ms[h] + EPS) * gn_ref[:, pl.ds(h * dv, dv)]
        gate = _sigmoid(mo_ref[0, :, pl.ds(h * dv, dv)].astype(F32))
        o_ref[0, :, pl.ds(h * dv, dv)] = (hn * gate).astype(BF16)

    for h in heads:
        n_loc = jnp.sum(kw[h], axis=0, keepdims=True)
        m_new = jnp.maximum(g_tot[h] + m_prev[h], m_loc[h])
        a_old = jnp.exp(g_tot[h] + m_prev[h] - m_new)
        a_new = jnp.exp(m_loc[h] - m_new)
        c_scr[h] = a_old * c_prev[h] + a_new * c_loc[h]
        n_scr[h] = a_old * n_prev[h] + a_new * n_loc
        m_scr[h] = jnp.broadcast_to(m_new, (1, LANES))


def _mlstm(z3, gates, gates_t, bias_col, bias_row, conv_w8, conv_b, gnorm, chunk=256):
    b, s, _ = z3.shape
    nc = s // chunk
    kern = functools.partial(_mlstm_kernel, chunk=chunk)
    qw = MLSTM_QK_WIDTH
    vw = MLSTM_WIDTH
    return pl.pallas_call(
        kern,
        grid=(b, nc),
        in_specs=[
            pl.BlockSpec((1, chunk, qw), lambda bi, c: (bi, c, COL_MQ // qw)),
            pl.BlockSpec((1, chunk, qw), lambda bi, c: (bi, c, COL_MK // qw)),
            pl.BlockSpec((1, chunk, vw), lambda bi, c: (bi, c, COL_MV // vw)),
            pl.BlockSpec((1, chunk, vw), lambda bi, c: (bi, c, COL_MO // vw)),
            pl.BlockSpec((chunk, LANES), lambda bi, c: (bi * nc + c, 0)),
            pl.BlockSpec((1, SUBLANES, chunk), lambda bi, c: (bi, 0, c)),
            pl.BlockSpec((1, LANES), lambda bi, c: (0, 0)),
            pl.BlockSpec((SUBLANES, LANES), lambda bi, c: (0, 0)),
            pl.BlockSpec((SUBLANES, 2 * qw), lambda bi, c: (0, 0)),
            pl.BlockSpec((1, 2 * qw), lambda bi, c: (0, 0)),
            pl.BlockSpec((1, vw), lambda bi, c: (0, 0)),
        ],
        out_specs=pl.BlockSpec((1, chunk, vw), lambda bi, c: (bi, c, 0)),
        out_shape=jax.ShapeDtypeStruct((b, s, vw), BF16),
        scratch_shapes=[
            pltpu.VMEM((MLSTM_HEADS, chunk + 2 * SUBLANES, MLSTM_QK_DIM), F32),
            pltpu.VMEM((MLSTM_HEADS, chunk + 2 * SUBLANES, MLSTM_QK_DIM), F32),
            pltpu.VMEM((MLSTM_HEADS, chunk, MLSTM_QK_DIM), F32),
            pltpu.VMEM((MLSTM_HEADS, chunk, MLSTM_QK_DIM), F32),
            pltpu.VMEM((MLSTM_HEADS, MLSTM_QK_DIM, MLSTM_V_DIM), F32),
            pltpu.VMEM((MLSTM_HEADS, 1, MLSTM_QK_DIM), F32),
            pltpu.VMEM((MLSTM_HEADS, 1, LANES), F32),
        ],
        compiler_params=pltpu.CompilerParams(
            dimension_semantics=("parallel", "arbitrary"),
            vmem_limit_bytes=VMEM_LIMIT),
        name="mlstm",
    )(z3, z3, z3, z3, gates, gates_t, bias_col, bias_row, conv_w8, conv_b, gnorm)


def _cross_delta(x, g_ref, wq_ref, kv_ref, wo_ref):
    hc = _rms(x, g_ref[...]).astype(BF16)
    cq = (_dot(hc, wq_ref[...]) * (CROSS_HEAD_DIM ** -0.5)).astype(BF16)
    outs = []
    for hh in range(CROSS_HEADS):
        lo = hh * CROSS_HEAD_DIM
        qh = cq[:, lo:lo + CROSS_HEAD_DIM]
        kh = kv_ref[0, :, lo:lo + CROSS_HEAD_DIM]
        vh = kv_ref[0, :, CROSS_WIDTH + lo:CROSS_WIDTH + lo + CROSS_HEAD_DIM]
        s = _dot_nt(qh, kh)
        p = jnp.exp(s - jnp.max(s, axis=-1, keepdims=True))
        l = jnp.sum(p, axis=-1, keepdims=True)
        outs.append((_dot(p.astype(BF16), vh) * (1.0 / l)).astype(BF16))
    co = jnp.concatenate(outs, axis=1)
    return _dot(co, wo_ref[...])


def _merge_kernel(att_ref, hm_ref, ga_ref, gm_ref, x_ref, wa_ref, wm_ref, wo_ref,
                  gc_ref, wcq_ref, kv_ref, wco_ref, o_ref):
    j = pl.program_id(1)

    @pl.when(j == 0)
    def _():
        o_ref[...] = x_ref[...]

    a = _dot(att_ref[...], wa_ref[...])
    bm = _dot(hm_ref[...], wm_ref[...])
    y = _sigmoid(ga_ref[...].astype(F32)) * a + _sigmoid(gm_ref[...].astype(F32)) * bm
    o_ref[...] += _dot(y.astype(BF16), wo_ref[...])

    @pl.when(j == pl.num_programs(1) - 1)
    def _():
        x1 = o_ref[...]
        o_ref[...] = x1 + _cross_delta(x1, gc_ref, wcq_ref, kv_ref, wco_ref)


def _merge(att2d, hm2d, z2d, x2d, wa, wm, wo, gc, wcq, ckv, wco, seq, tm=512, tn=1024):
    t = x2d.shape[0]
    nj = D_MODEL // tn
    n_mem = ckv.shape[1]
    per_batch = seq // tm
    return pl.pallas_call(
        _merge_kernel,
        grid=(t // tm, nj),
        in_specs=[
            pl.BlockSpec((tm, ATTN_WIDTH), lambda i, j: (i, 0)),
            pl.BlockSpec((tm, MLSTM_WIDTH), lambda i, j: (i, 0)),
            pl.BlockSpec((tm, tn), lambda i, j: (i, COL_GA // tn + j)),
            pl.BlockSpec((tm, tn), lambda i, j: (i, COL_GM // tn + j)),
            pl.BlockSpec((tm, D_MODEL), lambda i, j: (i, 0)),
            pl.BlockSpec((ATTN_WIDTH, tn), lambda i, j: (0, j)),
            pl.BlockSpec((MLSTM_WIDTH, tn), lambda i, j: (0, j)),
            pl.BlockSpec((tn, D_MODEL), lambda i, j: (j, 0)),
            pl.BlockSpec((1, D_MODEL), lambda i, j: (0, 0)),
            pl.BlockSpec((D_MODEL, CROSS_WIDTH), lambda i, j: (0, 0)),
            pl.BlockSpec((1, n_mem, 2 * CROSS_WIDTH), lambda i, j: (i // per_batch, 0, 0)),
            pl.BlockSpec((CROSS_WIDTH, D_MODEL), lambda i, j: (0, 0)),
        ],
        out_specs=pl.BlockSpec((tm, D_MODEL), lambda i, j: (i, 0)),
        out_shape=jax.ShapeDtypeStruct((t, D_MODEL), F32),
        compiler_params=pltpu.CompilerParams(
            dimension_semantics=("parallel", "arbitrary"), vmem_limit_bytes=VMEM_LIMIT),
        name="merge",
    )(att2d, hm2d, z2d, z2d, x2d, wa, wm, wo, gc, wcq, ckv, wco)


def _mlp_kernel(x_ref, g_ref, wu_ref, wd_ref, gf_ref, o_ref, h_scr, acc, *, final_norm):
    j = pl.program_id(1)

    @pl.when(j == 0)
    def _():
        x = x_ref[...]
        h_scr[...] = _rms(x, g_ref[...]).astype(BF16)
        acc[...] = x

    u = jnp.square(jnp.maximum(_dot(h_scr[...], wu_ref[...]), 0.0)).astype(BF16)
    acc[...] += _dot(u, wd_ref[...])

    @pl.when(j == pl.num_programs(1) - 1)
    def _():
        if final_norm:
            o_ref[...] = _rms(acc[...], gf_ref[...])
        else:
            o_ref[...] = acc[...]


def _mlp(x2d, g, wu, wd, gf, final_norm, tm=512, tf=1024):
    t = x2d.shape[0]
    kern = functools.partial(_mlp_kernel, final_norm=final_norm)
    return pl.pallas_call(
        kern,
        grid=(t // tm, D_FF // tf),
        in_specs=[
            pl.BlockSpec((tm, D_MODEL), lambda i, j: (i, 0)),
            pl.BlockSpec((1, D_MODEL), lambda i, j: (0, 0)),
            pl.BlockSpec((D_MODEL, tf), lambda i, j: (0, j)),
            pl.BlockSpec((tf, D_MODEL), lambda i, j: (j, 0)),
            pl.BlockSpec((1, D_MODEL), lambda i, j: (0, 0)),
        ],
        out_specs=pl.BlockSpec((tm, D_MODEL), lambda i, j: (i, 0)),
        out_shape=jax.ShapeDtypeStruct((t, D_MODEL), F32),
        scratch_shapes=[pltpu.VMEM((tm, D_MODEL), BF16), pltpu.VMEM((tm, D_MODEL), F32)],
        compiler_params=pltpu.CompilerParams(
            dimension_semantics=("parallel", "arbitrary"), vmem_limit_bytes=VMEM_LIMIT),
        name="mlp",
    )(x2d, g, wu, wd, gf)


def _alibi_terms():
    slopes = 2.0 ** (-8.0 * np.arange(1, ATTN_HEADS + 1, dtype=np.float64) / ATTN_HEADS)
    rem = slopes * LOG2E
    terms = []
    for _ in range(ALIBI_TERMS):
        t = rem.astype(np.float32).astype(ml_dtypes.bfloat16).astype(np.float64)
        terms.append(t)
        rem = rem - t
    return np.stack(terms, axis=1).reshape(-1).astype(np.float32)


def _pad_rows(a, rows):
    return jnp.pad(a, ((0, rows - a.shape[0]), (0, 0)))


def kernel(x, mem, norm_mix, w_in, b_igate, b_fgate, conv_w, conv_b, lam_q1, lam_k1, lam_q2, lam_k2, attn_norm, mlstm_norm, w_attn_br, w_mlstm_br, w_out, norm_cross, norm_mem, w_cq, w_ckv, w_co, norm_mlp, w_up, w_down, norm_final):
    b, s, _ = x.shape
    t = b * s
    tq = 256
    cs_terms = jnp.asarray(_alibi_terms())
    x2d = x.reshape(t, D_MODEL)
    for l in range(DEPTH):
        lam_init = 0.8 - 0.6 * math.exp(-0.3 * l)
        w = w_in[l]
        w_a, w_b, w_gate = _wprep(w.T)

        z2d, gates, qt, vt = _inproj(x2d, norm_mix[l][None, :], w_a, w_b, w_gate, tq)
        z3 = z2d.reshape(b, s, IN_MAIN)

        qt5 = qt.reshape(b, s // tq, ATTN_HEADS, 2 * ATTN_QK_DIM, tq)
        vt5 = vt.reshape(b, s // tq, ATTN_HEADS, ATTN_V_DIM, tq)
        lamv = _pad_rows(jnp.pad(jnp.stack([lam_q1[l], lam_k1[l], lam_q2[l], lam_k2[l]]),
                                 ((0, 0), (0, LANES - ATTN_QK_DIM))), SUBLANES)
        att, w_out_b, w_abr_b, w_mbr_b, w_up_b, w_down_b = _attn(
            z3, qt5, vt5, cs_terms, lamv, attn_norm[l][None, :],
            (w_out[l], w_attn_br[l], w_mlstm_br[l], w_up[l], w_down[l]), lam_init, tq=tq)

        gates_t = gates[:, :SUBLANES].reshape(b, s, SUBLANES).transpose(0, 2, 1)
        gate_bias = jnp.concatenate([b_igate[l], b_fgate[l]])
        bias_col = jnp.pad(gate_bias, (0, LANES - gate_bias.shape[0]))[None, :]
        bias_row = jnp.broadcast_to(gate_bias[:, None], (SUBLANES, LANES))
        hm = _mlstm(z3, gates, gates_t, bias_col, bias_row, _pad_rows(conv_w[l], SUBLANES),
                    conv_b[l][None, :], mlstm_norm[l][None, :])

        ckv = _memkv(mem, norm_mem[l][None, :], w_ckv[l])
        x2d = _merge(att.reshape(t, ATTN_WIDTH), hm.reshape(t, MLSTM_WIDTH), z2d, x2d,
                     w_abr_b, w_mbr_b, w_out_b, norm_cross[l][None, :], w_cq[l].astype(BF16), ckv,
                     w_co[l].astype(BF16), s)

        x2d = _mlp(x2d, norm_mlp[l][None, :], w_up_b, w_down_b,
                   norm_final[None, :], final_norm=(l == DEPTH - 1))
    return x2d.reshape(b, s, D_MODEL)
```

```python
import functools
import math

import ml_dtypes
import numpy as np
import jax
import jax.numpy as jnp
from jax import lax
from jax.experimental import pallas as pl
from jax.experimental.pallas import tpu as pltpu

F32 = jnp.float32
BF16 = jnp.bfloat16

D_MODEL = 2048
DEPTH = 1
ATTN_HEADS = 8
ATTN_QK_DIM = 64
ATTN_V_DIM = 128
ATTN_WIDTH = ATTN_HEADS * ATTN_V_DIM
MLSTM_HEADS = 4
MLSTM_QK_DIM = 128
MLSTM_V_DIM = 256
MLSTM_QK_WIDTH = MLSTM_HEADS * MLSTM_QK_DIM
MLSTM_WIDTH = MLSTM_HEADS * MLSTM_V_DIM
CONV_WIDTH = 4
CROSS_HEADS = 4
CROSS_HEAD_DIM = 128
CROSS_WIDTH = CROSS_HEADS * CROSS_HEAD_DIM
D_FF = 4 * D_MODEL
EPS = 1e-6
LANES = 128
SUBLANES = 8
NEG_BIG = -1e30
LOG2E = 1.4426950408889634
ALIBI_TERMS = 3

COL_AK = 0
COL_MQ = 1024
COL_MK = 1536
COL_MV = 2048
COL_MO = 3072
COL_GA = 4096
COL_GM = 6144
IN_MAIN = 8192
W_AQ_LO, W_AQ_HI = 0, 1024
W_AV_LO = 2048
GATE_LO = 6144
GATE_HI = 6152

VMEM_LIMIT = 56 * 1024 * 1024


def _rms(x, g):
    ms = jnp.mean(x * x, axis=-1, keepdims=True)
    return x * lax.rsqrt(ms + EPS) * g


def _sigmoid(x):
    return 0.5 * jnp.tanh(0.5 * x) + 0.5


def _log_sigmoid(x):
    return jnp.minimum(x, 0.0) - jnp.log(1.0 + jnp.exp(-jnp.abs(x)))


def _dot(a, b):
    return jnp.dot(a, b, preferred_element_type=F32)


def _dot_nt(a, b):
    return lax.dot_general(a, b, (((1,), (1,)), ((), ())), preferred_element_type=F32)


def _memkv_kernel(mem_ref, g_ref, w_ref, o_ref):
    mn = _rms(mem_ref[0], g_ref[...]).astype(BF16)
    o_ref[0] = _dot(mn, w_ref[...].astype(BF16)).astype(BF16)


def _memkv(mem, g, w_ckv):
    b, n_mem, _ = mem.shape
    return pl.pallas_call(
        _memkv_kernel,
        grid=(b,),
        in_specs=[
            pl.BlockSpec((1, n_mem, D_MODEL), lambda i: (i, 0, 0)),
            pl.BlockSpec((1, D_MODEL), lambda i: (0, 0)),
            pl.BlockSpec((D_MODEL, 2 * CROSS_WIDTH), lambda i: (0, 0)),
        ],
        out_specs=pl.BlockSpec((1, n_mem, 2 * CROSS_WIDTH), lambda i: (i, 0, 0)),
        out_shape=jax.ShapeDtypeStruct((b, n_mem, 2 * CROSS_WIDTH), BF16),
        compiler_params=pltpu.CompilerParams(
            dimension_semantics=("arbitrary",), vmem_limit_bytes=VMEM_LIMIT),
        name="memkv",
    )(mem, g, w_ckv)


def _wprep_kernel(a_ref, nxt_ref, wa_out, wb_out, wg_out, *, na):
    k = pl.program_id(0)
    ng = GATE_HI - GATE_LO

    @pl.when(k < na)
    def _():
        scale = jnp.where(k == W_AQ_LO // a_ref.shape[0], ATTN_QK_DIM ** -0.5 * LOG2E, 1.0)
        wa_out[...] = (a_ref[...] * scale).astype(BF16)

    @pl.when(k >= na)
    def _():
        wb_out[...] = jnp.concatenate([a_ref[pl.ds(ng, a_ref.shape[0] - ng), :], nxt_ref[0]],
                                      axis=0).astype(BF16)

    @pl.when(k == na)
    def _():
        pad = jnp.zeros((LANES - ng, D_MODEL), F32)
        wg_out[...] = jnp.concatenate([a_ref[pl.ds(0, ng), :], pad], axis=0).astype(BF16)


def _wprep(wt, tn=1024):
    rows = wt.shape[0]
    ng = GATE_HI - GATE_LO
    assert ng == SUBLANES and GATE_LO % tn == 0 and (rows - GATE_HI) % tn == 0
    assert W_AQ_LO == 0 and W_AQ_HI == tn
    na = GATE_LO // tn
    nb = (rows - GATE_HI) // tn
    groups = wt.reshape(rows // ng, ng, D_MODEL)
    per_block = tn // ng
    return pl.pallas_call(
        functools.partial(_wprep_kernel, na=na),
        grid=(na + nb,),
        in_specs=[
            pl.BlockSpec((tn, D_MODEL), lambda k: (k, 0)),
            pl.BlockSpec((1, ng, D_MODEL), lambda k: ((k + 1) * per_block, 0, 0)),
        ],
        out_specs=[
            pl.BlockSpec((tn, D_MODEL), lambda k: (jnp.minimum(k, na - 1), 0)),
            pl.BlockSpec((tn, D_MODEL), lambda k: (jnp.maximum(k - na, 0), 0)),
            pl.BlockSpec((LANES, D_MODEL), lambda k: (0, 0)),
        ],
        out_shape=[
            jax.ShapeDtypeStruct((GATE_LO, D_MODEL), BF16),
            jax.ShapeDtypeStruct((rows - GATE_HI, D_MODEL), BF16),
            jax.ShapeDtypeStruct((LANES, D_MODEL), BF16),
        ],
        compiler_params=pltpu.CompilerParams(
            dimension_semantics=("arbitrary",), vmem_limit_bytes=VMEM_LIMIT),
        name="wprep",
    )(wt, groups)


def _inproj_kernel(x_ref, g_ref, wa_hbm, wb_hbm, wg_ref, z_ref, gate_ref, qt_ref, vt_ref,
                   h_scr, wbuf, sem, *, na, nj, jq, jv):
    i = pl.program_id(0)
    j = pl.program_id(1)
    depth, tn, _ = wbuf.shape
    step = i * nj + j
    total = pl.num_programs(0) * nj

    def copies(s, slot):
        jj = s % nj
        ca = pltpu.make_async_copy(
            wa_hbm.at[pl.ds(pl.multiple_of(jnp.minimum(jj, na - 1) * tn, tn), tn), :],
            wbuf.at[slot], sem.at[slot])
        cb = pltpu.make_async_copy(
            wb_hbm.at[pl.ds(pl.multiple_of(jnp.maximum(jj - na, 0) * tn, tn), tn), :],
            wbuf.at[slot], sem.at[slot])
        return jj < na, ca, cb

    def start(s, slot):
        first, ca, cb = copies(s, slot)

        @pl.when(first)
        def _():
            ca.start()

        @pl.when(jnp.logical_not(first))
        def _():
            cb.start()

    @pl.when(step == 0)
    def _():
        for d in range(depth - 1):
            start(d, d)

    ahead = step + depth - 1

    @pl.when(ahead < total)
    def _():
        start(ahead, ahead % depth)

    @pl.when(j == 0)
    def _():
        h = _rms(x_ref[...], g_ref[...]).astype(BF16)
        h_scr[...] = h
        gate_ref[...] = _dot_nt(h, wg_ref[...])

    slot = step % depth
    first, ca, cb = copies(step, slot)

    @pl.when(first)
    def _():
        ca.wait()

    @pl.when(jnp.logical_not(first))
    def _():
        cb.wait()

    w_ref = wbuf.at[slot]

    @pl.when((j != jq) & (j != jv))
    def _():
        z_ref[...] = _dot_nt(h_scr[...], w_ref[...]).astype(BF16)

    def transposed(out_ref):
        xt = _dot_nt(w_ref[...], h_scr[...]).astype(BF16)
        tile = out_ref.shape[2]
        for n in range(out_ref.shape[0]):
            out_ref[n] = xt[:, n * tile:(n + 1) * tile]

    @pl.when(j == jq)
    def _():
        transposed(qt_ref)

    @pl.when(j == jv)
    def _():
        transposed(vt_ref)


def _inproj(x2d, g, w_a, w_b, w_gate, tkv, tm=1024, tn=1024, depth=3):
    t = x2d.shape[0]
    na = w_a.shape[0] // tn
    n = w_a.shape[0] + w_b.shape[0]
    ni, nj = t // tm, n // tn
    assert tn == ATTN_WIDTH and W_AQ_LO % tn == 0 and W_AQ_HI - W_AQ_LO == tn and W_AV_LO % tn == 0
    jq, jv = W_AQ_LO // tn, W_AV_LO // tn
    assert jq < jv and depth - 1 <= ni * nj

    def z_tile(i, j):
        skipped = (j >= jq).astype(jnp.int32) + (j >= jv).astype(jnp.int32)
        return i, jnp.maximum(j - skipped, 0)

    return pl.pallas_call(
        functools.partial(_inproj_kernel, na=na, nj=nj, jq=jq, jv=jv),
        grid=(ni, nj),
        in_specs=[
            pl.BlockSpec((tm, D_MODEL), lambda i, j: (i, 0)),
            pl.BlockSpec((1, D_MODEL), lambda i, j: (0, 0)),
            pl.BlockSpec(memory_space=pl.ANY),
            pl.BlockSpec(memory_space=pl.ANY),
            pl.BlockSpec((LANES, D_MODEL), lambda i, j: (0, 0)),
        ],
        out_specs=[
            pl.BlockSpec((tm, tn), z_tile),
            pl.BlockSpec((tm, LANES), lambda i, j: (i, 0)),
            pl.BlockSpec((tm // tkv, ATTN_WIDTH, tkv), lambda i, j: (i, 0, 0)),
            pl.BlockSpec((tm // tkv, ATTN_WIDTH, tkv), lambda i, j: (i, 0, 0)),
        ],
        out_shape=[
            jax.ShapeDtypeStruct((t, n - 2 * tn), BF16),
            jax.ShapeDtypeStruct((t, LANES), F32),
            jax.ShapeDtypeStruct((t // tkv, ATTN_WIDTH, tkv), BF16),
            jax.ShapeDtypeStruct((t // tkv, ATTN_WIDTH, tkv), BF16),
        ],
        scratch_shapes=[
            pltpu.VMEM((tm, D_MODEL), BF16),
            pltpu.VMEM((depth, tn, D_MODEL), BF16),
            pltpu.SemaphoreType.DMA((depth,)),
        ],
        compiler_params=pltpu.CompilerParams(
            dimension_semantics=("arbitrary", "arbitrary"), vmem_limit_bytes=VMEM_LIMIT),
        name="inproj",
    )(x2d, g, w_a, w_b, w_gate)


def _attn_kernel(cs_ref, lam_ref, qt_ref, k_ref, vt_ref, gain_ref, *refs, ncast, tq, lam_init):
    cast_in = refs[:ncast]
    o_ref = refs[ncast]
    cast_out = refs[ncast + 1:2 * ncast + 1]
    qs_ref, kf_ref, acc_ref, m_ref = refs[2 * ncast + 1:]
    tk = tq
    nfeat = ALIBI_TERMS
    dq = 2 * ATTN_QK_DIM
    dv = ATTN_V_DIM
    i = pl.program_id(1)
    cs = [[cs_ref[nfeat * h + t] for t in range(nfeat)] for h in range(ATTN_HEADS)]
    cs_tot = [sum(c[1:], c[0]) for c in cs]

    @pl.when(i == 0)
    def _():
        klane = lax.broadcasted_iota(jnp.int32, (tk, LANES), 1)
        krow = lax.broadcasted_iota(jnp.int32, (tk, LANES), 0).astype(F32)
        frow = lax.broadcasted_iota(jnp.int32, (LANES, 2 * tq), 0)
        fcol = lax.broadcasted_iota(jnp.int32, (LANES, 2 * tq), 1)
        fcol = jnp.where(fcol >= tq, fcol - tq, fcol).astype(F32)
        for h in range(ATTN_HEADS):
            kf = jnp.where(klane < nfeat, krow, 0.0)
            qf = jnp.where((frow >= nfeat) & (frow < 2 * nfeat), -fcol, 0.0)
            for t in range(nfeat):
                kf = jnp.where(klane == nfeat + t, cs[h][t], kf)
                qf = jnp.where(frow == t, cs[h][t], qf)
            kf_ref[h] = kf.astype(BF16)
            qs_ref[h, pl.ds(dq, LANES), :] = qf.astype(BF16)

    row = lax.broadcasted_iota(jnp.int32, (dq, tq), 0)
    for h in range(ATTN_HEADS):
        qt = qt_ref[0, 0, h]
        zero = jnp.zeros_like(qt)
        qs_ref[h, pl.ds(0, dq), pl.ds(0, tq)] = jnp.where(row < ATTN_QK_DIM, qt, zero)
        qs_ref[h, pl.ds(0, dq), pl.ds(tq, tq)] = jnp.where(row >= ATTN_QK_DIM, qt, zero)
    acc_ref[...] = jnp.zeros_like(acc_ref)
    m_ref[...] = jnp.full_like(m_ref, NEG_BIG)

    orow = lax.broadcasted_iota(jnp.int32, (2 * SUBLANES, tk), 0)
    ones_blk = jnp.where(orow == 0, 1.0, 0.0).astype(BF16)

    def scores(h, j):
        k = k_ref[0, pl.ds(pl.multiple_of(j * tk, tk), tk), pl.ds(h * dq, dq)]
        kx = jnp.concatenate([k, kf_ref[h]], axis=1)
        return _dot(kx, qs_ref[h])

    def softmax_pv(h, j, t, diagonal):
        if diagonal:
            krow = lax.broadcasted_iota(jnp.int32, (tk, 2 * tq), 0)
            qcol = lax.broadcasted_iota(jnp.int32, (tk, 2 * tq), 1)
            qcol = jnp.where(qcol >= tq, qcol - tq, qcol)
            t = jnp.where(krow <= qcol, t, NEG_BIG)
        soff = cs_tot[h] * ((j - i) * tk).astype(F32)
        m = m_ref[h]
        m_new = jnp.maximum(m, jnp.max(t, axis=0, keepdims=True) + soff)
        alpha = jnp.exp2(m - m_new)
        p = jnp.exp2(t - (m_new - soff)).astype(BF16)
        vx = jnp.concatenate([vt_ref[0, j, h], ones_blk], axis=0)
        acc_ref[h] = alpha * acc_ref[h] + _dot(vx, p)
        m_ref[h] = m_new

    def run(units):
        ahead = 2
        pending = [scores(*u[:2]) for u in units[:ahead]]
        for n, (h, j, diagonal) in enumerate(units):
            if n + ahead < len(units):
                pending.append(scores(*units[n + ahead][:2]))
            softmax_pv(h, j, pending.pop(0), diagonal)

    def tile_units(j, diagonal):
        return [(h, j, diagonal) for h in range(ATTN_HEADS)]

    group = 4

    def body(jj, carry):
        units = []
        for n in range(group):
            units += tile_units(group * jj + n, False)
        run(units)
        return carry

    lax.fori_loop(0, i // group, body, 0)

    for rest in range(group):
        @pl.when(i % group == rest)
        def _(rest=rest):
            units = []
            for n in range(rest):
                units += tile_units(i - rest + n, False)
            run(units + tile_units(i, True))

    lv = lam_ref[...]
    d1 = jnp.sum(lv[0:1] * lv[1:2], axis=-1, keepdims=True)
    d2 = jnp.sum(lv[2:3] * lv[3:4], axis=-1, keepdims=True)
    lam = jnp.exp(d1) - jnp.exp(d2) + lam_init
    for h in range(ATTN_HEADS):
        l = acc_ref[h, pl.ds(dv, 1), :]
        rl = 1.0 / l
        out = (acc_ref[h, pl.ds(0, dv), pl.ds(0, tq)] * rl[:, :tq]
               - lam * (acc_ref[h, pl.ds(0, dv), pl.ds(tq, tq)] * rl[:, tq:]))
        ms = jnp.mean(out * out, axis=0, keepdims=True)
        on = out * lax.rsqrt(ms + EPS)
        o_ref[0, :, pl.ds(h * dv, dv)] = (on.T * gain_ref[...] * (1.0 - lam_init)).astype(BF16)

    for src, dst in zip(cast_in, cast_out):
        dst[...] = src[...].astype(BF16)


def _attn(z3, qt5, vt5, cs_terms, lamv, gain, cast_weights, lam_init, tq=256):
    b, s, _ = z3.shape
    nq = s // tq
    steps = b * nq
    cast_specs = [pl.BlockSpec((w.shape[0] // steps, w.shape[1]), lambda bi, i: (bi * nq + i, 0))
                  for w in cast_weights]
    kern = functools.partial(_attn_kernel, ncast=len(cast_weights), tq=tq, lam_init=lam_init)
    width = ATTN_HEADS * 2 * ATTN_QK_DIM
    return pl.pallas_call(
        kern,
        grid=(b, nq),
        in_specs=[
            pl.BlockSpec(memory_space=pltpu.SMEM),
            pl.BlockSpec((SUBLANES, LANES), lambda bi, i: (0, 0)),
            pl.BlockSpec((1, 1, ATTN_HEADS, 2 * ATTN_QK_DIM, tq), lambda bi, i: (bi, i, 0, 0, 0)),
            pl.BlockSpec((1, s, width), lambda bi, i: (bi, 0, COL_AK // width)),
            pl.BlockSpec((1, nq, ATTN_HEADS, ATTN_V_DIM, tq), lambda bi, i: (bi, 0, 0, 0, 0)),
            pl.BlockSpec((1, ATTN_V_DIM), lambda bi, i: (0, 0)),
        ] + cast_specs,
        out_specs=[pl.BlockSpec((1, tq, ATTN_WIDTH), lambda bi, i: (bi, i, 0))] + cast_specs,
        out_shape=[jax.ShapeDtypeStruct((b, s, ATTN_WIDTH), BF16)]
        + [jax.ShapeDtypeStruct(w.shape, BF16) for w in cast_weights],
        scratch_shapes=[
            pltpu.VMEM((ATTN_HEADS, 2 * ATTN_QK_DIM + LANES, 2 * tq), BF16),
            pltpu.VMEM((ATTN_HEADS, tq, LANES), BF16),
            pltpu.VMEM((ATTN_HEADS, ATTN_V_DIM + 2 * SUBLANES, 2 * tq), F32),
            pltpu.VMEM((ATTN_HEADS, 1, 2 * tq), F32),
        ],
        compiler_params=pltpu.CompilerParams(
            dimension_semantics=("parallel", "arbitrary"),
            vmem_limit_bytes=VMEM_LIMIT),
        name="attn",
    )(cs_terms, lamv, qt5, z3, vt5, gain, *cast_weights)


def _split3(x):
    hi = x.astype(BF16)
    r = x - hi.astype(F32)
    mid = r.astype(BF16)
    lo = (r - mid.astype(F32)).astype(BF16)
    return hi, mid, lo


def _mlstm_kernel(uq_ref, uk_ref, v_ref, mo_ref, gcol_ref, grow_ref, bcol_ref, brow_ref,
                  cw_ref, cb_ref, gn_ref, o_ref, extq, extk, dstq, dstk, c_scr, n_scr, m_scr,
                  *, chunk):
    L = chunk
    nh = MLSTM_HEADS
    dk = MLSTM_QK_DIM
    dv = MLSTM_V_DIM
    heads = range(nh)

    @pl.when(pl.program_id(1) == 0)
    def _():
        extq[:, pl.ds(0, SUBLANES), :] = jnp.zeros((nh, SUBLANES, dk), F32)
        extk[:, pl.ds(0, SUBLANES), :] = jnp.zeros((nh, SUBLANES, dk), F32)
        c_scr[...] = jnp.zeros_like(c_scr)
        n_scr[...] = jnp.zeros_like(n_scr)
        m_scr[...] = jnp.zeros_like(m_scr)

    def conv_silu(u_ref, ext, dst, col0, scale):
        n8 = L // SUBLANES
        first = SUBLANES - (CONV_WIDTH - 1)
        outs = []
        for h in heads:
            cols = pl.ds(col0 + h * dk, dk)
            ext[h, pl.ds(SUBLANES, L), :] = u_ref[0, :, pl.ds(h * dk, dk)].astype(F32)
            w = cw_ref[:, cols]
            bias = cb_ref[:, cols]
            rows = [ext[h, pl.ds(first + s, n8, stride=SUBLANES), :]
                    for s in range(SUBLANES + CONV_WIDTH - 1)]
            for r in range(SUBLANES):
                y = bias
                for tap in range(CONV_WIDTH):
                    y = y + rows[r + tap] * w[tap:tap + 1]
                y = y * _sigmoid(y)
                dst[h, pl.ds(r, n8, stride=SUBLANES), :] = y if scale is None else y * scale
            ext[h, pl.ds(0, SUBLANES), :] = ext[h, pl.ds(L, SUBLANES), :]
            outs.append(dst[h])
        return outs

    q = conv_silu(uq_ref, extq, dstq, 0, None)
    k = conv_silu(uk_ref, extk, dstk, MLSTM_QK_WIDTH, dk ** -0.5)
    qb = [x.astype(BF16) for x in q]
    kb = [x.astype(BF16) for x in k]
    vb = [v_ref[0, :, pl.ds(h * dv, dv)] for h in heads]

    g_c = gcol_ref[...] + bcol_ref[...]
    g_r = grow_ref[0] + brow_ref[:, 0:1]
    lf_c = _log_sigmoid(g_c)
    lf_r = _log_sigmoid(g_r)

    r_i = lax.broadcasted_iota(jnp.int32, (L, L), 0)
    c_i = lax.broadcasted_iota(jnp.int32, (L, L), 1)
    causal = c_i <= r_i
    tri = jnp.where(causal, 1.0, 0.0).astype(BF16)
    tri_t = jnp.where(r_i <= c_i, 1.0, 0.0).astype(BF16)
    hi, mid, lo = _split3(lf_c)
    b_c = _dot(tri, hi) + _dot(tri, mid) + _dot(tri, lo)
    hi, mid, lo = _split3(lf_r)
    b_r = _dot(hi, tri_t) + _dot(mid, tri_t) + _dot(lo, tri_t)
    g_sum = jnp.sum(lf_r, axis=1, keepdims=True)

    lane = lax.broadcasted_iota(jnp.int32, (L, LANES), 1)

    def col(x, idx):
        return jnp.sum(jnp.where(lane == idx, x, 0.0), axis=1, keepdims=True)

    i_col = [col(g_c, h) for h in heads]
    b_col = [col(b_c, nh + h) for h in heads]
    i_row = [g_r[h:h + 1, :] for h in heads]
    b_row = [b_r[nh + h:nh + h + 1, :] for h in heads]
    g_tot = [g_sum[nh + h:nh + h + 1, :] for h in heads]

    c_prev = [c_scr[h] for h in heads]
    n_prev = [n_scr[h] for h in heads]
    m_prev = [m_scr[h][:, 0:1] for h in heads]

    s_qk = [_dot_nt(qb[h], kb[h]) for h in heads]
    inter = [_dot(qb[h], c_prev[h].astype(BF16)) for h in heads]
    m_loc = [jnp.max(g_tot[h] - b_row[h] + i_row[h], axis=1, keepdims=True) for h in heads]
    kw = [k[h] * jnp.exp(g_tot[h] - b_col[h] + i_col[h] - m_loc[h]) for h in heads]
    c_loc = [_dot(kw[h].T.astype(BF16), vb[h]) for h in heads]

    d = [jnp.where(causal, b_col[h] - b_row[h] + i_row[h], NEG_BIG) for h in heads]
    qn = [jnp.sum(q[h] * n_prev[h], axis=1, keepdims=True) for h in heads]
    d_max = [jnp.max(d[h], axis=1, keepdims=True) for h in heads]
    m_inter = [b_col[h] + m_prev[h] for h in heads]
    m_j = [jnp.maximum(m_inter[h], d_max[h]) for h in heads]
    w_inter = [jnp.exp(m_inter[h] - m_j[h]) for h in heads]
    p = [s_qk[h] * jnp.exp(d[h] - m_j[h]) for h in heads]
    p_sum = [jnp.sum(p[h], axis=1, keepdims=True) for h in heads]
    pv = [_dot(p[h].astype(BF16), vb[h]) for h in heads]
    hval = []
    for h in heads:
        den = w_inter[h] * qn[h] + p_sum[h]
        num = w_inter[h] * inter[h] + pv[h]
        hval.append(num * (1.0 / jnp.maximum(jnp.abs(den), jnp.exp(-m_j[h]))))
    ms = [jnp.mean(hval[h] * hval[h], axis=-1, keepdims=True) for h in heads]
    for h in heads:
        hn = hval[h] * lax.rsqrt(ms[h] + EPS) * gn_ref[:, pl.ds(h * dv, dv)]
        gate = _sigmoid(mo_ref[0, :, pl.ds(h * dv, dv)].astype(F32))
        o_ref[0, :, pl.ds(h * dv, dv)] = (hn * gate).astype(BF16)

    for h in heads:
        n_loc = jnp.sum(kw[h], axis=0, keepdims=True)
        m_new = jnp.maximum(g_tot[h] + m_prev[h], m_loc[h])
        a_old = jnp.exp(g_tot[h] + m_prev[h] - m_new)
        a_new = jnp.exp(m_loc[h] - m_new)
        c_scr[h] = a_old * c_prev[h] + a_new * c_loc[h]
        n_scr[h] = a_old * n_prev[h] + a_new * n_loc
        m_scr[h] = jnp.broadcast_to(m_new, (1, LANES))


def _mlstm(z3, gates, gates_t, bias_col, bias_row, conv_w8, conv_b, gnorm, chunk=256):
    b, s, _ = z3.shape
    nc = s // chunk
    kern = functools.partial(_mlstm_kernel, chunk=chunk)
    qw = MLSTM_QK_WIDTH
    vw = MLSTM_WIDTH
    return pl.pallas_call(
        kern,
        grid=(b, nc),
        in_specs=[
            pl.BlockSpec((1, chunk, qw), lambda bi, c: (bi, c, COL_MQ // qw)),
            pl.BlockSpec((1, chunk, qw), lambda bi, c: (bi, c, COL_MK // qw)),
            pl.BlockSpec((1, chunk, vw), lambda bi, c: (bi, c, COL_MV // vw)),
            pl.BlockSpec((1, chunk, vw), lambda bi, c: (bi, c, COL_MO // vw)),
            pl.BlockSpec((chunk, LANES), lambda bi, c: (bi * nc + c, 0)),
            pl.BlockSpec((1, SUBLANES, chunk), lambda bi, c: (bi, 0, c)),
            pl.BlockSpec((1, LANES), lambda bi, c: (0, 0)),
            pl.BlockSpec((SUBLANES, LANES), lambda bi, c: (0, 0)),
            pl.BlockSpec((SUBLANES, 2 * qw), lambda bi, c: (0, 0)),
            pl.BlockSpec((1, 2 * qw), lambda bi, c: (0, 0)),
            pl.BlockSpec((1, vw), lambda bi, c: (0, 0)),
        ],
        out_specs=pl.BlockSpec((1, chunk, vw), lambda bi, c: (bi, c, 0)),
        out_shape=jax.ShapeDtypeStruct((b, s, vw), BF16),
        scratch_shapes=[
            pltpu.VMEM((MLSTM_HEADS, chunk + 2 * SUBLANES, MLSTM_QK_DIM), F32),
            pltpu.VMEM((MLSTM_HEADS, chunk + 2 * SUBLANES, MLSTM_QK_DIM), F32),
            pltpu.VMEM((MLSTM_HEADS, chunk, MLSTM_QK_DIM), F32),
            pltpu.VMEM((MLSTM_HEADS, chunk, MLSTM_QK_DIM), F32),
            pltpu.VMEM((MLSTM_HEADS, MLSTM_QK_DIM, MLSTM_V_DIM), F32),
            pltpu.VMEM((MLSTM_HEADS, 1, MLSTM_QK_DIM), F32),
            pltpu.VMEM((MLSTM_HEADS, 1, LANES), F32),
        ],
        compiler_params=pltpu.CompilerParams(
            dimension_semantics=("parallel", "arbitrary"),
            vmem_limit_bytes=VMEM_LIMIT),
        name="mlstm",
    )(z3, z3, z3, z3, gates, gates_t, bias_col, bias_row, conv_w8, conv_b, gnorm)


def _cross_delta(x, g_ref, wq_ref, kv_ref, wo_ref):
    hc = _rms(x, g_ref[...]).astype(BF16)
    cq = (_dot(hc, wq_ref[...]) * (CROSS_HEAD_DIM ** -0.5)).astype(BF16)
    outs = []
    for hh in range(CROSS_HEADS):
        lo = hh * CROSS_HEAD_DIM
        qh = cq[:, lo:lo + CROSS_HEAD_DIM]
        kh = kv_ref[0, :, lo:lo + CROSS_HEAD_DIM]
        vh = kv_ref[0, :, CROSS_WIDTH + lo:CROSS_WIDTH + lo + CROSS_HEAD_DIM]
        s = _dot_nt(qh, kh)
        p = jnp.exp(s - jnp.max(s, axis=-1, keepdims=True))
        l = jnp.sum(p, axis=-1, keepdims=True)
        outs.append((_dot(p.astype(BF16), vh) * (1.0 / l)).astype(BF16))
    co = jnp.concatenate(outs, axis=1)
    return _dot(co, wo_ref[...])


def _merge_kernel(att_ref, hm_ref, ga_ref, gm_ref, x_ref, wa_ref, wm_ref, wo_ref,
                  gc_ref, wcq_ref, kv_ref, wco_ref, o_ref):
    j = pl.program_id(1)

    @pl.when(j == 0)
    def _():
        o_ref[...] = x_ref[...]

    a = _dot(att_ref[...], wa_ref[...])
    bm = _dot(hm_ref[...], wm_ref[...])
    y = _sigmoid(ga_ref[...].astype(F32)) * a + _sigmoid(gm_ref[...].astype(F32)) * bm
    o_ref[...] += _dot(y.astype(BF16), wo_ref[...])

    @pl.when(j == pl.num_programs(1) - 1)
    def _():
        x1 = o_ref[...]
        o_ref[...] = x1 + _cross_delta(x1, gc_ref, wcq_ref, kv_ref, wco_ref)


def _merge(att2d, hm2d, z2d, x2d, wa, wm, wo, gc, wcq, ckv, wco, seq, tm=512, tn=1024):
    t = x2d.shape[0]
    nj = D_MODEL // tn
    n_mem = ckv.shape[1]
    per_batch = seq // tm
    return pl.pallas_call(
        _merge_kernel,
        grid=(t // tm, nj),
        in_specs=[
            pl.BlockSpec((tm, ATTN_WIDTH), lambda i, j: (i, 0)),
            pl.BlockSpec((tm, MLSTM_WIDTH), lambda i, j: (i, 0)),
            pl.BlockSpec((tm, tn), lambda i, j: (i, COL_GA // tn + j)),
            pl.BlockSpec((tm, tn), lambda i, j: (i, COL_GM // tn + j)),
            pl.BlockSpec((tm, D_MODEL), lambda i, j: (i, 0)),
            pl.BlockSpec((ATTN_WIDTH, tn), lambda i, j: (0, j)),
            pl.BlockSpec((MLSTM_WIDTH, tn), lambda i, j: (0, j)),
            pl.BlockSpec((tn, D_MODEL), lambda i, j: (j, 0)),
            pl.BlockSpec((1, D_MODEL), lambda i, j: (0, 0)),
            pl.BlockSpec((D_MODEL, CROSS_WIDTH), lambda i, j: (0, 0)),
            pl.BlockSpec((1, n_mem, 2 * CROSS_WIDTH), lambda i, j: (i // per_batch, 0, 0)),
            pl.BlockSpec((CROSS_WIDTH, D_MODEL), lambda i, j: (0, 0)),
        ],
        out_specs=pl.BlockSpec((tm, D_MODEL), lambda i, j: (i, 0)),
        out_shape=jax.ShapeDtypeStruct((t, D_MODEL), F32),
        compiler_params=pltpu.CompilerParams(
            dimension_semantics=("parallel", "arbitrary"), vmem_limit_bytes=VMEM_LIMIT),
        name="merge",
    )(att2d, hm2d, z2d, z2d, x2d, wa, wm, wo, gc, wcq, ckv, wco)


def _mlp_kernel(x_ref, g_ref, wu_ref, wd_ref, gf_ref, o_ref, h_scr, acc, *, final_norm):
    j = pl.program_id(1)

    @pl.when(j == 0)
    def _():
        x = x_ref[...]
        h_scr[...] = _rms(x, g_ref[...]).astype(BF16)
        acc[...] = x

    u = jnp.square(jnp.maximum(_dot(h_scr[...], wu_ref[...]), 0.0)).astype(BF16)
    acc[...] += _dot(u, wd_ref[...])

    @pl.when(j == pl.num_programs(1) - 1)
    def _():
        if final_norm:
            o_ref[...] = _rms(acc[...], gf_ref[...])
        else:
            o_ref[...] = acc[...]


def _mlp(x2d, g, wu, wd, gf, final_norm, tm=512, tf=1024):
    t = x2d.shape[0]
    kern = functools.partial(_mlp_kernel, final_norm=final_norm)
    return pl.pallas_call(
        kern,
        grid=(t // tm, D_FF // tf),
        in_specs=[
            pl.BlockSpec((tm, D_MODEL), lambda i, j: (i, 0)),
            pl.BlockSpec((1, D_MODEL), lambda i, j: (0, 0)),
            pl.BlockSpec((D_MODEL, tf), lambda i, j: (0, j)),
            pl.BlockSpec((tf, D_MODEL), lambda i, j: (j, 0)),
            pl.BlockSpec((1, D_MODEL), lambda i, j: (0, 0)),
        ],
        out_specs=pl.BlockSpec((tm, D_MODEL), lambda i, j: (i, 0)),
        out_shape=jax.ShapeDtypeStruct((t, D_MODEL), F32),
        scratch_shapes=[pltpu.VMEM((tm, D_MODEL), BF16), pltpu.VMEM((tm, D_MODEL), F32)],
        compiler_params=pltpu.CompilerParams(
            dimension_semantics=("parallel", "arbitrary"), vmem_limit_bytes=VMEM_LIMIT),
        name="mlp",
    )(x2d, g, wu, wd, gf)


def _alibi_terms():
    slopes = 2.0 ** (-8.0 * np.arange(1, ATTN_HEADS + 1, dtype=np.float64) / ATTN_HEADS)
    rem = slopes * LOG2E
    terms = []
    for _ in range(ALIBI_TERMS):
        t = rem.astype(np.float32).astype(ml_dtypes.bfloat16).astype(np.float64)
        terms.append(t)
        rem = rem - t
    return np.stack(terms, axis=1).reshape(-1).astype(np.float32)


def _pad_rows(a, rows):
    return jnp.pad(a, ((0, rows - a.shape[0]), (0, 0)))


def kernel(x, mem, norm_mix, w_in, b_igate, b_fgate, conv_w, conv_b, lam_q1, lam_k1, lam_q2, lam_k2, attn_norm, mlstm_norm, w_attn_br, w_mlstm_br, w_out, norm_cross, norm_mem, w_cq, w_ckv, w_co, norm_mlp, w_up, w_down, norm_final):
    b, s, _ = x.shape
    t = b * s
    tq = 256
    cs_terms = jnp.asarray(_alibi_terms())
    x2d = x.reshape(t, D_MODEL)
    for l in range(DEPTH):
        lam_init = 0.8 - 0.6 * math.exp(-0.3 * l)
        w = w_in[l]
        w_a, w_b, w_gate = _wprep(w.T)

        z2d, gates, qt, vt = _inproj(x2d, norm_mix[l][None, :], w_a, w_b, w_gate, tq)
        z3 = z2d.reshape(b, s, IN_MAIN)

        qt5 = qt.reshape(b, s // tq, ATTN_HEADS, 2 * ATTN_QK_DIM, tq)
        vt5 = vt.reshape(b, s // tq, ATTN_HEADS, ATTN_V_DIM, tq)
        lamv = _pad_rows(jnp.pad(jnp.stack([lam_q1[l], lam_k1[l], lam_q2[l], lam_k2[l]]),
                                 ((0, 0), (0, LANES - ATTN_QK_DIM))), SUBLANES)
        att, w_out_b, w_abr_b, w_mbr_b, w_up_b, w_down_b = _attn(
            z3, qt5, vt5, cs_terms, lamv, attn_norm[l][None, :],
            (w_out[l], w_attn_br[l], w_mlstm_br[l], w_up[l], w_down[l]), lam_init, tq=tq)

        gates_t = gates[:, :SUBLANES].reshape(b, s, SUBLANES).transpose(0, 2, 1)
        gate_bias = jnp.concatenate([b_igate[l], b_fgate[l]])
        bias_col = jnp.pad(gate_bias, (0, LANES - gate_bias.shape[0]))[None, :]
        bias_row = jnp.broadcast_to(gate_bias[:, None], (SUBLANES, LANES))
        hm = _mlstm(z3, gates, gates_t, bias_col, bias_row, _pad_rows(conv_w[l], SUBLANES),
                    conv_b[l][None, :], mlstm_norm[l][None, :])

        ckv = _memkv(mem, norm_mem[l][None, :], w_ckv[l])
        x2d = _merge(att.reshape(t, ATTN_WIDTH), hm.reshape(t, MLSTM_WIDTH), z2d, x2d,
                     w_abr_b, w_mbr_b, w_out_b, norm_cross[l][None, :], w_cq[l].astype(BF16), ckv,
                     w_co[l].astype(BF16), s)

        x2d = _mlp(x2d, norm_mlp[l][None, :], w_up_b, w_down_b,
                   norm_final[None, :], final_norm=(l == DEPTH - 1))
    return x2d.reshape(b, s, D_MODEL)
```

```python
import functools
import math

import ml_dtypes
import numpy as np
import jax
import jax.numpy as jnp
from jax import lax
from jax.experimental import pallas as pl
from jax.experimental.pallas import tpu as pltpu

F32 = jnp.float32
BF16 = jnp.bfloat16

D_MODEL = 2048
DEPTH = 1
ATTN_HEADS = 8
ATTN_QK_DIM = 64
ATTN_V_DIM = 128
ATTN_WIDTH = ATTN_HEADS * ATTN_V_DIM
MLSTM_HEADS = 4
MLSTM_QK_DIM = 128
MLSTM_V_DIM = 256
MLSTM_QK_WIDTH = MLSTM_HEADS * MLSTM_QK_DIM
MLSTM_WIDTH = MLSTM_HEADS * MLSTM_V_DIM
CONV_WIDTH = 4
CROSS_HEADS = 4
CROSS_HEAD_DIM = 128
CROSS_WIDTH = CROSS_HEADS * CROSS_HEAD_DIM
D_FF = 4 * D_MODEL
EPS = 1e-6
LANES = 128
SUBLANES = 8
NEG_BIG = -1e30
LOG2E = 1.4426950408889634
ALIBI_TERMS = 3

COL_AK = 0
COL_MQ = 1024
COL_MK = 1536
COL_MV = 2048
COL_MO = 3072
COL_GA = 4096
COL_GM = 6144
IN_MAIN = 8192
W_AQ_LO, W_AQ_HI = 0, 1024
W_AV_LO = 2048
GATE_LO = 6144
GATE_HI = 6152

VMEM_LIMIT = 56 * 1024 * 1024


def _rms(x, g):
    ms = jnp.mean(x * x, axis=-1, keepdims=True)
    return x * lax.rsqrt(ms + EPS) * g


def _sigmoid(x):
    return 0.5 * jnp.tanh(0.5 * x) + 0.5


def _log_sigmoid(x):
    return jnp.minimum(x, 0.0) - jnp.log(1.0 + jnp.exp(-jnp.abs(x)))


def _dot(a, b):
    return jnp.dot(a, b, preferred_element_type=F32)


def _dot_nt(a, b):
    return lax.dot_general(a, b, (((1,), (1,)), ((), ())), preferred_element_type=F32)


def _memkv_kernel(mem_ref, g_ref, w_ref, o_ref):
    mn = _rms(mem_ref[0], g_ref[...]).astype(BF16)
    o_ref[0] = _dot(mn, w_ref[...].astype(BF16)).astype(BF16)


def _memkv(mem, g, w_ckv):
    b, n_mem, _ = mem.shape
    return pl.pallas_call(
        _memkv_kernel,
        grid=(b,),
        in_specs=[
            pl.BlockSpec((1, n_mem, D_MODEL), lambda i: (i, 0, 0)),
            pl.BlockSpec((1, D_MODEL), lambda i: (0, 0)),
            pl.BlockSpec((D_MODEL, 2 * CROSS_WIDTH), lambda i: (0, 0)),
        ],
        out_specs=pl.BlockSpec((1, n_mem, 2 * CROSS_WIDTH), lambda i: (i, 0, 0)),
        out_shape=jax.ShapeDtypeStruct((b, n_mem, 2 * CROSS_WIDTH), BF16),
        compiler_params=pltpu.CompilerParams(
            dimension_semantics=("arbitrary",), vmem_limit_bytes=VMEM_LIMIT),
        name="memkv",
    )(mem, g, w_ckv)


def _wprep_kernel(a_ref, nxt_ref, wa_out, wb_out, wg_out, *, na):
    k = pl.program_id(0)
    ng = GATE_HI - GATE_LO

    @pl.when(k < na)
    def _():
        scale = jnp.where(k == W_AQ_LO // a_ref.shape[0], ATTN_QK_DIM ** -0.5 * LOG2E, 1.0)
        wa_out[...] = (a_ref[...] * scale).astype(BF16)

    @pl.when(k >= na)
    def _():
        wb_out[...] = jnp.concatenate([a_ref[pl.ds(ng, a_ref.shape[0] - ng), :], nxt_ref[0]],
                                      axis=0).astype(BF16)

    @pl.when(k == na)
    def _():
        pad = jnp.zeros((LANES - ng, D_MODEL), F32)
        wg_out[...] = jnp.concatenate([a_ref[pl.ds(0, ng), :], pad], axis=0).astype(BF16)


def _wprep(wt, tn=1024):
    rows = wt.shape[0]
    ng = GATE_HI - GATE_LO
    assert ng == SUBLANES and GATE_LO % tn == 0 and (rows - GATE_HI) % tn == 0
    assert W_AQ_LO == 0 and W_AQ_HI == tn
    na = GATE_LO // tn
    nb = (rows - GATE_HI) // tn
    groups = wt.reshape(rows // ng, ng, D_MODEL)
    per_block = tn // ng
    return pl.pallas_call(
        functools.partial(_wprep_kernel, na=na),
        grid=(na + nb,),
        in_specs=[
            pl.BlockSpec((tn, D_MODEL), lambda k: (k, 0)),
            pl.BlockSpec((1, ng, D_MODEL), lambda k: ((k + 1) * per_block, 0, 0)),
        ],
        out_specs=[
            pl.BlockSpec((tn, D_MODEL), lambda k: (jnp.minimum(k, na - 1), 0)),
            pl.BlockSpec((tn, D_MODEL), lambda k: (jnp.maximum(k - na, 0), 0)),
            pl.BlockSpec((LANES, D_MODEL), lambda k: (0, 0)),
        ],
        out_shape=[
            jax.ShapeDtypeStruct((GATE_LO, D_MODEL), BF16),
            jax.ShapeDtypeStruct((rows - GATE_HI, D_MODEL), BF16),
            jax.ShapeDtypeStruct((LANES, D_MODEL), BF16),
        ],
        compiler_params=pltpu.CompilerParams(
            dimension_semantics=("arbitrary",), vmem_limit_bytes=VMEM_LIMIT),
        name="wprep",
    )(wt, groups)


def _inproj_kernel(x_ref, g_ref, wa_hbm, wb_hbm, wg_ref, z_ref, gate_ref, qt_ref, vt_ref,
                   h_scr, wbuf, sem, *, na, nj, jq, jv):
    i = pl.program_id(0)
    j = pl.program_id(1)
    depth, tn, _ = wbuf.shape
    step = i * nj + j
    total = pl.num_programs(0) * nj

    def copies(s, slot):
        jj = s % nj
        ca = pltpu.make_async_copy(
            wa_hbm.at[pl.ds(pl.multiple_of(jnp.minimum(jj, na - 1) * tn, tn), tn), :],
            wbuf.at[slot], sem.at[slot])
        cb = pltpu.make_async_copy(
            wb_hbm.at[pl.ds(pl.multiple_of(jnp.maximum(jj - na, 0) * tn, tn), tn), :],
            wbuf.at[slot], sem.at[slot])
        return jj < na, ca, cb

    def start(s, slot):
        first, ca, cb = copies(s, slot)

        @pl.when(first)
        def _():
            ca.start()

        @pl.when(jnp.logical_not(first))
        def _():
            cb.start()

    @pl.when(step == 0)
    def _():
        for d in range(depth - 1):
            start(d, d)

    ahead = step + depth - 1

    @pl.when(ahead < total)
    def _():
        start(ahead, ahead % depth)

    @pl.when(j == 0)
    def _():
        h = _rms(x_ref[...], g_ref[...]).astype(BF16)
        h_scr[...] = h
        gate_ref[...] = _dot_nt(h, wg_ref[...])

    slot = step % depth
    first, ca, cb = copies(step, slot)

    @pl.when(first)
    def _():
        ca.wait()

    @pl.when(jnp.logical_not(first))
    def _():
        cb.wait()

    w_ref = wbuf.at[slot]

    @pl.when((j != jq) & (j != jv))
    def _():
        z_ref[...] = _dot_nt(h_scr[...], w_ref[...]).astype(BF16)

    def transposed(out_ref):
        xt = _dot_nt(w_ref[...], h_scr[...]).astype(BF16)
        tile = out_ref.shape[2]
        for n in range(out_ref.shape[0]):
            out_ref[n] = xt[:, n * tile:(n + 1) * tile]

    @pl.when(j == jq)
    def _():
        transposed(qt_ref)

    @pl.when(j == jv)
    def _():
        transposed(vt_ref)


def _inproj(x2d, g, w_a, w_b, w_gate, tkv, tm=1024, tn=1024, depth=3):
    t = x2d.shape[0]
    na = w_a.shape[0] // tn
    n = w_a.shape[0] + w_b.shape[0]
    ni, nj = t // tm, n // tn
    assert tn == ATTN_WIDTH and W_AQ_LO % tn == 0 and W_AQ_HI - W_AQ_LO == tn and W_AV_LO % tn == 0
    jq, jv = W_AQ_LO // tn, W_AV_LO // tn
    assert jq < jv and depth - 1 <= ni * nj

    def z_tile(i, j):
        skipped = (j >= jq).astype(jnp.int32) + (j >= jv).astype(jnp.int32)
        return i, jnp.maximum(j - skipped, 0)

    return pl.pallas_call(
        functools.partial(_inproj_kernel, na=na, nj=nj, jq=jq, jv=jv),
        grid=(ni, nj),
        in_specs=[
            pl.BlockSpec((tm, D_MODEL), lambda i, j: (i, 0)),
            pl.BlockSpec((1, D_MODEL), lambda i, j: (0, 0)),
            pl.BlockSpec(memory_space=pl.ANY),
            pl.BlockSpec(memory_space=pl.ANY),
            pl.BlockSpec((LANES, D_MODEL), lambda i, j: (0, 0)),
        ],
        out_specs=[
            pl.BlockSpec((tm, tn), z_tile),
            pl.BlockSpec((tm, LANES), lambda i, j: (i, 0)),
            pl.BlockSpec((tm // tkv, ATTN_WIDTH, tkv), lambda i, j: (i, 0, 0)),
            pl.BlockSpec((tm // tkv, ATTN_WIDTH, tkv), lambda i, j: (i, 0, 0)),
        ],
        out_shape=[
            jax.ShapeDtypeStruct((t, n - 2 * tn), BF16),
            jax.ShapeDtypeStruct((t, LANES), F32),
            jax.ShapeDtypeStruct((t // tkv, ATTN_WIDTH, tkv), BF16),
            jax.ShapeDtypeStruct((t // tkv, ATTN_WIDTH, tkv), BF16),
        ],
        scratch_shapes=[
            pltpu.VMEM((tm, D_MODEL), BF16),
            pltpu.VMEM((depth, tn, D_MODEL), BF16),
            pltpu.SemaphoreType.DMA((depth,)),
        ],
        compiler_params=pltpu.CompilerParams(
            dimension_semantics=("arbitrary", "arbitrary"), vmem_limit_bytes=VMEM_LIMIT),
        name="inproj",
    )(x2d, g, w_a, w_b, w_gate)


def _attn_kernel(cs_ref, lam_ref, qt_ref, k_ref, vt_ref, gain_ref, *refs, ncast, tq, lam_init):
    cast_in = refs[:ncast]
    o_ref = refs[ncast]
    cast_out = refs[ncast + 1:2 * ncast + 1]
    qs_ref, kf_ref, acc_ref, m_ref = refs[2 * ncast + 1:]
    tk = tq
    nfeat = ALIBI_TERMS
    dq = 2 * ATTN_QK_DIM
    dv = ATTN_V_DIM
    i = pl.program_id(1)
    cs = [[cs_ref[nfeat * h + t] for t in range(nfeat)] for h in range(ATTN_HEADS)]
    cs_tot = [sum(c[1:], c[0]) for c in cs]

    @pl.when(i == 0)
    def _():
        klane = lax.broadcasted_iota(jnp.int32, (tk, LANES), 1)
        krow = lax.broadcasted_iota(jnp.int32, (tk, LANES), 0).astype(F32)
        frow = lax.broadcasted_iota(jnp.int32, (LANES, 2 * tq), 0)
        fcol = lax.broadcasted_iota(jnp.int32, (LANES, 2 * tq), 1)
        fcol = jnp.where(fcol >= tq, fcol - tq, fcol).astype(F32)
        for h in range(ATTN_HEADS):
            kf = jnp.where(klane < nfeat, krow, 0.0)
            qf = jnp.where((frow >= nfeat) & (frow < 2 * nfeat), -fcol, 0.0)
            for t in range(nfeat):
                kf = jnp.where(klane == nfeat + t, cs[h][t], kf)
                qf = jnp.where(frow == t, cs[h][t], qf)
            kf_ref[h] = kf.astype(BF16)
            qs_ref[h, pl.ds(dq, LANES), :] = qf.astype(BF16)

    row = lax.broadcasted_iota(jnp.int32, (dq, tq), 0)
    for h in range(ATTN_HEADS):
        qt = qt_ref[0, 0, h]
        zero = jnp.zeros_like(qt)
        qs_ref[h, pl.ds(0, dq), pl.ds(0, tq)] = jnp.where(row < ATTN_QK_DIM, qt, zero)
        qs_ref[h, pl.ds(0, dq), pl.ds(tq, tq)] = jnp.where(row >= ATTN_QK_DIM, qt, zero)
    acc_ref[...] = jnp.zeros_like(acc_ref)
    m_ref[...] = jnp.full_like(m_ref, NEG_BIG)

    orow = lax.broadcasted_iota(jnp.int32, (2 * SUBLANES, tk), 0)
    ones_blk = jnp.where(orow == 0, 1.0, 0.0).astype(BF16)

    def scores(h, j):
        k = k_ref[0, pl.ds(pl.multiple_of(j * tk, tk), tk), pl.ds(h * dq, dq)]
        kx = jnp.concatenate([k, kf_ref[h]], axis=1)
        return _dot(kx, qs_ref[h])

    def softmax_pv(h, j, t, diagonal):
        if diagonal:
            krow = lax.broadcasted_iota(jnp.int32, (tk, 2 * tq), 0)
            qcol = lax.broadcasted_iota(jnp.int32, (tk, 2 * tq), 1)
            qcol = jnp.where(qcol >= tq, qcol - tq, qcol)
            t = jnp.where(krow <= qcol, t, NEG_BIG)
        soff = cs_tot[h] * ((j - i) * tk).astype(F32)
        m = m_ref[h]
        m_new = jnp.maximum(m, jnp.max(t, axis=0, keepdims=True) + soff)
        alpha = jnp.exp2(m - m_new)
        p = jnp.exp2(t - (m_new - soff)).astype(BF16)
        vx = jnp.concatenate([vt_ref[0, j, h], ones_blk], axis=0)
        acc_ref[h] = alpha * acc_ref[h] + _dot(vx, p)
        m_ref[h] = m_new

    def run(units):
        ahead = 2
        pending = [scores(*u[:2]) for u in units[:ahead]]
        for n, (h, j, diagonal) in enumerate(units):
            if n + ahead < len(units):
                pending.append(scores(*units[n + ahead][:2]))
            softmax_pv(h, j, pending.pop(0), diagonal)

    def tile_units(j, diagonal):
        return [(h, j, diagonal) for h in range(ATTN_HEADS)]

    group = 4

    def body(jj, carry):
        units = []
        for n in range(group):
            units += tile_units(group * jj + n, False)
        run(units)
        return carry

    lax.fori_loop(0, i // group, body, 0)

    for rest in range(group):
        @pl.when(i % group == rest)
        def _(rest=rest):
            units = []
            for n in range(rest):
                units += tile_units(i - rest + n, False)
            run(units + tile_units(i, True))

    lv = lam_ref[...]
    d1 = jnp.sum(lv[0:1] * lv[1:2], axis=-1, keepdims=True)
    d2 = jnp.sum(lv[2:3] * lv[3:4], axis=-1, keepdims=True)
    lam = jnp.exp(d1) - jnp.exp(d2) + lam_init
    for h in range(ATTN_HEADS):
        l = acc_ref[h, pl.ds(dv, 1), :]
        rl = 1.0 / l
        out = (acc_ref[h, pl.ds(0, dv), pl.ds(0, tq)] * rl[:, :tq]
               - lam * (acc_ref[h, pl.ds(0, dv), pl.ds(tq, tq)] * rl[:, tq:]))
        ms = jnp.mean(out * out, axis=0, keepdims=True)
        on = out * lax.rsqrt(ms + EPS)
        o_ref[0, :, pl.ds(h * dv, dv)] = (on.T * gain_ref[...] * (1.0 - lam_init)).astype(BF16)

    for src, dst in zip(cast_in, cast_out):
        dst[...] = src[...].astype(BF16)


def _attn(z3, qt5, vt5, cs_terms, lamv, gain, cast_weights, lam_init, tq=256):
    b, s, _ = z3.shape
    nq = s // tq
    steps = b * nq
    cast_specs = [pl.BlockSpec((w.shape[0] // steps, w.shape[1]), lambda bi, i: (bi * nq + i, 0))
                  for w in cast_weights]
    kern = functools.partial(_attn_kernel, ncast=len(cast_weights), tq=tq, lam_init=lam_init)
    width = ATTN_HEADS * 2 * ATTN_QK_DIM
    return pl.pallas_call(
        kern,
        grid=(b, nq),
        in_specs=[
            pl.BlockSpec(memory_space=pltpu.SMEM),
            pl.BlockSpec((SUBLANES, LANES), lambda bi, i: (0, 0)),
            pl.BlockSpec((1, 1, ATTN_HEADS, 2 * ATTN_QK_DIM, tq), lambda bi, i: (bi, i, 0, 0, 0)),
            pl.BlockSpec((1, s, width), lambda bi, i: (bi, 0, COL_AK // width)),
            pl.BlockSpec((1, nq, ATTN_HEADS, ATTN_V_DIM, tq), lambda bi, i: (bi, 0, 0, 0, 0)),
            pl.BlockSpec((1, ATTN_V_DIM), lambda bi, i: (0, 0)),
        ] + cast_specs,
        out_specs=[pl.BlockSpec((1, tq, ATTN_WIDTH), lambda bi, i: (bi, i, 0))] + cast_specs,
        out_shape=[jax.ShapeDtypeStruct((b, s, ATTN_WIDTH), BF16)]
        + [jax.ShapeDtypeStruct(w.shape, BF16) for w in cast_weights],
        scratch_shapes=[
            pltpu.VMEM((ATTN_HEADS, 2 * ATTN_QK_DIM + LANES, 2 * tq), BF16),
            pltpu.VMEM((ATTN_HEADS, tq, LANES), BF16),
            pltpu.VMEM((ATTN_HEADS, ATTN_V_DIM + 2 * SUBLANES, 2 * tq), F32),
            pltpu.VMEM((ATTN_HEADS, 1, 2 * tq), F32),
        ],
        compiler_params=pltpu.CompilerParams(
            dimension_semantics=("parallel", "arbitrary"),
            vmem_limit_bytes=VMEM_LIMIT),
        name="attn",
    )(cs_terms, lamv, qt5, z3, vt5, gain, *cast_weights)


def _split3(x):
    hi = x.astype(BF16)
    r = x - hi.astype(F32)
    mid = r.astype(BF16)
    lo = (r - mid.astype(F32)).astype(BF16)
    return hi, mid, lo


def _mlstm_kernel(uq_ref, uk_ref, v_ref, mo_ref, gcol_ref, grow_ref, bcol_ref, brow_ref,
                  cw_ref, cb_ref, gn_ref, o_ref, extq, extk, dstq, dstk, c_scr, n_scr, m_scr,
                  *, chunk):
    L = chunk
    nh = MLSTM_HEADS
    dk = MLSTM_QK_DIM
    dv = MLSTM_V_DIM
    heads = range(nh)

    @pl.when(pl.program_id(1) == 0)
    def _():
        extq[:, pl.ds(0, SUBLANES), :] = jnp.zeros((nh, SUBLANES, dk), F32)
        extk[:, pl.ds(0, SUBLANES), :] = jnp.zeros((nh, SUBLANES, dk), F32)
        c_scr[...] = jnp.zeros_like(c_scr)
        n_scr[...] = jnp.zeros_like(n_scr)
        m_scr[...] = jnp.zeros_like(m_scr)

    def conv_silu(u_ref, ext, dst, col0, scale):
        n8 = L // SUBLANES
        first = SUBLANES - (CONV_WIDTH - 1)
        outs = []
        for h in heads:
            cols = pl.ds(col0 + h * dk, dk)
            ext[h, pl.ds(SUBLANES, L), :] = u_ref[0, :, pl.ds(h * dk, dk)].astype(F32)
            w = cw_ref[:, cols]
            bias = cb_ref[:, cols]
            rows = [ext[h, pl.ds(first + s, n8, stride=SUBLANES), :]
                    for s in range(SUBLANES + CONV_WIDTH - 1)]
            for r in range(SUBLANES):
                y = bias
                for tap in range(CONV_WIDTH):
                    y = y + rows[r + tap] * w[tap:tap + 1]
                y = y * _sigmoid(y)
                dst[h, pl.ds(r, n8, stride=SUBLANES), :] = y if scale is None else y * scale
            ext[h, pl.ds(0, SUBLANES), :] = ext[h, pl.ds(L, SUBLANES), :]
            outs.append(dst[h])
        return outs

    q = conv_silu(uq_ref, extq, dstq, 0, None)
    k = conv_silu(uk_ref, extk, dstk, MLSTM_QK_WIDTH, dk ** -0.5)
    qb = [x.astype(BF16) for x in q]
    kb = [x.astype(BF16) for x in k]
    vb = [v_ref[0, :, pl.ds(h * dv, dv)] for h in heads]

    g_c = gcol_ref[...] + bcol_ref[...]
    g_r = grow_ref[0] + brow_ref[:, 0:1]
    lf_c = _log_sigmoid(g_c)
    lf_r = _log_sigmoid(g_r)

    r_i = lax.broadcasted_iota(jnp.int32, (L, L), 0)
    c_i = lax.broadcasted_iota(jnp.int32, (L, L), 1)
    causal = c_i <= r_i
    tri = jnp.where(causal, 1.0, 0.0).astype(BF16)
    tri_t = jnp.where(r_i <= c_i, 1.0, 0.0).astype(BF16)
    hi, mid, lo = _split3(lf_c)
    b_c = _dot(tri, hi) + _dot(tri, mid) + _dot(tri, lo)
    hi, mid, lo = _split3(lf_r)
    b_r = _dot(hi, tri_t) + _dot(mid, tri_t) + _dot(lo, tri_t)
    g_sum = jnp.sum(lf_r, axis=1, keepdims=True)

    lane = lax.broadcasted_iota(jnp.int32, (L, LANES), 1)

    def col(x, idx):
        return jnp.sum(jnp.where(lane == idx, x, 0.0), axis=1, keepdims=True)

    i_col = [col(g_c, h) for h in heads]
    b_col = [col(b_c, nh + h) for h in heads]
    i_row = [g_r[h:h + 1, :] for h in heads]
    b_row = [b_r[nh + h:nh + h + 1, :] for h in heads]
    g_tot = [g_sum[nh + h:nh + h + 1, :] for h in heads]

    c_prev = [c_scr[h] for h in heads]
    n_prev = [n_scr[h] for h in heads]
    m_prev = [m_scr[h][:, 0:1] for h in heads]

    s_qk = [_dot_nt(qb[h], kb[h]) for h in heads]
    inter = [_dot(qb[h], c_prev[h].astype(BF16)) for h in heads]
    m_loc = [jnp.max(g_tot[h] - b_row[h] + i_row[h], axis=1, keepdims=True) for h in heads]
    kw = [k[h] * jnp.exp(g_tot[h] - b_col[h] + i_col[h] - m_loc[h]) for h in heads]
    c_loc = [_dot(kw[h].T.astype(BF16), vb[h]) for h in heads]

    d = [jnp.where(causal, b_col[h] - b_row[h] + i_row[h], NEG_BIG) for h in heads]
    qn = [jnp.sum(q[h] * n_prev[h], axis=1, keepdims=True) for h in heads]
    d_max = [jnp.max(d[h], axis=1, keepdims=True) for h in heads]
    m_inter = [b_col[h] + m_prev[h] for h in heads]
    m_j = [jnp.maximum(m_inter[h], d_max[h]) for h in heads]
    w_inter = [jnp.exp(m_inter[h] - m_j[h]) for h in heads]
    p = [s_qk[h] * jnp.exp(d[h] - m_j[h]) for h in heads]
    p_sum = [jnp.sum(p[h], axis=1, keepdims=True) for h in heads]
    pv = [_dot(p[h].astype(BF16), vb[h]) for h in heads]
    hval = []
    for h in heads:
        den = w_inter[h] * qn[h] + p_sum[h]
        num = w_inter[h] * inter[h] + pv[h]
        hval.append(num * (1.0 / jnp.maximum(jnp.abs(den), jnp.exp(-m_j[h]))))
    ms = [jnp.mean(hval[h] * hval[h], axis=-1, keepdims=True) for h in heads]
    for h in heads:
        hn = hval[h] * lax.rsqrt(ms[h] + EPS) * gn_ref[:, pl.ds(h * dv, dv)]
        gate = _sigmoid(mo_ref[0, :, pl.ds(h * dv, dv)].astype(F32))
        o_ref[0, :, pl.ds(h * dv, dv)] = (hn * gate).astype(BF16)

    for h in heads:
        n_loc = jnp.sum(kw[h], axis=0, keepdims=True)
        m_new = jnp.maximum(g_tot[h] + m_prev[h], m_loc[h])
        a_old = jnp.exp(g_tot[h] + m_prev[h] - m_new)
        a_new = jnp.exp(m_loc[h] - m_new)
        c_scr[h] = a_old * c_prev[h] + a_new * c_loc[h]
        n_scr[h] = a_old * n_prev[h] + a_new * n_loc
        m_scr[h] = jnp.broadcast_to(m_new, (1, LANES))


def _mlstm(z3, gates, gates_t, bias_col, bias_row, conv_w8, conv_b, gnorm, chunk=256):
    b, s, _ = z3.shape
    nc = s // chunk
    kern = functools.partial(_mlstm_kernel, chunk=chunk)
    qw = MLSTM_QK_WIDTH
    vw = MLSTM_WIDTH
    return pl.pallas_call(
        kern,
        grid=(b, nc),
        in_specs=[
            pl.BlockSpec((1, chunk, qw), lambda bi, c: (bi, c, COL_MQ // qw)),
            pl.BlockSpec((1, chunk, qw), lambda bi, c: (bi, c, COL_MK // qw)),
            pl.BlockSpec((1, chunk, vw), lambda bi, c: (bi, c, COL_MV // vw)),
            pl.BlockSpec((1, chunk, vw), lambda bi, c: (bi, c, COL_MO // vw)),
            pl.BlockSpec((chunk, LANES), lambda bi, c: (bi * nc + c, 0)),
            pl.BlockSpec((1, SUBLANES, chunk), lambda bi, c: (bi, 0, c)),
            pl.BlockSpec((1, LANES), lambda bi, c: (0, 0)),
            pl.BlockSpec((SUBLANES, LANES), lambda bi, c: (0, 0)),
            pl.BlockSpec((SUBLANES, 2 * qw), lambda bi, c: (0, 0)),
            pl.BlockSpec((1, 2 * qw), lambda bi, c: (0, 0)),
            pl.BlockSpec((1, vw), lambda bi, c: (0, 0)),
        ],
        out_specs=pl.BlockSpec((1, chunk, vw), lambda bi, c: (bi, c, 0)),
        out_shape=jax.ShapeDtypeStruct((b, s, vw), BF16),
        scratch_shapes=[
            pltpu.VMEM((MLSTM_HEADS, chunk + 2 * SUBLANES, MLSTM_QK_DIM), F32),
            pltpu.VMEM((MLSTM_HEADS, chunk + 2 * SUBLANES, MLSTM_QK_DIM), F32),
            pltpu.VMEM((MLSTM_HEADS, chunk, MLSTM_QK_DIM), F32),
            pltpu.VMEM((MLSTM_HEADS, chunk, MLSTM_QK_DIM), F32),
            pltpu.VMEM((MLSTM_HEADS, MLSTM_QK_DIM, MLSTM_V_DIM), F32),
            pltpu.VMEM((MLSTM_HEADS, 1, MLSTM_QK_DIM), F32),
            pltpu.VMEM((MLSTM_HEADS, 1, LANES), F32),
        ],
        compiler_params=pltpu.CompilerParams(
            dimension_semantics=("parallel", "arbitrary"),
            vmem_limit_bytes=VMEM_LIMIT),
        name="mlstm",
    )(z3, z3, z3, z3, gates, gates_t, bias_col, bias_row, conv_w8, conv_b, gnorm)


def _cross_delta(x, g_ref, wq_ref, kv_ref, wo_ref):
    hc = _rms(x, g_ref[...]).astype(BF16)
    cq = (_dot(hc, wq_ref[...]) * (CROSS_HEAD_DIM ** -0.5)).astype(BF16)
    outs = []
    for hh in range(CROSS_HEADS):
        lo = hh * CROSS_HEAD_DIM
        qh = cq[:, lo:lo + CROSS_HEAD_DIM]
        kh = kv_ref[0, :, lo:lo + CROSS_HEAD_DIM]
        vh = kv_ref[0, :, CROSS_WIDTH + lo:CROSS_WIDTH + lo + CROSS_HEAD_DIM]
        s = _dot_nt(qh, kh)
        p = jnp.exp(s - jnp.max(s, axis=-1, keepdims=True))
        l = jnp.sum(p, axis=-1, keepdims=True)
        outs.append((_dot(p.astype(BF16), vh) * (1.0 / l)).astype(BF16))
    co = jnp.concatenate(outs, axis=1)
    return _dot(co, wo_ref[...])


def _merge_kernel(att_ref, hm_ref, ga_ref, gm_ref, x_ref, wa_ref, wm_ref, wo_ref,
                  gc_ref, wcq_ref, kv_ref, wco_ref, o_ref):
    j = pl.program_id(1)

    @pl.when(j == 0)
    def _():
        o_ref[...] = x_ref[...]

    a = _dot(att_ref[...], wa_ref[...])
    bm = _dot(hm_ref[...], wm_ref[...])
    y = _sigmoid(ga_ref[...].astype(F32)) * a + _sigmoid(gm_ref[...].astype(F32)) * bm
    o_ref[...] += _dot(y.astype(BF16), wo_ref[...])

    @pl.when(j == pl.num_programs(1) - 1)
    def _():
        x1 = o_ref[...]
        o_ref[...] = x1 + _cross_delta(x1, gc_ref, wcq_ref, kv_ref, wco_ref)


def _merge(att2d, hm2d, z2d, x2d, wa, wm, wo, gc, wcq, ckv, wco, seq, tm=512, tn=1024):
    t = x2d.shape[0]
    nj = D_MODEL // tn
    n_mem = ckv.shape[1]
    per_batch = seq // tm
    return pl.pallas_call(
        _merge_kernel,
        grid=(t // tm, nj),
        in_specs=[
            pl.BlockSpec((tm, ATTN_WIDTH), lambda i, j: (i, 0)),
            pl.BlockSpec((tm, MLSTM_WIDTH), lambda i, j: (i, 0)),
            pl.BlockSpec((tm, tn), lambda i, j: (i, COL_GA // tn + j)),
            pl.BlockSpec((tm, tn), lambda i, j: (i, COL_GM // tn + j)),
            pl.BlockSpec((tm, D_MODEL), lambda i, j: (i, 0)),
            pl.BlockSpec((ATTN_WIDTH, tn), lambda i, j: (0, j)),
            pl.BlockSpec((MLSTM_WIDTH, tn), lambda i, j: (0, j)),
            pl.BlockSpec((tn, D_MODEL), lambda i, j: (j, 0)),
            pl.BlockSpec((1, D_MODEL), lambda i, j: (0, 0)),
            pl.BlockSpec((D_MODEL, CROSS_WIDTH), lambda i, j: (0, 0)),
            pl.BlockSpec((1, n_mem, 2 * CROSS_WIDTH), lambda i, j: (i // per_batch, 0, 0)),
            pl.BlockSpec((CROSS_WIDTH, D_MODEL), lambda i, j: (0, 0)),
        ],
        out_specs=pl.BlockSpec((tm, D_MODEL), lambda i, j: (i, 0)),
        out_shape=jax.ShapeDtypeStruct((t, D_MODEL), F32),
        compiler_params=pltpu.CompilerParams(
            dimension_semantics=("parallel", "arbitrary"), vmem_limit_bytes=VMEM_LIMIT),
        name="merge",
    )(att2d, hm2d, z2d, z2d, x2d, wa, wm, wo, gc, wcq, ckv, wco)


def _mlp_kernel(x_ref, g_ref, wu_hbm, wd_hbm, gf_ref, o_ref, h_scr, acc, ubuf, dbuf, sem,
                *, nj, final_norm):
    i = pl.program_id(0)
    j = pl.program_id(1)
    depth, _, tf = ubuf.shape
    step = i * nj + j
    total = pl.num_programs(0) * nj

    def copies(s, slot):
        off = pl.multiple_of((s % nj) * tf, tf)
        return (pltpu.make_async_copy(wu_hbm.at[:, pl.ds(off, tf)], ubuf.at[slot], sem.at[0, slot]),
                pltpu.make_async_copy(wd_hbm.at[pl.ds(off, tf), :], dbuf.at[slot], sem.at[1, slot]))

    def start(s, slot):
        for c in copies(s, slot):
            c.start()

    @pl.when(step == 0)
    def _():
        for d in range(depth - 1):
            start(d, d)

    ahead = step + depth - 1

    @pl.when(ahead < total)
    def _():
        start(ahead, ahead % depth)

    @pl.when(j == 0)
    def _():
        x = x_ref[...]
        h_scr[...] = _rms(x, g_ref[...]).astype(BF16)
        acc[...] = x

    slot = step % depth
    for c in copies(step, slot):
        c.wait()
    u = jnp.square(jnp.maximum(_dot(h_scr[...], ubuf[slot]), 0.0)).astype(BF16)
    acc[...] += _dot(u, dbuf[slot])

    @pl.when(j == nj - 1)
    def _():
        if final_norm:
            o_ref[...] = _rms(acc[...], gf_ref[...])
        else:
            o_ref[...] = acc[...]


def _mlp(x2d, g, wu, wd, gf, final_norm, tm=512, tf=1024, depth=3):
    t = x2d.shape[0]
    nj = D_FF // tf
    kern = functools.partial(_mlp_kernel, nj=nj, final_norm=final_norm)
    return pl.pallas_call(
        kern,
        grid=(t // tm, nj),
        in_specs=[
            pl.BlockSpec((tm, D_MODEL), lambda i, j: (i, 0)),
            pl.BlockSpec((1, D_MODEL), lambda i, j: (0, 0)),
            pl.BlockSpec(memory_space=pl.ANY),
            pl.BlockSpec(memory_space=pl.ANY),
            pl.BlockSpec((1, D_MODEL), lambda i, j: (0, 0)),
        ],
        out_specs=pl.BlockSpec((tm, D_MODEL), lambda i, j: (i, 0)),
        out_shape=jax.ShapeDtypeStruct((t, D_MODEL), F32),
        scratch_shapes=[
            pltpu.VMEM((tm, D_MODEL), BF16),
            pltpu.VMEM((tm, D_MODEL), F32),
            pltpu.VMEM((depth, D_MODEL, tf), BF16),
            pltpu.VMEM((depth, tf, D_MODEL), BF16),
            pltpu.SemaphoreType.DMA((2, depth)),
        ],
        compiler_params=pltpu.CompilerParams(
            dimension_semantics=("arbitrary", "arbitrary"), vmem_limit_bytes=VMEM_LIMIT),
        name="mlp",
    )(x2d, g, wu, wd, gf)


def _alibi_terms():
    slopes = 2.0 ** (-8.0 * np.arange(1, ATTN_HEADS + 1, dtype=np.float64) / ATTN_HEADS)
    rem = slopes * LOG2E
    terms = []
    for _ in range(ALIBI_TERMS):
        t = rem.astype(np.float32).astype(ml_dtypes.bfloat16).astype(np.float64)
        terms.append(t)
        rem = rem - t
    return np.stack(terms, axis=1).reshape(-1).astype(np.float32)


def _pad_rows(a, rows):
    return jnp.pad(a, ((0, rows - a.shape[0]), (0, 0)))


def kernel(x, mem, norm_mix, w_in, b_igate, b_fgate, conv_w, conv_b, lam_q1, lam_k1, lam_q2, lam_k2, attn_norm, mlstm_norm, w_attn_br, w_mlstm_br, w_out, norm_cross, norm_mem, w_cq, w_ckv, w_co, norm_mlp, w_up, w_down, norm_final):
    b, s, _ = x.shape
    t = b * s
    tq = 256
    cs_terms = jnp.asarray(_alibi_terms())
    x2d = x.reshape(t, D_MODEL)
    for l in range(DEPTH):
        lam_init = 0.8 - 0.6 * math.exp(-0.3 * l)
        w = w_in[l]
        w_a, w_b, w_gate = _wprep(w.T)

        z2d, gates, qt, vt = _inproj(x2d, norm_mix[l][None, :], w_a, w_b, w_gate, tq)
        z3 = z2d.reshape(b, s, IN_MAIN)

        qt5 = qt.reshape(b, s // tq, ATTN_HEADS, 2 * ATTN_QK_DIM, tq)
        vt5 = vt.reshape(b, s // tq, ATTN_HEADS, ATTN_V_DIM, tq)
        lamv = _pad_rows(jnp.pad(jnp.stack([lam_q1[l], lam_k1[l], lam_q2[l], lam_k2[l]]),
                                 ((0, 0), (0, LANES - ATTN_QK_DIM))), SUBLANES)
        att, w_out_b, w_abr_b, w_mbr_b, w_up_b, w_down_b = _attn(
            z3, qt5, vt5, cs_terms, lamv, attn_norm[l][None, :],
            (w_out[l], w_attn_br[l], w_mlstm_br[l], w_up[l], w_down[l]), lam_init, tq=tq)

        gates_t = gates[:, :SUBLANES].reshape(b, s, SUBLANES).transpose(0, 2, 1)
        gate_bias = jnp.concatenate([b_igate[l], b_fgate[l]])
        bias_col = jnp.pad(gate_bias, (0, LANES - gate_bias.shape[0]))[None, :]
        bias_row = jnp.broadcast_to(gate_bias[:, None], (SUBLANES, LANES))
        hm = _mlstm(z3, gates, gates_t, bias_col, bias_row, _pad_rows(conv_w[l], SUBLANES),
                    conv_b[l][None, :], mlstm_norm[l][None, :])

        ckv = _memkv(mem, norm_mem[l][None, :], w_ckv[l])
        x2d = _merge(att.reshape(t, ATTN_WIDTH), hm.reshape(t, MLSTM_WIDTH), z2d, x2d,
                     w_abr_b, w_mbr_b, w_out_b, norm_cross[l][None, :], w_cq[l].astype(BF16), ckv,
                     w_co[l].astype(BF16), s)

        x2d = _mlp(x2d, norm_mlp[l][None, :], w_up_b, w_down_b,
                   norm_final[None, :], final_norm=(l == DEPTH - 1))
    return x2d.reshape(b, s, D_MODEL)
```
